```python
import jax, jax.numpy as jnp
from jax import lax
import numpy as np

D_MODEL = 2048
BATCH = 2
SEQ = 16384
DEPTH = 1
DEC_BATCH = 32
DEC_SEQ = 64
PAST_LEN = 1024

CHUNK = 64
MIX_WIDTH = D_MODEL
M_HEADS = 4
M_WIDTH = MIX_WIDTH // 2
M_DV = M_WIDTH // M_HEADS
M_DK = M_DV // 2
QK_W = M_HEADS * M_DK
CONV_CH = MIX_WIDTH - M_WIDTH
CONV_WIDTH = 31
CONV_STATE = CONV_WIDTH - 1
N_MEM = 256
CA_HEADS = 4
CA_DH = 128
N_EXPERTS = 32
TOP_K = 4
D_FF = D_MODEL
SWIGLU_ALPHA = 1.702
SWIGLU_LIMIT = 7.0
MOE_ROWS = 128
EPS = 1e-6
IN_SPLITS = (QK_W, 2 * QK_W, 2 * QK_W + M_WIDTH, 2 * QK_W + 2 * M_WIDTH,
             2 * QK_W + 2 * M_WIDTH + M_HEADS, 2 * QK_W + 2 * M_WIDTH + 2 * M_HEADS,
             2 * QK_W + 2 * M_WIDTH + 2 * M_HEADS + CONV_CH)
IN_COLS = 2 * QK_W + 2 * M_WIDTH + 2 * M_HEADS + 2 * CONV_CH

kernel_name = 'hybrid_mlstm_conformer_moe_stream_step'


def rmsnorm(x, g):
    x32 = x.astype(jnp.float32)
    y = x32 * lax.rsqrt(jnp.mean(x32 * x32, axis=-1, keepdims=True) + EPS)
    return (y * g.astype(jnp.float32)).astype(x.dtype)


def mlstm_chunk(carry, blk):
    c0, n0, m0 = carry
    q, k, v, ig, lf = blk
    L = q.shape[1]
    b = jnp.cumsum(lf, axis=1)
    g = b + m0[:, None, :]
    causal = jnp.tril(jnp.ones((L, L), dtype=bool))[None, :, :, None]
    dmat = jnp.where(causal, b[:, :, None, :] - b[:, None, :, :] + ig[:, None, :, :], -jnp.inf)
    m = jnp.maximum(g, jnp.max(dmat, axis=2))
    w_state = jnp.exp(g - m)
    s = jnp.einsum('bthd,bshd->btsh', q, k) * jnp.exp(dmat - m[:, :, None, :])
    num = jnp.einsum('btsh,bshv->bthv', s, v) + w_state[..., None] * jnp.einsum('bthd,bhdv->bthv', q, c0)
    den = jnp.sum(s, axis=2) + w_state * jnp.einsum('bthd,bhd->bth', q, n0)
    h = num / jnp.maximum(jnp.abs(den), jnp.exp(-m))[..., None]
    b_last = b[:, -1]
    g_last = b_last + m0
    wk = b_last[:, None, :] - b + ig
    m_new = jnp.maximum(g_last, jnp.max(wk, axis=1))
    a0 = jnp.exp(g_last - m_new)
    ak = jnp.exp(wk - m_new[:, None, :])
    c_new = a0[:, :, None, None] * c0 + jnp.einsum('bsh,bshd,bshv->bhdv', ak, k, v)
    n_new = a0[:, :, None] * n0 + jnp.einsum('bsh,bshd->bhd', ak, k)
    return (c_new, n_new, m_new), h


def mlstm_seq(q, k, v, ig, lf, state):
    bsz, t = q.shape[0], q.shape[1]
    L = min(CHUNK, t)
    nc = t // L

    def to_chunks(a):
        return jnp.moveaxis(a.reshape((bsz, nc, L) + a.shape[2:]), 1, 0)

    state, h = lax.scan(mlstm_chunk, state,
                        (to_chunks(q), to_chunks(k), to_chunks(v), to_chunks(ig), to_chunks(lf)))
    h = jnp.moveaxis(h, 0, 1).reshape(bsz, t, M_HEADS, M_DV)
    return h, state


def mixer(h, state, conv_buf, lp):
    bsz, t, _ = h.shape
    z = h @ lp['w_in']
    q, k, v, o, gi, gf, ua, ug = jnp.split(z, IN_SPLITS, axis=-1)
    f32 = jnp.float32
    q = q.reshape(bsz, t, M_HEADS, M_DK).astype(f32) * (M_DK ** -0.5)
    k = k.reshape(bsz, t, M_HEADS, M_DK).astype(f32)
    v = v.reshape(bsz, t, M_HEADS, M_DV).astype(f32)
    bg = lp['b_gates'].astype(f32)
    ig = gi.astype(f32) + bg[:M_HEADS]
    lf = jax.nn.log_sigmoid(gf.astype(f32) + bg[M_HEADS:])
    hm, state = mlstm_seq(q, k, v, ig, lf, state)
    hm = hm * lax.rsqrt(jnp.mean(hm * hm, axis=-1, keepdims=True) + EPS) * lp['mh_norm_g'].astype(f32)
    og = jax.nn.sigmoid(o.astype(f32).reshape(bsz, t, M_HEADS, M_DV))
    hm = (og * hm).reshape(bsz, t, M_WIDTH).astype(h.dtype)
    u = ua * jax.nn.sigmoid(ug)
    ext = jnp.concatenate([conv_buf.astype(u.dtype), u], axis=1)
    c = lax.conv_general_dilated(ext, lp['conv_w'][:, None, :].astype(u.dtype), window_strides=(1,),
                                 padding='VALID', dimension_numbers=('NWC', 'WIO', 'NWC'),
                                 feature_group_count=CONV_CH) + lp['conv_b'].astype(u.dtype)
    c32 = c.astype(f32)
    mu = jnp.mean(c32, axis=-1, keepdims=True)
    var = jnp.mean(jnp.square(c32 - mu), axis=-1, keepdims=True)
    cn = (c32 - mu) * lax.rsqrt(var + EPS) * lp['conv_ln_g'].astype(f32) + lp['conv_ln_b'].astype(f32)
    cv = (cn * jax.nn.sigmoid(cn)).astype(h.dtype)
    out = jnp.concatenate([hm, cv], axis=-1) @ lp['w_out']
    return out, state, ext[:, -CONV_STATE:]


def cross_attn(h, mem_k, mem_v, wq, wo):
    bsz, t, _ = h.shape
    q = (h @ wq).reshape(bsz, t, CA_HEADS, CA_DH)
    s = jnp.einsum('bthd,bmhd->bhtm', q, mem_k.astype(h.dtype)).astype(jnp.float32) * (CA_DH ** -0.5)
    p = jax.nn.softmax(s, axis=-1).astype(h.dtype)
    o = jnp.einsum('bhtm,bmhd->bthd', p, mem_v.astype(h.dtype)).reshape(bsz, t, CA_HEADS * CA_DH)
    return o @ wo


def moe(h, router_w, router_b, w1, b1, w2, b2):
    n, d = h.shape
    logits = (h @ router_w).astype(jnp.float32) + router_b.astype(jnp.float32)
    top_v, top_e = lax.top_k(logits, TOP_K)
    gates = jax.nn.softmax(top_v, axis=-1)
    a = n * TOP_K
    flat_e = top_e.reshape(a)
    flat_t = jnp.arange(a, dtype=jnp.int32) // TOP_K
    flat_g = gates.reshape(a).astype(h.dtype)
    order = jnp.argsort(flat_e)
    sorted_e = flat_e[order]
    counts = jnp.bincount(flat_e, length=N_EXPERTS)
    starts = jnp.cumsum(counts) - counts
    padded = (counts + MOE_ROWS - 1) // MOE_ROWS * MOE_ROWS
    pad_ends = jnp.cumsum(padded)
    pad_starts = pad_ends - padded
    dest = pad_starts[sorted_e] + (jnp.arange(a, dtype=jnp.int32) - starts[sorted_e])
    n_blocks = -(-a // MOE_ROWS) + N_EXPERTS
    rows = n_blocks * MOE_ROWS
    row_tok = jnp.full((rows,), n, jnp.int32).at[dest].set(flat_t[order])
    row_gate = jnp.zeros((rows,), h.dtype).at[dest].set(flat_g[order])
    block_e = jnp.minimum(jnp.searchsorted(pad_ends, jnp.arange(n_blocks) * MOE_ROWS, side='right'),
                          N_EXPERTS - 1).astype(jnp.int32)
    h_pad = jnp.concatenate([h, jnp.zeros((1, d), h.dtype)], axis=0)

    def body(acc, blk):
        e, tok, g = blk
        xb = h_pad[tok]
        u = xb @ w1[e] + b1[e]
        glu = jnp.minimum(u[:, :D_FF], SWIGLU_LIMIT)
        lin = jnp.clip(u[:, D_FF:], -SWIGLU_LIMIT, SWIGLU_LIMIT)
        act = glu * jax.nn.sigmoid(SWIGLU_ALPHA * glu) * (lin + 1)
        out = (act @ w2[e] + b2[e]) * g[:, None]
        return acc.at[tok].add(out.astype(acc.dtype)), None

    acc, _ = lax.scan(body, jnp.zeros((n + 1, d), h.dtype),
                      (block_e, row_tok.reshape(n_blocks, MOE_ROWS), row_gate.reshape(n_blocks, MOE_ROWS)))
    return acc[:n]


def layer(x, state, conv_buf, mem_k, mem_v, lp):
    mix, state, conv_buf = mixer(rmsnorm(x, lp['norm_mix_g']), state, conv_buf, lp)
    x = x + mix
    x = x + cross_attn(rmsnorm(x, lp['norm_ca_g']), mem_k, mem_v, lp['ca_wq'], lp['ca_wo'])
    bsz, t, d = x.shape
    hn = rmsnorm(x, lp['norm_moe_g']).reshape(bsz * t, d)
    x = x + moe(hn, lp['router_w'], lp['router_b'], lp['moe_w1'], lp['moe_b1'],
                lp['moe_w2'], lp['moe_b2']).reshape(bsz, t, d)
    return x, state, conv_buf


def setup_inputs(seed: int = 0) -> dict:
    key = jax.random.key(seed)
    ks = iter(jax.random.split(key, 48))

    def nrm(shape, scale):
        return jax.random.normal(next(ks), shape, jnp.float32) * scale

    def gain(shape):
        return 1.0 + nrm(shape, 0.02)

    b_gates = jnp.concatenate([nrm((DEPTH, M_HEADS), 0.1),
                               3.0 + 3.0 * jax.random.uniform(next(ks), (DEPTH, M_HEADS), jnp.float32)], axis=-1)
    return {
        'x_prompt': nrm((BATCH, SEQ, D_MODEL), 1.0),
        'x_sample': nrm((DEC_BATCH, DEC_SEQ, D_MODEL), 1.0),
        'mem_prompt': nrm((BATCH, N_MEM, D_MODEL), 1.0),
        'state_mlstm_C': nrm((DEPTH, DEC_BATCH, M_HEADS, M_DK, M_DV), 0.5),
        'state_mlstm_n': nrm((DEPTH, DEC_BATCH, M_HEADS, M_DK), 0.5),
        'state_mlstm_m': nrm((DEPTH, DEC_BATCH, M_HEADS), 1.0),
        'state_conv': nrm((DEPTH, DEC_BATCH, CONV_STATE, CONV_CH), 0.5),
        'cache_mem_k': nrm((DEPTH, DEC_BATCH, N_MEM, CA_HEADS, CA_DH), 1.0),
        'cache_mem_v': nrm((DEPTH, DEC_BATCH, N_MEM, CA_HEADS, CA_DH), 1.0),
        'norm_mix_g': gain((DEPTH, D_MODEL)),
        'w_in': nrm((DEPTH, D_MODEL, IN_COLS), D_MODEL ** -0.5),
        'b_gates': b_gates,
        'mh_norm_g': gain((DEPTH, M_HEADS, M_DV)),
        'conv_w': nrm((DEPTH, CONV_WIDTH, CONV_CH), CONV_WIDTH ** -0.5),
        'conv_b': nrm((DEPTH, CONV_CH), 0.02),
        'conv_ln_g': gain((DEPTH, CONV_CH)),
        'conv_ln_b': nrm((DEPTH, CONV_CH), 0.02),
        'w_out': nrm((DEPTH, MIX_WIDTH, D_MODEL), MIX_WIDTH ** -0.5),
        'norm_ca_g': gain((DEPTH, D_MODEL)),
        'norm_mem_g': gain((DEPTH, D_MODEL)),
        'ca_wq': nrm((DEPTH, D_MODEL, CA_HEADS * CA_DH), D_MODEL ** -0.5),
        'ca_wk': nrm((DEPTH, D_MODEL, CA_HEADS * CA_DH), D_MODEL ** -0.5),
        'ca_wv': nrm((DEPTH, D_MODEL, CA_HEADS * CA_DH), D_MODEL ** -0.5),
        'ca_wo': nrm((DEPTH, CA_HEADS * CA_DH, D_MODEL), (CA_HEADS * CA_DH) ** -0.5),
        'norm_moe_g': gain((DEPTH, D_MODEL)),
        'router_w': nrm((DEPTH, D_MODEL, N_EXPERTS), D_MODEL ** -0.5),
        'router_b': nrm((DEPTH, N_EXPERTS), 0.01),
        'moe_w1': nrm((DEPTH, N_EXPERTS, D_MODEL, 2 * D_FF), D_MODEL ** -0.5),
        'moe_b1': nrm((DEPTH, N_EXPERTS, 2 * D_FF), 0.02),
        'moe_w2': nrm((DEPTH, N_EXPERTS, D_FF, D_MODEL), D_FF ** -0.5),
        'moe_b2': nrm((DEPTH, N_EXPERTS, D_MODEL), 0.02),
        'final_norm_g': gain((D_MODEL,)),
    }


def reference(x_prompt, x_sample, mem_prompt, state_mlstm_C, state_mlstm_n, state_mlstm_m, state_conv,
              cache_mem_k, cache_mem_v, norm_mix_g, w_in, b_gates, mh_norm_g, conv_w, conv_b, conv_ln_g,
              conv_ln_b, w_out, norm_ca_g, norm_mem_g, ca_wq, ca_wk, ca_wv, ca_wo, norm_moe_g, router_w,
              router_b, moe_w1, moe_b1, moe_w2, moe_b2, final_norm_g):
    xp, xs = x_prompt, x_sample
    bp = xp.shape[0]
    f32 = jnp.float32
    p_c, p_n, p_m, p_conv, p_mk, p_mv = [], [], [], [], [], []
    s_c, s_n, s_m, s_conv = [], [], [], []
    for l in range(DEPTH):
        lp = {'norm_mix_g': norm_mix_g[l], 'w_in': w_in[l], 'b_gates': b_gates[l], 'mh_norm_g': mh_norm_g[l],
              'conv_w': conv_w[l], 'conv_b': conv_b[l], 'conv_ln_g': conv_ln_g[l], 'conv_ln_b': conv_ln_b[l],
              'w_out': w_out[l], 'norm_ca_g': norm_ca_g[l], 'ca_wq': ca_wq[l], 'ca_wo': ca_wo[l],
              'norm_moe_g': norm_moe_g[l], 'router_w': router_w[l], 'router_b': router_b[l],
              'moe_w1': moe_w1[l], 'moe_b1': moe_b1[l], 'moe_w2': moe_w2[l], 'moe_b2': moe_b2[l]}
        mn = rmsnorm(mem_prompt, norm_mem_g[l])
        mk = (mn @ ca_wk[l]).reshape(bp, N_MEM, CA_HEADS, CA_DH)
        mv = (mn @ ca_wv[l]).reshape(bp, N_MEM, CA_HEADS, CA_DH)
        st0 = (jnp.zeros((bp, M_HEADS, M_DK, M_DV), f32), jnp.zeros((bp, M_HEADS, M_DK), f32),
               jnp.zeros((bp, M_HEADS), f32))
        buf0 = jnp.zeros((bp, CONV_STATE, CONV_CH), xp.dtype)
        xp, (c1, n1, m1), buf1 = layer(xp, st0, buf0, mk, mv, lp)
        p_c.append(c1); p_n.append(n1); p_m.append(m1); p_conv.append(buf1); p_mk.append(mk); p_mv.append(mv)
        sst = (state_mlstm_C[l].astype(f32), state_mlstm_n[l].astype(f32), state_mlstm_m[l].astype(f32))
        xs, (c2, n2, m2), buf2 = layer(xs, sst, state_conv[l], cache_mem_k[l], cache_mem_v[l], lp)
        s_c.append(c2); s_n.append(n2); s_m.append(m2); s_conv.append(buf2)
    y_prompt = rmsnorm(xp, final_norm_g)
    y_sample = rmsnorm(xs, final_norm_g)
    return (y_prompt, y_sample, jnp.stack(p_c), jnp.stack(p_n), jnp.stack(p_m), jnp.stack(p_conv),
            jnp.stack(p_mk), jnp.stack(p_mv), jnp.stack(s_c), jnp.stack(s_n), jnp.stack(s_m), jnp.stack(s_conv))
```

```python
import functools

import jax
import jax.numpy as jnp
from jax import lax
from jax.experimental import pallas as pl
from jax.experimental.pallas import tpu as pltpu

F32 = jnp.float32
BF16 = jnp.bfloat16
HIGHEST = lax.Precision.HIGHEST

D_MODEL = 2048
M_HEADS = 4
M_DV = 256
M_DK = 128
QK_W = M_HEADS * M_DK
M_WIDTH = M_HEADS * M_DV
CONV_CH = 1024
CONV_WIDTH = 31
CONV_STATE = CONV_WIDTH - 1
CONV_PAD = 32
N_MEM = 256
CA_HEADS = 4
CA_DH = 128
CA_W = CA_HEADS * CA_DH
N_EXPERTS = 32
TOP_K = 4
D_FF = 2048
SWIGLU_ALPHA = 1.702
SWIGLU_LIMIT = 7.0
EPS = 1e-6
LANES = 128
SUBLANES = 8
NEG_BIG = -1e30

PROJ_TM = 1024
PROJ_TN = 1024
GLU_HALF = PROJ_TN // 2
TOK_TILE = 512
MOE_TM = 512
MOE_TF = 512
MLSTM_CHUNK = 256
CONV_TT = 256
CONV_RB = 32
CONV_LC = 512


def _cparams(sem, vmem_mb):
    return pltpu.CompilerParams(dimension_semantics=sem, vmem_limit_bytes=vmem_mb << 20)


def _log_sigmoid(x):
    return jnp.minimum(x, 0.0) - jnp.log1p(jnp.exp(-jnp.abs(x)))


def _sigmoid(x):
    return 1.0 / (1.0 + jnp.exp(-x))


def _rms(x, g):
    ms = jnp.mean(x * x, axis=-1, keepdims=True)
    return x * lax.rsqrt(ms + EPS) * g


def _proj_in_kernel(x_ref, g_ref, w_ref, wg_ref, bg_ref, qkvo_ref, u_ref, gates_ref, h_ref, *, chunk):
    j = pl.program_id(1)
    tm = x_ref.shape[0]

    @pl.when(j == 0)
    def _():
        h = _rms(x_ref[...], g_ref[...]).astype(BF16)
        h_ref[...] = h
        gt = lax.dot_general(wg_ref[...], h, (((1,), (1,)), ((), ())), preferred_element_type=F32)
        gt = gt + bg_ref[...]
        row = lax.broadcasted_iota(jnp.int32, gt.shape, 0)
        gt = jnp.where(row < M_HEADS, gt, _log_sigmoid(gt))
        for c in range(tm // chunk):
            gates_ref[c] = gt[:, c * chunk:(c + 1) * chunk]

    z = jnp.dot(h_ref[...], w_ref[...], preferred_element_type=F32)

    @pl.when(j < 3)
    def _():
        qkvo_ref[...] = z.astype(BF16)

    @pl.when(j >= 3)
    def _():
        u_ref[...] = (z[:, :GLU_HALF] * _sigmoid(z[:, GLU_HALF:])).astype(BF16)


def _proj_in(x, g, w_main, wg, bg, chunk):
    n = x.shape[0]
    tm = min(PROJ_TM, n)
    nj = w_main.shape[1] // PROJ_TN
    return pl.pallas_call(
        functools.partial(_proj_in_kernel, chunk=chunk),
        grid=(n // tm, nj),
        in_specs=[
            pl.BlockSpec((tm, D_MODEL), lambda i, j: (i, 0)),
            pl.BlockSpec((1, D_MODEL), lambda i, j: (0, 0)),
            pl.BlockSpec((D_MODEL, PROJ_TN), lambda i, j: (0, j)),
            pl.BlockSpec((2 * M_HEADS, D_MODEL), lambda i, j: (0, 0)),
            pl.BlockSpec((2 * M_HEADS, 1), lambda i, j: (0, 0)),
        ],
        out_specs=[
            pl.BlockSpec((tm, PROJ_TN), lambda i, j: (i, jnp.minimum(j, 2))),
            pl.BlockSpec((tm, GLU_HALF), lambda i, j: (i, jnp.maximum(j - 3, 0))),
            pl.BlockSpec((tm // chunk, 2 * M_HEADS, chunk), lambda i, j: (i, 0, 0)),
        ],
        out_shape=[
            jax.ShapeDtypeStruct((n, 2 * QK_W + 2 * M_WIDTH), BF16),
            jax.ShapeDtypeStruct((n, CONV_CH), BF16),
            jax.ShapeDtypeStruct((n // chunk, 2 * M_HEADS, chunk), F32),
        ],
        scratch_shapes=[pltpu.VMEM((tm, D_MODEL), BF16)],
        compiler_params=_cparams(("parallel", "arbitrary"), 48),
        name="proj_in",
    )(x, g, w_main, wg, bg)


def _conv_kernel(u_ref, st_ref, w_ref, cb_ref, lg_ref, lb_ref, cv_ref, ns_ref, e_ref, er_ref, c_ref, *, bb, tt):
    t = pl.program_id(1)
    nt = pl.num_programs(1)
    ext = tt + CONV_PAD

    @pl.when(t == 0)
    def _():
        e_ref[:, 0:CONV_PAD, :] = st_ref[...]

    @pl.when(t > 0)
    def _():
        e_ref[:, 0:CONV_PAD, :] = e_ref[:, tt:ext, :]

    e_ref[:, CONV_PAD:ext, :] = u_ref[...].astype(F32)
    e_ref[:, ext:, :] = jnp.zeros((bb, SUBLANES, CONV_CH), F32)

    base = CONV_PAD - CONV_STATE
    for b in range(bb):
        for r in range(1, SUBLANES):
            er_ref[r - 1] = e_ref[b, r:r + ext, :]

        def row_block(rb, carry):
            r0 = pl.multiple_of(rb * CONV_RB, CONV_RB)
            for lc in range(CONV_CH // CONV_LC):
                ls = slice(lc * CONV_LC, (lc + 1) * CONV_LC)
                acc = jnp.zeros((CONV_RB, CONV_LC), F32) + cb_ref[:, ls]
                for j in range(CONV_WIDTH):
                    off = j + base
                    a, r = (off // SUBLANES) * SUBLANES, off % SUBLANES
                    if r == 0:
                        tap = e_ref[b, pl.ds(r0 + a, CONV_RB), ls]
                    else:
                        tap = er_ref[r - 1, pl.ds(r0 + a, CONV_RB), ls]
                    acc = acc + w_ref[j:j + 1, ls] * tap
                c_ref[:, ls] = acc
            c = c_ref[...]
            mu = jnp.mean(c, axis=-1, keepdims=True)
            cc = c - mu
            var = jnp.mean(cc * cc, axis=-1, keepdims=True)
            cn = cc * lax.rsqrt(var + EPS) * lg_ref[...] + lb_ref[...]
            cv_ref[b, pl.ds(r0, CONV_RB), :] = (cn * _sigmoid(cn)).astype(BF16)
            return carry

        lax.fori_loop(0, tt // CONV_RB, row_block, 0)

    @pl.when(t == nt - 1)
    def _():
        ns_ref[...] = e_ref[:, ext - CONV_STATE:ext, :]


def _conv(u, state_pad, conv_w, conv_b, ln_g, ln_b, bb, tt):
    bsz, t, _ = u.shape
    return pl.pallas_call(
        functools.partial(_conv_kernel, bb=bb, tt=tt),
        grid=(bsz // bb, t // tt),
        in_specs=[
            pl.BlockSpec((bb, tt, CONV_CH), lambda b, s: (b, s, 0)),
            pl.BlockSpec((bb, CONV_PAD, CONV_CH), lambda b, s: (b, 0, 0)),
            pl.BlockSpec((CONV_WIDTH, CONV_CH), lambda b, s: (0, 0)),
            pl.BlockSpec((1, CONV_CH), lambda b, s: (0, 0)),
            pl.BlockSpec((1, CONV_CH), lambda b, s: (0, 0)),
            pl.BlockSpec((1, CONV_CH), lambda b, s: (0, 0)),
        ],
        out_specs=[
            pl.BlockSpec((bb, tt, CONV_CH), lambda b, s: (b, s, 0)),
            pl.BlockSpec((bb, CONV_STATE, CONV_CH), lambda b, s: (b, 0, 0)),
        ],
        out_shape=[
            jax.ShapeDtypeStruct((bsz, t, CONV_CH), BF16),
            jax.ShapeDtypeStruct((bsz, CONV_STATE, CONV_CH), F32),
        ],
        scratch_shapes=[
            pltpu.VMEM((bb, tt + CONV_PAD + SUBLANES, CONV_CH), F32),
            pltpu.VMEM((SUBLANES - 1, tt + CONV_PAD, CONV_CH), F32),
            pltpu.VMEM((CONV_RB, CONV_CH), F32),
        ],
        compiler_params=_cparams(("parallel", "arbitrary"), 40),
        name="conv",
    )(u, state_pad, conv_w, conv_b, ln_g, ln_b)


def _mlstm_kernel(q_ref, k_ref, v_ref, o_ref, gt_ref, c0_ref, n0_ref, m0_ref, ng_ref,
                  hm_ref, cn_ref, nn_ref, mn_ref, c_s, n_s, m_s, *, bb, L):
    ci = pl.program_id(1)
    nc = pl.num_programs(1)

    @pl.when(ci == 0)
    def _():
        c_s[...] = c0_ref[...]
        n_s[...] = n0_ref[...]
        m_s[...] = m0_ref[...]

    rt = lax.broadcasted_iota(jnp.int32, (L, L), 0)
    cs = lax.broadcasted_iota(jnp.int32, (L, L), 1)
    causal = cs <= rt
    lower = causal.astype(F32)
    upper = (rt <= cs).astype(F32)
    eye = (rt == cs).astype(F32)
    nt_dims = (((1,), (1,)), ((), ()))
    scale = M_DK ** -0.5

    for b in range(bb):
        rows = gt_ref[b]
        cum_rows = jnp.dot(rows, upper, precision=HIGHEST, preferred_element_type=F32)
        cum_cols = lax.dot_general(lower, rows, nt_dims, precision=HIGHEST, preferred_element_type=F32)
        id_cols = lax.dot_general(eye, rows, nt_dims, precision=HIGHEST, preferred_element_type=F32)
        for h in range(M_HEADS):
            ig_row = rows[h:h + 1, :]
            ig_col = id_cols[:, h:h + 1]
            b_row = cum_rows[M_HEADS + h:M_HEADS + h + 1, :]
            b_col = cum_cols[:, M_HEADS + h:M_HEADS + h + 1]
            m0 = m_s[b, h:h + 1, 0:1]
            q = q_ref[b, :, h * M_DK:(h + 1) * M_DK]
            k = k_ref[b, :, h * M_DK:(h + 1) * M_DK]
            v = v_ref[b, :, h * M_DV:(h + 1) * M_DV]
            c0 = c_s[b, h]
            n0 = n_s[b, h:h + 1, :]

            g_col = b_col + m0
            dmat = jnp.where(causal, b_col - b_row + ig_row, -jnp.inf)
            m_col = jnp.maximum(g_col, jnp.max(dmat, axis=1, keepdims=True))
            w_state = jnp.exp(g_col - m_col) * scale
            qk = lax.dot_general(q, k, nt_dims, preferred_element_type=F32)
            s = qk * (jnp.exp(dmat - m_col) * scale)
            num = jnp.dot(s.astype(BF16), v, preferred_element_type=F32)
            num = num + w_state * jnp.dot(q, c0.astype(BF16), preferred_element_type=F32)
            qn = jnp.sum(q.astype(F32) * n0, axis=1, keepdims=True)
            den = jnp.sum(s, axis=1, keepdims=True) + w_state * qn
            hh = num / jnp.maximum(jnp.abs(den), jnp.exp(-m_col))

            hh = hh * lax.rsqrt(jnp.mean(hh * hh, axis=-1, keepdims=True) + EPS) * ng_ref[h:h + 1, :]
            og = _sigmoid(o_ref[b, :, h * M_DV:(h + 1) * M_DV].astype(F32))
            hm_ref[b, :, h * M_DV:(h + 1) * M_DV] = (og * hh).astype(BF16)

            b_last = b_row[:, L - 1:L]
            g_last = b_last + m0
            wk_row = b_last - b_row + ig_row
            m_new = jnp.maximum(g_last, jnp.max(wk_row, axis=1, keepdims=True))
            a0 = jnp.exp(g_last - m_new)
            ak_col = jnp.exp(b_last - b_col + ig_col - m_new)
            ks = k.astype(F32) * ak_col
            kv = lax.dot_general(ks.astype(BF16), v, (((0,), (0,)), ((), ())), preferred_element_type=F32)
            c_s[b, h] = a0 * c0 + kv
            n_s[b, h:h + 1, :] = a0 * n0 + jnp.sum(ks, axis=0, keepdims=True)
            m_s[b, h:h + 1, :] = jnp.broadcast_to(m_new, (1, LANES))

    @pl.when(ci == nc - 1)
    def _():
        cn_ref[...] = c_s[...]
        nn_ref[...] = n_s[...]
        mn_ref[...] = m_s[...]


def _mlstm(qkvo, gates, c0, n0, m0, ng, bb, L):
    bsz, t, _ = qkvo.shape
    nc = t // L
    gates = gates.reshape(bsz, nc, 2 * M_HEADS, L)
    st_c = pl.BlockSpec((bb, M_HEADS, M_DK, M_DV), lambda b, c: (b, 0, 0, 0))
    st_n = pl.BlockSpec((bb, M_HEADS, M_DK), lambda b, c: (b, 0, 0))
    return pl.pallas_call(
        functools.partial(_mlstm_kernel, bb=bb, L=L),
        grid=(bsz // bb, nc),
        in_specs=[
            pl.BlockSpec((bb, L, QK_W), lambda b, c: (b, c, 0)),
            pl.BlockSpec((bb, L, QK_W), lambda b, c: (b, c, 1)),
            pl.BlockSpec((bb, L, M_WIDTH), lambda b, c: (b, c, 1)),
            pl.BlockSpec((bb, L, M_WIDTH), lambda b, c: (b, c, 2)),
            pl.BlockSpec((bb, None, 2 * M_HEADS, L), lambda b, c: (b, c, 0, 0)),
            st_c, st_n, st_n,
            pl.BlockSpec((M_HEADS, M_DV), lambda b, c: (0, 0)),
        ],
        out_specs=[
            pl.BlockSpec((bb, L, M_WIDTH), lambda b, c: (b, c, 0)),
            st_c, st_n, st_n,
        ],
        out_shape=[
            jax.ShapeDtypeStruct((bsz, t, M_WIDTH), BF16),
            jax.ShapeDtypeStruct((bsz, M_HEADS, M_DK, M_DV), F32),
            jax.ShapeDtypeStruct((bsz, M_HEADS, M_DK), F32),
            jax.ShapeDtypeStruct((bsz, M_HEADS, LANES), F32),
        ],
        scratch_shapes=[
            pltpu.VMEM((bb, M_HEADS, M_DK, M_DV), F32),
            pltpu.VMEM((bb, M_HEADS, M_DK), F32),
            pltpu.VMEM((bb, M_HEADS, LANES), F32),
        ],
        compiler_params=_cparams(("parallel", "arbitrary"), 40),
        name="mlstm",
    )(qkvo, qkvo, qkvo, qkvo, gates, c0, n0, m0, ng)


def _proj_out_kernel(x_ref, hm_ref, cv_ref, wa_ref, wb_ref, o_ref):
    acc = jnp.dot(hm_ref[...], wa_ref[...], preferred_element_type=F32)
    acc = acc + jnp.dot(cv_ref[...], wb_ref[...], preferred_element_type=F32)
    o_ref[...] = x_ref[...] + acc


def _proj_out(x, hm, cv, wa, wb):
    n = x.shape[0]
    tm = min(TOK_TILE, n)
    return pl.pallas_call(
        _proj_out_kernel,
        grid=(n // tm,),
        in_specs=[
            pl.BlockSpec((tm, D_MODEL), lambda i: (i, 0)),
            pl.BlockSpec((tm, M_WIDTH), lambda i: (i, 0)),
            pl.BlockSpec((tm, CONV_CH), lambda i: (i, 0)),
            pl.BlockSpec((M_WIDTH, D_MODEL), lambda i: (0, 0)),
            pl.BlockSpec((CONV_CH, D_MODEL), lambda i: (0, 0)),
        ],
        out_specs=pl.BlockSpec((tm, D_MODEL), lambda i: (i, 0)),
        out_shape=jax.ShapeDtypeStruct((n, D_MODEL), F32),
        compiler_params=_cparams(("parallel",), 48),
        name="proj_out",
    )(x, hm, cv, wa, wb)


def _mem_kv_kernel(m_ref, g_ref, w_ref, k_ref, v_ref):
    mn = _rms(m_ref[...], g_ref[...]).astype(BF16)
    z = jnp.dot(mn, w_ref[...], preferred_element_type=F32)
    k_ref[...] = z[:, :CA_W]
    v_ref[...] = z[:, CA_W:]


def _mem_kv(mem, g, wkv):
    n = mem.shape[0]
    return pl.pallas_call(
        _mem_kv_kernel,
        grid=(1,),
        in_specs=[
            pl.BlockSpec((n, D_MODEL), lambda i: (0, 0)),
            pl.BlockSpec((1, D_MODEL), lambda i: (0, 0)),
            pl.BlockSpec((D_MODEL, 2 * CA_W), lambda i: (0, 0)),
        ],
        out_specs=[pl.BlockSpec((n, CA_W), lambda i: (0, 0))] * 2,
        out_shape=[jax.ShapeDtypeStruct((n, CA_W), F32)] * 2,
        compiler_params=_cparams(("arbitrary",), 40),
        name="mem_kv",
    )(mem, g, wkv)


def _attn_router_kernel(x_ref, gca_ref, wq_ref, k_ref, v_ref, wo_ref, gmoe_ref, rw_ref, rb_ref, cnt0_ref,
                        hn_in_ref, x2_ref, hn_ref, te_ref, tg_ref, tr_ref, cnt_ref, o_s, cnt_s, *, bb, tt):
    del hn_in_ref
    rows = bb * tt

    @pl.when((pl.program_id(0) == 0) & (pl.program_id(1) == 0))
    def _():
        cnt_s[...] = cnt0_ref[...]

    x = x_ref[...].reshape(rows, D_MODEL)
    h = _rms(x, gca_ref[...]).astype(BF16)
    q = jnp.dot(h, wq_ref[...], preferred_element_type=F32) * (CA_DH ** -0.5)
    q = q.astype(BF16)
    for b in range(bb):
        for hd in range(CA_HEADS):
            cols = slice(hd * CA_DH, (hd + 1) * CA_DH)
            qb = q[b * tt:(b + 1) * tt, cols]
            s = lax.dot_general(qb, k_ref[b, :, cols], (((1,), (1,)), ((), ())), preferred_element_type=F32)
            e = jnp.exp(s - jnp.max(s, axis=-1, keepdims=True))
            p = (e / jnp.sum(e, axis=-1, keepdims=True)).astype(BF16)
            o_s[b * tt:(b + 1) * tt, cols] = jnp.dot(p, v_ref[b, :, cols], preferred_element_type=F32).astype(BF16)
    x2 = x + jnp.dot(o_s[...], wo_ref[...], preferred_element_type=F32)
    x2_ref[...] = x2.reshape(bb, tt, D_MODEL)

    hn = _rms(x2, gmoe_ref[...])
    hn_bf = hn.astype(BF16)
    hn_ref[...] = hn_bf
    logits = jnp.dot(hn_bf, rw_ref[...], preferred_element_type=F32) + rb_ref[...]
    lane = lax.broadcasted_iota(jnp.int32, (rows, LANES), 1)
    work = logits
    sel = []
    vals = []
    for _ in range(TOP_K):
        mx = jnp.max(work, axis=-1, keepdims=True)
        idx = jnp.min(jnp.where(work == mx, lane, LANES), axis=-1, keepdims=True)
        hit = lane == idx
        sel.append((idx, hit))
        vals.append(mx)
        work = jnp.where(hit, -jnp.inf, work)
    ex = [jnp.exp(vv - vals[0]) for vv in vals]
    tot = ex[0] + ex[1] + ex[2] + ex[3]
    assigned = jnp.zeros((rows, LANES), F32)
    for _, hit in sel:
        assigned = assigned + hit.astype(F32)
    rt = lax.broadcasted_iota(jnp.int32, (rows, rows), 0)
    cs = lax.broadcasted_iota(jnp.int32, (rows, rows), 1)
    strict = (cs < rt).astype(BF16)
    before = jnp.dot(strict, assigned.astype(BF16), preferred_element_type=F32) + cnt_s[...]
    te = jnp.zeros((rows, LANES), jnp.int32)
    tg = jnp.zeros((rows, LANES), F32)
    tr = jnp.zeros((rows, LANES), F32)
    for kk, (idx, hit) in enumerate(sel):
        rank = jnp.sum(jnp.where(hit, before, 0.0), axis=-1, keepdims=True)
        te = jnp.where(lane == kk, idx, te)
        tg = jnp.where(lane == kk, ex[kk] / tot, tg)
        tr = jnp.where(lane == kk, rank, tr)
    te_ref[...] = te
    tg_ref[...] = tg
    tr_ref[...] = tr.astype(jnp.int32)
    cnt_s[...] = cnt_s[...] + jnp.sum(assigned, axis=0, keepdims=True)
    cnt_ref[...] = cnt_s[...]


def _attn_router(x1, mem_k, mem_v, cnt0, hn_all, row0, p, bb, tt):
    bsz, t, _ = x1.shape
    rows = bb * tt
    n = bsz * t
    nb, nt = bsz // bb, t // tt
    blk0 = row0 // rows
    tok = lambda b, s: (b * nt + s, 0)
    const = lambda b, s: (0, 0)
    grid_spec = dict(
        grid=(nb, nt),
        in_specs=[
            pl.BlockSpec((bb, tt, D_MODEL), lambda b, s: (b, s, 0)),
            pl.BlockSpec((1, D_MODEL), const),
            pl.BlockSpec((D_MODEL, CA_W), const),
            pl.BlockSpec((bb, N_MEM, CA_W), lambda b, s: (b, 0, 0)),
            pl.BlockSpec((bb, N_MEM, CA_W), lambda b, s: (b, 0, 0)),
            pl.BlockSpec((CA_W, D_MODEL), const),
            pl.BlockSpec((1, D_MODEL), const),
            pl.BlockSpec((D_MODEL, LANES), const),
            pl.BlockSpec((1, LANES), const),
            pl.BlockSpec((1, LANES), const),
            pl.BlockSpec(memory_space=pl.ANY),
        ],
        out_specs=[
            pl.BlockSpec((bb, tt, D_MODEL), lambda b, s: (b, s, 0)),
            pl.BlockSpec((rows, D_MODEL), lambda b, s: (blk0 + b * nt + s, 0)),
            pl.BlockSpec((rows, LANES), tok),
            pl.BlockSpec((rows, LANES), tok),
            pl.BlockSpec((rows, LANES), tok),
            pl.BlockSpec((1, LANES), const),
        ],
    )
    return pl.pallas_call(
        functools.partial(_attn_router_kernel, bb=bb, tt=tt),
        out_shape=[
            jax.ShapeDtypeStruct((bsz, t, D_MODEL), F32),
            jax.ShapeDtypeStruct(hn_all.shape, BF16),
            jax.ShapeDtypeStruct((n, LANES), jnp.int32),
            jax.ShapeDtypeStruct((n, LANES), F32),
            jax.ShapeDtypeStruct((n, LANES), jnp.int32),
            jax.ShapeDtypeStruct((1, LANES), F32),
        ],
        scratch_shapes=[pltpu.VMEM((rows, CA_W), BF16), pltpu.VMEM((1, LANES), F32)],
        input_output_aliases={10: 1},
        compiler_params=_cparams(("arbitrary", "arbitrary"), 48),
        name="attn_router",
        **grid_spec,
    )(x1, p["g_ca"], p["wq"], mem_k, mem_v, p["wo"], p["g_moe"], p["router_w"], p["router_b"], cnt0, hn_all)


def _moe_kernel(te_ref, nu_ref, x_ref, w1g_ref, w1l_ref, b1g_ref, b1l_ref, w2_ref, b2_ref, y_ref, acc_ref):
    t = pl.program_id(0)
    f = pl.program_id(1)
    nf = pl.num_programs(1)

    @pl.when(t < nu_ref[0])
    def _():
        x = x_ref[...]
        glu = jnp.dot(x, w1g_ref[0], preferred_element_type=F32) + b1g_ref[0]
        lin = jnp.dot(x, w1l_ref[0], preferred_element_type=F32) + b1l_ref[0]
        glu = jnp.minimum(glu, SWIGLU_LIMIT)
        lin = jnp.clip(lin, -SWIGLU_LIMIT, SWIGLU_LIMIT)
        act = glu * _sigmoid(SWIGLU_ALPHA * glu) * (lin + 1.0)
        part = jnp.dot(act.astype(BF16), w2_ref[0], preferred_element_type=F32)

        @pl.when(f == 0)
        def _():
            acc_ref[...] = part + b2_ref[0]

        @pl.when(f > 0)
        def _():
            acc_ref[...] += part

        @pl.when(f == nf - 1)
        def _():
            y_ref[...] = acc_ref[...].astype(BF16)


def _moe(tile_e, n_used, x_sorted, w1, b1, w2, b2):
    r = x_sorted.shape[0]
    n_tiles = r // MOE_TM
    nf = D_FF // MOE_TF

    def tile(t, f, te, nu):
        return jnp.minimum(t, nu[0] - 1)

    def ftile(t, f, te, nu):
        return jnp.where(t < nu[0], f, nf - 1)

    grid_spec = pltpu.PrefetchScalarGridSpec(
        num_scalar_prefetch=2,
        grid=(n_tiles, nf),
        in_specs=[
            pl.BlockSpec((MOE_TM, D_MODEL), lambda t, f, te, nu: (tile(t, f, te, nu), 0)),
            pl.BlockSpec((1, D_MODEL, MOE_TF), lambda t, f, te, nu: (te[t], 0, ftile(t, f, te, nu))),
            pl.BlockSpec((1, D_MODEL, MOE_TF), lambda t, f, te, nu: (te[t], 0, nf + ftile(t, f, te, nu))),
            pl.BlockSpec((1, 1, MOE_TF), lambda t, f, te, nu: (te[t], 0, ftile(t, f, te, nu))),
            pl.BlockSpec((1, 1, MOE_TF), lambda t, f, te, nu: (te[t], 0, nf + ftile(t, f, te, nu))),
            pl.BlockSpec((1, MOE_TF, D_MODEL), lambda t, f, te, nu: (te[t], ftile(t, f, te, nu), 0)),
            pl.BlockSpec((1, 1, D_MODEL), lambda t, f, te, nu: (te[t], 0, 0)),
        ],
        out_specs=pl.BlockSpec((MOE_TM, D_MODEL), lambda t, f, te, nu: (tile(t, f, te, nu), 0)),
        scratch_shapes=[pltpu.VMEM((MOE_TM, D_MODEL), F32)],
    )
    return pl.pallas_call(
        _moe_kernel,
        grid_spec=grid_spec,
        out_shape=jax.ShapeDtypeStruct((r, D_MODEL), BF16),
        compiler_params=_cparams(("arbitrary", "arbitrary"), 48),
        name="moe",
    )(tile_e, n_used, x_sorted, w1, w1, b1, b1, w2, b2)


def _combine_kernel(x_ref, y0_ref, y1_ref, y2_ref, y3_ref, tg_ref, g_ref, o_ref):
    tg = tg_ref[...]
    acc = x_ref[...]
    for kk, y_ref in enumerate((y0_ref, y1_ref, y2_ref, y3_ref)):
        acc = acc + y_ref[0].astype(F32) * tg[:, kk:kk + 1]
    o_ref[...] = _rms(acc, g_ref[...])


def _combine(x2, yg, tg, g, row0):
    n = x2.shape[0]
    tm = min(TOK_TILE, n)
    blk0 = row0 // tm
    yspec = [pl.BlockSpec((1, tm, D_MODEL), functools.partial(lambda i, kk: (kk, blk0 + i, 0), kk=kk))
             for kk in range(TOP_K)]
    return pl.pallas_call(
        _combine_kernel,
        grid=(n // tm,),
        in_specs=[pl.BlockSpec((tm, D_MODEL), lambda i: (i, 0))] + yspec + [
            pl.BlockSpec((tm, LANES), lambda i: (i, 0)),
            pl.BlockSpec((1, D_MODEL), lambda i: (0, 0)),
        ],
        out_specs=pl.BlockSpec((tm, D_MODEL), lambda i: (i, 0)),
        out_shape=jax.ShapeDtypeStruct((n, D_MODEL), F32),
        compiler_params=_cparams(("parallel",), 48),
        name="combine",
    )(x2, yg, yg, yg, yg, tg, g)


def _group_tiles(bsz, t):
    tt = min(t, TOK_TILE)
    bb = max(1, min(bsz, TOK_TILE // tt))
    return bb, tt


def _layer_group(x, c0, n0, m0, conv_state, mem_k, mem_v, cnt0, hn_all, row0, p):
    bsz, t, _ = x.shape
    n = bsz * t
    L = min(MLSTM_CHUNK, t)
    qkvo, u, gates = _proj_in(x.reshape(n, D_MODEL), p["g_mix"], p["w_main"], p["wg"], p["bg"], L)

    ctt = min(CONV_TT, t)
    cbb = max(1, min(bsz, CONV_TT // ctt))
    state_pad = jnp.pad(conv_state, ((0, 0), (CONV_PAD - CONV_STATE, 0), (0, 0)))
    cv, new_conv = _conv(u.reshape(bsz, t, CONV_CH), state_pad, p["conv_w"], p["conv_b"],
                         p["ln_g"], p["ln_b"], cbb, ctt)

    mbb = max(1, min(bsz, LANES // L)) if L < LANES else 1
    m0b = jnp.broadcast_to(m0[:, :, None], (bsz, M_HEADS, LANES))
    hm, c1, n1, m1 = _mlstm(qkvo.reshape(bsz, t, -1), gates, c0, n0, m0b, p["mh_g"], mbb, L)

    x1 = _proj_out(x.reshape(n, D_MODEL), hm.reshape(n, M_WIDTH), cv.reshape(n, CONV_CH), p["wo_a"], p["wo_b"])

    abb, att = _group_tiles(bsz, t)
    x2, hn_all, te, tg, tr, cnt = _attn_router(x1.reshape(bsz, t, D_MODEL), mem_k, mem_v, cnt0, hn_all, row0,
                                               p, abb, att)
    return x2.reshape(n, D_MODEL), hn_all, te, tg, tr, cnt, (c1, n1, m1[:, :, 0], new_conv)


def kernel(x_prompt, x_sample, mem_prompt, state_mlstm_C, state_mlstm_n, state_mlstm_m, state_conv, cache_mem_k, cache_mem_v, norm_mix_g, w_in, b_gates, mh_norm_g, conv_w, conv_b, conv_ln_g, conv_ln_b, w_out, norm_ca_g, norm_mem_g, ca_wq, ca_wk, ca_wv, ca_wo, norm_moe_g, router_w, router_b, moe_w1, moe_b1, moe_w2, moe_b2, final_norm_g):
    assert w_in.shape[0] == 1, "single layer"
    bp, tp, _ = x_prompt.shape
    bs, ts, _ = x_sample.shape
    n_p, n_s = bp * tp, bs * ts
    n_all = n_p + n_s

    wi = w_in[0]
    o0 = 0
    parts = []
    for width in (QK_W, QK_W, M_WIDTH, M_WIDTH, M_HEADS, M_HEADS, CONV_CH, CONV_CH):
        parts.append(wi[:, o0:o0 + width])
        o0 += width
    w_q, w_k, w_v, w_o, w_gi, w_gf, w_ua, w_ug = parts
    glu_cols = []
    for c in range(CONV_CH // GLU_HALF):
        glu_cols += [w_ua[:, c * GLU_HALF:(c + 1) * GLU_HALF], w_ug[:, c * GLU_HALF:(c + 1) * GLU_HALF]]
    p = {
        "g_mix": norm_mix_g[0][None, :],
        "w_main": jnp.concatenate([w_q, w_k, w_v, w_o] + glu_cols, axis=1).astype(BF16),
        "wg": jnp.concatenate([w_gi, w_gf], axis=1).T.astype(BF16),
        "bg": b_gates[0][:, None],
        "mh_g": mh_norm_g[0],
        "conv_w": conv_w[0],
        "conv_b": conv_b[0][None, :],
        "ln_g": conv_ln_g[0][None, :],
        "ln_b": conv_ln_b[0][None, :],
        "wo_a": w_out[0][:M_WIDTH].astype(BF16),
        "wo_b": w_out[0][M_WIDTH:].astype(BF16),
        "g_ca": norm_ca_g[0][None, :],
        "wq": ca_wq[0].astype(BF16),
        "wo": ca_wo[0].astype(BF16),
        "g_moe": norm_moe_g[0][None, :],
        "router_w": jnp.pad(router_w[0], ((0, 0), (0, LANES - N_EXPERTS))).astype(BF16),
        "router_b": jnp.concatenate([router_b[0], jnp.full((LANES - N_EXPERTS,), NEG_BIG, F32)])[None, :],
    }

    wkv = jnp.concatenate([ca_wk[0], ca_wv[0]], axis=1).astype(BF16)
    mk, mv = _mem_kv(mem_prompt.reshape(bp * N_MEM, D_MODEL), norm_mem_g[0][None, :], wkv)
    mk = mk.reshape(bp, N_MEM, CA_W)
    mv = mv.reshape(bp, N_MEM, CA_W)
    hn_all = jnp.zeros((n_all, D_MODEL), BF16)
    cnt0 = jnp.zeros((1, LANES), F32)
    zc = jnp.zeros((bp, M_HEADS, M_DK, M_DV), F32)
    zn = jnp.zeros((bp, M_HEADS, M_DK), F32)
    zm = jnp.zeros((bp, M_HEADS), F32)
    zconv = jnp.zeros((bp, CONV_STATE, CONV_CH), F32)
    x2p, hn_all, te_p, tg_p, tr_p, cnt1, st_p = _layer_group(
        x_prompt, zc, zn, zm, zconv, mk.astype(BF16), mv.astype(BF16), cnt0, hn_all, 0, p)

    x2s, hn_all, te_s, tg_s, tr_s, cnt2, st_s = _layer_group(
        x_sample, state_mlstm_C[0], state_mlstm_n[0], state_mlstm_m[0], state_conv[0],
        cache_mem_k[0].reshape(bs, N_MEM, CA_W).astype(BF16), cache_mem_v[0].reshape(bs, N_MEM, CA_W).astype(BF16),
        cnt1, hn_all, n_p, p)

    te = jnp.concatenate([te_p[:, :TOP_K], te_s[:, :TOP_K]], axis=0)
    tr = jnp.concatenate([tr_p[:, :TOP_K], tr_s[:, :TOP_K]], axis=0)
    counts = cnt2[0, :N_EXPERTS].astype(jnp.int32)
    tiles_per_e = (counts + MOE_TM - 1) // MOE_TM
    tile_end = jnp.cumsum(tiles_per_e)
    row_start = (tile_end - tiles_per_e) * MOE_TM
    n_tiles = -(-(n_all * TOP_K) // MOE_TM) + N_EXPERTS
    n_used = tile_end[-1:]
    tile_e = jnp.minimum(jnp.searchsorted(tile_end, jnp.arange(n_tiles, dtype=jnp.int32), side="right"),
                         N_EXPERTS - 1).astype(jnp.int32)
    tile_e = jnp.where(jnp.arange(n_tiles) < n_used[0], tile_e, tile_e[jnp.maximum(n_used[0] - 1, 0)])
    dest = row_start[te] + tr
    tok_ids = jnp.broadcast_to(jnp.arange(n_all, dtype=jnp.int32)[:, None], (n_all, TOP_K))
    row_tok = jnp.zeros((n_tiles * MOE_TM,), jnp.int32).at[dest.reshape(-1)].set(tok_ids.reshape(-1))
    x_sorted = jnp.take(hn_all, row_tok, axis=0)

    w1 = moe_w1[0].astype(BF16)
    w2 = moe_w2[0].astype(BF16)
    y_sorted = _moe(tile_e, n_used.astype(jnp.int32), x_sorted, w1, moe_b1[0][:, None, :], w2,
                    moe_b2[0][:, None, :])

    yg = jnp.take(y_sorted, dest.T.reshape(-1), axis=0).reshape(TOP_K, n_all, D_MODEL)
    fg = final_norm_g[None, :]
    y_prompt = _combine(x2p, yg, tg_p, fg, 0).reshape(bp, tp, D_MODEL)
    y_sample = _combine(x2s, yg, tg_s, fg, n_p).reshape(bs, ts, D_MODEL)

    c1, n1, m1, conv1 = st_p
    c2, n2, m2, conv2 = st_s
    mk4 = mk.reshape(1, bp, N_MEM, CA_HEADS, CA_DH)
    mv4 = mv.reshape(1, bp, N_MEM, CA_HEADS, CA_DH)
    return (y_prompt, y_sample, c1[None], n1[None], m1[None], conv1[None], mk4, mv4,
            c2[None], n2[None], m2[None], conv2[None])
```

```python
import functools

import jax
import jax.numpy as jnp
from jax import lax
from jax.experimental import pallas as pl
from jax.experimental.pallas import tpu as pltpu

F32 = jnp.float32
BF16 = jnp.bfloat16
HIGHEST = lax.Precision.HIGHEST

D_MODEL = 2048
M_HEADS = 4
M_DV = 256
M_DK = 128
QK_W = M_HEADS * M_DK
M_WIDTH = M_HEADS * M_DV
CONV_CH = 1024
CONV_WIDTH = 31
CONV_STATE = CONV_WIDTH - 1
CONV_PAD = 32
N_MEM = 256
CA_HEADS = 4
CA_DH = 128
CA_W = CA_HEADS * CA_DH
N_EXPERTS = 32
TOP_K = 4
D_FF = 2048
SWIGLU_ALPHA = 1.702
SWIGLU_LIMIT = 7.0
EPS = 1e-6
LANES = 128
SUBLANES = 8
NEG_BIG = -1e30

PROJ_TM = 1024
PROJ_TN = 1024
GLU_HALF = PROJ_TN // 2
TOK_TILE = 512
MOE_TM = 512
MOE_TF = 512
MOE_TN = 1024
MLSTM_CHUNK = 256
CONV_TT = 256
CONV_RB = 32
CONV_LC = 512


def _cparams(sem, vmem_mb):
    return pltpu.CompilerParams(dimension_semantics=sem, vmem_limit_bytes=vmem_mb << 20)


def _log_sigmoid(x):
    return jnp.minimum(x, 0.0) - jnp.log1p(jnp.exp(-jnp.abs(x)))


def _sigmoid(x):
    return 1.0 / (1.0 + jnp.exp(-x))


def _rms(x, g):
    ms = jnp.mean(x * x, axis=-1, keepdims=True)
    return x * lax.rsqrt(ms + EPS) * g


def _proj_in_kernel(x_ref, g_ref, w_ref, wg_ref, bg_ref, qkvo_ref, u_ref, gates_ref, h_ref, *, chunk):
    j = pl.program_id(1)
    tm = x_ref.shape[0]

    @pl.when(j == 0)
    def _():
        h = _rms(x_ref[...], g_ref[...]).astype(BF16)
        h_ref[...] = h
        gt = lax.dot_general(wg_ref[...], h, (((1,), (1,)), ((), ())), preferred_element_type=F32)
        gt = gt + bg_ref[...]
        row = lax.broadcasted_iota(jnp.int32, gt.shape, 0)
        gt = jnp.where(row < M_HEADS, gt, _log_sigmoid(gt))
        for c in range(tm // chunk):
            gates_ref[c] = gt[:, c * chunk:(c + 1) * chunk]

    z = jnp.dot(h_ref[...], w_ref[...], preferred_element_type=F32)

    @pl.when(j < 3)
    def _():
        qkvo_ref[...] = z.astype(BF16)

    @pl.when(j >= 3)
    def _():
        u_ref[...] = (z[:, :GLU_HALF] * _sigmoid(z[:, GLU_HALF:])).astype(BF16)


def _proj_in(x, g, w_main, wg, bg, chunk):
    n = x.shape[0]
    tm = min(PROJ_TM, n)
    nj = w_main.shape[1] // PROJ_TN
    return pl.pallas_call(
        functools.partial(_proj_in_kernel, chunk=chunk),
        grid=(n // tm, nj),
        in_specs=[
            pl.BlockSpec((tm, D_MODEL), lambda i, j: (i, 0)),
            pl.BlockSpec((1, D_MODEL), lambda i, j: (0, 0)),
            pl.BlockSpec((D_MODEL, PROJ_TN), lambda i, j: (0, j)),
            pl.BlockSpec((2 * M_HEADS, D_MODEL), lambda i, j: (0, 0)),
            pl.BlockSpec((2 * M_HEADS, 1), lambda i, j: (0, 0)),
        ],
        out_specs=[
            pl.BlockSpec((tm, PROJ_TN), lambda i, j: (i, jnp.minimum(j, 2))),
            pl.BlockSpec((tm, GLU_HALF), lambda i, j: (i, jnp.maximum(j - 3, 0))),
            pl.BlockSpec((tm // chunk, 2 * M_HEADS, chunk), lambda i, j: (i, 0, 0)),
        ],
        out_shape=[
            jax.ShapeDtypeStruct((n, 2 * QK_W + 2 * M_WIDTH), BF16),
            jax.ShapeDtypeStruct((n, CONV_CH), BF16),
            jax.ShapeDtypeStruct((n // chunk, 2 * M_HEADS, chunk), F32),
        ],
        scratch_shapes=[pltpu.VMEM((tm, D_MODEL), BF16)],
        compiler_params=_cparams(("parallel", "arbitrary"), 48),
        name="proj_in",
    )(x, g, w_main, wg, bg)


def _conv_kernel(u_ref, st_ref, w_ref, cb_ref, lg_ref, lb_ref, cv_ref, ns_ref, e_ref, er_ref, c_ref, *, bb, tt):
    t = pl.program_id(1)
    nt = pl.num_programs(1)
    ext = tt + CONV_PAD

    @pl.when(t == 0)
    def _():
        e_ref[:, 0:CONV_PAD, :] = st_ref[...]

    @pl.when(t > 0)
    def _():
        e_ref[:, 0:CONV_PAD, :] = e_ref[:, tt:ext, :]

    e_ref[:, CONV_PAD:ext, :] = u_ref[...].astype(F32)
    e_ref[:, ext:, :] = jnp.zeros((bb, SUBLANES, CONV_CH), F32)

    base = CONV_PAD - CONV_STATE
    for b in range(bb):
        for r in range(1, SUBLANES):
            er_ref[r - 1] = e_ref[b, r:r + ext, :]

        def row_block(rb, carry):
            r0 = pl.multiple_of(rb * CONV_RB, CONV_RB)
            for lc in range(CONV_CH // CONV_LC):
                ls = slice(lc * CONV_LC, (lc + 1) * CONV_LC)
                acc = jnp.zeros((CONV_RB, CONV_LC), F32) + cb_ref[:, ls]
                for j in range(CONV_WIDTH):
                    off = j + base
                    a, r = (off // SUBLANES) * SUBLANES, off % SUBLANES
                    if r == 0:
                        tap = e_ref[b, pl.ds(r0 + a, CONV_RB), ls]
                    else:
                        tap = er_ref[r - 1, pl.ds(r0 + a, CONV_RB), ls]
                    acc = acc + w_ref[j:j + 1, ls] * tap
                c_ref[:, ls] = acc
            c = c_ref[...]
            mu = jnp.mean(c, axis=-1, keepdims=True)
            cc = c - mu
            var = jnp.mean(cc * cc, axis=-1, keepdims=True)
            cn = cc * lax.rsqrt(var + EPS) * lg_ref[...] + lb_ref[...]
            cv_ref[b, pl.ds(r0, CONV_RB), :] = (cn * _sigmoid(cn)).astype(BF16)
            return carry

        lax.fori_loop(0, tt // CONV_RB, row_block, 0)

    @pl.when(t == nt - 1)
    def _():
        ns_ref[...] = e_ref[:, ext - CONV_STATE:ext, :]


def _conv(u, state_pad, conv_w, conv_b, ln_g, ln_b, bb, tt):
    bsz, t, _ = u.shape
    return pl.pallas_call(
        functools.partial(_conv_kernel, bb=bb, tt=tt),
        grid=(bsz // bb, t // tt),
        in_specs=[
            pl.BlockSpec((bb, tt, CONV_CH), lambda b, s: (b, s, 0)),
            pl.BlockSpec((bb, CONV_PAD, CONV_CH), lambda b, s: (b, 0, 0)),
            pl.BlockSpec((CONV_WIDTH, CONV_CH), lambda b, s: (0, 0)),
            pl.BlockSpec((1, CONV_CH), lambda b, s: (0, 0)),
            pl.BlockSpec((1, CONV_CH), lambda b, s: (0, 0)),
            pl.BlockSpec((1, CONV_CH), lambda b, s: (0, 0)),
        ],
        out_specs=[
            pl.BlockSpec((bb, tt, CONV_CH), lambda b, s: (b, s, 0)),
            pl.BlockSpec((bb, CONV_STATE, CONV_CH), lambda b, s: (b, 0, 0)),
        ],
        out_shape=[
            jax.ShapeDtypeStruct((bsz, t, CONV_CH), BF16),
            jax.ShapeDtypeStruct((bsz, CONV_STATE, CONV_CH), F32),
        ],
        scratch_shapes=[
            pltpu.VMEM((bb, tt + CONV_PAD + SUBLANES, CONV_CH), F32),
            pltpu.VMEM((SUBLANES - 1, tt + CONV_PAD, CONV_CH), F32),
            pltpu.VMEM((CONV_RB, CONV_CH), F32),
        ],
        compiler_params=_cparams(("parallel", "arbitrary"), 40),
        name="conv",
    )(u, state_pad, conv_w, conv_b, ln_g, ln_b)


def _mlstm_kernel(q_ref, k_ref, v_ref, o_ref, gt_ref, c0_ref, n0_ref, m0_ref, ng_ref,
                  hm_ref, cn_ref, nn_ref, mn_ref, c_s, n_s, m_s, *, bb, L):
    ci = pl.program_id(1)
    nc = pl.num_programs(1)

    @pl.when(ci == 0)
    def _():
        c_s[...] = c0_ref[...]
        n_s[...] = n0_ref[...]
        m_s[...] = m0_ref[...]

    rt = lax.broadcasted_iota(jnp.int32, (L, L), 0)
    cs = lax.broadcasted_iota(jnp.int32, (L, L), 1)
    causal = cs <= rt
    lower = causal.astype(F32)
    upper = (rt <= cs).astype(F32)
    eye = (rt == cs).astype(F32)
    nt_dims = (((1,), (1,)), ((), ()))
    scale = M_DK ** -0.5

    for b in range(bb):
        rows = gt_ref[b]
        cum_rows = jnp.dot(rows, upper, precision=HIGHEST, preferred_element_type=F32)
        cum_cols = lax.dot_general(lower, rows, nt_dims, precision=HIGHEST, preferred_element_type=F32)
        id_cols = lax.dot_general(eye, rows, nt_dims, precision=HIGHEST, preferred_element_type=F32)
        for h in range(M_HEADS):
            ig_row = rows[h:h + 1, :]
            ig_col = id_cols[:, h:h + 1]
            b_row = cum_rows[M_HEADS + h:M_HEADS + h + 1, :]
            b_col = cum_cols[:, M_HEADS + h:M_HEADS + h + 1]
            m0 = m_s[b, h:h + 1, 0:1]
            q = q_ref[b, :, h * M_DK:(h + 1) * M_DK]
            k = k_ref[b, :, h * M_DK:(h + 1) * M_DK]
            v = v_ref[b, :, h * M_DV:(h + 1) * M_DV]
            c0 = c_s[b, h]
            n0 = n_s[b, h:h + 1, :]

            g_col = b_col + m0
            dmat = jnp.where(causal, b_col - b_row + ig_row, -jnp.inf)
            m_col = jnp.maximum(g_col, jnp.max(dmat, axis=1, keepdims=True))
            w_state = jnp.exp(g_col - m_col) * scale
            qk = lax.dot_general(q, k, nt_dims, preferred_element_type=F32)
            s = qk * (jnp.exp(dmat - m_col) * scale)
            num = jnp.dot(s.astype(BF16), v, preferred_element_type=F32)
            num = num + w_state * jnp.dot(q, c0.astype(BF16), preferred_element_type=F32)
            qn = jnp.sum(q.astype(F32) * n0, axis=1, keepdims=True)
            den = jnp.sum(s, axis=1, keepdims=True) + w_state * qn
            hh = num / jnp.maximum(jnp.abs(den), jnp.exp(-m_col))

            hh = hh * lax.rsqrt(jnp.mean(hh * hh, axis=-1, keepdims=True) + EPS) * ng_ref[h:h + 1, :]
            og = _sigmoid(o_ref[b, :, h * M_DV:(h + 1) * M_DV].astype(F32))
            hm_ref[b, :, h * M_DV:(h + 1) * M_DV] = (og * hh).astype(BF16)

            b_last = b_row[:, L - 1:L]
            g_last = b_last + m0
            wk_row = b_last - b_row + ig_row
            m_new = jnp.maximum(g_last, jnp.max(wk_row, axis=1, keepdims=True))
            a0 = jnp.exp(g_last - m_new)
            ak_col = jnp.exp(b_last - b_col + ig_col - m_new)
            ks = k.astype(F32) * ak_col
            kv = lax.dot_general(ks.astype(BF16), v, (((0,), (0,)), ((), ())), preferred_element_type=F32)
            c_s[b, h] = a0 * c0 + kv
            n_s[b, h:h + 1, :] = a0 * n0 + jnp.sum(ks, axis=0, keepdims=True)
            m_s[b, h:h + 1, :] = jnp.broadcast_to(m_new, (1, LANES))

    @pl.when(ci == nc - 1)
    def _():
        cn_ref[...] = c_s[...]
        nn_ref[...] = n_s[...]
        mn_ref[...] = m_s[...]


def _mlstm(qkvo, gates, c0, n0, m0, ng, bb, L):
    bsz, t, _ = qkvo.shape
    nc = t // L
    gates = gates.reshape(bsz, nc, 2 * M_HEADS, L)
    st_c = pl.BlockSpec((bb, M_HEADS, M_DK, M_DV), lambda b, c: (b, 0, 0, 0))
    st_n = pl.BlockSpec((bb, M_HEADS, M_DK), lambda b, c: (b, 0, 0))
    return pl.pallas_call(
        functools.partial(_mlstm_kernel, bb=bb, L=L),
        grid=(bsz // bb, nc),
        in_specs=[
            pl.BlockSpec((bb, L, QK_W), lambda b, c: (b, c, 0)),
            pl.BlockSpec((bb, L, QK_W), lambda b, c: (b, c, 1)),
            pl.BlockSpec((bb, L, M_WIDTH), lambda b, c: (b, c, 1)),
            pl.BlockSpec((bb, L, M_WIDTH), lambda b, c: (b, c, 2)),
            pl.BlockSpec((bb, None, 2 * M_HEADS, L), lambda b, c: (b, c, 0, 0)),
            st_c, st_n, st_n,
            pl.BlockSpec((M_HEADS, M_DV), lambda b, c: (0, 0)),
        ],
        out_specs=[
            pl.BlockSpec((bb, L, M_WIDTH), lambda b, c: (b, c, 0)),
            st_c, st_n, st_n,
        ],
        out_shape=[
            jax.ShapeDtypeStruct((bsz, t, M_WIDTH), BF16),
            jax.ShapeDtypeStruct((bsz, M_HEADS, M_DK, M_DV), F32),
            jax.ShapeDtypeStruct((bsz, M_HEADS, M_DK), F32),
            jax.ShapeDtypeStruct((bsz, M_HEADS, LANES), F32),
        ],
        scratch_shapes=[
            pltpu.VMEM((bb, M_HEADS, M_DK, M_DV), F32),
            pltpu.VMEM((bb, M_HEADS, M_DK), F32),
            pltpu.VMEM((bb, M_HEADS, LANES), F32),
        ],
        compiler_params=_cparams(("parallel", "arbitrary"), 40),
        name="mlstm",
    )(qkvo, qkvo, qkvo, qkvo, gates, c0, n0, m0, ng)


def _proj_out_kernel(x_ref, hm_ref, cv_ref, wa_ref, wb_ref, o_ref):
    acc = jnp.dot(hm_ref[...], wa_ref[...], preferred_element_type=F32)
    acc = acc + jnp.dot(cv_ref[...], wb_ref[...], preferred_element_type=F32)
    o_ref[...] = x_ref[...] + acc


def _proj_out(x, hm, cv, wa, wb):
    n = x.shape[0]
    tm = min(TOK_TILE, n)
    return pl.pallas_call(
        _proj_out_kernel,
        grid=(n // tm,),
        in_specs=[
            pl.BlockSpec((tm, D_MODEL), lambda i: (i, 0)),
            pl.BlockSpec((tm, M_WIDTH), lambda i: (i, 0)),
            pl.BlockSpec((tm, CONV_CH), lambda i: (i, 0)),
            pl.BlockSpec((M_WIDTH, D_MODEL), lambda i: (0, 0)),
            pl.BlockSpec((CONV_CH, D_MODEL), lambda i: (0, 0)),
        ],
        out_specs=pl.BlockSpec((tm, D_MODEL), lambda i: (i, 0)),
        out_shape=jax.ShapeDtypeStruct((n, D_MODEL), F32),
        compiler_params=_cparams(("parallel",), 48),
        name="proj_out",
    )(x, hm, cv, wa, wb)


def _mem_kv_kernel(m_ref, g_ref, w_ref, k_ref, v_ref):
    mn = _rms(m_ref[...], g_ref[...]).astype(BF16)
    z = jnp.dot(mn, w_ref[...], preferred_element_type=F32)
    k_ref[...] = z[:, :CA_W]
    v_ref[...] = z[:, CA_W:]


def _mem_kv(mem, g, wkv):
    n = mem.shape[0]
    return pl.pallas_call(
        _mem_kv_kernel,
        grid=(1,),
        in_specs=[
            pl.BlockSpec((n, D_MODEL), lambda i: (0, 0)),
            pl.BlockSpec((1, D_MODEL), lambda i: (0, 0)),
            pl.BlockSpec((D_MODEL, 2 * CA_W), lambda i: (0, 0)),
        ],
        out_specs=[pl.BlockSpec((n, CA_W), lambda i: (0, 0))] * 2,
        out_shape=[jax.ShapeDtypeStruct((n, CA_W), F32)] * 2,
        compiler_params=_cparams(("arbitrary",), 40),
        name="mem_kv",
    )(mem, g, wkv)


def _attn_router_kernel(x_ref, gca_ref, wq_ref, k_ref, v_ref, wo_ref, gmoe_ref, rw_ref, rb_ref, cnt0_ref,
                        hn_in_ref, x2_ref, hn_ref, te_ref, tg_ref, tr_ref, cnt_ref, o_s, cnt_s, *, bb, tt):
    del hn_in_ref
    rows = bb * tt

    @pl.when((pl.program_id(0) == 0) & (pl.program_id(1) == 0))
    def _():
        cnt_s[...] = cnt0_ref[...]

    x = x_ref[...].reshape(rows, D_MODEL)
    h = _rms(x, gca_ref[...]).astype(BF16)
    q = jnp.dot(h, wq_ref[...], preferred_element_type=F32) * (CA_DH ** -0.5)
    q = q.astype(BF16)
    for b in range(bb):
        for hd in range(CA_HEADS):
            cols = slice(hd * CA_DH, (hd + 1) * CA_DH)
            qb = q[b * tt:(b + 1) * tt, cols]
            s = lax.dot_general(qb, k_ref[b, :, cols], (((1,), (1,)), ((), ())), preferred_element_type=F32)
            e = jnp.exp(s - jnp.max(s, axis=-1, keepdims=True))
            p = (e / jnp.sum(e, axis=-1, keepdims=True)).astype(BF16)
            o_s[b * tt:(b + 1) * tt, cols] = jnp.dot(p, v_ref[b, :, cols], preferred_element_type=F32).astype(BF16)
    x2 = x + jnp.dot(o_s[...], wo_ref[...], preferred_element_type=F32)
    x2_ref[...] = x2.reshape(bb, tt, D_MODEL)

    hn = _rms(x2, gmoe_ref[...])
    hn_bf = hn.astype(BF16)
    hn_ref[...] = hn_bf
    logits = jnp.dot(hn_bf, rw_ref[...], preferred_element_type=F32) + rb_ref[...]
    lane = lax.broadcasted_iota(jnp.int32, (rows, LANES), 1)
    work = logits
    sel = []
    vals = []
    for _ in range(TOP_K):
        mx = jnp.max(work, axis=-1, keepdims=True)
        idx = jnp.min(jnp.where(work == mx, lane, LANES), axis=-1, keepdims=True)
        hit = lane == idx
        sel.append((idx, hit))
        vals.append(mx)
        work = jnp.where(hit, -jnp.inf, work)
    ex = [jnp.exp(vv - vals[0]) for vv in vals]
    tot = ex[0] + ex[1] + ex[2] + ex[3]
    assigned = jnp.zeros((rows, LANES), F32)
    for _, hit in sel:
        assigned = assigned + hit.astype(F32)
    rt = lax.broadcasted_iota(jnp.int32, (rows, rows), 0)
    cs = lax.broadcasted_iota(jnp.int32, (rows, rows), 1)
    strict = (cs < rt).astype(BF16)
    before = jnp.dot(strict, assigned.astype(BF16), preferred_element_type=F32) + cnt_s[...]
    te = jnp.zeros((rows, LANES), jnp.int32)
    tg = jnp.zeros((rows, LANES), F32)
    tr = jnp.zeros((rows, LANES), F32)
    for kk, (idx, hit) in enumerate(sel):
        rank = jnp.sum(jnp.where(hit, before, 0.0), axis=-1, keepdims=True)
        te = jnp.where(lane == kk, idx, te)
        tg = jnp.where(lane == kk, ex[kk] / tot, tg)
        tr = jnp.where(lane == kk, rank, tr)
    te_ref[...] = te
    tg_ref[...] = tg
    tr_ref[...] = tr.astype(jnp.int32)
    cnt_s[...] = cnt_s[...] + jnp.sum(assigned, axis=0, keepdims=True)
    cnt_ref[...] = cnt_s[...]


def _attn_router(x1, mem_k, mem_v, cnt0, hn_all, row0, p, bb, tt):
    bsz, t, _ = x1.shape
    rows = bb * tt
    n = bsz * t
    nb, nt = bsz // bb, t // tt
    blk0 = row0 // rows
    tok = lambda b, s: (b * nt + s, 0)
    const = lambda b, s: (0, 0)
    grid_spec = dict(
        grid=(nb, nt),
        in_specs=[
            pl.BlockSpec((bb, tt, D_MODEL), lambda b, s: (b, s, 0)),
            pl.BlockSpec((1, D_MODEL), const),
            pl.BlockSpec((D_MODEL, CA_W), const),
            pl.BlockSpec((bb, N_MEM, CA_W), lambda b, s: (b, 0, 0)),
            pl.BlockSpec((bb, N_MEM, CA_W), lambda b, s: (b, 0, 0)),
            pl.BlockSpec((CA_W, D_MODEL), const),
            pl.BlockSpec((1, D_MODEL), const),
            pl.BlockSpec((D_MODEL, LANES), const),
            pl.BlockSpec((1, LANES), const),
            pl.BlockSpec((1, LANES), const),
            pl.BlockSpec(memory_space=pl.ANY),
        ],
        out_specs=[
            pl.BlockSpec((bb, tt, D_MODEL), lambda b, s: (b, s, 0)),
            pl.BlockSpec((rows, D_MODEL), lambda b, s: (blk0 + b * nt + s, 0)),
            pl.BlockSpec((rows, LANES), tok),
            pl.BlockSpec((rows, LANES), tok),
            pl.BlockSpec((rows, LANES), tok),
            pl.BlockSpec((1, LANES), const),
        ],
    )
    return pl.pallas_call(
        functools.partial(_attn_router_kernel, bb=bb, tt=tt),
        out_shape=[
            jax.ShapeDtypeStruct((bsz, t, D_MODEL), F32),
            jax.ShapeDtypeStruct(hn_all.shape, BF16),
            jax.ShapeDtypeStruct((n, LANES), jnp.int32),
            jax.ShapeDtypeStruct((n, LANES), F32),
            jax.ShapeDtypeStruct((n, LANES), jnp.int32),
            jax.ShapeDtypeStruct((1, LANES), F32),
        ],
        scratch_shapes=[pltpu.VMEM((rows, CA_W), BF16), pltpu.VMEM((1, LANES), F32)],
        input_output_aliases={10: 1},
        compiler_params=_cparams(("arbitrary", "arbitrary"), 48),
        name="attn_router",
        **grid_spec,
    )(x1, p["g_ca"], p["wq"], mem_k, mem_v, p["wo"], p["g_moe"], p["router_w"], p["router_b"], cnt0, hn_all)


def _expert_changed(te_ref):
    t = pl.program_id(1)
    return (t == 0) | (te_ref[t] != te_ref[jnp.maximum(t - 1, 0)])


def _moe_up_kernel(te_ref, nu_ref, x_ref, w1g_ref, w1l_ref, b1g_ref, b1l_ref, act_ref, wg_s, wl_s):
    @pl.when(_expert_changed(te_ref))
    def _():
        wg_s[...] = w1g_ref[0].astype(BF16)
        wl_s[...] = w1l_ref[0].astype(BF16)

    @pl.when(pl.program_id(1) < nu_ref[0])
    def _():
        x = x_ref[...]
        glu = jnp.dot(x, wg_s[...], preferred_element_type=F32) + b1g_ref[0]
        lin = jnp.dot(x, wl_s[...], preferred_element_type=F32) + b1l_ref[0]
        glu = jnp.minimum(glu, SWIGLU_LIMIT)
        lin = jnp.clip(lin, -SWIGLU_LIMIT, SWIGLU_LIMIT)
        act_ref[...] = (glu * _sigmoid(SWIGLU_ALPHA * glu) * (lin + 1.0)).astype(BF16)


def _moe_down_kernel(te_ref, nu_ref, a_ref, w2_ref, b2_ref, y_ref, w2_s):
    @pl.when(_expert_changed(te_ref))
    def _():
        w2_s[...] = w2_ref[0].astype(BF16)

    @pl.when(pl.program_id(1) < nu_ref[0])
    def _():
        y = jnp.dot(a_ref[...], w2_s[...], preferred_element_type=F32) + b2_ref[0]
        y_ref[...] = y.astype(BF16)


def _moe(tile_e, n_used, x_sorted, w1, b1, w2, b2):
    r = x_sorted.shape[0]
    n_tiles = r // MOE_TM
    nf = D_FF // MOE_TF
    nn = D_MODEL // MOE_TN

    def row(c, t, te, nu):
        return jnp.minimum(t, nu[0] - 1)

    act = pl.pallas_call(
        _moe_up_kernel,
        grid_spec=pltpu.PrefetchScalarGridSpec(
            num_scalar_prefetch=2,
            grid=(nf, n_tiles),
            in_specs=[
                pl.BlockSpec((MOE_TM, D_MODEL), lambda c, t, te, nu: (row(c, t, te, nu), 0)),
                pl.BlockSpec((1, D_MODEL, MOE_TF), lambda c, t, te, nu: (te[t], 0, c)),
                pl.BlockSpec((1, D_MODEL, MOE_TF), lambda c, t, te, nu: (te[t], 0, nf + c)),
                pl.BlockSpec((1, 1, MOE_TF), lambda c, t, te, nu: (te[t], 0, c)),
                pl.BlockSpec((1, 1, MOE_TF), lambda c, t, te, nu: (te[t], 0, nf + c)),
            ],
            out_specs=pl.BlockSpec((MOE_TM, MOE_TF), lambda c, t, te, nu: (row(c, t, te, nu), c)),
            scratch_shapes=[pltpu.VMEM((D_MODEL, MOE_TF), BF16), pltpu.VMEM((D_MODEL, MOE_TF), BF16)],
        ),
        out_shape=jax.ShapeDtypeStruct((r, D_FF), BF16),
        compiler_params=_cparams(("arbitrary", "arbitrary"), 48),
        name="moe_up",
    )(tile_e, n_used, x_sorted, w1, w1, b1, b1)

    return pl.pallas_call(
        _moe_down_kernel,
        grid_spec=pltpu.PrefetchScalarGridSpec(
            num_scalar_prefetch=2,
            grid=(nn, n_tiles),
            in_specs=[
                pl.BlockSpec((MOE_TM, D_FF), lambda c, t, te, nu: (row(c, t, te, nu), 0)),
                pl.BlockSpec((1, D_FF, MOE_TN), lambda c, t, te, nu: (te[t], 0, c)),
                pl.BlockSpec((1, 1, MOE_TN), lambda c, t, te, nu: (te[t], 0, c)),
            ],
            out_specs=pl.BlockSpec((MOE_TM, MOE_TN), lambda c, t, te, nu: (row(c, t, te, nu), c)),
            scratch_shapes=[pltpu.VMEM((D_FF, MOE_TN), BF16)],
        ),
        out_shape=jax.ShapeDtypeStruct((r, D_MODEL), BF16),
        compiler_params=_cparams(("arbitrary", "arbitrary"), 48),
        name="moe_down",
    )(tile_e, n_used, act, w2, b2)


def _combine_kernel(x_ref, y0_ref, y1_ref, y2_ref, y3_ref, tg_ref, g_ref, o_ref):
    tg = tg_ref[...]
    acc = x_ref[...]
    for kk, y_ref in enumerate((y0_ref, y1_ref, y2_ref, y3_ref)):
        acc = acc + y_ref[0].astype(F32) * tg[:, kk:kk + 1]
    o_ref[...] = _rms(acc, g_ref[...])


def _combine(x2, yg, tg, g, row0):
    n = x2.shape[0]
    tm = min(TOK_TILE, n)
    blk0 = row0 // tm
    yspec = [pl.BlockSpec((1, tm, D_MODEL), functools.partial(lambda i, kk: (kk, blk0 + i, 0), kk=kk))
             for kk in range(TOP_K)]
    return pl.pallas_call(
        _combine_kernel,
        grid=(n // tm,),
        in_specs=[pl.BlockSpec((tm, D_MODEL), lambda i: (i, 0))] + yspec + [
            pl.BlockSpec((tm, LANES), lambda i: (i, 0)),
            pl.BlockSpec((1, D_MODEL), lambda i: (0, 0)),
        ],
        out_specs=pl.BlockSpec((tm, D_MODEL), lambda i: (i, 0)),
        out_shape=jax.ShapeDtypeStruct((n, D_MODEL), F32),
        compiler_params=_cparams(("parallel",), 48),
        name="combine",
    )(x2, yg, yg, yg, yg, tg, g)


def _group_tiles(bsz, t):
    tt = min(t, TOK_TILE)
    bb = max(1, min(bsz, TOK_TILE // tt))
    return bb, tt


def _layer_group(x, c0, n0, m0, conv_state, mem_k, mem_v, cnt0, hn_all, row0, p):
    bsz, t, _ = x.shape
    n = bsz * t
    L = min(MLSTM_CHUNK, t)
    qkvo, u, gates = _proj_in(x.reshape(n, D_MODEL), p["g_mix"], p["w_main"], p["wg"], p["bg"], L)

    ctt = min(CONV_TT, t)
    cbb = max(1, min(bsz, CONV_TT // ctt))
    state_pad = jnp.pad(conv_state, ((0, 0), (CONV_PAD - CONV_STATE, 0), (0, 0)))
    cv, new_conv = _conv(u.reshape(bsz, t, CONV_CH), state_pad, p["conv_w"], p["conv_b"],
                         p["ln_g"], p["ln_b"], cbb, ctt)

    mbb = max(1, min(bsz, LANES // L)) if L < LANES else 1
    m0b = jnp.broadcast_to(m0[:, :, None], (bsz, M_HEADS, LANES))
    hm, c1, n1, m1 = _mlstm(qkvo.reshape(bsz, t, -1), gates, c0, n0, m0b, p["mh_g"], mbb, L)

    x1 = _proj_out(x.reshape(n, D_MODEL), hm.reshape(n, M_WIDTH), cv.reshape(n, CONV_CH), p["wo_a"], p["wo_b"])

    abb, att = _group_tiles(bsz, t)
    x2, hn_all, te, tg, tr, cnt = _attn_router(x1.reshape(bsz, t, D_MODEL), mem_k, mem_v, cnt0, hn_all, row0,
                                               p, abb, att)
    return x2.reshape(n, D_MODEL), hn_all, te, tg, tr, cnt, (c1, n1, m1[:, :, 0], new_conv)


def kernel(x_prompt, x_sample, mem_prompt, state_mlstm_C, state_mlstm_n, state_mlstm_m, state_conv, cache_mem_k, cache_mem_v, norm_mix_g, w_in, b_gates, mh_norm_g, conv_w, conv_b, conv_ln_g, conv_ln_b, w_out, norm_ca_g, norm_mem_g, ca_wq, ca_wk, ca_wv, ca_wo, norm_moe_g, router_w, router_b, moe_w1, moe_b1, moe_w2, moe_b2, final_norm_g):
    assert w_in.shape[0] == 1, "single layer"
    bp, tp, _ = x_prompt.shape
    bs, ts, _ = x_sample.shape
    n_p, n_s = bp * tp, bs * ts
    n_all = n_p + n_s

    wi = w_in[0]
    o0 = 0
    parts = []
    for width in (QK_W, QK_W, M_WIDTH, M_WIDTH, M_HEADS, M_HEADS, CONV_CH, CONV_CH):
        parts.append(wi[:, o0:o0 + width])
        o0 += width
    w_q, w_k, w_v, w_o, w_gi, w_gf, w_ua, w_ug = parts
    glu_cols = []
    for c in range(CONV_CH // GLU_HALF):
        glu_cols += [w_ua[:, c * GLU_HALF:(c + 1) * GLU_HALF], w_ug[:, c * GLU_HALF:(c + 1) * GLU_HALF]]
    p = {
        "g_mix": norm_mix_g[0][None, :],
        "w_main": jnp.concatenate([w_q, w_k, w_v, w_o] + glu_cols, axis=1).astype(BF16),
        "wg": jnp.concatenate([w_gi, w_gf], axis=1).T.astype(BF16),
        "bg": b_gates[0][:, None],
        "mh_g": mh_norm_g[0],
        "conv_w": conv_w[0],
        "conv_b": conv_b[0][None, :],
        "ln_g": conv_ln_g[0][None, :],
        "ln_b": conv_ln_b[0][None, :],
        "wo_a": w_out[0][:M_WIDTH].astype(BF16),
        "wo_b": w_out[0][M_WIDTH:].astype(BF16),
        "g_ca": norm_ca_g[0][None, :],
        "wq": ca_wq[0].astype(BF16),
        "wo": ca_wo[0].astype(BF16),
        "g_moe": norm_moe_g[0][None, :],
        "router_w": jnp.pad(router_w[0], ((0, 0), (0, LANES - N_EXPERTS))).astype(BF16),
        "router_b": jnp.concatenate([router_b[0], jnp.full((LANES - N_EXPERTS,), NEG_BIG, F32)])[None, :],
    }

    wkv = jnp.concatenate([ca_wk[0], ca_wv[0]], axis=1).astype(BF16)
    mk, mv = _mem_kv(mem_prompt.reshape(bp * N_MEM, D_MODEL), norm_mem_g[0][None, :], wkv)
    mk = mk.reshape(bp, N_MEM, CA_W)
    mv = mv.reshape(bp, N_MEM, CA_W)
    hn_all = jnp.zeros((n_all, D_MODEL), BF16)
    cnt0 = jnp.zeros((1, LANES), F32)
    zc = jnp.zeros((bp, M_HEADS, M_DK, M_DV), F32)
    zn = jnp.zeros((bp, M_HEADS, M_DK), F32)
    zm = jnp.zeros((bp, M_HEADS), F32)
    zconv = jnp.zeros((bp, CONV_STATE, CONV_CH), F32)
    x2p, hn_all, te_p, tg_p, tr_p, cnt1, st_p = _layer_group(
        x_prompt, zc, zn, zm, zconv, mk.astype(BF16), mv.astype(BF16), cnt0, hn_all, 0, p)

    x2s, hn_all, te_s, tg_s, tr_s, cnt2, st_s = _layer_group(
        x_sample, state_mlstm_C[0], state_mlstm_n[0], state_mlstm_m[0], state_conv[0],
        cache_mem_k[0].reshape(bs, N_MEM, CA_W).astype(BF16), cache_mem_v[0].reshape(bs, N_MEM, CA_W).astype(BF16),
        cnt1, hn_all, n_p, p)

    te = jnp.concatenate([te_p[:, :TOP_K], te_s[:, :TOP_K]], axis=0)
    tr = jnp.concatenate([tr_p[:, :TOP_K], tr_s[:, :TOP_K]], axis=0)
    counts = cnt2[0, :N_EXPERTS].astype(jnp.int32)
    tiles_per_e = (counts + MOE_TM - 1) // MOE_TM
    tile_end = jnp.cumsum(tiles_per_e)
    row_start = (tile_end - tiles_per_e) * MOE_TM
    n_tiles = -(-(n_all * TOP_K) // MOE_TM) + N_EXPERTS
    n_used = tile_end[-1:]
    tile_ids = jnp.minimum(jnp.arange(n_tiles, dtype=jnp.int32), n_used[0] - 1)
    tile_e = jnp.minimum(jnp.sum(tile_ids[:, None] >= tile_end[None, :], axis=1), N_EXPERTS - 1).astype(jnp.int32)
    dest = row_start[te] + tr
    tok_ids = jnp.broadcast_to(jnp.arange(n_all, dtype=jnp.int32)[:, None], (n_all, TOP_K))
    row_tok = jnp.zeros((n_tiles * MOE_TM,), jnp.int32).at[dest.reshape(-1)].set(
        tok_ids.reshape(-1), unique_indices=True, mode="promise_in_bounds")
    x_sorted = hn_all.at[row_tok].get(mode="promise_in_bounds")

    y_sorted = _moe(tile_e, n_used.astype(jnp.int32), x_sorted, moe_w1[0], moe_b1[0][:, None, :], moe_w2[0],
                    moe_b2[0][:, None, :])

    yg = y_sorted.at[dest.T.reshape(-1)].get(mode="promise_in_bounds").reshape(TOP_K, n_all, D_MODEL)
    fg = final_norm_g[None, :]
    y_prompt = _combine(x2p, yg, tg_p, fg, 0).reshape(bp, tp, D_MODEL)
    y_sample = _combine(x2s, yg, tg_s, fg, n_p).reshape(bs, ts, D_MODEL)

    c1, n1, m1, conv1 = st_p
    c2, n2, m2, conv2 = st_s
    mk4 = mk.reshape(1, bp, N_MEM, CA_HEADS, CA_DH)
    mv4 = mv.reshape(1, bp, N_MEM, CA_HEADS, CA_DH)
    return (y_prompt, y_sample, c1[None], n1[None], m1[None], conv1[None], mk4, mv4,
            c2[None], n2[None], m2[None], conv2[None])
```

```python
import functools

import jax
import jax.numpy as jnp
from jax import lax
from jax.experimental import pallas as pl
from jax.experimental.pallas import tpu as pltpu

F32 = jnp.float32
BF16 = jnp.bfloat16
HIGHEST = lax.Precision.HIGHEST

D_MODEL = 2048
HALF_D = D_MODEL // 2
M_HEADS = 4
M_DV = 256
M_DK = 128
QK_W = M_HEADS * M_DK
M_WIDTH = M_HEADS * M_DV
CONV_CH = 1024
CONV_WIDTH = 31
CONV_STATE = CONV_WIDTH - 1
CONV_PAD = 32
N_MEM = 256
CA_HEADS = 4
CA_DH = 128
CA_W = CA_HEADS * CA_DH
N_EXPERTS = 32
TOP_K = 4
D_FF = 2048
SWIGLU_ALPHA = 1.702
SWIGLU_LIMIT = 7.0
EPS = 1e-6
LANES = 128
SUBLANES = 8
NEG_BIG = -1e30

PROJ_TM = 1024
PROJ_TN = 1024
GLU_HALF = PROJ_TN // 2
TOK_TILE = 512
MOE_TM = 512
MOE_TF = 512
MOE_TN = 1024
MLSTM_CHUNK = 256
CONV_TT = 256
CONV_RB = 32
CONV_LC = 512


def _cparams(sem, vmem_mb):
    return pltpu.CompilerParams(dimension_semantics=sem, vmem_limit_bytes=vmem_mb << 20)


def _log_sigmoid(x):
    return jnp.minimum(x, 0.0) - jnp.log1p(jnp.exp(-jnp.abs(x)))


def _sigmoid(x):
    return 1.0 / (1.0 + jnp.exp(-x))


def _pack_pairs(lo, hi):
    ulo = lax.bitcast_convert_type(lo.astype(BF16).astype(F32), jnp.uint32)
    uhi = lax.bitcast_convert_type(hi.astype(BF16).astype(F32), jnp.uint32)
    return (ulo >> 16) | uhi


def _unpack_pairs(w):
    lo = lax.bitcast_convert_type(w << 16, F32)
    hi = lax.bitcast_convert_type(w & jnp.uint32(0xFFFF0000), F32)
    return lo, hi


def _rms(x, g):
    ms = jnp.mean(x * x, axis=-1, keepdims=True)
    return x * lax.rsqrt(ms + EPS) * g


def _proj_in_kernel(x_ref, g_ref, w_ref, wg_ref, bg_ref, qkvo_ref, u_ref, gates_ref, h_ref, *, chunk):
    j = pl.program_id(1)
    tm = x_ref.shape[0]

    @pl.when(j == 0)
    def _():
        h = _rms(x_ref[...], g_ref[...]).astype(BF16)
        h_ref[...] = h
        gt = lax.dot_general(wg_ref[...], h, (((1,), (1,)), ((), ())), preferred_element_type=F32)
        gt = gt + bg_ref[...]
        row = lax.broadcasted_iota(jnp.int32, gt.shape, 0)
        gt = jnp.where(row < M_HEADS, gt, _log_sigmoid(gt))
        for c in range(tm // chunk):
            gates_ref[c] = gt[:, c * chunk:(c + 1) * chunk]

    z = jnp.dot(h_ref[...], w_ref[...], preferred_element_type=F32)

    @pl.when(j < 3)
    def _():
        qkvo_ref[...] = z.astype(BF16)

    @pl.when(j >= 3)
    def _():
        u_ref[...] = (z[:, :GLU_HALF] * _sigmoid(z[:, GLU_HALF:])).astype(BF16)


def _proj_in(x, g, w_main, wg, bg, chunk):
    n = x.shape[0]
    tm = min(PROJ_TM, n)
    nj = w_main.shape[1] // PROJ_TN
    return pl.pallas_call(
        functools.partial(_proj_in_kernel, chunk=chunk),
        grid=(n // tm, nj),
        in_specs=[
            pl.BlockSpec((tm, D_MODEL), lambda i, j: (i, 0)),
            pl.BlockSpec((1, D_MODEL), lambda i, j: (0, 0)),
            pl.BlockSpec((D_MODEL, PROJ_TN), lambda i, j: (0, j)),
            pl.BlockSpec((2 * M_HEADS, D_MODEL), lambda i, j: (0, 0)),
            pl.BlockSpec((2 * M_HEADS, 1), lambda i, j: (0, 0)),
        ],
        out_specs=[
            pl.BlockSpec((tm, PROJ_TN), lambda i, j: (i, jnp.minimum(j, 2))),
            pl.BlockSpec((tm, GLU_HALF), lambda i, j: (i, jnp.maximum(j - 3, 0))),
            pl.BlockSpec((tm // chunk, 2 * M_HEADS, chunk), lambda i, j: (i, 0, 0)),
        ],
        out_shape=[
            jax.ShapeDtypeStruct((n, 2 * QK_W + 2 * M_WIDTH), BF16),
            jax.ShapeDtypeStruct((n, CONV_CH), BF16),
            jax.ShapeDtypeStruct((n // chunk, 2 * M_HEADS, chunk), F32),
        ],
        scratch_shapes=[pltpu.VMEM((tm, D_MODEL), BF16)],
        compiler_params=_cparams(("parallel", "arbitrary"), 48),
        name="proj_in",
    )(x, g, w_main, wg, bg)


def _conv_kernel(u_ref, st_ref, w_ref, cb_ref, lg_ref, lb_ref, cv_ref, ns_ref, e_ref, er_ref, c_ref, *, bb, tt):
    t = pl.program_id(1)
    nt = pl.num_programs(1)
    ext = tt + CONV_PAD

    @pl.when(t == 0)
    def _():
        e_ref[:, 0:CONV_PAD, :] = st_ref[...]

    @pl.when(t > 0)
    def _():
        e_ref[:, 0:CONV_PAD, :] = e_ref[:, tt:ext, :]

    e_ref[:, CONV_PAD:ext, :] = u_ref[...].astype(F32)
    e_ref[:, ext:, :] = jnp.zeros((bb, SUBLANES, CONV_CH), F32)

    base = CONV_PAD - CONV_STATE
    for b in range(bb):
        for r in range(1, SUBLANES):
            er_ref[r - 1] = e_ref[b, r:r + ext, :]

        def row_block(rb, carry):
            r0 = pl.multiple_of(rb * CONV_RB, CONV_RB)
            for lc in range(CONV_CH // CONV_LC):
                ls = slice(lc * CONV_LC, (lc + 1) * CONV_LC)
                acc = jnp.zeros((CONV_RB, CONV_LC), F32) + cb_ref[:, ls]
                for j in range(CONV_WIDTH):
                    off = j + base
                    a, r = (off // SUBLANES) * SUBLANES, off % SUBLANES
                    if r == 0:
                        tap = e_ref[b, pl.ds(r0 + a, CONV_RB), ls]
                    else:
                        tap = er_ref[r - 1, pl.ds(r0 + a, CONV_RB), ls]
                    acc = acc + w_ref[j:j + 1, ls] * tap
                c_ref[:, ls] = acc
            c = c_ref[...]
            mu = jnp.mean(c, axis=-1, keepdims=True)
            cc = c - mu
            var = jnp.mean(cc * cc, axis=-1, keepdims=True)
            cn = cc * lax.rsqrt(var + EPS) * lg_ref[...] + lb_ref[...]
            cv_ref[b, pl.ds(r0, CONV_RB), :] = (cn * _sigmoid(cn)).astype(BF16)
            return carry

        lax.fori_loop(0, tt // CONV_RB, row_block, 0)

    @pl.when(t == nt - 1)
    def _():
        ns_ref[...] = e_ref[:, ext - CONV_STATE:ext, :]


def _conv(u, state_pad, conv_w, conv_b, ln_g, ln_b, bb, tt):
    bsz, t, _ = u.shape
    return pl.pallas_call(
        functools.partial(_conv_kernel, bb=bb, tt=tt),
        grid=(bsz // bb, t // tt),
        in_specs=[
            pl.BlockSpec((bb, tt, CONV_CH), lambda b, s: (b, s, 0)),
            pl.BlockSpec((bb, CONV_PAD, CONV_CH), lambda b, s: (b, 0, 0)),
            pl.BlockSpec((CONV_WIDTH, CONV_CH), lambda b, s: (0, 0)),
            pl.BlockSpec((1, CONV_CH), lambda b, s: (0, 0)),
            pl.BlockSpec((1, CONV_CH), lambda b, s: (0, 0)),
            pl.BlockSpec((1, CONV_CH), lambda b, s: (0, 0)),
        ],
        out_specs=[
            pl.BlockSpec((bb, tt, CONV_CH), lambda b, s: (b, s, 0)),
            pl.BlockSpec((bb, CONV_STATE, CONV_CH), lambda b, s: (b, 0, 0)),
        ],
        out_shape=[
            jax.ShapeDtypeStruct((bsz, t, CONV_CH), BF16),
            jax.ShapeDtypeStruct((bsz, CONV_STATE, CONV_CH), F32),
        ],
        scratch_shapes=[
            pltpu.VMEM((bb, tt + CONV_PAD + SUBLANES, CONV_CH), F32),
            pltpu.VMEM((SUBLANES - 1, tt + CONV_PAD, CONV_CH), F32),
            pltpu.VMEM((CONV_RB, CONV_CH), F32),
        ],
        compiler_params=_cparams(("parallel", "arbitrary"), 40),
        name="conv",
    )(u, state_pad, conv_w, conv_b, ln_g, ln_b)


def _mlstm_kernel(q_ref, k_ref, v_ref, o_ref, gt_ref, c0_ref, n0_ref, m0_ref, ng_ref,
                  hm_ref, cn_ref, nn_ref, mn_ref, c_s, n_s, m_s, *, bb, L):
    ci = pl.program_id(1)
    nc = pl.num_programs(1)

    @pl.when(ci == 0)
    def _():
        c_s[...] = c0_ref[...]
        n_s[...] = n0_ref[...]
        m_s[...] = m0_ref[...]

    rt = lax.broadcasted_iota(jnp.int32, (L, L), 0)
    cs = lax.broadcasted_iota(jnp.int32, (L, L), 1)
    causal = cs <= rt
    lower = causal.astype(F32)
    upper = (rt <= cs).astype(F32)
    eye = (rt == cs).astype(F32)
    nt_dims = (((1,), (1,)), ((), ()))
    scale = M_DK ** -0.5

    for b in range(bb):
        rows = gt_ref[b]
        cum_rows = jnp.dot(rows, upper, precision=HIGHEST, preferred_element_type=F32)
        cum_cols = lax.dot_general(lower, rows, nt_dims, precision=HIGHEST, preferred_element_type=F32)
        id_cols = lax.dot_general(eye, rows, nt_dims, precision=HIGHEST, preferred_element_type=F32)
        for h in range(M_HEADS):
            ig_row = rows[h:h + 1, :]
            ig_col = id_cols[:, h:h + 1]
            b_row = cum_rows[M_HEADS + h:M_HEADS + h + 1, :]
            b_col = cum_cols[:, M_HEADS + h:M_HEADS + h + 1]
            m0 = m_s[b, h:h + 1, 0:1]
            q = q_ref[b, :, h * M_DK:(h + 1) * M_DK]
            k = k_ref[b, :, h * M_DK:(h + 1) * M_DK]
            v = v_ref[b, :, h * M_DV:(h + 1) * M_DV]
            c0 = c_s[b, h]
            n0 = n_s[b, h:h + 1, :]

            g_col = b_col + m0
            dmat = jnp.where(causal, b_col - b_row + ig_row, -jnp.inf)
            m_col = jnp.maximum(g_col, jnp.max(dmat, axis=1, keepdims=True))
            w_state = jnp.exp(g_col - m_col) * scale
            qk = lax.dot_general(q, k, nt_dims, preferred_element_type=F32)
            s = qk * (jnp.exp(dmat - m_col) * scale)
            num = jnp.dot(s.astype(BF16), v, preferred_element_type=F32)
            num = num + w_state * jnp.dot(q, c0.astype(BF16), preferred_element_type=F32)
            qn = jnp.sum(q.astype(F32) * n0, axis=1, keepdims=True)
            den = jnp.sum(s, axis=1, keepdims=True) + w_state * qn
            hh = num / jnp.maximum(jnp.abs(den), jnp.exp(-m_col))

            hh = hh * lax.rsqrt(jnp.mean(hh * hh, axis=-1, keepdims=True) + EPS) * ng_ref[h:h + 1, :]
            og = _sigmoid(o_ref[b, :, h * M_DV:(h + 1) * M_DV].astype(F32))
            hm_ref[b, :, h * M_DV:(h + 1) * M_DV] = (og * hh).astype(BF16)

            b_last = b_row[:, L - 1:L]
            g_last = b_last + m0
            wk_row = b_last - b_row + ig_row
            m_new = jnp.maximum(g_last, jnp.max(wk_row, axis=1, keepdims=True))
            a0 = jnp.exp(g_last - m_new)
            ak_col = jnp.exp(b_last - b_col + ig_col - m_new)
            ks = k.astype(F32) * ak_col
            kv = lax.dot_general(ks.astype(BF16), v, (((0,), (0,)), ((), ())), preferred_element_type=F32)
            c_s[b, h] = a0 * c0 + kv
            n_s[b, h:h + 1, :] = a0 * n0 + jnp.sum(ks, axis=0, keepdims=True)
            m_s[b, h:h + 1, :] = jnp.broadcast_to(m_new, (1, LANES))

    @pl.when(ci == nc - 1)
    def _():
        cn_ref[...] = c_s[...]
        nn_ref[...] = n_s[...]
        mn_ref[...] = m_s[...]


def _mlstm(qkvo, gates, c0, n0, m0, ng, bb, L):
    bsz, t, _ = qkvo.shape
    nc = t // L
    gates = gates.reshape(bsz, nc, 2 * M_HEADS, L)
    st_c = pl.BlockSpec((bb, M_HEADS, M_DK, M_DV), lambda b, c: (b, 0, 0, 0))
    st_n = pl.BlockSpec((bb, M_HEADS, M_DK), lambda b, c: (b, 0, 0))
    return pl.pallas_call(
        functools.partial(_mlstm_kernel, bb=bb, L=L),
        grid=(bsz // bb, nc),
        in_specs=[
            pl.BlockSpec((bb, L, QK_W), lambda b, c: (b, c, 0)),
            pl.BlockSpec((bb, L, QK_W), lambda b, c: (b, c, 1)),
            pl.BlockSpec((bb, L, M_WIDTH), lambda b, c: (b, c, 1)),
            pl.BlockSpec((bb, L, M_WIDTH), lambda b, c: (b, c, 2)),
            pl.BlockSpec((bb, None, 2 * M_HEADS, L), lambda b, c: (b, c, 0, 0)),
            st_c, st_n, st_n,
            pl.BlockSpec((M_HEADS, M_DV), lambda b, c: (0, 0)),
        ],
        out_specs=[
            pl.BlockSpec((bb, L, M_WIDTH), lambda b, c: (b, c, 0)),
            st_c, st_n, st_n,
        ],
        out_shape=[
            jax.ShapeDtypeStruct((bsz, t, M_WIDTH), BF16),
            jax.ShapeDtypeStruct((bsz, M_HEADS, M_DK, M_DV), F32),
            jax.ShapeDtypeStruct((bsz, M_HEADS, M_DK), F32),
            jax.ShapeDtypeStruct((bsz, M_HEADS, LANES), F32),
        ],
        scratch_shapes=[
            pltpu.VMEM((bb, M_HEADS, M_DK, M_DV), F32),
            pltpu.VMEM((bb, M_HEADS, M_DK), F32),
            pltpu.VMEM((bb, M_HEADS, LANES), F32),
        ],
        compiler_params=_cparams(("parallel", "arbitrary"), 40),
        name="mlstm",
    )(qkvo, qkvo, qkvo, qkvo, gates, c0, n0, m0, ng)


def _proj_out_kernel(x_ref, hm_ref, cv_ref, wa_ref, wb_ref, o_ref):
    acc = jnp.dot(hm_ref[...], wa_ref[...], preferred_element_type=F32)
    acc = acc + jnp.dot(cv_ref[...], wb_ref[...], preferred_element_type=F32)
    o_ref[...] = x_ref[...] + acc


def _proj_out(x, hm, cv, wa, wb):
    n = x.shape[0]
    tm = min(TOK_TILE, n)
    return pl.pallas_call(
        _proj_out_kernel,
        grid=(n // tm,),
        in_specs=[
            pl.BlockSpec((tm, D_MODEL), lambda i: (i, 0)),
            pl.BlockSpec((tm, M_WIDTH), lambda i: (i, 0)),
            pl.BlockSpec((tm, CONV_CH), lambda i: (i, 0)),
            pl.BlockSpec((M_WIDTH, D_MODEL), lambda i: (0, 0)),
            pl.BlockSpec((CONV_CH, D_MODEL), lambda i: (0, 0)),
        ],
        out_specs=pl.BlockSpec((tm, D_MODEL), lambda i: (i, 0)),
        out_shape=jax.ShapeDtypeStruct((n, D_MODEL), F32),
        compiler_params=_cparams(("parallel",), 48),
        name="proj_out",
    )(x, hm, cv, wa, wb)


def _mem_kv_kernel(m_ref, g_ref, w_ref, k_ref, v_ref):
    mn = _rms(m_ref[...], g_ref[...]).astype(BF16)
    z = jnp.dot(mn, w_ref[...], preferred_element_type=F32)
    k_ref[...] = z[:, :CA_W]
    v_ref[...] = z[:, CA_W:]


def _mem_kv(mem, g, wkv):
    n = mem.shape[0]
    return pl.pallas_call(
        _mem_kv_kernel,
        grid=(1,),
        in_specs=[
            pl.BlockSpec((n, D_MODEL), lambda i: (0, 0)),
            pl.BlockSpec((1, D_MODEL), lambda i: (0, 0)),
            pl.BlockSpec((D_MODEL, 2 * CA_W), lambda i: (0, 0)),
        ],
        out_specs=[pl.BlockSpec((n, CA_W), lambda i: (0, 0))] * 2,
        out_shape=[jax.ShapeDtypeStruct((n, CA_W), F32)] * 2,
        compiler_params=_cparams(("arbitrary",), 40),
        name="mem_kv",
    )(mem, g, wkv)


def _attn_router_kernel(x_ref, gca_ref, wq_ref, k_ref, v_ref, wo_ref, gmoe_ref, rw_ref, rb_ref, cnt0_ref,
                        hn_in_ref, x2_ref, hn_ref, te_ref, tg_ref, tr_ref, cnt_ref, o_s, cnt_s, *, bb, tt):
    del hn_in_ref
    rows = bb * tt

    @pl.when((pl.program_id(0) == 0) & (pl.program_id(1) == 0))
    def _():
        cnt_s[...] = cnt0_ref[...]

    x = x_ref[...].reshape(rows, D_MODEL)
    h = _rms(x, gca_ref[...]).astype(BF16)
    q = jnp.dot(h, wq_ref[...], preferred_element_type=F32) * (CA_DH ** -0.5)
    q = q.astype(BF16)
    for b in range(bb):
        for hd in range(CA_HEADS):
            cols = slice(hd * CA_DH, (hd + 1) * CA_DH)
            qb = q[b * tt:(b + 1) * tt, cols]
            s = lax.dot_general(qb, k_ref[b, :, cols], (((1,), (1,)), ((), ())), preferred_element_type=F32)
            e = jnp.exp(s - jnp.max(s, axis=-1, keepdims=True))
            p = (e / jnp.sum(e, axis=-1, keepdims=True)).astype(BF16)
            o_s[b * tt:(b + 1) * tt, cols] = jnp.dot(p, v_ref[b, :, cols], preferred_element_type=F32).astype(BF16)
    x2 = x + jnp.dot(o_s[...], wo_ref[...], preferred_element_type=F32)
    x2_ref[...] = x2.reshape(bb, tt, D_MODEL)

    hn = _rms(x2, gmoe_ref[...])
    hn_bf = hn.astype(BF16)
    hn_ref[...] = _pack_pairs(hn[:, :HALF_D], hn[:, HALF_D:])
    logits = jnp.dot(hn_bf, rw_ref[...], preferred_element_type=F32) + rb_ref[...]
    lane = lax.broadcasted_iota(jnp.int32, (rows, LANES), 1)
    work = logits
    sel = []
    vals = []
    for _ in range(TOP_K):
        mx = jnp.max(work, axis=-1, keepdims=True)
        idx = jnp.min(jnp.where(work == mx, lane, LANES), axis=-1, keepdims=True)
        hit = lane == idx
        sel.append((idx, hit))
        vals.append(mx)
        work = jnp.where(hit, -jnp.inf, work)
    ex = [jnp.exp(vv - vals[0]) for vv in vals]
    tot = ex[0] + ex[1] + ex[2] + ex[3]
    assigned = jnp.zeros((rows, LANES), F32)
    for _, hit in sel:
        assigned = assigned + hit.astype(F32)
    rt = lax.broadcasted_iota(jnp.int32, (rows, rows), 0)
    cs = lax.broadcasted_iota(jnp.int32, (rows, rows), 1)
    strict = (cs < rt).astype(BF16)
    before = jnp.dot(strict, assigned.astype(BF16), preferred_element_type=F32) + cnt_s[...]
    te = jnp.zeros((rows, LANES), jnp.int32)
    tg = jnp.zeros((rows, LANES), F32)
    tr = jnp.zeros((rows, LANES), F32)
    for kk, (idx, hit) in enumerate(sel):
        rank = jnp.sum(jnp.where(hit, before, 0.0), axis=-1, keepdims=True)
        te = jnp.where(lane == kk, idx, te)
        tg = jnp.where(lane == kk, ex[kk] / tot, tg)
        tr = jnp.where(lane == kk, rank, tr)
    te_ref[...] = te
    tg_ref[...] = tg
    tr_ref[...] = tr.astype(jnp.int32)
    cnt_s[...] = cnt_s[...] + jnp.sum(assigned, axis=0, keepdims=True)
    cnt_ref[...] = cnt_s[...]


def _attn_router(x1, mem_k, mem_v, cnt0, hn_all, row0, p, bb, tt):
    bsz, t, _ = x1.shape
    rows = bb * tt
    n = bsz * t
    nb, nt = bsz // bb, t // tt
    blk0 = row0 // rows
    tok = lambda b, s: (b * nt + s, 0)
    const = lambda b, s: (0, 0)
    grid_spec = dict(
        grid=(nb, nt),
        in_specs=[
            pl.BlockSpec((bb, tt, D_MODEL), lambda b, s: (b, s, 0)),
            pl.BlockSpec((1, D_MODEL), const),
            pl.BlockSpec((D_MODEL, CA_W), const),
            pl.BlockSpec((bb, N_MEM, CA_W), lambda b, s: (b, 0, 0)),
            pl.BlockSpec((bb, N_MEM, CA_W), lambda b, s: (b, 0, 0)),
            pl.BlockSpec((CA_W, D_MODEL), const),
            pl.BlockSpec((1, D_MODEL), const),
            pl.BlockSpec((D_MODEL, LANES), const),
            pl.BlockSpec((1, LANES), const),
            pl.BlockSpec((1, LANES), const),
            pl.BlockSpec(memory_space=pl.ANY),
        ],
        out_specs=[
            pl.BlockSpec((bb, tt, D_MODEL), lambda b, s: (b, s, 0)),
            pl.BlockSpec((rows, HALF_D), lambda b, s: (blk0 + b * nt + s, 0)),
            pl.BlockSpec((rows, LANES), tok),
            pl.BlockSpec((rows, LANES), tok),
            pl.BlockSpec((rows, LANES), tok),
            pl.BlockSpec((1, LANES), const),
        ],
    )
    return pl.pallas_call(
        functools.partial(_attn_router_kernel, bb=bb, tt=tt),
        out_shape=[
            jax.ShapeDtypeStruct((bsz, t, D_MODEL), F32),
            jax.ShapeDtypeStruct(hn_all.shape, jnp.uint32),
            jax.ShapeDtypeStruct((n, LANES), jnp.int32),
            jax.ShapeDtypeStruct((n, LANES), F32),
            jax.ShapeDtypeStruct((n, LANES), jnp.int32),
            jax.ShapeDtypeStruct((1, LANES), F32),
        ],
        scratch_shapes=[pltpu.VMEM((rows, CA_W), BF16), pltpu.VMEM((1, LANES), F32)],
        input_output_aliases={10: 1},
        compiler_params=_cparams(("arbitrary", "arbitrary"), 48),
        name="attn_router",
        **grid_spec,
    )(x1, p["g_ca"], p["wq"], mem_k, mem_v, p["wo"], p["g_moe"], p["router_w"], p["router_b"], cnt0, hn_all)


def _expert_changed(te_ref):
    t = pl.program_id(1)
    return (t == 0) | (te_ref[t] != te_ref[jnp.maximum(t - 1, 0)])


def _moe_up_kernel(te_ref, nu_ref, x_ref, w1g_ref, w1l_ref, b1g_ref, b1l_ref, act_ref, wg_s, wl_s):
    @pl.when(_expert_changed(te_ref))
    def _():
        wg_s[...] = w1g_ref[0].astype(BF16)
        wl_s[...] = w1l_ref[0].astype(BF16)

    @pl.when(pl.program_id(1) < nu_ref[0])
    def _():
        lo, hi = _unpack_pairs(x_ref[...])
        x = jnp.concatenate([lo.astype(BF16), hi.astype(BF16)], axis=1)
        glu = jnp.dot(x, wg_s[...], preferred_element_type=F32) + b1g_ref[0]
        lin = jnp.dot(x, wl_s[...], preferred_element_type=F32) + b1l_ref[0]
        glu = jnp.minimum(glu, SWIGLU_LIMIT)
        lin = jnp.clip(lin, -SWIGLU_LIMIT, SWIGLU_LIMIT)
        act_ref[...] = (glu * _sigmoid(SWIGLU_ALPHA * glu) * (lin + 1.0)).astype(BF16)


def _moe_down_kernel(te_ref, nu_ref, a_ref, w2_ref, b2_ref, y_ref, w2_s):
    @pl.when(_expert_changed(te_ref))
    def _():
        w2_s[...] = w2_ref[0].astype(BF16)

    @pl.when(pl.program_id(1) < nu_ref[0])
    def _():
        y = jnp.dot(a_ref[...], w2_s[...], preferred_element_type=F32) + b2_ref[0]
        y_ref[...] = _pack_pairs(y[:, :MOE_TN // 2], y[:, MOE_TN // 2:])


def _moe(tile_e, n_used, x_sorted, w1, b1, w2, b2):
    r = x_sorted.shape[0]
    n_tiles = r // MOE_TM
    nf = D_FF // MOE_TF
    nn = D_MODEL // MOE_TN

    def row(c, t, te, nu):
        return jnp.maximum(jnp.minimum(t, nu[0] - 1), 0)

    act = pl.pallas_call(
        _moe_up_kernel,
        grid_spec=pltpu.PrefetchScalarGridSpec(
            num_scalar_prefetch=2,
            grid=(nf, n_tiles),
            in_specs=[
                pl.BlockSpec((MOE_TM, HALF_D), lambda c, t, te, nu: (row(c, t, te, nu), 0)),
                pl.BlockSpec((1, D_MODEL, MOE_TF), lambda c, t, te, nu: (te[t], 0, c)),
                pl.BlockSpec((1, D_MODEL, MOE_TF), lambda c, t, te, nu: (te[t], 0, nf + c)),
                pl.BlockSpec((1, 1, MOE_TF), lambda c, t, te, nu: (te[t], 0, c)),
                pl.BlockSpec((1, 1, MOE_TF), lambda c, t, te, nu: (te[t], 0, nf + c)),
            ],
            out_specs=pl.BlockSpec((MOE_TM, MOE_TF), lambda c, t, te, nu: (row(c, t, te, nu), c)),
            scratch_shapes=[pltpu.VMEM((D_MODEL, MOE_TF), BF16), pltpu.VMEM((D_MODEL, MOE_TF), BF16)],
        ),
        out_shape=jax.ShapeDtypeStruct((r, D_FF), BF16),
        compiler_params=_cparams(("arbitrary", "arbitrary"), 48),
        name="moe_up",
    )(tile_e, n_used, x_sorted, w1, w1, b1, b1)

    return pl.pallas_call(
        _moe_down_kernel,
        grid_spec=pltpu.PrefetchScalarGridSpec(
            num_scalar_prefetch=2,
            grid=(nn, n_tiles),
            in_specs=[
                pl.BlockSpec((MOE_TM, D_FF), lambda c, t, te, nu: (row(c, t, te, nu), 0)),
                pl.BlockSpec((1, D_FF, MOE_TN), lambda c, t, te, nu: (te[t], 0, c)),
                pl.BlockSpec((1, 1, MOE_TN), lambda c, t, te, nu: (te[t], 0, c)),
            ],
            out_specs=pl.BlockSpec((MOE_TM, MOE_TN // 2), lambda c, t, te, nu: (row(c, t, te, nu), c)),
            scratch_shapes=[pltpu.VMEM((D_FF, MOE_TN), BF16)],
        ),
        out_shape=jax.ShapeDtypeStruct((r, HALF_D), jnp.uint32),
        compiler_params=_cparams(("arbitrary", "arbitrary"), 48),
        name="moe_down",
    )(tile_e, n_used, act, w2, b2)


def _combine_kernel(x_ref, y0_ref, y1_ref, y2_ref, y3_ref, tg_ref, g_ref, o_ref):
    tg = tg_ref[...]
    acc = x_ref[...]
    half = MOE_TN // 2
    for kk, y_ref in enumerate((y0_ref, y1_ref, y2_ref, y3_ref)):
        cols = []
        for c in range(D_MODEL // MOE_TN):
            cols += list(_unpack_pairs(y_ref[0, :, c * half:(c + 1) * half]))
        acc = acc + jnp.concatenate(cols, axis=1) * tg[:, kk:kk + 1]
    o_ref[...] = _rms(acc, g_ref[...])


def _combine(x2, yg, tg, g, row0):
    n = x2.shape[0]
    tm = min(TOK_TILE, n)
    blk0 = row0 // tm
    yspec = [pl.BlockSpec((1, tm, HALF_D), functools.partial(lambda i, kk: (kk, blk0 + i, 0), kk=kk))
             for kk in range(TOP_K)]
    return pl.pallas_call(
        _combine_kernel,
        grid=(n // tm,),
        in_specs=[pl.BlockSpec((tm, D_MODEL), lambda i: (i, 0))] + yspec + [
            pl.BlockSpec((tm, LANES), lambda i: (i, 0)),
            pl.BlockSpec((1, D_MODEL), lambda i: (0, 0)),
        ],
        out_specs=pl.BlockSpec((tm, D_MODEL), lambda i: (i, 0)),
        out_shape=jax.ShapeDtypeStruct((n, D_MODEL), F32),
        compiler_params=_cparams(("parallel",), 48),
        name="combine",
    )(x2, yg, yg, yg, yg, tg, g)


def _group_tiles(bsz, t):
    tt = min(t, TOK_TILE)
    bb = max(1, min(bsz, TOK_TILE // tt))
    return bb, tt


def _layer_group(x, c0, n0, m0, conv_state, mem_k, mem_v, cnt0, hn_all, row0, p):
    bsz, t, _ = x.shape
    n = bsz * t
    L = min(MLSTM_CHUNK, t)
    qkvo, u, gates = _proj_in(x.reshape(n, D_MODEL), p["g_mix"], p["w_main"], p["wg"], p["bg"], L)

    ctt = min(CONV_TT, t)
    cbb = max(1, min(bsz, CONV_TT // ctt))
    state_pad = jnp.pad(conv_state, ((0, 0), (CONV_PAD - CONV_STATE, 0), (0, 0)))
    cv, new_conv = _conv(u.reshape(bsz, t, CONV_CH), state_pad, p["conv_w"], p["conv_b"],
                         p["ln_g"], p["ln_b"], cbb, ctt)

    mbb = max(1, min(bsz, LANES // L)) if L < LANES else 1
    m0b = jnp.broadcast_to(m0[:, :, None], (bsz, M_HEADS, LANES))
    hm, c1, n1, m1 = _mlstm(qkvo.reshape(bsz, t, -1), gates, c0, n0, m0b, p["mh_g"], mbb, L)

    x1 = _proj_out(x.reshape(n, D_MODEL), hm.reshape(n, M_WIDTH), cv.reshape(n, CONV_CH), p["wo_a"], p["wo_b"])

    abb, att = _group_tiles(bsz, t)
    x2, hn_all, te, tg, tr, cnt = _attn_router(x1.reshape(bsz, t, D_MODEL), mem_k, mem_v, cnt0, hn_all, row0,
                                               p, abb, att)
    return x2.reshape(n, D_MODEL), hn_all, te, tg, tr, cnt, (c1, n1, m1[:, :, 0], new_conv)


def kernel(x_prompt, x_sample, mem_prompt, state_mlstm_C, state_mlstm_n, state_mlstm_m, state_conv, cache_mem_k, cache_mem_v, norm_mix_g, w_in, b_gates, mh_norm_g, conv_w, conv_b, conv_ln_g, conv_ln_b, w_out, norm_ca_g, norm_mem_g, ca_wq, ca_wk, ca_wv, ca_wo, norm_moe_g, router_w, router_b, moe_w1, moe_b1, moe_w2, moe_b2, final_norm_g):
    assert w_in.shape[0] == 1, "single layer"
    bp, tp, _ = x_prompt.shape
    bs, ts, _ = x_sample.shape
    n_p, n_s = bp * tp, bs * ts
    n_all = n_p + n_s

    wi = w_in[0]
    o0 = 0
    parts = []
    for width in (QK_W, QK_W, M_WIDTH, M_WIDTH, M_HEADS, M_HEADS, CONV_CH, CONV_CH):
        parts.append(wi[:, o0:o0 + width])
        o0 += width
    w_q, w_k, w_v, w_o, w_gi, w_gf, w_ua, w_ug = parts
    glu_cols = []
    for c in range(CONV_CH // GLU_HALF):
        glu_cols += [w_ua[:, c * GLU_HALF:(c + 1) * GLU_HALF], w_ug[:, c * GLU_HALF:(c + 1) * GLU_HALF]]
    p = {
        "g_mix": norm_mix_g[0][None, :],
        "w_main": jnp.concatenate([w_q, w_k, w_v, w_o] + glu_cols, axis=1).astype(BF16),
        "wg": jnp.concatenate([w_gi, w_gf], axis=1).T.astype(BF16),
        "bg": b_gates[0][:, None],
        "mh_g": mh_norm_g[0],
        "conv_w": conv_w[0],
        "conv_b": conv_b[0][None, :],
        "ln_g": conv_ln_g[0][None, :],
        "ln_b": conv_ln_b[0][None, :],
        "wo_a": w_out[0][:M_WIDTH].astype(BF16),
        "wo_b": w_out[0][M_WIDTH:].astype(BF16),
        "g_ca": norm_ca_g[0][None, :],
        "wq": ca_wq[0].astype(BF16),
        "wo": ca_wo[0].astype(BF16),
        "g_moe": norm_moe_g[0][None, :],
        "router_w": jnp.pad(router_w[0], ((0, 0), (0, LANES - N_EXPERTS))).astype(BF16),
        "router_b": jnp.concatenate([router_b[0], jnp.full((LANES - N_EXPERTS,), NEG_BIG, F32)])[None, :],
    }

    wkv = jnp.concatenate([ca_wk[0], ca_wv[0]], axis=1).astype(BF16)
    mk, mv = _mem_kv(mem_prompt.reshape(bp * N_MEM, D_MODEL), norm_mem_g[0][None, :], wkv)
    mk = mk.reshape(bp, N_MEM, CA_W)
    mv = mv.reshape(bp, N_MEM, CA_W)
    hn_all = jnp.zeros((n_all, HALF_D), jnp.uint32)
    cnt0 = jnp.zeros((1, LANES), F32)
    zc = jnp.zeros((bp, M_HEADS, M_DK, M_DV), F32)
    zn = jnp.zeros((bp, M_HEADS, M_DK), F32)
    zm = jnp.zeros((bp, M_HEADS), F32)
    zconv = jnp.zeros((bp, CONV_STATE, CONV_CH), F32)
    x2p, hn_all, te_p, tg_p, tr_p, cnt1, st_p = _layer_group(
        x_prompt, zc, zn, zm, zconv, mk.astype(BF16), mv.astype(BF16), cnt0, hn_all, 0, p)

    x2s, hn_all, te_s, tg_s, tr_s, cnt2, st_s = _layer_group(
        x_sample, state_mlstm_C[0], state_mlstm_n[0], state_mlstm_m[0], state_conv[0],
        cache_mem_k[0].reshape(bs, N_MEM, CA_W).astype(BF16), cache_mem_v[0].reshape(bs, N_MEM, CA_W).astype(BF16),
        cnt1, hn_all, n_p, p)

    te = jnp.concatenate([te_p[:, :TOP_K], te_s[:, :TOP_K]], axis=0)
    tr = jnp.concatenate([tr_p[:, :TOP_K], tr_s[:, :TOP_K]], axis=0)
    counts = cnt2[0, :N_EXPERTS].astype(jnp.int32)
    tiles_per_e = (counts + MOE_TM - 1) // MOE_TM
    tile_end = jnp.cumsum(tiles_per_e)
    row_start = (tile_end - tiles_per_e) * MOE_TM
    n_tiles = -(-(n_all * TOP_K) // MOE_TM) + N_EXPERTS
    n_used = tile_end[-1:]
    tile_ids = jnp.minimum(jnp.arange(n_tiles, dtype=jnp.int32), n_used[0] - 1)
    tile_e = jnp.minimum(jnp.sum(tile_ids[:, None] >= tile_end[None, :], axis=1), N_EXPERTS - 1).astype(jnp.int32)
    dest = row_start[te] + tr
    tok_ids = jnp.broadcast_to(jnp.arange(n_all, dtype=jnp.int32)[:, None], (n_all, TOP_K))
    row_tok = jnp.zeros((n_tiles * MOE_TM,), jnp.int32).at[dest.reshape(-1)].set(
        tok_ids.reshape(-1), unique_indices=True, mode="promise_in_bounds")
    x_sorted = hn_all.at[row_tok].get(mode="promise_in_bounds")

    y_sorted = _moe(tile_e, n_used.astype(jnp.int32), x_sorted, moe_w1[0], moe_b1[0][:, None, :], moe_w2[0],
                    moe_b2[0][:, None, :])

    yg = y_sorted.at[dest.T.reshape(-1)].get(mode="promise_in_bounds").reshape(TOP_K, n_all, HALF_D)
    fg = final_norm_g[None, :]
    y_prompt = _combine(x2p, yg, tg_p, fg, 0).reshape(bp, tp, D_MODEL)
    y_sample = _combine(x2s, yg, tg_s, fg, n_p).reshape(bs, ts, D_MODEL)

    c1, n1, m1, conv1 = st_p
    c2, n2, m2, conv2 = st_s
    mk4 = mk.reshape(1, bp, N_MEM, CA_HEADS, CA_DH)
    mv4 = mv.reshape(1, bp, N_MEM, CA_HEADS, CA_DH)
    return (y_prompt, y_sample, c1[None], n1[None], m1[None], conv1[None], mk4, mv4,
            c2[None], n2[None], m2[None], conv2[None])
```

```python
import functools

import jax
import jax.numpy as jnp
from jax import lax
from jax.experimental import pallas as pl
from jax.experimental.pallas import tpu as pltpu

F32 = jnp.float32
BF16 = jnp.bfloat16
HIGHEST = lax.Precision.HIGHEST

D_MODEL = 2048
HALF_D = D_MODEL // 2
M_HEADS = 4
M_DV = 256
M_DK = 128
QK_W = M_HEADS * M_DK
M_WIDTH = M_HEADS * M_DV
CONV_CH = 1024
CONV_WIDTH = 31
CONV_STATE = CONV_WIDTH - 1
CONV_PAD = 32
N_MEM = 256
CA_HEADS = 4
CA_DH = 128
CA_W = CA_HEADS * CA_DH
N_EXPERTS = 32
TOP_K = 4
D_FF = 2048
SWIGLU_ALPHA = 1.702
SWIGLU_LIMIT = 7.0
EPS = 1e-6
LANES = 128
SUBLANES = 8
NEG_BIG = -1e30

PROJ_TM = 1024
PROJ_TN = 1024
GLU_HALF = PROJ_TN // 2
TOK_TILE = 512
MOE_TM = 512
MOE_TF = 512
MOE_TN = 1024
MLSTM_CHUNK = 256
CONV_TT = 256
CONV_RB = 32
CONV_LC = 512


def _cparams(sem, vmem_mb):
    return pltpu.CompilerParams(dimension_semantics=sem, vmem_limit_bytes=vmem_mb << 20)


def _log_sigmoid(x):
    return jnp.minimum(x, 0.0) - jnp.log1p(jnp.exp(-jnp.abs(x)))


def _sigmoid(x):
    return 1.0 / (1.0 + jnp.exp(-x))


def _pack_pairs(lo, hi):
    ulo = lax.bitcast_convert_type(lo.astype(BF16).astype(F32), jnp.uint32)
    uhi = lax.bitcast_convert_type(hi.astype(BF16).astype(F32), jnp.uint32)
    return (ulo >> 16) | uhi


def _unpack_pairs(w):
    lo = lax.bitcast_convert_type(w << 16, F32)
    hi = lax.bitcast_convert_type(w & jnp.uint32(0xFFFF0000), F32)
    return lo, hi


def _rms(x, g):
    ms = jnp.mean(x * x, axis=-1, keepdims=True)
    return x * lax.rsqrt(ms + EPS) * g


def _proj_in_kernel(x_ref, g_ref, w_ref, wg_ref, bg_ref, qkvo_ref, u_ref, gates_ref, h_ref, *, chunk):
    j = pl.program_id(1)
    tm = x_ref.shape[0]

    @pl.when(j == 0)
    def _():
        h = _rms(x_ref[...], g_ref[...]).astype(BF16)
        h_ref[...] = h
        gt = lax.dot_general(wg_ref[...], h, (((1,), (1,)), ((), ())), preferred_element_type=F32)
        gt = gt + bg_ref[...]
        row = lax.broadcasted_iota(jnp.int32, gt.shape, 0)
        gt = jnp.where(row < M_HEADS, gt, _log_sigmoid(gt))
        for c in range(tm // chunk):
            gates_ref[c] = gt[:, c * chunk:(c + 1) * chunk]

    z = jnp.dot(h_ref[...], w_ref[...], preferred_element_type=F32)

    @pl.when(j < 3)
    def _():
        qkvo_ref[...] = z.astype(BF16)

    @pl.when(j >= 3)
    def _():
        u_ref[...] = (z[:, :GLU_HALF] * _sigmoid(z[:, GLU_HALF:])).astype(BF16)


def _proj_in(x, g, w_main, wg, bg, chunk):
    n = x.shape[0]
    tm = min(PROJ_TM, n)
    nj = w_main.shape[1] // PROJ_TN
    return pl.pallas_call(
        functools.partial(_proj_in_kernel, chunk=chunk),
        grid=(n // tm, nj),
        in_specs=[
            pl.BlockSpec((tm, D_MODEL), lambda i, j: (i, 0)),
            pl.BlockSpec((1, D_MODEL), lambda i, j: (0, 0)),
            pl.BlockSpec((D_MODEL, PROJ_TN), lambda i, j: (0, j)),
            pl.BlockSpec((2 * M_HEADS, D_MODEL), lambda i, j: (0, 0)),
            pl.BlockSpec((2 * M_HEADS, 1), lambda i, j: (0, 0)),
        ],
        out_specs=[
            pl.BlockSpec((tm, PROJ_TN), lambda i, j: (i, jnp.minimum(j, 2))),
            pl.BlockSpec((tm, GLU_HALF), lambda i, j: (i, jnp.maximum(j - 3, 0))),
            pl.BlockSpec((tm // chunk, 2 * M_HEADS, chunk), lambda i, j: (i, 0, 0)),
        ],
        out_shape=[
            jax.ShapeDtypeStruct((n, 2 * QK_W + 2 * M_WIDTH), BF16),
            jax.ShapeDtypeStruct((n, CONV_CH), BF16),
            jax.ShapeDtypeStruct((n // chunk, 2 * M_HEADS, chunk), F32),
        ],
        scratch_shapes=[pltpu.VMEM((tm, D_MODEL), BF16)],
        compiler_params=_cparams(("parallel", "arbitrary"), 48),
        name="proj_in",
    )(x, g, w_main, wg, bg)


def _conv_kernel(u_ref, st_ref, w_ref, cb_ref, lg_ref, lb_ref, cv_ref, ns_ref, e_ref, er_ref, c_ref, *, bb, tt):
    t = pl.program_id(1)
    nt = pl.num_programs(1)
    ext = tt + CONV_PAD

    @pl.when(t == 0)
    def _():
        e_ref[:, 0:CONV_PAD, :] = st_ref[...]

    @pl.when(t > 0)
    def _():
        e_ref[:, 0:CONV_PAD, :] = e_ref[:, tt:ext, :]

    e_ref[:, CONV_PAD:ext, :] = u_ref[...].astype(F32)
    e_ref[:, ext:, :] = jnp.zeros((bb, SUBLANES, CONV_CH), F32)

    base = CONV_PAD - CONV_STATE
    for b in range(bb):
        for r in range(1, SUBLANES):
            er_ref[r - 1] = e_ref[b, r:r + ext, :]

        def row_block(rb, carry):
            r0 = pl.multiple_of(rb * CONV_RB, CONV_RB)
            for lc in range(CONV_CH // CONV_LC):
                ls = slice(lc * CONV_LC, (lc + 1) * CONV_LC)
                acc = jnp.zeros((CONV_RB, CONV_LC), F32) + cb_ref[:, ls]
                for j in range(CONV_WIDTH):
                    off = j + base
                    a, r = (off // SUBLANES) * SUBLANES, off % SUBLANES
                    if r == 0:
                        tap = e_ref[b, pl.ds(r0 + a, CONV_RB), ls]
                    else:
                        tap = er_ref[r - 1, pl.ds(r0 + a, CONV_RB), ls]
                    acc = acc + w_ref[j:j + 1, ls] * tap
                c_ref[:, ls] = acc
            c = c_ref[...]
            mu = jnp.mean(c, axis=-1, keepdims=True)
            cc = c - mu
            var = jnp.mean(cc * cc, axis=-1, keepdims=True)
            cn = cc * lax.rsqrt(var + EPS) * lg_ref[...] + lb_ref[...]
            cv_ref[b, pl.ds(r0, CONV_RB), :] = (cn * _sigmoid(cn)).astype(BF16)
            return carry

        lax.fori_loop(0, tt // CONV_RB, row_block, 0)

    @pl.when(t == nt - 1)
    def _():
        ns_ref[...] = e_ref[:, ext - CONV_STATE:ext, :]


def _conv(u, state_pad, conv_w, conv_b, ln_g, ln_b, bb, tt):
    bsz, t, _ = u.shape
    return pl.pallas_call(
        functools.partial(_conv_kernel, bb=bb, tt=tt),
        grid=(bsz // bb, t // tt),
        in_specs=[
            pl.BlockSpec((bb, tt, CONV_CH), lambda b, s: (b, s, 0)),
            pl.BlockSpec((bb, CONV_PAD, CONV_CH), lambda b, s: (b, 0, 0)),
            pl.BlockSpec((CONV_WIDTH, CONV_CH), lambda b, s: (0, 0)),
            pl.BlockSpec((1, CONV_CH), lambda b, s: (0, 0)),
            pl.BlockSpec((1, CONV_CH), lambda b, s: (0, 0)),
            pl.BlockSpec((1, CONV_CH), lambda b, s: (0, 0)),
        ],
        out_specs=[
            pl.BlockSpec((bb, tt, CONV_CH), lambda b, s: (b, s, 0)),
            pl.BlockSpec((bb, CONV_STATE, CONV_CH), lambda b, s: (b, 0, 0)),
        ],
        out_shape=[
            jax.ShapeDtypeStruct((bsz, t, CONV_CH), BF16),
            jax.ShapeDtypeStruct((bsz, CONV_STATE, CONV_CH), F32),
        ],
        scratch_shapes=[
            pltpu.VMEM((bb, tt + CONV_PAD + SUBLANES, CONV_CH), F32),
            pltpu.VMEM((SUBLANES - 1, tt + CONV_PAD, CONV_CH), F32),
            pltpu.VMEM((CONV_RB, CONV_CH), F32),
        ],
        compiler_params=_cparams(("parallel", "arbitrary"), 40),
        name="conv",
    )(u, state_pad, conv_w, conv_b, ln_g, ln_b)


def _mlstm_kernel(q_ref, k_ref, v_ref, o_ref, gt_ref, c0_ref, n0_ref, m0_ref, ng_ref,
                  hm_ref, cn_ref, nn_ref, mn_ref, c_s, n_s, m_s, *, bb, L):
    ci = pl.program_id(1)
    nc = pl.num_programs(1)

    @pl.when(ci == 0)
    def _():
        c_s[...] = c0_ref[...]
        n_s[...] = n0_ref[...]
        m_s[...] = m0_ref[...]

    rt = lax.broadcasted_iota(jnp.int32, (L, L), 0)
    cs = lax.broadcasted_iota(jnp.int32, (L, L), 1)
    causal = cs <= rt
    lower = causal.astype(F32)
    upper = (rt <= cs).astype(F32)
    eye = (rt == cs).astype(F32)
    nt_dims = (((1,), (1,)), ((), ()))
    scale = M_DK ** -0.5

    for b in range(bb):
        rows = gt_ref[b]
        cum_rows = jnp.dot(rows, upper, precision=HIGHEST, preferred_element_type=F32)
        cum_cols = lax.dot_general(lower, rows, nt_dims, precision=HIGHEST, preferred_element_type=F32)
        id_cols = lax.dot_general(eye, rows, nt_dims, precision=HIGHEST, preferred_element_type=F32)
        for h in range(M_HEADS):
            ig_row = rows[h:h + 1, :]
            ig_col = id_cols[:, h:h + 1]
            b_row = cum_rows[M_HEADS + h:M_HEADS + h + 1, :]
            b_col = cum_cols[:, M_HEADS + h:M_HEADS + h + 1]
            m0 = m_s[b, h:h + 1, 0:1]
            q = q_ref[b, :, h * M_DK:(h + 1) * M_DK]
            k = k_ref[b, :, h * M_DK:(h + 1) * M_DK]
            v = v_ref[b, :, h * M_DV:(h + 1) * M_DV]
            c0 = c_s[b, h]
            n0 = n_s[b, h:h + 1, :]

            g_col = b_col + m0
            dmat = jnp.where(causal, b_col - b_row + ig_row, -jnp.inf)
            m_col = jnp.maximum(g_col, jnp.max(dmat, axis=1, keepdims=True))
            w_state = jnp.exp(g_col - m_col) * scale
            qk = lax.dot_general(q, k, nt_dims, preferred_element_type=F32)
            s = qk * (jnp.exp(dmat - m_col) * scale)
            num = jnp.dot(s.astype(BF16), v, preferred_element_type=F32)
            num = num + w_state * jnp.dot(q, c0.astype(BF16), preferred_element_type=F32)
            qn = jnp.sum(q.astype(F32) * n0, axis=1, keepdims=True)
            den = jnp.sum(s, axis=1, keepdims=True) + w_state * qn
            hh = num / jnp.maximum(jnp.abs(den), jnp.exp(-m_col))

            hh = hh * lax.rsqrt(jnp.mean(hh * hh, axis=-1, keepdims=True) + EPS) * ng_ref[h:h + 1, :]
            og = _sigmoid(o_ref[b, :, h * M_DV:(h + 1) * M_DV].astype(F32))
            hm_ref[b, :, h * M_DV:(h + 1) * M_DV] = (og * hh).astype(BF16)

            b_last = b_row[:, L - 1:L]
            g_last = b_last + m0
            wk_row = b_last - b_row + ig_row
            m_new = jnp.maximum(g_last, jnp.max(wk_row, axis=1, keepdims=True))
            a0 = jnp.exp(g_last - m_new)
            ak_col = jnp.exp(b_last - b_col + ig_col - m_new)
            ks = k.astype(F32) * ak_col
            kv = lax.dot_general(ks.astype(BF16), v, (((0,), (0,)), ((), ())), preferred_element_type=F32)
            c_s[b, h] = a0 * c0 + kv
            n_s[b, h:h + 1, :] = a0 * n0 + jnp.sum(ks, axis=0, keepdims=True)
            m_s[b, h:h + 1, :] = jnp.broadcast_to(m_new, (1, LANES))

    @pl.when(ci == nc - 1)
    def _():
        cn_ref[...] = c_s[...]
        nn_ref[...] = n_s[...]
        mn_ref[...] = m_s[...]


def _mlstm(qkvo, gates, c0, n0, m0, ng, bb, L):
    bsz, t, _ = qkvo.shape
    nc = t // L
    gates = gates.reshape(bsz, nc, 2 * M_HEADS, L)
    st_c = pl.BlockSpec((bb, M_HEADS, M_DK, M_DV), lambda b, c: (b, 0, 0, 0))
    st_n = pl.BlockSpec((bb, M_HEADS, M_DK), lambda b, c: (b, 0, 0))
    return pl.pallas_call(
        functools.partial(_mlstm_kernel, bb=bb, L=L),
        grid=(bsz // bb, nc),
        in_specs=[
            pl.BlockSpec((bb, L, QK_W), lambda b, c: (b, c, 0)),
            pl.BlockSpec((bb, L, QK_W), lambda b, c: (b, c, 1)),
            pl.BlockSpec((bb, L, M_WIDTH), lambda b, c: (b, c, 1)),
            pl.BlockSpec((bb, L, M_WIDTH), lambda b, c: (b, c, 2)),
            pl.BlockSpec((bb, None, 2 * M_HEADS, L), lambda b, c: (b, c, 0, 0)),
            st_c, st_n, st_n,
            pl.BlockSpec((M_HEADS, M_DV), lambda b, c: (0, 0)),
        ],
        out_specs=[
            pl.BlockSpec((bb, L, M_WIDTH), lambda b, c: (b, c, 0)),
            st_c, st_n, st_n,
        ],
        out_shape=[
            jax.ShapeDtypeStruct((bsz, t, M_WIDTH), BF16),
            jax.ShapeDtypeStruct((bsz, M_HEADS, M_DK, M_DV), F32),
            jax.ShapeDtypeStruct((bsz, M_HEADS, M_DK), F32),
            jax.ShapeDtypeStruct((bsz, M_HEADS, LANES), F32),
        ],
        scratch_shapes=[
            pltpu.VMEM((bb, M_HEADS, M_DK, M_DV), F32),
            pltpu.VMEM((bb, M_HEADS, M_DK), F32),
            pltpu.VMEM((bb, M_HEADS, LANES), F32),
        ],
        compiler_params=_cparams(("parallel", "arbitrary"), 40),
        name="mlstm",
    )(qkvo, qkvo, qkvo, qkvo, gates, c0, n0, m0, ng)


def _proj_out_kernel(x_ref, hm_ref, cv_ref, wa_ref, wb_ref, o_ref):
    acc = jnp.dot(hm_ref[...], wa_ref[...], preferred_element_type=F32)
    acc = acc + jnp.dot(cv_ref[...], wb_ref[...], preferred_element_type=F32)
    o_ref[...] = x_ref[...] + acc


def _proj_out(x, hm, cv, wa, wb):
    n = x.shape[0]
    tm = min(TOK_TILE, n)
    return pl.pallas_call(
        _proj_out_kernel,
        grid=(n // tm,),
        in_specs=[
            pl.BlockSpec((tm, D_MODEL), lambda i: (i, 0)),
            pl.BlockSpec((tm, M_WIDTH), lambda i: (i, 0)),
            pl.BlockSpec((tm, CONV_CH), lambda i: (i, 0)),
            pl.BlockSpec((M_WIDTH, D_MODEL), lambda i: (0, 0)),
            pl.BlockSpec((CONV_CH, D_MODEL), lambda i: (0, 0)),
        ],
        out_specs=pl.BlockSpec((tm, D_MODEL), lambda i: (i, 0)),
        out_shape=jax.ShapeDtypeStruct((n, D_MODEL), F32),
        compiler_params=_cparams(("parallel",), 48),
        name="proj_out",
    )(x, hm, cv, wa, wb)


def _mem_kv_kernel(m_ref, g_ref, w_ref, k_ref, v_ref):
    mn = _rms(m_ref[...], g_ref[...]).astype(BF16)
    z = jnp.dot(mn, w_ref[...], preferred_element_type=F32)
    k_ref[...] = z[:, :CA_W]
    v_ref[...] = z[:, CA_W:]


def _mem_kv(mem, g, wkv):
    n = mem.shape[0]
    return pl.pallas_call(
        _mem_kv_kernel,
        grid=(1,),
        in_specs=[
            pl.BlockSpec((n, D_MODEL), lambda i: (0, 0)),
            pl.BlockSpec((1, D_MODEL), lambda i: (0, 0)),
            pl.BlockSpec((D_MODEL, 2 * CA_W), lambda i: (0, 0)),
        ],
        out_specs=[pl.BlockSpec((n, CA_W), lambda i: (0, 0))] * 2,
        out_shape=[jax.ShapeDtypeStruct((n, CA_W), F32)] * 2,
        compiler_params=_cparams(("arbitrary",), 40),
        name="mem_kv",
    )(mem, g, wkv)


def _attn_router_kernel(x_ref, gca_ref, wq_ref, k_ref, v_ref, wo_ref, gmoe_ref, rw_ref, rb_ref, cnt0_ref,
                        hn_in_ref, x2_ref, hn_ref, te_ref, tg_ref, tr_ref, cnt_ref, o_s, cnt_s, *, bb, tt):
    del hn_in_ref
    rows = bb * tt

    @pl.when((pl.program_id(0) == 0) & (pl.program_id(1) == 0))
    def _():
        cnt_s[...] = cnt0_ref[...]

    x = x_ref[...].reshape(rows, D_MODEL)
    h = _rms(x, gca_ref[...]).astype(BF16)
    q = jnp.dot(h, wq_ref[...], preferred_element_type=F32) * (CA_DH ** -0.5)
    q = q.astype(BF16)
    for b in range(bb):
        for hd in range(CA_HEADS):
            cols = slice(hd * CA_DH, (hd + 1) * CA_DH)
            qb = q[b * tt:(b + 1) * tt, cols]
            s = lax.dot_general(qb, k_ref[b, :, cols], (((1,), (1,)), ((), ())), preferred_element_type=F32)
            e = jnp.exp(s - jnp.max(s, axis=-1, keepdims=True))
            p = (e / jnp.sum(e, axis=-1, keepdims=True)).astype(BF16)
            o_s[b * tt:(b + 1) * tt, cols] = jnp.dot(p, v_ref[b, :, cols], preferred_element_type=F32).astype(BF16)
    x2 = x + jnp.dot(o_s[...], wo_ref[...], preferred_element_type=F32)
    x2_ref[...] = x2.reshape(bb, tt, D_MODEL)

    hn = _rms(x2, gmoe_ref[...])
    hn_bf = hn.astype(BF16)
    hn_ref[...] = _pack_pairs(hn[:, :HALF_D], hn[:, HALF_D:])
    logits = jnp.dot(hn_bf, rw_ref[...], preferred_element_type=F32) + rb_ref[...]
    lane = lax.broadcasted_iota(jnp.int32, (rows, LANES), 1)
    work = logits
    sel = []
    vals = []
    for _ in range(TOP_K):
        mx = jnp.max(work, axis=-1, keepdims=True)
        idx = jnp.min(jnp.where(work == mx, lane, LANES), axis=-1, keepdims=True)
        hit = lane == idx
        sel.append((idx, hit))
        vals.append(mx)
        work = jnp.where(hit, -jnp.inf, work)
    ex = [jnp.exp(vv - vals[0]) for vv in vals]
    tot = ex[0] + ex[1] + ex[2] + ex[3]
    assigned = jnp.zeros((rows, LANES), F32)
    for _, hit in sel:
        assigned = assigned + hit.astype(F32)
    rt = lax.broadcasted_iota(jnp.int32, (rows, rows), 0)
    cs = lax.broadcasted_iota(jnp.int32, (rows, rows), 1)
    strict = (cs < rt).astype(BF16)
    before = jnp.dot(strict, assigned.astype(BF16), preferred_element_type=F32) + cnt_s[...]
    te = jnp.zeros((rows, LANES), jnp.int32)
    tg = jnp.zeros((rows, LANES), F32)
    tr = jnp.zeros((rows, LANES), F32)
    for kk, (idx, hit) in enumerate(sel):
        rank = jnp.sum(jnp.where(hit, before, 0.0), axis=-1, keepdims=True)
        te = jnp.where(lane == kk, idx, te)
        tg = jnp.where(lane == kk, ex[kk] / tot, tg)
        tr = jnp.where(lane == kk, rank, tr)
    te_ref[...] = te
    tg_ref[...] = tg
    tr_ref[...] = tr.astype(jnp.int32)
    cnt_s[...] = cnt_s[...] + jnp.sum(assigned, axis=0, keepdims=True)
    cnt_ref[...] = cnt_s[...]


def _attn_router(x1, mem_k, mem_v, cnt0, hn_all, row0, p, bb, tt):
    bsz, t, _ = x1.shape
    rows = bb * tt
    n = bsz * t
    nb, nt = bsz // bb, t // tt
    blk0 = row0 // rows
    tok = lambda b, s: (b * nt + s, 0)
    const = lambda b, s: (0, 0)
    grid_spec = dict(
        grid=(nb, nt),
        in_specs=[
            pl.BlockSpec((bb, tt, D_MODEL), lambda b, s: (b, s, 0)),
            pl.BlockSpec((1, D_MODEL), const),
            pl.BlockSpec((D_MODEL, CA_W), const),
            pl.BlockSpec((bb, N_MEM, CA_W), lambda b, s: (b, 0, 0)),
            pl.BlockSpec((bb, N_MEM, CA_W), lambda b, s: (b, 0, 0)),
            pl.BlockSpec((CA_W, D_MODEL), const),
            pl.BlockSpec((1, D_MODEL), const),
            pl.BlockSpec((D_MODEL, LANES), const),
            pl.BlockSpec((1, LANES), const),
            pl.BlockSpec((1, LANES), const),
            pl.BlockSpec(memory_space=pl.ANY),
        ],
        out_specs=[
            pl.BlockSpec((bb, tt, D_MODEL), lambda b, s: (b, s, 0)),
            pl.BlockSpec((rows, HALF_D), lambda b, s: (blk0 + b * nt + s, 0)),
            pl.BlockSpec((rows, LANES), tok),
            pl.BlockSpec((rows, LANES), tok),
            pl.BlockSpec((rows, LANES), tok),
            pl.BlockSpec((1, LANES), const),
        ],
    )
    return pl.pallas_call(
        functools.partial(_attn_router_kernel, bb=bb, tt=tt),
        out_shape=[
            jax.ShapeDtypeStruct((bsz, t, D_MODEL), F32),
            jax.ShapeDtypeStruct(hn_all.shape, jnp.uint32),
            jax.ShapeDtypeStruct((n, LANES), jnp.int32),
            jax.ShapeDtypeStruct((n, LANES), F32),
            jax.ShapeDtypeStruct((n, LANES), jnp.int32),
            jax.ShapeDtypeStruct((1, LANES), F32),
        ],
        scratch_shapes=[pltpu.VMEM((rows, CA_W), BF16), pltpu.VMEM((1, LANES), F32)],
        input_output_aliases={10: 1},
        compiler_params=_cparams(("arbitrary", "arbitrary"), 48),
        name="attn_router",
        **grid_spec,
    )(x1, p["g_ca"], p["wq"], mem_k, mem_v, p["wo"], p["g_moe"], p["router_w"], p["router_b"], cnt0, hn_all)


def _expert_changed(te_ref):
    t = pl.program_id(1)
    return (t == 0) | (te_ref[t] != te_ref[jnp.maximum(t - 1, 0)])


def _moe_up_kernel(te_ref, nu_ref, x_ref, w1g_ref, w1l_ref, b1g_ref, b1l_ref, act_ref, wg_s, wl_s):
    @pl.when(_expert_changed(te_ref))
    def _():
        wg_s[...] = w1g_ref[0].astype(BF16)
        wl_s[...] = w1l_ref[0].astype(BF16)

    @pl.when(pl.program_id(1) < nu_ref[0])
    def _():
        lo, hi = _unpack_pairs(x_ref[...])
        x = jnp.concatenate([lo.astype(BF16), hi.astype(BF16)], axis=1)
        glu = jnp.dot(x, wg_s[...], preferred_element_type=F32) + b1g_ref[0]
        lin = jnp.dot(x, wl_s[...], preferred_element_type=F32) + b1l_ref[0]
        glu = jnp.minimum(glu, SWIGLU_LIMIT)
        lin = jnp.clip(lin, -SWIGLU_LIMIT, SWIGLU_LIMIT)
        act_ref[...] = (glu * _sigmoid(SWIGLU_ALPHA * glu) * (lin + 1.0)).astype(BF16)


def _moe_down_kernel(te_ref, nu_ref, a_ref, w2_ref, b2_ref, y_ref, w2_s):
    @pl.when(_expert_changed(te_ref))
    def _():
        w2_s[...] = w2_ref[0].astype(BF16)

    @pl.when(pl.program_id(1) < nu_ref[0])
    def _():
        y = jnp.dot(a_ref[...], w2_s[...], preferred_element_type=F32) + b2_ref[0]
        y_ref[...] = _pack_pairs(y[:, :MOE_TN // 2], y[:, MOE_TN // 2:])


def _moe(tile_e, n_used, x_sorted, w1, b1, w2, b2):
    r = x_sorted.shape[0]
    n_tiles = r // MOE_TM
    nf = D_FF // MOE_TF
    nn = D_MODEL // MOE_TN

    def row(c, t, te, nu):
        return jnp.maximum(jnp.minimum(t, nu[0] - 1), 0)

    act = pl.pallas_call(
        _moe_up_kernel,
        grid_spec=pltpu.PrefetchScalarGridSpec(
            num_scalar_prefetch=2,
            grid=(nf, n_tiles),
            in_specs=[
                pl.BlockSpec((MOE_TM, HALF_D), lambda c, t, te, nu: (row(c, t, te, nu), 0)),
                pl.BlockSpec((1, D_MODEL, MOE_TF), lambda c, t, te, nu: (te[t], 0, c)),
                pl.BlockSpec((1, D_MODEL, MOE_TF), lambda c, t, te, nu: (te[t], 0, nf + c)),
                pl.BlockSpec((1, 1, MOE_TF), lambda c, t, te, nu: (te[t], 0, c)),
                pl.BlockSpec((1, 1, MOE_TF), lambda c, t, te, nu: (te[t], 0, nf + c)),
            ],
            out_specs=pl.BlockSpec((MOE_TM, MOE_TF), lambda c, t, te, nu: (row(c, t, te, nu), c)),
            scratch_shapes=[pltpu.VMEM((D_MODEL, MOE_TF), BF16), pltpu.VMEM((D_MODEL, MOE_TF), BF16)],
        ),
        out_shape=jax.ShapeDtypeStruct((r, D_FF), BF16),
        compiler_params=_cparams(("arbitrary", "arbitrary"), 48),
        name="moe_up",
    )(tile_e, n_used, x_sorted, w1, w1, b1, b1)

    return pl.pallas_call(
        _moe_down_kernel,
        grid_spec=pltpu.PrefetchScalarGridSpec(
            num_scalar_prefetch=2,
            grid=(nn, n_tiles),
            in_specs=[
                pl.BlockSpec((MOE_TM, D_FF), lambda c, t, te, nu: (row(c, t, te, nu), 0)),
                pl.BlockSpec((1, D_FF, MOE_TN), lambda c, t, te, nu: (te[t], 0, c)),
                pl.BlockSpec((1, 1, MOE_TN), lambda c, t, te, nu: (te[t], 0, c)),
            ],
            out_specs=pl.BlockSpec((MOE_TM, MOE_TN // 2), lambda c, t, te, nu: (row(c, t, te, nu), c)),
            scratch_shapes=[pltpu.VMEM((D_FF, MOE_TN), BF16)],
        ),
        out_shape=jax.ShapeDtypeStruct((r, HALF_D), jnp.uint32),
        compiler_params=_cparams(("arbitrary", "arbitrary"), 48),
        name="moe_down",
    )(tile_e, n_used, act, w2, b2)


def _combine_kernel(x_ref, y0_ref, y1_ref, y2_ref, y3_ref, tg_ref, g_ref, o_ref):
    tg = tg_ref[...]
    acc = x_ref[...]
    half = MOE_TN // 2
    for kk, y_ref in enumerate((y0_ref, y1_ref, y2_ref, y3_ref)):
        cols = []
        for c in range(D_MODEL // MOE_TN):
            cols += list(_unpack_pairs(y_ref[0, :, c * half:(c + 1) * half]))
        acc = acc + jnp.concatenate(cols, axis=1) * tg[:, kk:kk + 1]
    o_ref[...] = _rms(acc, g_ref[...])


def _combine(x2, yg, tg, g, row0):
    n = x2.shape[0]
    tm = min(TOK_TILE, n)
    blk0 = row0 // tm
    yspec = [pl.BlockSpec((1, tm, HALF_D), functools.partial(lambda i, kk: (kk, blk0 + i, 0), kk=kk))
             for kk in range(TOP_K)]
    return pl.pallas_call(
        _combine_kernel,
        grid=(n // tm,),
        in_specs=[pl.BlockSpec((tm, D_MODEL), lambda i: (i, 0))] + yspec + [
            pl.BlockSpec((tm, LANES), lambda i: (i, 0)),
            pl.BlockSpec((1, D_MODEL), lambda i: (0, 0)),
        ],
        out_specs=pl.BlockSpec((tm, D_MODEL), lambda i: (i, 0)),
        out_shape=jax.ShapeDtypeStruct((n, D_MODEL), F32),
        compiler_params=_cparams(("parallel",), 48),
        name="combine",
    )(x2, yg, yg, yg, yg, tg, g)


def _group_tiles(bsz, t):
    tt = min(t, TOK_TILE)
    bb = max(1, min(bsz, TOK_TILE // tt))
    return bb, tt


def _layer_group(x, c0, n0, m0, conv_state, mem_k, mem_v, cnt0, hn_all, row0, p):
    bsz, t, _ = x.shape
    n = bsz * t
    L = min(MLSTM_CHUNK, t)
    qkvo, u, gates = _proj_in(x.reshape(n, D_MODEL), p["g_mix"], p["w_main"], p["wg"], p["bg"], L)

    ctt = min(CONV_TT, t)
    cbb = max(1, min(bsz, CONV_TT // ctt))
    state_pad = jnp.pad(conv_state, ((0, 0), (CONV_PAD - CONV_STATE, 0), (0, 0)))
    cv, new_conv = _conv(u.reshape(bsz, t, CONV_CH), state_pad, p["conv_w"], p["conv_b"],
                         p["ln_g"], p["ln_b"], cbb, ctt)

    mbb = max(1, min(bsz, LANES // L)) if L < LANES else 1
    m0b = jnp.broadcast_to(m0[:, :, None], (bsz, M_HEADS, LANES))
    hm, c1, n1, m1 = _mlstm(qkvo.reshape(bsz, t, -1), gates, c0, n0, m0b, p["mh_g"], mbb, L)

    x1 = _proj_out(x.reshape(n, D_MODEL), hm.reshape(n, M_WIDTH), cv.reshape(n, CONV_CH), p["wo_a"], p["wo_b"])

    abb, att = _group_tiles(bsz, t)
    x2, hn_all, te, tg, tr, cnt = _attn_router(x1.reshape(bsz, t, D_MODEL), mem_k, mem_v, cnt0, hn_all, row0,
                                               p, abb, att)
    return x2.reshape(n, D_MODEL), hn_all, te, tg, tr, cnt, (c1, n1, m1[:, :, 0], new_conv)


def kernel(x_prompt, x_sample, mem_prompt, state_mlstm_C, state_mlstm_n, state_mlstm_m, state_conv, cache_mem_k, cache_mem_v, norm_mix_g, w_in, b_gates, mh_norm_g, conv_w, conv_b, conv_ln_g, conv_ln_b, w_out, norm_ca_g, norm_mem_g, ca_wq, ca_wk, ca_wv, ca_wo, norm_moe_g, router_w, router_b, moe_w1, moe_b1, moe_w2, moe_b2, final_norm_g):
    assert w_in.shape[0] == 1, "single layer"
    bp, tp, _ = x_prompt.shape
    bs, ts, _ = x_sample.shape
    n_p, n_s = bp * tp, bs * ts
    n_all = n_p + n_s

    wi = w_in[0]
    o0 = 0
    parts = []
    for width in (QK_W, QK_W, M_WIDTH, M_WIDTH, M_HEADS, M_HEADS, CONV_CH, CONV_CH):
        parts.append(wi[:, o0:o0 + width])
        o0 += width
    w_q, w_k, w_v, w_o, w_gi, w_gf, w_ua, w_ug = parts
    glu_cols = []
    for c in range(CONV_CH // GLU_HALF):
        glu_cols += [w_ua[:, c * GLU_HALF:(c + 1) * GLU_HALF], w_ug[:, c * GLU_HALF:(c + 1) * GLU_HALF]]
    p = {
        "g_mix": norm_mix_g[0][None, :],
        "w_main": jnp.concatenate([w_q, w_k, w_v, w_o] + glu_cols, axis=1).astype(BF16),
        "wg": jnp.concatenate([w_gi, w_gf], axis=1).T.astype(BF16),
        "bg": b_gates[0][:, None],
        "mh_g": mh_norm_g[0],
        "conv_w": conv_w[0],
        "conv_b": conv_b[0][None, :],
        "ln_g": conv_ln_g[0][None, :],
        "ln_b": conv_ln_b[0][None, :],
        "wo_a": w_out[0][:M_WIDTH].astype(BF16),
        "wo_b": w_out[0][M_WIDTH:].astype(BF16),
        "g_ca": norm_ca_g[0][None, :],
        "wq": ca_wq[0].astype(BF16),
        "wo": ca_wo[0].astype(BF16),
        "g_moe": norm_moe_g[0][None, :],
        "router_w": jnp.pad(router_w[0], ((0, 0), (0, LANES - N_EXPERTS))).astype(BF16),
        "router_b": jnp.concatenate([router_b[0], jnp.full((LANES - N_EXPERTS,), NEG_BIG, F32)])[None, :],
    }

    wkv = jnp.concatenate([ca_wk[0], ca_wv[0]], axis=1).astype(BF16)
    mk, mv = _mem_kv(mem_prompt.reshape(bp * N_MEM, D_MODEL), norm_mem_g[0][None, :], wkv)
    mk = mk.reshape(bp, N_MEM, CA_W)
    mv = mv.reshape(bp, N_MEM, CA_W)
    hn_all = jnp.zeros((n_all, HALF_D), jnp.uint32)
    cnt0 = jnp.zeros((1, LANES), F32)
    zc = jnp.zeros((bp, M_HEADS, M_DK, M_DV), F32)
    zn = jnp.zeros((bp, M_HEADS, M_DK), F32)
    zm = jnp.zeros((bp, M_HEADS), F32)
    zconv = jnp.zeros((bp, CONV_STATE, CONV_CH), F32)
    x2p, hn_all, te_p, tg_p, tr_p, cnt1, st_p = _layer_group(
        x_prompt, zc, zn, zm, zconv, mk.astype(BF16), mv.astype(BF16), cnt0, hn_all, 0, p)

    x2s, hn_all, te_s, tg_s, tr_s, cnt2, st_s = _layer_group(
        x_sample, state_mlstm_C[0], state_mlstm_n[0], state_mlstm_m[0], state_conv[0],
        cache_mem_k[0].reshape(bs, N_MEM, CA_W).astype(BF16), cache_mem_v[0].reshape(bs, N_MEM, CA_W).astype(BF16),
        cnt1, hn_all, n_p, p)

    te = jnp.concatenate([te_p[:, :TOP_K], te_s[:, :TOP_K]], axis=0)
    tr = jnp.concatenate([tr_p[:, :TOP_K], tr_s[:, :TOP_K]], axis=0)
    counts = cnt2[0, :N_EXPERTS].astype(jnp.int32)
    tiles_per_e = (counts + MOE_TM - 1) // MOE_TM
    tile_end = jnp.cumsum(tiles_per_e)
    row_start = (tile_end - tiles_per_e) * MOE_TM
    n_tiles = -(-(n_all * TOP_K) // MOE_TM) + N_EXPERTS
    n_used = tile_end[-1:]
    tile_ids = jnp.minimum(jnp.arange(n_tiles, dtype=jnp.int32), n_used[0] - 1)
    tile_e = jnp.minimum(jnp.sum(tile_ids[:, None] >= tile_end[None, :], axis=1), N_EXPERTS - 1).astype(jnp.int32)
    dest = row_start[te] + tr
    tok_ids = jnp.broadcast_to(jnp.arange(n_all, dtype=jnp.int32)[:, None], (n_all, TOP_K))
    row_tok = (jnp.arange(n_tiles * MOE_TM, dtype=jnp.int32) % n_all).at[dest.reshape(-1)].set(
        tok_ids.reshape(-1), unique_indices=True, mode="promise_in_bounds")
    x_sorted = hn_all.at[row_tok].get(mode="promise_in_bounds")

    y_sorted = _moe(tile_e, n_used.astype(jnp.int32), x_sorted, moe_w1[0], moe_b1[0][:, None, :], moe_w2[0],
                    moe_b2[0][:, None, :])

    yg = y_sorted.at[dest.T.reshape(-1)].get(mode="promise_in_bounds").reshape(TOP_K, n_all, HALF_D)
    fg = final_norm_g[None, :]
    y_prompt = _combine(x2p, yg, tg_p, fg, 0).reshape(bp, tp, D_MODEL)
    y_sample = _combine(x2s, yg, tg_s, fg, n_p).reshape(bs, ts, D_MODEL)

    c1, n1, m1, conv1 = st_p
    c2, n2, m2, conv2 = st_s
    mk4 = mk.reshape(1, bp, N_MEM, CA_HEADS, CA_DH)
    mv4 = mv.reshape(1, bp, N_MEM, CA_HEADS, CA_DH)
    return (y_prompt, y_sample, c1[None], n1[None], m1[None], conv1[None], mk4, mv4,
            c2[None], n2[None], m2[None], conv2[None])
```

```python
import functools

import jax
import jax.numpy as jnp
from jax import lax
from jax.experimental import pallas as pl
from jax.experimental.pallas import tpu as pltpu
from jax.experimental.pallas import tpu_sc as plsc

F32 = jnp.float32
BF16 = jnp.bfloat16
HIGHEST = lax.Precision.HIGHEST

D_MODEL = 2048
HALF_D = D_MODEL // 2
M_HEADS = 4
M_DV = 256
M_DK = 128
QK_W = M_HEADS * M_DK
M_WIDTH = M_HEADS * M_DV
CONV_CH = 1024
CONV_WIDTH = 31
CONV_STATE = CONV_WIDTH - 1
CONV_PAD = 32
N_MEM = 256
CA_HEADS = 4
CA_DH = 128
CA_W = CA_HEADS * CA_DH
N_EXPERTS = 32
TOP_K = 4
D_FF = 2048
SWIGLU_ALPHA = 1.702
SWIGLU_LIMIT = 7.0
EPS = 1e-6
LANES = 128
SUBLANES = 8
NEG_BIG = -1e30

PROJ_TM = 1024
PROJ_TN = 1024
GLU_HALF = PROJ_TN // 2
TOK_TILE = 512
MOE_TM = 512
MOE_TF = 512
MOE_TN = 1024
MLSTM_CHUNK = 256
CONV_TT = 256
CONV_RB = 32
CONV_LC = 512
SC_WIN = 32


def _cparams(sem, vmem_mb):
    return pltpu.CompilerParams(dimension_semantics=sem, vmem_limit_bytes=vmem_mb << 20)


def _log_sigmoid(x):
    return jnp.minimum(x, 0.0) - jnp.log1p(jnp.exp(-jnp.abs(x)))


def _sigmoid(x):
    return 1.0 / (1.0 + jnp.exp(-x))


def _pack_pairs(lo, hi):
    ulo = lax.bitcast_convert_type(lo.astype(BF16).astype(F32), jnp.uint32)
    uhi = lax.bitcast_convert_type(hi.astype(BF16).astype(F32), jnp.uint32)
    return (ulo >> 16) | uhi


def _unpack_pairs(w):
    lo = lax.bitcast_convert_type(w << 16, F32)
    hi = lax.bitcast_convert_type(w & jnp.uint32(0xFFFF0000), F32)
    return lo, hi


def _rms(x, g):
    ms = jnp.mean(x * x, axis=-1, keepdims=True)
    return x * lax.rsqrt(ms + EPS) * g


def _proj_in_kernel(x_ref, g_ref, w_ref, wg_ref, bg_ref, qkvo_ref, u_ref, gates_ref, h_ref, *, chunk):
    j = pl.program_id(1)
    tm = x_ref.shape[0]

    @pl.when(j == 0)
    def _():
        h = _rms(x_ref[...], g_ref[...]).astype(BF16)
        h_ref[...] = h
        gt = lax.dot_general(wg_ref[...], h, (((1,), (1,)), ((), ())), preferred_element_type=F32)
        gt = gt + bg_ref[...]
        row = lax.broadcasted_iota(jnp.int32, gt.shape, 0)
        gt = jnp.where(row < M_HEADS, gt, _log_sigmoid(gt))
        for c in range(tm // chunk):
            gates_ref[c] = gt[:, c * chunk:(c + 1) * chunk]

    z = jnp.dot(h_ref[...], w_ref[...], preferred_element_type=F32)

    @pl.when(j < 3)
    def _():
        qkvo_ref[...] = z.astype(BF16)

    @pl.when(j >= 3)
    def _():
        u_ref[...] = (z[:, :GLU_HALF] * _sigmoid(z[:, GLU_HALF:])).astype(BF16)


def _proj_in(x, g, w_main, wg, bg, chunk):
    n = x.shape[0]
    tm = min(PROJ_TM, n)
    nj = w_main.shape[1] // PROJ_TN
    return pl.pallas_call(
        functools.partial(_proj_in_kernel, chunk=chunk),
        grid=(n // tm, nj),
        in_specs=[
            pl.BlockSpec((tm, D_MODEL), lambda i, j: (i, 0)),
            pl.BlockSpec((1, D_MODEL), lambda i, j: (0, 0)),
            pl.BlockSpec((D_MODEL, PROJ_TN), lambda i, j: (0, j)),
            pl.BlockSpec((2 * M_HEADS, D_MODEL), lambda i, j: (0, 0)),
            pl.BlockSpec((2 * M_HEADS, 1), lambda i, j: (0, 0)),
        ],
        out_specs=[
            pl.BlockSpec((tm, PROJ_TN), lambda i, j: (i, jnp.minimum(j, 2))),
            pl.BlockSpec((tm, GLU_HALF), lambda i, j: (i, jnp.maximum(j - 3, 0))),
            pl.BlockSpec((tm // chunk, 2 * M_HEADS, chunk), lambda i, j: (i, 0, 0)),
        ],
        out_shape=[
            jax.ShapeDtypeStruct((n, 2 * QK_W + 2 * M_WIDTH), BF16),
            jax.ShapeDtypeStruct((n, CONV_CH), BF16),
            jax.ShapeDtypeStruct((n // chunk, 2 * M_HEADS, chunk), F32),
        ],
        scratch_shapes=[pltpu.VMEM((tm, D_MODEL), BF16)],
        compiler_params=_cparams(("parallel", "arbitrary"), 48),
        name="proj_in",
    )(x, g, w_main, wg, bg)


def _conv_kernel(u_ref, st_ref, w_ref, cb_ref, lg_ref, lb_ref, cv_ref, ns_ref, e_ref, er_ref, c_ref, *, bb, tt):
    t = pl.program_id(1)
    nt = pl.num_programs(1)
    ext = tt + CONV_PAD

    @pl.when(t == 0)
    def _():
        e_ref[:, 0:CONV_PAD, :] = st_ref[...]

    @pl.when(t > 0)
    def _():
        e_ref[:, 0:CONV_PAD, :] = e_ref[:, tt:ext, :]

    e_ref[:, CONV_PAD:ext, :] = u_ref[...].astype(F32)
    e_ref[:, ext:, :] = jnp.zeros((bb, SUBLANES, CONV_CH), F32)

    base = CONV_PAD - CONV_STATE
    for b in range(bb):
        for r in range(1, SUBLANES):
            er_ref[r - 1] = e_ref[b, r:r + ext, :]

        def row_block(rb, carry):
            r0 = pl.multiple_of(rb * CONV_RB, CONV_RB)
            for lc in range(CONV_CH // CONV_LC):
                ls = slice(lc * CONV_LC, (lc + 1) * CONV_LC)
                acc = jnp.zeros((CONV_RB, CONV_LC), F32) + cb_ref[:, ls]
                for j in range(CONV_WIDTH):
                    off = j + base
                    a, r = (off // SUBLANES) * SUBLANES, off % SUBLANES
                    if r == 0:
                        tap = e_ref[b, pl.ds(r0 + a, CONV_RB), ls]
                    else:
                        tap = er_ref[r - 1, pl.ds(r0 + a, CONV_RB), ls]
                    acc = acc + w_ref[j:j + 1, ls] * tap
                c_ref[:, ls] = acc
            c = c_ref[...]
            mu = jnp.mean(c, axis=-1, keepdims=True)
            cc = c - mu
            var = jnp.mean(cc * cc, axis=-1, keepdims=True)
            cn = cc * lax.rsqrt(var + EPS) * lg_ref[...] + lb_ref[...]
            cv_ref[b, pl.ds(r0, CONV_RB), :] = (cn * _sigmoid(cn)).astype(BF16)
            return carry

        lax.fori_loop(0, tt // CONV_RB, row_block, 0)

    @pl.when(t == nt - 1)
    def _():
        ns_ref[...] = e_ref[:, ext - CONV_STATE:ext, :]


def _conv(u, state_pad, conv_w, conv_b, ln_g, ln_b, bb, tt):
    bsz, t, _ = u.shape
    return pl.pallas_call(
        functools.partial(_conv_kernel, bb=bb, tt=tt),
        grid=(bsz // bb, t // tt),
        in_specs=[
            pl.BlockSpec((bb, tt, CONV_CH), lambda b, s: (b, s, 0)),
            pl.BlockSpec((bb, CONV_PAD, CONV_CH), lambda b, s: (b, 0, 0)),
            pl.BlockSpec((CONV_WIDTH, CONV_CH), lambda b, s: (0, 0)),
            pl.BlockSpec((1, CONV_CH), lambda b, s: (0, 0)),
            pl.BlockSpec((1, CONV_CH), lambda b, s: (0, 0)),
            pl.BlockSpec((1, CONV_CH), lambda b, s: (0, 0)),
        ],
        out_specs=[
            pl.BlockSpec((bb, tt, CONV_CH), lambda b, s: (b, s, 0)),
            pl.BlockSpec((bb, CONV_STATE, CONV_CH), lambda b, s: (b, 0, 0)),
        ],
        out_shape=[
            jax.ShapeDtypeStruct((bsz, t, CONV_CH), BF16),
            jax.ShapeDtypeStruct((bsz, CONV_STATE, CONV_CH), F32),
        ],
        scratch_shapes=[
            pltpu.VMEM((bb, tt + CONV_PAD + SUBLANES, CONV_CH), F32),
            pltpu.VMEM((SUBLANES - 1, tt + CONV_PAD, CONV_CH), F32),
            pltpu.VMEM((CONV_RB, CONV_CH), F32),
        ],
        compiler_params=_cparams(("parallel", "arbitrary"), 40),
        name="conv",
    )(u, state_pad, conv_w, conv_b, ln_g, ln_b)


def _mlstm_kernel(q_ref, k_ref, v_ref, o_ref, gt_ref, c0_ref, n0_ref, m0_ref, ng_ref,
                  hm_ref, cn_ref, nn_ref, mn_ref, c_s, n_s, m_s, *, bb, L):
    ci = pl.program_id(1)
    nc = pl.num_programs(1)

    @pl.when(ci == 0)
    def _():
        c_s[...] = c0_ref[...]
        n_s[...] = n0_ref[...]
        m_s[...] = m0_ref[...]

    rt = lax.broadcasted_iota(jnp.int32, (L, L), 0)
    cs = lax.broadcasted_iota(jnp.int32, (L, L), 1)
    causal = cs <= rt
    lower = causal.astype(F32)
    upper = (rt <= cs).astype(F32)
    eye = (rt == cs).astype(F32)
    nt_dims = (((1,), (1,)), ((), ()))
    scale = M_DK ** -0.5

    for b in range(bb):
        rows = gt_ref[b]
        cum_rows = jnp.dot(rows, upper, precision=HIGHEST, preferred_element_type=F32)
        cum_cols = lax.dot_general(lower, rows, nt_dims, precision=HIGHEST, preferred_element_type=F32)
        id_cols = lax.dot_general(eye, rows, nt_dims, precision=HIGHEST, preferred_element_type=F32)
        for h in range(M_HEADS):
            ig_row = rows[h:h + 1, :]
            ig_col = id_cols[:, h:h + 1]
            b_row = cum_rows[M_HEADS + h:M_HEADS + h + 1, :]
            b_col = cum_cols[:, M_HEADS + h:M_HEADS + h + 1]
            m0 = m_s[b, h:h + 1, 0:1]
            q = q_ref[b, :, h * M_DK:(h + 1) * M_DK]
            k = k_ref[b, :, h * M_DK:(h + 1) * M_DK]
            v = v_ref[b, :, h * M_DV:(h + 1) * M_DV]
            c0 = c_s[b, h]
            n0 = n_s[b, h:h + 1, :]

            g_col = b_col + m0
            dmat = jnp.where(causal, b_col - b_row + ig_row, -jnp.inf)
            m_col = jnp.maximum(g_col, jnp.max(dmat, axis=1, keepdims=True))
            w_state = jnp.exp(g_col - m_col) * scale
            qk = lax.dot_general(q, k, nt_dims, preferred_element_type=F32)
            s = qk * (jnp.exp(dmat - m_col) * scale)
            num = jnp.dot(s.astype(BF16), v, preferred_element_type=F32)
            num = num + w_state * jnp.dot(q, c0.astype(BF16), preferred_element_type=F32)
            qn = jnp.sum(q.astype(F32) * n0, axis=1, keepdims=True)
            den = jnp.sum(s, axis=1, keepdims=True) + w_state * qn
            hh = num / jnp.maximum(jnp.abs(den), jnp.exp(-m_col))

            hh = hh * lax.rsqrt(jnp.mean(hh * hh, axis=-1, keepdims=True) + EPS) * ng_ref[h:h + 1, :]
            og = _sigmoid(o_ref[b, :, h * M_DV:(h + 1) * M_DV].astype(F32))
            hm_ref[b, :, h * M_DV:(h + 1) * M_DV] = (og * hh).astype(BF16)

            b_last = b_row[:, L - 1:L]
            g_last = b_last + m0
            wk_row = b_last - b_row + ig_row
            m_new = jnp.maximum(g_last, jnp.max(wk_row, axis=1, keepdims=True))
            a0 = jnp.exp(g_last - m_new)
            ak_col = jnp.exp(b_last - b_col + ig_col - m_new)
            ks = k.astype(F32) * ak_col
            kv = lax.dot_general(ks.astype(BF16), v, (((0,), (0,)), ((), ())), preferred_element_type=F32)
            c_s[b, h] = a0 * c0 + kv
            n_s[b, h:h + 1, :] = a0 * n0 + jnp.sum(ks, axis=0, keepdims=True)
            m_s[b, h:h + 1, :] = jnp.broadcast_to(m_new, (1, LANES))

    @pl.when(ci == nc - 1)
    def _():
        cn_ref[...] = c_s[...]
        nn_ref[...] = n_s[...]
        mn_ref[...] = m_s[...]


def _mlstm(qkvo, gates, c0, n0, m0, ng, bb, L):
    bsz, t, _ = qkvo.shape
    nc = t // L
    gates = gates.reshape(bsz, nc, 2 * M_HEADS, L)
    st_c = pl.BlockSpec((bb, M_HEADS, M_DK, M_DV), lambda b, c: (b, 0, 0, 0))
    st_n = pl.BlockSpec((bb, M_HEADS, M_DK), lambda b, c: (b, 0, 0))
    return pl.pallas_call(
        functools.partial(_mlstm_kernel, bb=bb, L=L),
        grid=(bsz // bb, nc),
        in_specs=[
            pl.BlockSpec((bb, L, QK_W), lambda b, c: (b, c, 0)),
            pl.BlockSpec((bb, L, QK_W), lambda b, c: (b, c, 1)),
            pl.BlockSpec((bb, L, M_WIDTH), lambda b, c: (b, c, 1)),
            pl.BlockSpec((bb, L, M_WIDTH), lambda b, c: (b, c, 2)),
            pl.BlockSpec((bb, None, 2 * M_HEADS, L), lambda b, c: (b, c, 0, 0)),
            st_c, st_n, st_n,
            pl.BlockSpec((M_HEADS, M_DV), lambda b, c: (0, 0)),
        ],
        out_specs=[
            pl.BlockSpec((bb, L, M_WIDTH), lambda b, c: (b, c, 0)),
            st_c, st_n, st_n,
        ],
        out_shape=[
            jax.ShapeDtypeStruct((bsz, t, M_WIDTH), BF16),
            jax.ShapeDtypeStruct((bsz, M_HEADS, M_DK, M_DV), F32),
            jax.ShapeDtypeStruct((bsz, M_HEADS, M_DK), F32),
            jax.ShapeDtypeStruct((bsz, M_HEADS, LANES), F32),
        ],
        scratch_shapes=[
            pltpu.VMEM((bb, M_HEADS, M_DK, M_DV), F32),
            pltpu.VMEM((bb, M_HEADS, M_DK), F32),
            pltpu.VMEM((bb, M_HEADS, LANES), F32),
        ],
        compiler_params=_cparams(("parallel", "arbitrary"), 40),
        name="mlstm",
    )(qkvo, qkvo, qkvo, qkvo, gates, c0, n0, m0, ng)


def _proj_out_kernel(x_ref, hm_ref, cv_ref, wa_ref, wb_ref, o_ref):
    acc = jnp.dot(hm_ref[...], wa_ref[...], preferred_element_type=F32)
    acc = acc + jnp.dot(cv_ref[...], wb_ref[...], preferred_element_type=F32)
    o_ref[...] = x_ref[...] + acc


def _proj_out(x, hm, cv, wa, wb):
    n = x.shape[0]
    tm = min(TOK_TILE, n)
    return pl.pallas_call(
        _proj_out_kernel,
        grid=(n // tm,),
        in_specs=[
            pl.BlockSpec((tm, D_MODEL), lambda i: (i, 0)),
            pl.BlockSpec((tm, M_WIDTH), lambda i: (i, 0)),
            pl.BlockSpec((tm, CONV_CH), lambda i: (i, 0)),
            pl.BlockSpec((M_WIDTH, D_MODEL), lambda i: (0, 0)),
            pl.BlockSpec((CONV_CH, D_MODEL), lambda i: (0, 0)),
        ],
        out_specs=pl.BlockSpec((tm, D_MODEL), lambda i: (i, 0)),
        out_shape=jax.ShapeDtypeStruct((n, D_MODEL), F32),
        compiler_params=_cparams(("parallel",), 48),
        name="proj_out",
    )(x, hm, cv, wa, wb)


def _mem_kv_kernel(m_ref, g_ref, w_ref, k_ref, v_ref):
    mn = _rms(m_ref[...], g_ref[...]).astype(BF16)
    z = jnp.dot(mn, w_ref[...], preferred_element_type=F32)
    k_ref[...] = z[:, :CA_W]
    v_ref[...] = z[:, CA_W:]


def _mem_kv(mem, g, wkv):
    n = mem.shape[0]
    return pl.pallas_call(
        _mem_kv_kernel,
        grid=(1,),
        in_specs=[
            pl.BlockSpec((n, D_MODEL), lambda i: (0, 0)),
            pl.BlockSpec((1, D_MODEL), lambda i: (0, 0)),
            pl.BlockSpec((D_MODEL, 2 * CA_W), lambda i: (0, 0)),
        ],
        out_specs=[pl.BlockSpec((n, CA_W), lambda i: (0, 0))] * 2,
        out_shape=[jax.ShapeDtypeStruct((n, CA_W), F32)] * 2,
        compiler_params=_cparams(("arbitrary",), 40),
        name="mem_kv",
    )(mem, g, wkv)


def _attn_router_kernel(x_ref, gca_ref, wq_ref, k_ref, v_ref, wo_ref, gmoe_ref, rw_ref, rb_ref, cnt0_ref,
                        hn_in_ref, x2_ref, hn_ref, te_ref, tg_ref, tr_ref, cnt_ref, o_s, cnt_s, *, bb, tt):
    del hn_in_ref
    rows = bb * tt

    @pl.when((pl.program_id(0) == 0) & (pl.program_id(1) == 0))
    def _():
        cnt_s[...] = cnt0_ref[...]

    x = x_ref[...].reshape(rows, D_MODEL)
    h = _rms(x, gca_ref[...]).astype(BF16)
    q = jnp.dot(h, wq_ref[...], preferred_element_type=F32) * (CA_DH ** -0.5)
    q = q.astype(BF16)
    for b in range(bb):
        for hd in range(CA_HEADS):
            cols = slice(hd * CA_DH, (hd + 1) * CA_DH)
            qb = q[b * tt:(b + 1) * tt, cols]
            s = lax.dot_general(qb, k_ref[b, :, cols], (((1,), (1,)), ((), ())), preferred_element_type=F32)
            e = jnp.exp(s - jnp.max(s, axis=-1, keepdims=True))
            p = (e / jnp.sum(e, axis=-1, keepdims=True)).astype(BF16)
            o_s[b * tt:(b + 1) * tt, cols] = jnp.dot(p, v_ref[b, :, cols], preferred_element_type=F32).astype(BF16)
    x2 = x + jnp.dot(o_s[...], wo_ref[...], preferred_element_type=F32)
    x2_ref[...] = x2.reshape(bb, tt, D_MODEL)

    hn = _rms(x2, gmoe_ref[...])
    hn_bf = hn.astype(BF16)
    hn_ref[...] = _pack_pairs(hn[:, :HALF_D], hn[:, HALF_D:])
    logits = jnp.dot(hn_bf, rw_ref[...], preferred_element_type=F32) + rb_ref[...]
    lane = lax.broadcasted_iota(jnp.int32, (rows, LANES), 1)
    work = logits
    sel = []
    vals = []
    for _ in range(TOP_K):
        mx = jnp.max(work, axis=-1, keepdims=True)
        idx = jnp.min(jnp.where(work == mx, lane, LANES), axis=-1, keepdims=True)
        hit = lane == idx
        sel.append((idx, hit))
        vals.append(mx)
        work = jnp.where(hit, -jnp.inf, work)
    ex = [jnp.exp(vv - vals[0]) for vv in vals]
    tot = ex[0] + ex[1] + ex[2] + ex[3]
    assigned = jnp.zeros((rows, LANES), F32)
    for _, hit in sel:
        assigned = assigned + hit.astype(F32)
    rt = lax.broadcasted_iota(jnp.int32, (rows, rows), 0)
    cs = lax.broadcasted_iota(jnp.int32, (rows, rows), 1)
    strict = (cs < rt).astype(BF16)
    before = jnp.dot(strict, assigned.astype(BF16), preferred_element_type=F32) + cnt_s[...]
    te = jnp.zeros((rows, LANES), jnp.int32)
    tg = jnp.zeros((rows, LANES), F32)
    tr = jnp.zeros((rows, LANES), F32)
    for kk, (idx, hit) in enumerate(sel):
        rank = jnp.sum(jnp.where(hit, before, 0.0), axis=-1, keepdims=True)
        te = jnp.where(lane == kk, idx, te)
        tg = jnp.where(lane == kk, ex[kk] / tot, tg)
        tr = jnp.where(lane == kk, rank, tr)
    te_ref[...] = te
    tg_ref[...] = tg
    tr_ref[...] = tr.astype(jnp.int32)
    cnt_s[...] = cnt_s[...] + jnp.sum(assigned, axis=0, keepdims=True)
    cnt_ref[...] = cnt_s[...]


def _attn_router(x1, mem_k, mem_v, cnt0, hn_all, row0, p, bb, tt):
    bsz, t, _ = x1.shape
    rows = bb * tt
    n = bsz * t
    nb, nt = bsz // bb, t // tt
    blk0 = row0 // rows
    tok = lambda b, s: (b * nt + s, 0)
    const = lambda b, s: (0, 0)
    grid_spec = dict(
        grid=(nb, nt),
        in_specs=[
            pl.BlockSpec((bb, tt, D_MODEL), lambda b, s: (b, s, 0)),
            pl.BlockSpec((1, D_MODEL), const),
            pl.BlockSpec((D_MODEL, CA_W), const),
            pl.BlockSpec((bb, N_MEM, CA_W), lambda b, s: (b, 0, 0)),
            pl.BlockSpec((bb, N_MEM, CA_W), lambda b, s: (b, 0, 0)),
            pl.BlockSpec((CA_W, D_MODEL), const),
            pl.BlockSpec((1, D_MODEL), const),
            pl.BlockSpec((D_MODEL, LANES), const),
            pl.BlockSpec((1, LANES), const),
            pl.BlockSpec((1, LANES), const),
            pl.BlockSpec(memory_space=pl.ANY),
        ],
        out_specs=[
            pl.BlockSpec((bb, tt, D_MODEL), lambda b, s: (b, s, 0)),
            pl.BlockSpec((rows, HALF_D), lambda b, s: (blk0 + b * nt + s, 0)),
            pl.BlockSpec((rows, LANES), tok),
            pl.BlockSpec((rows, LANES), tok),
            pl.BlockSpec((rows, LANES), tok),
            pl.BlockSpec((1, LANES), const),
        ],
    )
    return pl.pallas_call(
        functools.partial(_attn_router_kernel, bb=bb, tt=tt),
        out_shape=[
            jax.ShapeDtypeStruct((bsz, t, D_MODEL), F32),
            jax.ShapeDtypeStruct(hn_all.shape, jnp.uint32),
            jax.ShapeDtypeStruct((n, LANES), jnp.int32),
            jax.ShapeDtypeStruct((n, LANES), F32),
            jax.ShapeDtypeStruct((n, LANES), jnp.int32),
            jax.ShapeDtypeStruct((1, LANES), F32),
        ],
        scratch_shapes=[pltpu.VMEM((rows, CA_W), BF16), pltpu.VMEM((1, LANES), F32)],
        input_output_aliases={10: 1},
        compiler_params=_cparams(("arbitrary", "arbitrary"), 48),
        name="attn_router",
        **grid_spec,
    )(x1, p["g_ca"], p["wq"], mem_k, mem_v, p["wo"], p["g_moe"], p["router_w"], p["router_b"], cnt0, hn_all)


def _sc_scatter_rows(rows, dest3, n_out):
    n, d = rows.shape
    nwin = dest3.shape[0]
    mesh = plsc.VectorSubcoreMesh(core_axis_name="c", subcore_axis_name="s")
    workers = mesh.num_cores * mesh.num_subcores
    assert nwin * SC_WIN == n and nwin % workers == 0, (n, nwin, workers)
    per = nwin // workers

    @functools.partial(
        pl.kernel, out_type=jax.ShapeDtypeStruct((n_out, d), rows.dtype), mesh=mesh,
        scratch_types=[pltpu.VMEM((SC_WIN, d), rows.dtype), pltpu.VMEM((TOP_K, SC_WIN), jnp.int32),
                       pltpu.SemaphoreType.DMA])
    def scatter(x_hbm, i_hbm, o_hbm, buf, idx, sem):
        wid = lax.axis_index("s") * mesh.num_cores + lax.axis_index("c")

        @pl.loop(0, per)
        def _(j):
            w = wid * per + j
            pltpu.sync_copy(x_hbm.at[pl.ds(w * SC_WIN, SC_WIN)], buf)
            pltpu.sync_copy(i_hbm.at[w], idx)
            copies = [pltpu.async_copy(buf, o_hbm.at[idx.at[kk]], sem) for kk in range(TOP_K)]
            for cp in copies:
                cp.wait()

    return scatter(rows, dest3)


def _expert_changed(te_ref):
    t = pl.program_id(1)
    return (t == 0) | (te_ref[t] != te_ref[jnp.maximum(t - 1, 0)])


def _moe_up_kernel(te_ref, nu_ref, x_ref, w1g_ref, w1l_ref, b1g_ref, b1l_ref, act_ref, wg_s, wl_s):
    @pl.when(_expert_changed(te_ref))
    def _():
        wg_s[...] = w1g_ref[0].astype(BF16)
        wl_s[...] = w1l_ref[0].astype(BF16)

    def up(words):
        lo, hi = _unpack_pairs(words)
        x = jnp.concatenate([lo.astype(BF16), hi.astype(BF16)], axis=1)
        glu = jnp.dot(x, wg_s[...], preferred_element_type=F32) + b1g_ref[0]
        lin = jnp.dot(x, wl_s[...], preferred_element_type=F32) + b1l_ref[0]
        glu = jnp.minimum(glu, SWIGLU_LIMIT)
        lin = jnp.clip(lin, -SWIGLU_LIMIT, SWIGLU_LIMIT)
        act_ref[...] = (glu * _sigmoid(SWIGLU_ALPHA * glu) * (lin + 1.0)).astype(BF16)

    t = pl.program_id(1)
    used = t < nu_ref[0]
    valid = nu_ref[1 + t]

    @pl.when(used & (valid >= MOE_TM))
    def _():
        up(x_ref[...])

    @pl.when(used & (valid < MOE_TM))
    def _():
        rowid = lax.broadcasted_iota(jnp.int32, (MOE_TM, HALF_D), 0)
        up(jnp.where(rowid < valid, x_ref[...], jnp.uint32(0)))


def _moe_down_kernel(te_ref, nu_ref, a_ref, w2_ref, b2_ref, y_ref, w2_s):
    @pl.when(_expert_changed(te_ref))
    def _():
        w2_s[...] = w2_ref[0].astype(BF16)

    @pl.when(pl.program_id(1) < nu_ref[0])
    def _():
        y = jnp.dot(a_ref[...], w2_s[...], preferred_element_type=F32) + b2_ref[0]
        y_ref[...] = _pack_pairs(y[:, :MOE_TN // 2], y[:, MOE_TN // 2:])


def _moe(tile_e, n_used, x_sorted, w1, b1, w2, b2):
    r = x_sorted.shape[0]
    n_tiles = r // MOE_TM
    nf = D_FF // MOE_TF
    nn = D_MODEL // MOE_TN

    def row(c, t, te, nu):
        return jnp.maximum(jnp.minimum(t, nu[0] - 1), 0)

    act = pl.pallas_call(
        _moe_up_kernel,
        grid_spec=pltpu.PrefetchScalarGridSpec(
            num_scalar_prefetch=2,
            grid=(nf, n_tiles),
            in_specs=[
                pl.BlockSpec((MOE_TM, HALF_D), lambda c, t, te, nu: (row(c, t, te, nu), 0)),
                pl.BlockSpec((1, D_MODEL, MOE_TF), lambda c, t, te, nu: (te[t], 0, c)),
                pl.BlockSpec((1, D_MODEL, MOE_TF), lambda c, t, te, nu: (te[t], 0, nf + c)),
                pl.BlockSpec((1, 1, MOE_TF), lambda c, t, te, nu: (te[t], 0, c)),
                pl.BlockSpec((1, 1, MOE_TF), lambda c, t, te, nu: (te[t], 0, nf + c)),
            ],
            out_specs=pl.BlockSpec((MOE_TM, MOE_TF), lambda c, t, te, nu: (row(c, t, te, nu), c)),
            scratch_shapes=[pltpu.VMEM((D_MODEL, MOE_TF), BF16), pltpu.VMEM((D_MODEL, MOE_TF), BF16)],
        ),
        out_shape=jax.ShapeDtypeStruct((r, D_FF), BF16),
        compiler_params=_cparams(("arbitrary", "arbitrary"), 48),
        name="moe_up",
    )(tile_e, n_used, x_sorted, w1, w1, b1, b1)

    return pl.pallas_call(
        _moe_down_kernel,
        grid_spec=pltpu.PrefetchScalarGridSpec(
            num_scalar_prefetch=2,
            grid=(nn, n_tiles),
            in_specs=[
                pl.BlockSpec((MOE_TM, D_FF), lambda c, t, te, nu: (row(c, t, te, nu), 0)),
                pl.BlockSpec((1, D_FF, MOE_TN), lambda c, t, te, nu: (te[t], 0, c)),
                pl.BlockSpec((1, 1, MOE_TN), lambda c, t, te, nu: (te[t], 0, c)),
            ],
            out_specs=pl.BlockSpec((MOE_TM, MOE_TN // 2), lambda c, t, te, nu: (row(c, t, te, nu), c)),
            scratch_shapes=[pltpu.VMEM((D_FF, MOE_TN), BF16)],
        ),
        out_shape=jax.ShapeDtypeStruct((r, HALF_D), jnp.uint32),
        compiler_params=_cparams(("arbitrary", "arbitrary"), 48),
        name="moe_down",
    )(tile_e, n_used, act, w2, b2)


def _combine_kernel(x_ref, y0_ref, y1_ref, y2_ref, y3_ref, tg_ref, g_ref, o_ref):
    tg = tg_ref[...]
    acc = x_ref[...]
    half = MOE_TN // 2
    for kk, y_ref in enumerate((y0_ref, y1_ref, y2_ref, y3_ref)):
        cols = []
        for c in range(D_MODEL // MOE_TN):
            cols += list(_unpack_pairs(y_ref[0, :, c * half:(c + 1) * half]))
        acc = acc + jnp.concatenate(cols, axis=1) * tg[:, kk:kk + 1]
    o_ref[...] = _rms(acc, g_ref[...])


def _combine(x2, yg, tg, g, row0):
    n = x2.shape[0]
    tm = min(TOK_TILE, n)
    blk0 = row0 // tm
    yspec = [pl.BlockSpec((1, tm, HALF_D), functools.partial(lambda i, kk: (kk, blk0 + i, 0), kk=kk))
             for kk in range(TOP_K)]
    return pl.pallas_call(
        _combine_kernel,
        grid=(n // tm,),
        in_specs=[pl.BlockSpec((tm, D_MODEL), lambda i: (i, 0))] + yspec + [
            pl.BlockSpec((tm, LANES), lambda i: (i, 0)),
            pl.BlockSpec((1, D_MODEL), lambda i: (0, 0)),
        ],
        out_specs=pl.BlockSpec((tm, D_MODEL), lambda i: (i, 0)),
        out_shape=jax.ShapeDtypeStruct((n, D_MODEL), F32),
        compiler_params=_cparams(("parallel",), 48),
        name="combine",
    )(x2, yg, yg, yg, yg, tg, g)


def _group_tiles(bsz, t):
    tt = min(t, TOK_TILE)
    bb = max(1, min(bsz, TOK_TILE // tt))
    return bb, tt


def _layer_group(x, c0, n0, m0, conv_state, mem_k, mem_v, cnt0, hn_all, row0, p):
    bsz, t, _ = x.shape
    n = bsz * t
    L = min(MLSTM_CHUNK, t)
    qkvo, u, gates = _proj_in(x.reshape(n, D_MODEL), p["g_mix"], p["w_main"], p["wg"], p["bg"], L)

    ctt = min(CONV_TT, t)
    cbb = max(1, min(bsz, CONV_TT // ctt))
    state_pad = jnp.pad(conv_state, ((0, 0), (CONV_PAD - CONV_STATE, 0), (0, 0)))
    cv, new_conv = _conv(u.reshape(bsz, t, CONV_CH), state_pad, p["conv_w"], p["conv_b"],
                         p["ln_g"], p["ln_b"], cbb, ctt)

    mbb = max(1, min(bsz, LANES // L)) if L < LANES else 1
    m0b = jnp.broadcast_to(m0[:, :, None], (bsz, M_HEADS, LANES))
    hm, c1, n1, m1 = _mlstm(qkvo.reshape(bsz, t, -1), gates, c0, n0, m0b, p["mh_g"], mbb, L)

    x1 = _proj_out(x.reshape(n, D_MODEL), hm.reshape(n, M_WIDTH), cv.reshape(n, CONV_CH), p["wo_a"], p["wo_b"])

    abb, att = _group_tiles(bsz, t)
    x2, hn_all, te, tg, tr, cnt = _attn_router(x1.reshape(bsz, t, D_MODEL), mem_k, mem_v, cnt0, hn_all, row0,
                                               p, abb, att)
    return x2.reshape(n, D_MODEL), hn_all, te, tg, tr, cnt, (c1, n1, m1[:, :, 0], new_conv)


def kernel(x_prompt, x_sample, mem_prompt, state_mlstm_C, state_mlstm_n, state_mlstm_m, state_conv, cache_mem_k, cache_mem_v, norm_mix_g, w_in, b_gates, mh_norm_g, conv_w, conv_b, conv_ln_g, conv_ln_b, w_out, norm_ca_g, norm_mem_g, ca_wq, ca_wk, ca_wv, ca_wo, norm_moe_g, router_w, router_b, moe_w1, moe_b1, moe_w2, moe_b2, final_norm_g):
    assert w_in.shape[0] == 1, "single layer"
    bp, tp, _ = x_prompt.shape
    bs, ts, _ = x_sample.shape
    n_p, n_s = bp * tp, bs * ts
    n_all = n_p + n_s

    wi = w_in[0]
    o0 = 0
    parts = []
    for width in (QK_W, QK_W, M_WIDTH, M_WIDTH, M_HEADS, M_HEADS, CONV_CH, CONV_CH):
        parts.append(wi[:, o0:o0 + width])
        o0 += width
    w_q, w_k, w_v, w_o, w_gi, w_gf, w_ua, w_ug = parts
    glu_cols = []
    for c in range(CONV_CH // GLU_HALF):
        glu_cols += [w_ua[:, c * GLU_HALF:(c + 1) * GLU_HALF], w_ug[:, c * GLU_HALF:(c + 1) * GLU_HALF]]
    p = {
        "g_mix": norm_mix_g[0][None, :],
        "w_main": jnp.concatenate([w_q, w_k, w_v, w_o] + glu_cols, axis=1).astype(BF16),
        "wg": jnp.concatenate([w_gi, w_gf], axis=1).T.astype(BF16),
        "bg": b_gates[0][:, None],
        "mh_g": mh_norm_g[0],
        "conv_w": conv_w[0],
        "conv_b": conv_b[0][None, :],
        "ln_g": conv_ln_g[0][None, :],
        "ln_b": conv_ln_b[0][None, :],
        "wo_a": w_out[0][:M_WIDTH].astype(BF16),
        "wo_b": w_out[0][M_WIDTH:].astype(BF16),
        "g_ca": norm_ca_g[0][None, :],
        "wq": ca_wq[0].astype(BF16),
        "wo": ca_wo[0].astype(BF16),
        "g_moe": norm_moe_g[0][None, :],
        "router_w": jnp.pad(router_w[0], ((0, 0), (0, LANES - N_EXPERTS))).astype(BF16),
        "router_b": jnp.concatenate([router_b[0], jnp.full((LANES - N_EXPERTS,), NEG_BIG, F32)])[None, :],
    }

    wkv = jnp.concatenate([ca_wk[0], ca_wv[0]], axis=1).astype(BF16)
    mk, mv = _mem_kv(mem_prompt.reshape(bp * N_MEM, D_MODEL), norm_mem_g[0][None, :], wkv)
    mk = mk.reshape(bp, N_MEM, CA_W)
    mv = mv.reshape(bp, N_MEM, CA_W)
    hn_all = jnp.zeros((n_all, HALF_D), jnp.uint32)
    cnt0 = jnp.zeros((1, LANES), F32)
    zc = jnp.zeros((bp, M_HEADS, M_DK, M_DV), F32)
    zn = jnp.zeros((bp, M_HEADS, M_DK), F32)
    zm = jnp.zeros((bp, M_HEADS), F32)
    zconv = jnp.zeros((bp, CONV_STATE, CONV_CH), F32)
    x2p, hn_all, te_p, tg_p, tr_p, cnt1, st_p = _layer_group(
        x_prompt, zc, zn, zm, zconv, mk.astype(BF16), mv.astype(BF16), cnt0, hn_all, 0, p)

    x2s, hn_all, te_s, tg_s, tr_s, cnt2, st_s = _layer_group(
        x_sample, state_mlstm_C[0], state_mlstm_n[0], state_mlstm_m[0], state_conv[0],
        cache_mem_k[0].reshape(bs, N_MEM, CA_W).astype(BF16), cache_mem_v[0].reshape(bs, N_MEM, CA_W).astype(BF16),
        cnt1, hn_all, n_p, p)

    te = jnp.concatenate([te_p[:, :TOP_K], te_s[:, :TOP_K]], axis=0)
    tr = jnp.concatenate([tr_p[:, :TOP_K], tr_s[:, :TOP_K]], axis=0)
    counts = cnt2[0, :N_EXPERTS].astype(jnp.int32)
    tiles_per_e = (counts + MOE_TM - 1) // MOE_TM
    tile_end = jnp.cumsum(tiles_per_e)
    row_start = (tile_end - tiles_per_e) * MOE_TM
    n_tiles = -(-(n_all * TOP_K) // MOE_TM) + N_EXPERTS
    n_used = tile_end[-1:]
    tile_ids = jnp.minimum(jnp.arange(n_tiles, dtype=jnp.int32), n_used[0] - 1)
    tile_e = jnp.minimum(jnp.sum(tile_ids[:, None] >= tile_end[None, :], axis=1), N_EXPERTS - 1).astype(jnp.int32)
    tile_first = jnp.arange(n_tiles, dtype=jnp.int32) - (tile_end - tiles_per_e)[tile_e]
    tile_valid = jnp.clip(counts[tile_e] - tile_first * MOE_TM, 0, MOE_TM)
    tile_valid = jnp.where(jnp.arange(n_tiles) < n_used[0], tile_valid, 0).astype(jnp.int32)
    moe_meta = jnp.concatenate([n_used.astype(jnp.int32), tile_valid])
    dest = row_start[te] + tr
    dest3 = dest.reshape(n_all // SC_WIN, SC_WIN, TOP_K).transpose(0, 2, 1)
    x_sorted = _sc_scatter_rows(hn_all, dest3, n_tiles * MOE_TM)

    y_sorted = _moe(tile_e, moe_meta, x_sorted, moe_w1[0], moe_b1[0][:, None, :], moe_w2[0],
                    moe_b2[0][:, None, :])

    yg = y_sorted.at[dest.T.reshape(-1)].get(mode="promise_in_bounds").reshape(TOP_K, n_all, HALF_D)
    fg = final_norm_g[None, :]
    y_prompt = _combine(x2p, yg, tg_p, fg, 0).reshape(bp, tp, D_MODEL)
    y_sample = _combine(x2s, yg, tg_s, fg, n_p).reshape(bs, ts, D_MODEL)

    c1, n1, m1, conv1 = st_p
    c2, n2, m2, conv2 = st_s
    mk4 = mk.reshape(1, bp, N_MEM, CA_HEADS, CA_DH)
    mv4 = mv.reshape(1, bp, N_MEM, CA_HEADS, CA_DH)
    return (y_prompt, y_sample, c1[None], n1[None], m1[None], conv1[None], mk4, mv4,
            c2[None], n2[None], m2[None], conv2[None])
```

```python
import functools

import jax
import jax.numpy as jnp
from jax import lax
from jax.experimental import pallas as pl
from jax.experimental.pallas import tpu as pltpu
from jax.experimental.pallas import tpu_sc as plsc

F32 = jnp.float32
BF16 = jnp.bfloat16
HIGHEST = lax.Precision.HIGHEST

D_MODEL = 2048
HALF_D = D_MODEL // 2
M_HEADS = 4
M_DV = 256
M_DK = 128
QK_W = M_HEADS * M_DK
M_WIDTH = M_HEADS * M_DV
CONV_CH = 1024
CONV_WIDTH = 31
CONV_STATE = CONV_WIDTH - 1
CONV_PAD = 32
N_MEM = 256
CA_HEADS = 4
CA_DH = 128
CA_W = CA_HEADS * CA_DH
N_EXPERTS = 32
TOP_K = 4
D_FF = 2048
SWIGLU_ALPHA = 1.702
SWIGLU_LIMIT = 7.0
EPS = 1e-6
LANES = 128
SUBLANES = 8
NEG_BIG = -1e30

PROJ_TM = 1024
PROJ_TN = 1024
GLU_HALF = PROJ_TN // 2
TOK_TILE = 512
MOE_TM = 512
MOE_TF = 1024
MOE_TN = 2048
MOE_VMEM_MB = 60
MLSTM_CHUNK = 256
CONV_TT = 256
CONV_RB = 32
CONV_LC = 512
SC_WIN = 32


def _cparams(sem, vmem_mb):
    return pltpu.CompilerParams(dimension_semantics=sem, vmem_limit_bytes=vmem_mb << 20)


def _log_sigmoid(x):
    return jnp.minimum(x, 0.0) - jnp.log1p(jnp.exp(-jnp.abs(x)))


def _sigmoid(x):
    return 1.0 / (1.0 + jnp.exp(-x))


def _pack_pairs(lo, hi):
    ulo = lax.bitcast_convert_type(lo.astype(BF16).astype(F32), jnp.uint32)
    uhi = lax.bitcast_convert_type(hi.astype(BF16).astype(F32), jnp.uint32)
    return (ulo >> 16) | uhi


def _unpack_pairs(w):
    lo = lax.bitcast_convert_type(w << 16, F32)
    hi = lax.bitcast_convert_type(w & jnp.uint32(0xFFFF0000), F32)
    return lo, hi


def _rms(x, g):
    ms = jnp.mean(x * x, axis=-1, keepdims=True)
    return x * lax.rsqrt(ms + EPS) * g


def _proj_in_kernel(x_ref, g_ref, w_ref, wg_ref, bg_ref, qkvo_ref, u_ref, gates_ref, h_ref, *, chunk):
    j = pl.program_id(1)
    tm = x_ref.shape[0]

    @pl.when(j == 0)
    def _():
        h = _rms(x_ref[...], g_ref[...]).astype(BF16)
        h_ref[...] = h
        gt = lax.dot_general(wg_ref[...], h, (((1,), (1,)), ((), ())), preferred_element_type=F32)
        gt = gt + bg_ref[...]
        row = lax.broadcasted_iota(jnp.int32, gt.shape, 0)
        gt = jnp.where(row < M_HEADS, gt, _log_sigmoid(gt))
        for c in range(tm // chunk):
            gates_ref[c] = gt[:, c * chunk:(c + 1) * chunk]

    z = jnp.dot(h_ref[...], w_ref[...], preferred_element_type=F32)

    @pl.when(j < 3)
    def _():
        qkvo_ref[...] = z.astype(BF16)

    @pl.when(j >= 3)
    def _():
        u_ref[...] = (z[:, :GLU_HALF] * _sigmoid(z[:, GLU_HALF:])).astype(BF16)


def _proj_in(x, g, w_main, wg, bg, chunk):
    n = x.shape[0]
    tm = min(PROJ_TM, n)
    nj = w_main.shape[1] // PROJ_TN
    return pl.pallas_call(
        functools.partial(_proj_in_kernel, chunk=chunk),
        grid=(n // tm, nj),
        in_specs=[
            pl.BlockSpec((tm, D_MODEL), lambda i, j: (i, 0)),
            pl.BlockSpec((1, D_MODEL), lambda i, j: (0, 0)),
            pl.BlockSpec((D_MODEL, PROJ_TN), lambda i, j: (0, j)),
            pl.BlockSpec((2 * M_HEADS, D_MODEL), lambda i, j: (0, 0)),
            pl.BlockSpec((2 * M_HEADS, 1), lambda i, j: (0, 0)),
        ],
        out_specs=[
            pl.BlockSpec((tm, PROJ_TN), lambda i, j: (i, jnp.minimum(j, 2))),
            pl.BlockSpec((tm, GLU_HALF), lambda i, j: (i, jnp.maximum(j - 3, 0))),
            pl.BlockSpec((tm // chunk, 2 * M_HEADS, chunk), lambda i, j: (i, 0, 0)),
        ],
        out_shape=[
            jax.ShapeDtypeStruct((n, 2 * QK_W + 2 * M_WIDTH), BF16),
            jax.ShapeDtypeStruct((n, CONV_CH), BF16),
            jax.ShapeDtypeStruct((n // chunk, 2 * M_HEADS, chunk), F32),
        ],
        scratch_shapes=[pltpu.VMEM((tm, D_MODEL), BF16)],
        compiler_params=_cparams(("parallel", "arbitrary"), 48),
        name="proj_in",
    )(x, g, w_main, wg, bg)


def _conv_kernel(u_ref, st_ref, w_ref, cb_ref, lg_ref, lb_ref, cv_ref, ns_ref, e_ref, er_ref, c_ref, *, bb, tt):
    t = pl.program_id(1)
    nt = pl.num_programs(1)
    ext = tt + CONV_PAD

    @pl.when(t == 0)
    def _():
        e_ref[:, 0:CONV_PAD, :] = st_ref[...]

    @pl.when(t > 0)
    def _():
        e_ref[:, 0:CONV_PAD, :] = e_ref[:, tt:ext, :]

    e_ref[:, CONV_PAD:ext, :] = u_ref[...].astype(F32)
    e_ref[:, ext:, :] = jnp.zeros((bb, SUBLANES, CONV_CH), F32)

    base = CONV_PAD - CONV_STATE
    for b in range(bb):
        for r in range(1, SUBLANES):
            er_ref[r - 1] = e_ref[b, r:r + ext, :]

        def row_block(rb, carry):
            r0 = pl.multiple_of(rb * CONV_RB, CONV_RB)
            for lc in range(CONV_CH // CONV_LC):
                ls = slice(lc * CONV_LC, (lc + 1) * CONV_LC)
                acc = jnp.zeros((CONV_RB, CONV_LC), F32) + cb_ref[:, ls]
                for j in range(CONV_WIDTH):
                    off = j + base
                    a, r = (off // SUBLANES) * SUBLANES, off % SUBLANES
                    if r == 0:
                        tap = e_ref[b, pl.ds(r0 + a, CONV_RB), ls]
                    else:
                        tap = er_ref[r - 1, pl.ds(r0 + a, CONV_RB), ls]
                    acc = acc + w_ref[j:j + 1, ls] * tap
                c_ref[:, ls] = acc
            c = c_ref[...]
            mu = jnp.mean(c, axis=-1, keepdims=True)
            cc = c - mu
            var = jnp.mean(cc * cc, axis=-1, keepdims=True)
            cn = cc * lax.rsqrt(var + EPS) * lg_ref[...] + lb_ref[...]
            cv_ref[b, pl.ds(r0, CONV_RB), :] = (cn * _sigmoid(cn)).astype(BF16)
            return carry

        lax.fori_loop(0, tt // CONV_RB, row_block, 0)

    @pl.when(t == nt - 1)
    def _():
        ns_ref[...] = e_ref[:, ext - CONV_STATE:ext, :]


def _conv(u, state_pad, conv_w, conv_b, ln_g, ln_b, bb, tt):
    bsz, t, _ = u.shape
    return pl.pallas_call(
        functools.partial(_conv_kernel, bb=bb, tt=tt),
        grid=(bsz // bb, t // tt),
        in_specs=[
            pl.BlockSpec((bb, tt, CONV_CH), lambda b, s: (b, s, 0)),
            pl.BlockSpec((bb, CONV_PAD, CONV_CH), lambda b, s: (b, 0, 0)),
            pl.BlockSpec((CONV_WIDTH, CONV_CH), lambda b, s: (0, 0)),
            pl.BlockSpec((1, CONV_CH), lambda b, s: (0, 0)),
            pl.BlockSpec((1, CONV_CH), lambda b, s: (0, 0)),
            pl.BlockSpec((1, CONV_CH), lambda b, s: (0, 0)),
        ],
        out_specs=[
            pl.BlockSpec((bb, tt, CONV_CH), lambda b, s: (b, s, 0)),
            pl.BlockSpec((bb, CONV_STATE, CONV_CH), lambda b, s: (b, 0, 0)),
        ],
        out_shape=[
            jax.ShapeDtypeStruct((bsz, t, CONV_CH), BF16),
            jax.ShapeDtypeStruct((bsz, CONV_STATE, CONV_CH), F32),
        ],
        scratch_shapes=[
            pltpu.VMEM((bb, tt + CONV_PAD + SUBLANES, CONV_CH), F32),
            pltpu.VMEM((SUBLANES - 1, tt + CONV_PAD, CONV_CH), F32),
            pltpu.VMEM((CONV_RB, CONV_CH), F32),
        ],
        compiler_params=_cparams(("parallel", "arbitrary"), 40),
        name="conv",
    )(u, state_pad, conv_w, conv_b, ln_g, ln_b)


def _mlstm_kernel(q_ref, k_ref, v_ref, o_ref, gt_ref, c0_ref, n0_ref, m0_ref, ng_ref,
                  hm_ref, cn_ref, nn_ref, mn_ref, c_s, n_s, m_s, *, bb, L):
    ci = pl.program_id(1)
    nc = pl.num_programs(1)

    @pl.when(ci == 0)
    def _():
        c_s[...] = c0_ref[...]
        n_s[...] = n0_ref[...]
        m_s[...] = m0_ref[...]

    rt = lax.broadcasted_iota(jnp.int32, (L, L), 0)
    cs = lax.broadcasted_iota(jnp.int32, (L, L), 1)
    causal = cs <= rt
    lower = causal.astype(F32)
    upper = (rt <= cs).astype(F32)
    eye = (rt == cs).astype(F32)
    nt_dims = (((1,), (1,)), ((), ()))
    scale = M_DK ** -0.5

    for b in range(bb):
        rows = gt_ref[b]
        cum_rows = jnp.dot(rows, upper, precision=HIGHEST, preferred_element_type=F32)
        cum_cols = lax.dot_general(lower, rows, nt_dims, precision=HIGHEST, preferred_element_type=F32)
        id_cols = lax.dot_general(eye, rows, nt_dims, precision=HIGHEST, preferred_element_type=F32)
        for h in range(M_HEADS):
            ig_row = rows[h:h + 1, :]
            ig_col = id_cols[:, h:h + 1]
            b_row = cum_rows[M_HEADS + h:M_HEADS + h + 1, :]
            b_col = cum_cols[:, M_HEADS + h:M_HEADS + h + 1]
            m0 = m_s[b, h:h + 1, 0:1]
            q = q_ref[b, :, h * M_DK:(h + 1) * M_DK]
            k = k_ref[b, :, h * M_DK:(h + 1) * M_DK]
            v = v_ref[b, :, h * M_DV:(h + 1) * M_DV]
            c0 = c_s[b, h]
            n0 = n_s[b, h:h + 1, :]

            g_col = b_col + m0
            dmat = jnp.where(causal, b_col - b_row + ig_row, -jnp.inf)
            m_col = jnp.maximum(g_col, jnp.max(dmat, axis=1, keepdims=True))
            w_state = jnp.exp(g_col - m_col) * scale
            qk = lax.dot_general(q, k, nt_dims, preferred_element_type=F32)
            s = qk * (jnp.exp(dmat - m_col) * scale)
            num = jnp.dot(s.astype(BF16), v, preferred_element_type=F32)
            num = num + w_state * jnp.dot(q, c0.astype(BF16), preferred_element_type=F32)
            qn = jnp.sum(q.astype(F32) * n0, axis=1, keepdims=True)
            den = jnp.sum(s, axis=1, keepdims=True) + w_state * qn
            hh = num / jnp.maximum(jnp.abs(den), jnp.exp(-m_col))

            hh = hh * lax.rsqrt(jnp.mean(hh * hh, axis=-1, keepdims=True) + EPS) * ng_ref[h:h + 1, :]
            og = _sigmoid(o_ref[b, :, h * M_DV:(h + 1) * M_DV].astype(F32))
            hm_ref[b, :, h * M_DV:(h + 1) * M_DV] = (og * hh).astype(BF16)

            b_last = b_row[:, L - 1:L]
            g_last = b_last + m0
            wk_row = b_last - b_row + ig_row
            m_new = jnp.maximum(g_last, jnp.max(wk_row, axis=1, keepdims=True))
            a0 = jnp.exp(g_last - m_new)
            ak_col = jnp.exp(b_last - b_col + ig_col - m_new)
            ks = k.astype(F32) * ak_col
            kv = lax.dot_general(ks.astype(BF16), v, (((0,), (0,)), ((), ())), preferred_element_type=F32)
            c_s[b, h] = a0 * c0 + kv
            n_s[b, h:h + 1, :] = a0 * n0 + jnp.sum(ks, axis=0, keepdims=True)
            m_s[b, h:h + 1, :] = jnp.broadcast_to(m_new, (1, LANES))

    @pl.when(ci == nc - 1)
    def _():
        cn_ref[...] = c_s[...]
        nn_ref[...] = n_s[...]
        mn_ref[...] = m_s[...]


def _mlstm(qkvo, gates, c0, n0, m0, ng, bb, L):
    bsz, t, _ = qkvo.shape
    nc = t // L
    gates = gates.reshape(bsz, nc, 2 * M_HEADS, L)
    st_c = pl.BlockSpec((bb, M_HEADS, M_DK, M_DV), lambda b, c: (b, 0, 0, 0))
    st_n = pl.BlockSpec((bb, M_HEADS, M_DK), lambda b, c: (b, 0, 0))
    return pl.pallas_call(
        functools.partial(_mlstm_kernel, bb=bb, L=L),
        grid=(bsz // bb, nc),
        in_specs=[
            pl.BlockSpec((bb, L, QK_W), lambda b, c: (b, c, 0)),
            pl.BlockSpec((bb, L, QK_W), lambda b, c: (b, c, 1)),
            pl.BlockSpec((bb, L, M_WIDTH), lambda b, c: (b, c, 1)),
            pl.BlockSpec((bb, L, M_WIDTH), lambda b, c: (b, c, 2)),
            pl.BlockSpec((bb, None, 2 * M_HEADS, L), lambda b, c: (b, c, 0, 0)),
            st_c, st_n, st_n,
            pl.BlockSpec((M_HEADS, M_DV), lambda b, c: (0, 0)),
        ],
        out_specs=[
            pl.BlockSpec((bb, L, M_WIDTH), lambda b, c: (b, c, 0)),
            st_c, st_n, st_n,
        ],
        out_shape=[
            jax.ShapeDtypeStruct((bsz, t, M_WIDTH), BF16),
            jax.ShapeDtypeStruct((bsz, M_HEADS, M_DK, M_DV), F32),
            jax.ShapeDtypeStruct((bsz, M_HEADS, M_DK), F32),
            jax.ShapeDtypeStruct((bsz, M_HEADS, LANES), F32),
        ],
        scratch_shapes=[
            pltpu.VMEM((bb, M_HEADS, M_DK, M_DV), F32),
            pltpu.VMEM((bb, M_HEADS, M_DK), F32),
            pltpu.VMEM((bb, M_HEADS, LANES), F32),
        ],
        compiler_params=_cparams(("parallel", "arbitrary"), 40),
        name="mlstm",
    )(qkvo, qkvo, qkvo, qkvo, gates, c0, n0, m0, ng)


def _proj_out_kernel(x_ref, hm_ref, cv_ref, wa_ref, wb_ref, o_ref):
    acc = jnp.dot(hm_ref[...], wa_ref[...], preferred_element_type=F32)
    acc = acc + jnp.dot(cv_ref[...], wb_ref[...], preferred_element_type=F32)
    o_ref[...] = x_ref[...] + acc


def _proj_out(x, hm, cv, wa, wb):
    n = x.shape[0]
    tm = min(TOK_TILE, n)
    return pl.pallas_call(
        _proj_out_kernel,
        grid=(n // tm,),
        in_specs=[
            pl.BlockSpec((tm, D_MODEL), lambda i: (i, 0)),
            pl.BlockSpec((tm, M_WIDTH), lambda i: (i, 0)),
            pl.BlockSpec((tm, CONV_CH), lambda i: (i, 0)),
            pl.BlockSpec((M_WIDTH, D_MODEL), lambda i: (0, 0)),
            pl.BlockSpec((CONV_CH, D_MODEL), lambda i: (0, 0)),
        ],
        out_specs=pl.BlockSpec((tm, D_MODEL), lambda i: (i, 0)),
        out_shape=jax.ShapeDtypeStruct((n, D_MODEL), F32),
        compiler_params=_cparams(("parallel",), 48),
        name="proj_out",
    )(x, hm, cv, wa, wb)


def _mem_kv_kernel(m_ref, g_ref, w_ref, k_ref, v_ref):
    mn = _rms(m_ref[...], g_ref[...]).astype(BF16)
    z = jnp.dot(mn, w_ref[...], preferred_element_type=F32)
    k_ref[...] = z[:, :CA_W]
    v_ref[...] = z[:, CA_W:]


def _mem_kv(mem, g, wkv):
    n = mem.shape[0]
    return pl.pallas_call(
        _mem_kv_kernel,
        grid=(1,),
        in_specs=[
            pl.BlockSpec((n, D_MODEL), lambda i: (0, 0)),
            pl.BlockSpec((1, D_MODEL), lambda i: (0, 0)),
            pl.BlockSpec((D_MODEL, 2 * CA_W), lambda i: (0, 0)),
        ],
        out_specs=[pl.BlockSpec((n, CA_W), lambda i: (0, 0))] * 2,
        out_shape=[jax.ShapeDtypeStruct((n, CA_W), F32)] * 2,
        compiler_params=_cparams(("arbitrary",), 40),
        name="mem_kv",
    )(mem, g, wkv)


def _attn_router_kernel(x_ref, gca_ref, wq_ref, k_ref, v_ref, wo_ref, gmoe_ref, rw_ref, rb_ref, cnt0_ref,
                        hn_in_ref, x2_ref, hn_ref, te_ref, tg_ref, tr_ref, cnt_ref, o_s, cnt_s, *, bb, tt):
    del hn_in_ref
    rows = bb * tt

    @pl.when((pl.program_id(0) == 0) & (pl.program_id(1) == 0))
    def _():
        cnt_s[...] = cnt0_ref[...]

    x = x_ref[...].reshape(rows, D_MODEL)
    h = _rms(x, gca_ref[...]).astype(BF16)
    q = jnp.dot(h, wq_ref[...], preferred_element_type=F32) * (CA_DH ** -0.5)
    q = q.astype(BF16)
    for b in range(bb):
        for hd in range(CA_HEADS):
            cols = slice(hd * CA_DH, (hd + 1) * CA_DH)
            qb = q[b * tt:(b + 1) * tt, cols]
            s = lax.dot_general(qb, k_ref[b, :, cols], (((1,), (1,)), ((), ())), preferred_element_type=F32)
            e = jnp.exp(s - jnp.max(s, axis=-1, keepdims=True))
            p = (e / jnp.sum(e, axis=-1, keepdims=True)).astype(BF16)
            o_s[b * tt:(b + 1) * tt, cols] = jnp.dot(p, v_ref[b, :, cols], preferred_element_type=F32).astype(BF16)
    x2 = x + jnp.dot(o_s[...], wo_ref[...], preferred_element_type=F32)
    x2_ref[...] = x2.reshape(bb, tt, D_MODEL)

    hn = _rms(x2, gmoe_ref[...])
    hn_bf = hn.astype(BF16)
    hn_ref[...] = _pack_pairs(hn[:, :HALF_D], hn[:, HALF_D:])
    logits = jnp.dot(hn_bf, rw_ref[...], preferred_element_type=F32) + rb_ref[...]
    lane = lax.broadcasted_iota(jnp.int32, (rows, LANES), 1)
    work = logits
    sel = []
    vals = []
    for _ in range(TOP_K):
        mx = jnp.max(work, axis=-1, keepdims=True)
        idx = jnp.min(jnp.where(work == mx, lane, LANES), axis=-1, keepdims=True)
        hit = lane == idx
        sel.append((idx, hit))
        vals.append(mx)
        work = jnp.where(hit, -jnp.inf, work)
    ex = [jnp.exp(vv - vals[0]) for vv in vals]
    tot = ex[0] + ex[1] + ex[2] + ex[3]
    assigned = jnp.zeros((rows, LANES), F32)
    for _, hit in sel:
        assigned = assigned + hit.astype(F32)
    rt = lax.broadcasted_iota(jnp.int32, (rows, rows), 0)
    cs = lax.broadcasted_iota(jnp.int32, (rows, rows), 1)
    strict = (cs < rt).astype(BF16)
    before = jnp.dot(strict, assigned.astype(BF16), preferred_element_type=F32) + cnt_s[...]
    te = jnp.zeros((rows, LANES), jnp.int32)
    tg = jnp.zeros((rows, LANES), F32)
    tr = jnp.zeros((rows, LANES), F32)
    for kk, (idx, hit) in enumerate(sel):
        rank = jnp.sum(jnp.where(hit, before, 0.0), axis=-1, keepdims=True)
        te = jnp.where(lane == kk, idx, te)
        tg = jnp.where(lane == kk, ex[kk] / tot, tg)
        tr = jnp.where(lane == kk, rank, tr)
    te_ref[...] = te
    tg_ref[...] = tg
    tr_ref[...] = tr.astype(jnp.int32)
    cnt_s[...] = cnt_s[...] + jnp.sum(assigned, axis=0, keepdims=True)
    cnt_ref[...] = cnt_s[...]


def _attn_router(x1, mem_k, mem_v, cnt0, hn_all, row0, p, bb, tt):
    bsz, t, _ = x1.shape
    rows = bb * tt
    n = bsz * t
    nb, nt = bsz // bb, t // tt
    blk0 = row0 // rows
    tok = lambda b, s: (b * nt + s, 0)
    const = lambda b, s: (0, 0)
    grid_spec = dict(
        grid=(nb, nt),
        in_specs=[
            pl.BlockSpec((bb, tt, D_MODEL), lambda b, s: (b, s, 0)),
            pl.BlockSpec((1, D_MODEL), const),
            pl.BlockSpec((D_MODEL, CA_W), const),
            pl.BlockSpec((bb, N_MEM, CA_W), lambda b, s: (b, 0, 0)),
            pl.BlockSpec((bb, N_MEM, CA_W), lambda b, s: (b, 0, 0)),
            pl.BlockSpec((CA_W, D_MODEL), const),
            pl.BlockSpec((1, D_MODEL), const),
            pl.BlockSpec((D_MODEL, LANES), const),
            pl.BlockSpec((1, LANES), const),
            pl.BlockSpec((1, LANES), const),
            pl.BlockSpec(memory_space=pl.ANY),
        ],
        out_specs=[
            pl.BlockSpec((bb, tt, D_MODEL), lambda b, s: (b, s, 0)),
            pl.BlockSpec((rows, HALF_D), lambda b, s: (blk0 + b * nt + s, 0)),
            pl.BlockSpec((rows, LANES), tok),
            pl.BlockSpec((rows, LANES), tok),
            pl.BlockSpec((rows, LANES), tok),
            pl.BlockSpec((1, LANES), const),
        ],
    )
    return pl.pallas_call(
        functools.partial(_attn_router_kernel, bb=bb, tt=tt),
        out_shape=[
            jax.ShapeDtypeStruct((bsz, t, D_MODEL), F32),
            jax.ShapeDtypeStruct(hn_all.shape, jnp.uint32),
            jax.ShapeDtypeStruct((n, LANES), jnp.int32),
            jax.ShapeDtypeStruct((n, LANES), F32),
            jax.ShapeDtypeStruct((n, LANES), jnp.int32),
            jax.ShapeDtypeStruct((1, LANES), F32),
        ],
        scratch_shapes=[pltpu.VMEM((rows, CA_W), BF16), pltpu.VMEM((1, LANES), F32)],
        input_output_aliases={10: 1},
        compiler_params=_cparams(("arbitrary", "arbitrary"), 48),
        name="attn_router",
        **grid_spec,
    )(x1, p["g_ca"], p["wq"], mem_k, mem_v, p["wo"], p["g_moe"], p["router_w"], p["router_b"], cnt0, hn_all)


def _sc_scatter_rows(rows, dest3, n_out):
    n, d = rows.shape
    nwin = dest3.shape[0]
    mesh = plsc.VectorSubcoreMesh(core_axis_name="c", subcore_axis_name="s")
    workers = mesh.num_cores * mesh.num_subcores
    assert nwin * SC_WIN == n and nwin % workers == 0, (n, nwin, workers)
    per = nwin // workers

    @functools.partial(
        pl.kernel, out_type=jax.ShapeDtypeStruct((n_out, d), rows.dtype), mesh=mesh,
        scratch_types=[pltpu.VMEM((SC_WIN, d), rows.dtype), pltpu.VMEM((TOP_K, SC_WIN), jnp.int32),
                       pltpu.SemaphoreType.DMA])
    def scatter(x_hbm, i_hbm, o_hbm, buf, idx, sem):
        wid = lax.axis_index("s") * mesh.num_cores + lax.axis_index("c")

        @pl.loop(0, per)
        def _(j):
            w = wid * per + j
            pltpu.sync_copy(x_hbm.at[pl.ds(w * SC_WIN, SC_WIN)], buf)
            pltpu.sync_copy(i_hbm.at[w], idx)
            copies = [pltpu.async_copy(buf, o_hbm.at[idx.at[kk]], sem) for kk in range(TOP_K)]
            for cp in copies:
                cp.wait()

    return scatter(rows, dest3)


def _expert_changed(te_ref):
    t = pl.program_id(1)
    return (t == 0) | (te_ref[t] != te_ref[jnp.maximum(t - 1, 0)])


def _moe_up_kernel(te_ref, nu_ref, x_ref, w1g_ref, w1l_ref, b1g_ref, b1l_ref, act_ref, wg_s, wl_s):
    @pl.when(_expert_changed(te_ref))
    def _():
        wg_s[...] = w1g_ref[0].astype(BF16)
        wl_s[...] = w1l_ref[0].astype(BF16)

    def up(words):
        lo, hi = _unpack_pairs(words)
        x = jnp.concatenate([lo.astype(BF16), hi.astype(BF16)], axis=1)
        glu = jnp.dot(x, wg_s[...], preferred_element_type=F32) + b1g_ref[0]
        lin = jnp.dot(x, wl_s[...], preferred_element_type=F32) + b1l_ref[0]
        glu = jnp.minimum(glu, SWIGLU_LIMIT)
        lin = jnp.clip(lin, -SWIGLU_LIMIT, SWIGLU_LIMIT)
        act_ref[...] = (glu * _sigmoid(SWIGLU_ALPHA * glu) * (lin + 1.0)).astype(BF16)

    t = pl.program_id(1)
    used = t < nu_ref[0]
    valid = nu_ref[1 + t]

    @pl.when(used & (valid >= MOE_TM))
    def _():
        up(x_ref[...])

    @pl.when(used & (valid < MOE_TM))
    def _():
        rowid = lax.broadcasted_iota(jnp.int32, (MOE_TM, HALF_D), 0)
        up(jnp.where(rowid < valid, x_ref[...], jnp.uint32(0)))


def _moe_down_kernel(te_ref, nu_ref, a_ref, w2_ref, b2_ref, y_ref, w2_s):
    @pl.when(_expert_changed(te_ref))
    def _():
        w2_s[...] = w2_ref[0].astype(BF16)

    @pl.when(pl.program_id(1) < nu_ref[0])
    def _():
        y = jnp.dot(a_ref[...], w2_s[...], preferred_element_type=F32) + b2_ref[0]
        y_ref[...] = _pack_pairs(y[:, :MOE_TN // 2], y[:, MOE_TN // 2:])


def _moe(tile_e, n_used, x_sorted, w1, b1, w2, b2):
    r = x_sorted.shape[0]
    n_tiles = r // MOE_TM
    nf = D_FF // MOE_TF
    nn = D_MODEL // MOE_TN

    def row(c, t, te, nu):
        return jnp.maximum(jnp.minimum(t, nu[0] - 1), 0)

    act = pl.pallas_call(
        _moe_up_kernel,
        grid_spec=pltpu.PrefetchScalarGridSpec(
            num_scalar_prefetch=2,
            grid=(nf, n_tiles),
            in_specs=[
                pl.BlockSpec((MOE_TM, HALF_D), lambda c, t, te, nu: (row(c, t, te, nu), 0)),
                pl.BlockSpec((1, D_MODEL, MOE_TF), lambda c, t, te, nu: (te[t], 0, c)),
                pl.BlockSpec((1, D_MODEL, MOE_TF), lambda c, t, te, nu: (te[t], 0, nf + c)),
                pl.BlockSpec((1, 1, MOE_TF), lambda c, t, te, nu: (te[t], 0, c)),
                pl.BlockSpec((1, 1, MOE_TF), lambda c, t, te, nu: (te[t], 0, nf + c)),
            ],
            out_specs=pl.BlockSpec((MOE_TM, MOE_TF), lambda c, t, te, nu: (row(c, t, te, nu), c)),
            scratch_shapes=[pltpu.VMEM((D_MODEL, MOE_TF), BF16), pltpu.VMEM((D_MODEL, MOE_TF), BF16)],
        ),
        out_shape=jax.ShapeDtypeStruct((r, D_FF), BF16),
        compiler_params=_cparams(("arbitrary", "arbitrary"), MOE_VMEM_MB),
        name="moe_up",
    )(tile_e, n_used, x_sorted, w1, w1, b1, b1)

    return pl.pallas_call(
        _moe_down_kernel,
        grid_spec=pltpu.PrefetchScalarGridSpec(
            num_scalar_prefetch=2,
            grid=(nn, n_tiles),
            in_specs=[
                pl.BlockSpec((MOE_TM, D_FF), lambda c, t, te, nu: (row(c, t, te, nu), 0)),
                pl.BlockSpec((1, D_FF, MOE_TN), lambda c, t, te, nu: (te[t], 0, c)),
                pl.BlockSpec((1, 1, MOE_TN), lambda c, t, te, nu: (te[t], 0, c)),
            ],
            out_specs=pl.BlockSpec((MOE_TM, MOE_TN // 2), lambda c, t, te, nu: (row(c, t, te, nu), c)),
            scratch_shapes=[pltpu.VMEM((D_FF, MOE_TN), BF16)],
        ),
        out_shape=jax.ShapeDtypeStruct((r, HALF_D), jnp.uint32),
        compiler_params=_cparams(("arbitrary", "arbitrary"), MOE_VMEM_MB),
        name="moe_down",
    )(tile_e, n_used, act, w2, b2)


def _combine_kernel(x_ref, y0_ref, y1_ref, y2_ref, y3_ref, tg_ref, g_ref, o_ref):
    tg = tg_ref[...]
    acc = x_ref[...]
    half = MOE_TN // 2
    for kk, y_ref in enumerate((y0_ref, y1_ref, y2_ref, y3_ref)):
        cols = []
        for c in range(D_MODEL // MOE_TN):
            cols += list(_unpack_pairs(y_ref[0, :, c * half:(c + 1) * half]))
        acc = acc + jnp.concatenate(cols, axis=1) * tg[:, kk:kk + 1]
    o_ref[...] = _rms(acc, g_ref[...])


def _combine(x2, yg, tg, g, row0):
    n = x2.shape[0]
    tm = min(TOK_TILE, n)
    blk0 = row0 // tm
    yspec = [pl.BlockSpec((1, tm, HALF_D), functools.partial(lambda i, kk: (kk, blk0 + i, 0), kk=kk))
             for kk in range(TOP_K)]
    return pl.pallas_call(
        _combine_kernel,
        grid=(n // tm,),
        in_specs=[pl.BlockSpec((tm, D_MODEL), lambda i: (i, 0))] + yspec + [
            pl.BlockSpec((tm, LANES), lambda i: (i, 0)),
            pl.BlockSpec((1, D_MODEL), lambda i: (0, 0)),
        ],
        out_specs=pl.BlockSpec((tm, D_MODEL), lambda i: (i, 0)),
        out_shape=jax.ShapeDtypeStruct((n, D_MODEL), F32),
        compiler_params=_cparams(("parallel",), 48),
        name="combine",
    )(x2, yg, yg, yg, yg, tg, g)


def _group_tiles(bsz, t):
    tt = min(t, TOK_TILE)
    bb = max(1, min(bsz, TOK_TILE // tt))
    return bb, tt


def _layer_group(x, c0, n0, m0, conv_state, mem_k, mem_v, cnt0, hn_all, row0, p):
    bsz, t, _ = x.shape
    n = bsz * t
    L = min(MLSTM_CHUNK, t)
    qkvo, u, gates = _proj_in(x.reshape(n, D_MODEL), p["g_mix"], p["w_main"], p["wg"], p["bg"], L)

    ctt = min(CONV_TT, t)
    cbb = max(1, min(bsz, CONV_TT // ctt))
    state_pad = jnp.pad(conv_state, ((0, 0), (CONV_PAD - CONV_STATE, 0), (0, 0)))
    cv, new_conv = _conv(u.reshape(bsz, t, CONV_CH), state_pad, p["conv_w"], p["conv_b"],
                         p["ln_g"], p["ln_b"], cbb, ctt)

    mbb = max(1, min(bsz, LANES // L)) if L < LANES else 1
    m0b = jnp.broadcast_to(m0[:, :, None], (bsz, M_HEADS, LANES))
    hm, c1, n1, m1 = _mlstm(qkvo.reshape(bsz, t, -1), gates, c0, n0, m0b, p["mh_g"], mbb, L)

    x1 = _proj_out(x.reshape(n, D_MODEL), hm.reshape(n, M_WIDTH), cv.reshape(n, CONV_CH), p["wo_a"], p["wo_b"])

    abb, att = _group_tiles(bsz, t)
    x2, hn_all, te, tg, tr, cnt = _attn_router(x1.reshape(bsz, t, D_MODEL), mem_k, mem_v, cnt0, hn_all, row0,
                                               p, abb, att)
    return x2.reshape(n, D_MODEL), hn_all, te, tg, tr, cnt, (c1, n1, m1[:, :, 0], new_conv)


def kernel(x_prompt, x_sample, mem_prompt, state_mlstm_C, state_mlstm_n, state_mlstm_m, state_conv, cache_mem_k, cache_mem_v, norm_mix_g, w_in, b_gates, mh_norm_g, conv_w, conv_b, conv_ln_g, conv_ln_b, w_out, norm_ca_g, norm_mem_g, ca_wq, ca_wk, ca_wv, ca_wo, norm_moe_g, router_w, router_b, moe_w1, moe_b1, moe_w2, moe_b2, final_norm_g):
    assert w_in.shape[0] == 1, "single layer"
    bp, tp, _ = x_prompt.shape
    bs, ts, _ = x_sample.shape
    n_p, n_s = bp * tp, bs * ts
    n_all = n_p + n_s

    wi = w_in[0]
    o0 = 0
    parts = []
    for width in (QK_W, QK_W, M_WIDTH, M_WIDTH, M_HEADS, M_HEADS, CONV_CH, CONV_CH):
        parts.append(wi[:, o0:o0 + width])
        o0 += width
    w_q, w_k, w_v, w_o, w_gi, w_gf, w_ua, w_ug = parts
    glu_cols = []
    for c in range(CONV_CH // GLU_HALF):
        glu_cols += [w_ua[:, c * GLU_HALF:(c + 1) * GLU_HALF], w_ug[:, c * GLU_HALF:(c + 1) * GLU_HALF]]
    p = {
        "g_mix": norm_mix_g[0][None, :],
        "w_main": jnp.concatenate([w_q, w_k, w_v, w_o] + glu_cols, axis=1).astype(BF16),
        "wg": jnp.concatenate([w_gi, w_gf], axis=1).T.astype(BF16),
        "bg": b_gates[0][:, None],
        "mh_g": mh_norm_g[0],
        "conv_w": conv_w[0],
        "conv_b": conv_b[0][None, :],
        "ln_g": conv_ln_g[0][None, :],
        "ln_b": conv_ln_b[0][None, :],
        "wo_a": w_out[0][:M_WIDTH].astype(BF16),
        "wo_b": w_out[0][M_WIDTH:].astype(BF16),
        "g_ca": norm_ca_g[0][None, :],
        "wq": ca_wq[0].astype(BF16),
        "wo": ca_wo[0].astype(BF16),
        "g_moe": norm_moe_g[0][None, :],
        "router_w": jnp.pad(router_w[0], ((0, 0), (0, LANES - N_EXPERTS))).astype(BF16),
        "router_b": jnp.concatenate([router_b[0], jnp.full((LANES - N_EXPERTS,), NEG_BIG, F32)])[None, :],
    }

    wkv = jnp.concatenate([ca_wk[0], ca_wv[0]], axis=1).astype(BF16)
    mk, mv = _mem_kv(mem_prompt.reshape(bp * N_MEM, D_MODEL), norm_mem_g[0][None, :], wkv)
    mk = mk.reshape(bp, N_MEM, CA_W)
    mv = mv.reshape(bp, N_MEM, CA_W)
    hn_all = jnp.zeros((n_all, HALF_D), jnp.uint32)
    cnt0 = jnp.zeros((1, LANES), F32)
    zc = jnp.zeros((bp, M_HEADS, M_DK, M_DV), F32)
    zn = jnp.zeros((bp, M_HEADS, M_DK), F32)
    zm = jnp.zeros((bp, M_HEADS), F32)
    zconv = jnp.zeros((bp, CONV_STATE, CONV_CH), F32)
    x2p, hn_all, te_p, tg_p, tr_p, cnt1, st_p = _layer_group(
        x_prompt, zc, zn, zm, zconv, mk.astype(BF16), mv.astype(BF16), cnt0, hn_all, 0, p)

    x2s, hn_all, te_s, tg_s, tr_s, cnt2, st_s = _layer_group(
        x_sample, state_mlstm_C[0], state_mlstm_n[0], state_mlstm_m[0], state_conv[0],
        cache_mem_k[0].reshape(bs, N_MEM, CA_W).astype(BF16), cache_mem_v[0].reshape(bs, N_MEM, CA_W).astype(BF16),
        cnt1, hn_all, n_p, p)

    te = jnp.concatenate([te_p[:, :TOP_K], te_s[:, :TOP_K]], axis=0)
    tr = jnp.concatenate([tr_p[:, :TOP_K], tr_s[:, :TOP_K]], axis=0)
    counts = cnt2[0, :N_EXPERTS].astype(jnp.int32)
    tiles_per_e = (counts + MOE_TM - 1) // MOE_TM
    tile_end = jnp.cumsum(tiles_per_e)
    row_start = (tile_end - tiles_per_e) * MOE_TM
    n_tiles = -(-(n_all * TOP_K) // MOE_TM) + N_EXPERTS
    n_used = tile_end[-1:]
    tile_ids = jnp.minimum(jnp.arange(n_tiles, dtype=jnp.int32), n_used[0] - 1)
    tile_e = jnp.minimum(jnp.sum(tile_ids[:, None] >= tile_end[None, :], axis=1), N_EXPERTS - 1).astype(jnp.int32)
    tile_first = jnp.arange(n_tiles, dtype=jnp.int32) - (tile_end - tiles_per_e)[tile_e]
    tile_valid = jnp.clip(counts[tile_e] - tile_first * MOE_TM, 0, MOE_TM)
    tile_valid = jnp.where(jnp.arange(n_tiles) < n_used[0], tile_valid, 0).astype(jnp.int32)
    moe_meta = jnp.concatenate([n_used.astype(jnp.int32), tile_valid])
    dest = row_start[te] + tr
    dest3 = dest.reshape(n_all // SC_WIN, SC_WIN, TOP_K).transpose(0, 2, 1)
    x_sorted = _sc_scatter_rows(hn_all, dest3, n_tiles * MOE_TM)

    y_sorted = _moe(tile_e, moe_meta, x_sorted, moe_w1[0], moe_b1[0][:, None, :], moe_w2[0],
                    moe_b2[0][:, None, :])

    yg = y_sorted.at[dest.T.reshape(-1)].get(mode="promise_in_bounds").reshape(TOP_K, n_all, HALF_D)
    fg = final_norm_g[None, :]
    y_prompt = _combine(x2p, yg, tg_p, fg, 0).reshape(bp, tp, D_MODEL)
    y_sample = _combine(x2s, yg, tg_s, fg, n_p).reshape(bs, ts, D_MODEL)

    c1, n1, m1, conv1 = st_p
    c2, n2, m2, conv2 = st_s
    mk4 = mk.reshape(1, bp, N_MEM, CA_HEADS, CA_DH)
    mv4 = mv.reshape(1, bp, N_MEM, CA_HEADS, CA_DH)
    return (y_prompt, y_sample, c1[None], n1[None], m1[None], conv1[None], mk4, mv4,
            c2[None], n2[None], m2[None], conv2[None])
```

```python
import functools

import jax
import jax.numpy as jnp
from jax import lax
from jax.experimental import pallas as pl
from jax.experimental.pallas import tpu as pltpu
from jax.experimental.pallas import tpu_sc as plsc

F32 = jnp.float32
BF16 = jnp.bfloat16
HIGHEST = lax.Precision.HIGHEST

D_MODEL = 2048
HALF_D = D_MODEL // 2
M_HEADS = 4
M_DV = 256
M_DK = 128
QK_W = M_HEADS * M_DK
M_WIDTH = M_HEADS * M_DV
CONV_CH = 1024
CONV_WIDTH = 31
CONV_STATE = CONV_WIDTH - 1
CONV_PAD = 32
N_MEM = 256
CA_HEADS = 4
CA_DH = 128
CA_W = CA_HEADS * CA_DH
N_EXPERTS = 32
TOP_K = 4
D_FF = 2048
SWIGLU_ALPHA = 1.702
SWIGLU_LIMIT = 7.0
EPS = 1e-6
LANES = 128
SUBLANES = 8
NEG_BIG = -1e30

PROJ_TM = 1024
PROJ_STEPS = 4
QKVO_TN = (2 * QK_W + 2 * M_WIDTH) // PROJ_STEPS
GLU_TN = CONV_CH // PROJ_STEPS
PROJ_TN = QKVO_TN + 2 * GLU_TN
CONV_SUB = PROJ_TM // PROJ_STEPS
TOK_TILE = 512
MOE_TM = 512
MOE_TF = 1024
MOE_TN = 2048
MOE_VMEM_MB = 60
MLSTM_CHUNK = 256
CONV_TT = 256
CONV_RB = 32
CONV_LC = 512
SC_WIN = 32


def _cparams(sem, vmem_mb):
    return pltpu.CompilerParams(dimension_semantics=sem, vmem_limit_bytes=vmem_mb << 20)


def _log_sigmoid(x):
    return jnp.minimum(x, 0.0) - jnp.log1p(jnp.exp(-jnp.abs(x)))


def _sigmoid(x):
    return 1.0 / (1.0 + jnp.exp(-x))


def _pack_pairs(lo, hi):
    ulo = lax.bitcast_convert_type(lo.astype(BF16).astype(F32), jnp.uint32)
    uhi = lax.bitcast_convert_type(hi.astype(BF16).astype(F32), jnp.uint32)
    return (ulo >> 16) | uhi


def _unpack_pairs(w):
    lo = lax.bitcast_convert_type(w << 16, F32)
    hi = lax.bitcast_convert_type(w & jnp.uint32(0xFFFF0000), F32)
    return lo, hi


def _rms(x, g):
    ms = jnp.mean(x * x, axis=-1, keepdims=True)
    return x * lax.rsqrt(ms + EPS) * g


def _proj_prologue(x_ref, g_ref, wg_ref, bg_ref, h_ref, gates_ref, chunk):
    tm = x_ref.shape[0]
    h = _rms(x_ref[...], g_ref[...]).astype(BF16)
    h_ref[...] = h
    gt = lax.dot_general(wg_ref[...], h, (((1,), (1,)), ((), ())), preferred_element_type=F32)
    gt = gt + bg_ref[...]
    row = lax.broadcasted_iota(jnp.int32, gt.shape, 0)
    gt = jnp.where(row < M_HEADS, gt, _log_sigmoid(gt))
    for c in range(tm // chunk):
        gates_ref[c] = gt[:, c * chunk:(c + 1) * chunk]


def _proj_step(h_ref, w_ref):
    z = jnp.dot(h_ref[...], w_ref[...], preferred_element_type=F32)
    glu = z[:, QKVO_TN:QKVO_TN + GLU_TN] * _sigmoid(z[:, QKVO_TN + GLU_TN:])
    return z[:, :QKVO_TN].astype(BF16), glu.astype(BF16)


def _proj_in_kernel(x_ref, g_ref, w_ref, wg_ref, bg_ref, qkvo_ref, u_ref, gates_ref, h_ref, *, chunk):
    @pl.when(pl.program_id(1) == 0)
    def _():
        _proj_prologue(x_ref, g_ref, wg_ref, bg_ref, h_ref, gates_ref, chunk)

    qkvo_ref[...], u_ref[...] = _proj_step(h_ref, w_ref)


def _proj_in(x, g, w_main, wg, bg, chunk):
    n = x.shape[0]
    tm = min(PROJ_TM, n)
    return pl.pallas_call(
        functools.partial(_proj_in_kernel, chunk=chunk),
        grid=(n // tm, PROJ_STEPS),
        in_specs=[
            pl.BlockSpec((tm, D_MODEL), lambda i, j: (i, 0)),
            pl.BlockSpec((1, D_MODEL), lambda i, j: (0, 0)),
            pl.BlockSpec((D_MODEL, PROJ_TN), lambda i, j: (0, j)),
            pl.BlockSpec((2 * M_HEADS, D_MODEL), lambda i, j: (0, 0)),
            pl.BlockSpec((2 * M_HEADS, 1), lambda i, j: (0, 0)),
        ],
        out_specs=[
            pl.BlockSpec((tm, QKVO_TN), lambda i, j: (i, j)),
            pl.BlockSpec((tm, GLU_TN), lambda i, j: (i, j)),
            pl.BlockSpec((tm // chunk, 2 * M_HEADS, chunk), lambda i, j: (i, 0, 0)),
        ],
        out_shape=[
            jax.ShapeDtypeStruct((n, 2 * QK_W + 2 * M_WIDTH), BF16),
            jax.ShapeDtypeStruct((n, CONV_CH), BF16),
            jax.ShapeDtypeStruct((n // chunk, 2 * M_HEADS, chunk), F32),
        ],
        scratch_shapes=[pltpu.VMEM((tm, D_MODEL), BF16)],
        compiler_params=_cparams(("parallel", "arbitrary"), 48),
        name="proj_in",
    )(x, g, w_main, wg, bg)


def _proj_in_conv_kernel(x_ref, g_ref, w_ref, wg_ref, bg_ref, st_ref, cw_ref, cb_ref, lg_ref, lb_ref,
                         qkvo_ref, gates_ref, cv_ref, ns_ref, h_ref, u_s, hist_s, e_ref, er_ref, c_ref,
                         *, chunk, tiles_per_seq):
    i = pl.program_id(0)
    j = pl.program_id(1)
    p = i - 1
    sub = CONV_SUB
    ext = sub + CONV_PAD

    @pl.when((i == 0) & (j == 0))
    def _():
        u_s[...] = jnp.zeros(u_s.shape, BF16)
        hist_s[...] = jnp.zeros(hist_s.shape, F32)
        e_ref[:, ext:, :] = jnp.zeros((1, SUBLANES, CONV_CH), F32)

    @pl.when(j == 0)
    def _():
        _proj_prologue(x_ref, g_ref, wg_ref, bg_ref, h_ref, gates_ref, chunk)

    @pl.when((j == 0) & (p >= 0) & (p % tiles_per_seq == 0))
    def _():
        hist_s[...] = st_ref[0]

    r0 = pl.multiple_of(j * sub, sub)
    e_ref[0, 0:CONV_PAD, :] = hist_s[...]
    for c in range(PROJ_STEPS):
        e_ref[0, CONV_PAD:ext, c * GLU_TN:(c + 1) * GLU_TN] = u_s[(i + 1) % 2, c, pl.ds(r0, sub), :].astype(F32)
    _conv_rows(e_ref, er_ref, c_ref, cw_ref, cb_ref, lg_ref, lb_ref, cv_ref, 0, sub, unroll=True)
    hist_s[...] = e_ref[0, sub:ext, :]

    qkvo, u = _proj_step(h_ref, w_ref)
    qkvo_ref[...] = qkvo
    u_s[i % 2, j] = u

    @pl.when((j == PROJ_STEPS - 1) & (p >= 0) & (p % tiles_per_seq == tiles_per_seq - 1))
    def _():
        ns_ref[0] = e_ref[0, ext - CONV_STATE:ext, :]


def _proj_in_conv(x, g, w_main, wg, bg, chunk, state_pad, conv_w, conv_b, ln_g, ln_b, seq_len):
    n = x.shape[0]
    tm = PROJ_TM
    n_tiles = n // tm
    tiles_per_seq = seq_len // tm
    bsz = n // seq_len
    assert tm == PROJ_STEPS * CONV_SUB and seq_len % tm == 0
    last = n_tiles - 1

    def tile(i, j):
        return jnp.minimum(i, last)

    def seq(i, j):
        return jnp.clip((i - 1) // tiles_per_seq, 0, bsz - 1)

    const2 = lambda i, j: (0, 0)
    return pl.pallas_call(
        functools.partial(_proj_in_conv_kernel, chunk=chunk, tiles_per_seq=tiles_per_seq),
        grid=(n_tiles + 1, PROJ_STEPS),
        in_specs=[
            pl.BlockSpec((tm, D_MODEL), lambda i, j: (tile(i, j), 0)),
            pl.BlockSpec((1, D_MODEL), const2),
            pl.BlockSpec((D_MODEL, PROJ_TN), lambda i, j: (0, j)),
            pl.BlockSpec((2 * M_HEADS, D_MODEL), const2),
            pl.BlockSpec((2 * M_HEADS, 1), const2),
            pl.BlockSpec((1, CONV_PAD, CONV_CH), lambda i, j: (seq(i, j), 0, 0)),
            pl.BlockSpec((CONV_WIDTH, CONV_CH), const2),
            pl.BlockSpec((1, CONV_CH), const2),
            pl.BlockSpec((1, CONV_CH), const2),
            pl.BlockSpec((1, CONV_CH), const2),
        ],
        out_specs=[
            pl.BlockSpec((tm, QKVO_TN), lambda i, j: (tile(i, j), jnp.where(i <= last, j, PROJ_STEPS - 1))),
            pl.BlockSpec((tm // chunk, 2 * M_HEADS, chunk), lambda i, j: (tile(i, j), 0, 0)),
            pl.BlockSpec((CONV_SUB, CONV_CH), lambda i, j: (jnp.maximum((i - 1) * PROJ_STEPS + j, 0), 0)),
            pl.BlockSpec((1, CONV_STATE, CONV_CH), lambda i, j: (seq(i, j), 0, 0)),
        ],
        out_shape=[
            jax.ShapeDtypeStruct((n, 2 * QK_W + 2 * M_WIDTH), BF16),
            jax.ShapeDtypeStruct((n // chunk, 2 * M_HEADS, chunk), F32),
            jax.ShapeDtypeStruct((n, CONV_CH), BF16),
            jax.ShapeDtypeStruct((bsz, CONV_STATE, CONV_CH), F32),
        ],
        scratch_shapes=[
            pltpu.VMEM((tm, D_MODEL), BF16),
            pltpu.VMEM((2, PROJ_STEPS, tm, GLU_TN), BF16),
            pltpu.VMEM((CONV_PAD, CONV_CH), F32),
            pltpu.VMEM((1, CONV_SUB + CONV_PAD + SUBLANES, CONV_CH), F32),
            pltpu.VMEM((SUBLANES - 1, CONV_SUB + CONV_PAD, CONV_CH), F32),
            pltpu.VMEM((CONV_SUB, CONV_CH), F32),
        ],
        compiler_params=_cparams(("arbitrary", "arbitrary"), 56),
        name="proj_in_conv",
    )(x, g, w_main, wg, bg, state_pad, conv_w, conv_b, ln_g, ln_b)


def _conv_rows(e_ref, er_ref, c_ref, w_ref, cb_ref, lg_ref, lb_ref, out_ref, b, rows, unroll):
    ext = rows + CONV_PAD
    base = CONV_PAD - CONV_STATE
    for r in range(1, SUBLANES):
        er_ref[r - 1] = e_ref[b, r:r + ext, :]

    def row_block(r0):
        for lc in range(CONV_CH // CONV_LC):
            ls = slice(lc * CONV_LC, (lc + 1) * CONV_LC)
            acc = jnp.zeros((CONV_RB, CONV_LC), F32) + cb_ref[:, ls]
            for j in range(CONV_WIDTH):
                off = j + base
                a, r = (off // SUBLANES) * SUBLANES, off % SUBLANES
                if r == 0:
                    tap = e_ref[b, pl.ds(r0 + a, CONV_RB), ls]
                else:
                    tap = er_ref[r - 1, pl.ds(r0 + a, CONV_RB), ls]
                acc = acc + w_ref[j:j + 1, ls] * tap
            c_ref[pl.ds(r0, CONV_RB), ls] = acc
        c = c_ref[pl.ds(r0, CONV_RB), :]
        mu = jnp.mean(c, axis=-1, keepdims=True)
        cc = c - mu
        var = jnp.mean(cc * cc, axis=-1, keepdims=True)
        cn = cc * lax.rsqrt(var + EPS) * lg_ref[...] + lb_ref[...]
        out_ref[pl.ds(r0, CONV_RB), :] = (cn * _sigmoid(cn)).astype(BF16)

    if unroll:
        for rb in range(rows // CONV_RB):
            row_block(rb * CONV_RB)
    else:
        def body(rb, carry):
            row_block(pl.multiple_of(rb * CONV_RB, CONV_RB))
            return carry

        lax.fori_loop(0, rows // CONV_RB, body, 0)


def _conv_kernel(u_ref, st_ref, w_ref, cb_ref, lg_ref, lb_ref, cv_ref, ns_ref, e_ref, er_ref, c_ref, *, bb, tt):
    t = pl.program_id(1)
    nt = pl.num_programs(1)
    ext = tt + CONV_PAD

    @pl.when(t == 0)
    def _():
        e_ref[:, 0:CONV_PAD, :] = st_ref[...]

    @pl.when(t > 0)
    def _():
        e_ref[:, 0:CONV_PAD, :] = e_ref[:, tt:ext, :]

    e_ref[:, CONV_PAD:ext, :] = u_ref[...].astype(F32)
    e_ref[:, ext:, :] = jnp.zeros((bb, SUBLANES, CONV_CH), F32)

    for b in range(bb):
        _conv_rows(e_ref, er_ref, c_ref, w_ref, cb_ref, lg_ref, lb_ref, cv_ref.at[b], b, tt, unroll=False)

    @pl.when(t == nt - 1)
    def _():
        ns_ref[...] = e_ref[:, ext - CONV_STATE:ext, :]


def _conv(u, state_pad, conv_w, conv_b, ln_g, ln_b, bb, tt):
    bsz, t, _ = u.shape
    return pl.pallas_call(
        functools.partial(_conv_kernel, bb=bb, tt=tt),
        grid=(bsz // bb, t // tt),
        in_specs=[
            pl.BlockSpec((bb, tt, CONV_CH), lambda b, s: (b, s, 0)),
            pl.BlockSpec((bb, CONV_PAD, CONV_CH), lambda b, s: (b, 0, 0)),
            pl.BlockSpec((CONV_WIDTH, CONV_CH), lambda b, s: (0, 0)),
            pl.BlockSpec((1, CONV_CH), lambda b, s: (0, 0)),
            pl.BlockSpec((1, CONV_CH), lambda b, s: (0, 0)),
            pl.BlockSpec((1, CONV_CH), lambda b, s: (0, 0)),
        ],
        out_specs=[
            pl.BlockSpec((bb, tt, CONV_CH), lambda b, s: (b, s, 0)),
            pl.BlockSpec((bb, CONV_STATE, CONV_CH), lambda b, s: (b, 0, 0)),
        ],
        out_shape=[
            jax.ShapeDtypeStruct((bsz, t, CONV_CH), BF16),
            jax.ShapeDtypeStruct((bsz, CONV_STATE, CONV_CH), F32),
        ],
        scratch_shapes=[
            pltpu.VMEM((bb, tt + CONV_PAD + SUBLANES, CONV_CH), F32),
            pltpu.VMEM((SUBLANES - 1, tt + CONV_PAD, CONV_CH), F32),
            pltpu.VMEM((tt, CONV_CH), F32),
        ],
        compiler_params=_cparams(("parallel", "arbitrary"), 40),
        name="conv",
    )(u, state_pad, conv_w, conv_b, ln_g, ln_b)


def _mlstm_kernel(q_ref, k_ref, v_ref, o_ref, gt_ref, c0_ref, n0_ref, m0_ref, ng_ref,
                  hm_ref, cn_ref, nn_ref, mn_ref, c_s, n_s, m_s, *, bb, L):
    ci = pl.program_id(1)
    nc = pl.num_programs(1)

    @pl.when(ci == 0)
    def _():
        c_s[...] = c0_ref[...]
        n_s[...] = n0_ref[...]
        m_s[...] = m0_ref[...]

    rt = lax.broadcasted_iota(jnp.int32, (L, L), 0)
    cs = lax.broadcasted_iota(jnp.int32, (L, L), 1)
    causal = cs <= rt
    lower = causal.astype(F32)
    upper = (rt <= cs).astype(F32)
    eye = (rt == cs).astype(F32)
    nt_dims = (((1,), (1,)), ((), ()))
    scale = M_DK ** -0.5

    for b in range(bb):
        rows = gt_ref[b]
        cum_rows = jnp.dot(rows, upper, precision=HIGHEST, preferred_element_type=F32)
        cum_cols = lax.dot_general(lower, rows, nt_dims, precision=HIGHEST, preferred_element_type=F32)
        id_cols = lax.dot_general(eye, rows, nt_dims, precision=HIGHEST, preferred_element_type=F32)
        for h in range(M_HEADS):
            ig_row = rows[h:h + 1, :]
            ig_col = id_cols[:, h:h + 1]
            b_row = cum_rows[M_HEADS + h:M_HEADS + h + 1, :]
            b_col = cum_cols[:, M_HEADS + h:M_HEADS + h + 1]
            m0 = m_s[b, h:h + 1, 0:1]
            q = q_ref[b, :, h * M_DK:(h + 1) * M_DK]
            k = k_ref[b, :, h * M_DK:(h + 1) * M_DK]
            v = v_ref[b, :, h * M_DV:(h + 1) * M_DV]
            c0 = c_s[b, h]
            n0 = n_s[b, h:h + 1, :]

            g_col = b_col + m0
            dmat = jnp.where(causal, b_col - b_row + ig_row, -jnp.inf)
            m_col = jnp.maximum(g_col, jnp.max(dmat, axis=1, keepdims=True))
            w_state = jnp.exp(g_col - m_col) * scale
            qk = lax.dot_general(q, k, nt_dims, preferred_element_type=F32)
            s = qk * (jnp.exp(dmat - m_col) * scale)
            num = jnp.dot(s.astype(BF16), v, preferred_element_type=F32)
            num = num + w_state * jnp.dot(q, c0.astype(BF16), preferred_element_type=F32)
            qn = jnp.sum(q.astype(F32) * n0, axis=1, keepdims=True)
            den = jnp.sum(s, axis=1, keepdims=True) + w_state * qn
            hh = num / jnp.maximum(jnp.abs(den), jnp.exp(-m_col))

            hh = hh * lax.rsqrt(jnp.mean(hh * hh, axis=-1, keepdims=True) + EPS) * ng_ref[h:h + 1, :]
            og = _sigmoid(o_ref[b, :, h * M_DV:(h + 1) * M_DV].astype(F32))
            hm_ref[b, :, h * M_DV:(h + 1) * M_DV] = (og * hh).astype(BF16)

            b_last = b_row[:, L - 1:L]
            g_last = b_last + m0
            wk_row = b_last - b_row + ig_row
            m_new = jnp.maximum(g_last, jnp.max(wk_row, axis=1, keepdims=True))
            a0 = jnp.exp(g_last - m_new)
            ak_col = jnp.exp(b_last - b_col + ig_col - m_new)
            ks = k.astype(F32) * ak_col
            kv = lax.dot_general(ks.astype(BF16), v, (((0,), (0,)), ((), ())), preferred_element_type=F32)
            c_s[b, h] = a0 * c0 + kv
            n_s[b, h:h + 1, :] = a0 * n0 + jnp.sum(ks, axis=0, keepdims=True)
            m_s[b, h:h + 1, :] = jnp.broadcast_to(m_new, (1, LANES))

    @pl.when(ci == nc - 1)
    def _():
        cn_ref[...] = c_s[...]
        nn_ref[...] = n_s[...]
        mn_ref[...] = m_s[...]


def _mlstm(qkvo, gates, c0, n0, m0, ng, bb, L):
    bsz, t, _ = qkvo.shape
    nc = t // L
    gates = gates.reshape(bsz, nc, 2 * M_HEADS, L)
    st_c = pl.BlockSpec((bb, M_HEADS, M_DK, M_DV), lambda b, c: (b, 0, 0, 0))
    st_n = pl.BlockSpec((bb, M_HEADS, M_DK), lambda b, c: (b, 0, 0))
    return pl.pallas_call(
        functools.partial(_mlstm_kernel, bb=bb, L=L),
        grid=(bsz // bb, nc),
        in_specs=[
            pl.BlockSpec((bb, L, QK_W), lambda b, c: (b, c, 0)),
            pl.BlockSpec((bb, L, QK_W), lambda b, c: (b, c, 1)),
            pl.BlockSpec((bb, L, M_WIDTH), lambda b, c: (b, c, 1)),
            pl.BlockSpec((bb, L, M_WIDTH), lambda b, c: (b, c, 2)),
            pl.BlockSpec((bb, None, 2 * M_HEADS, L), lambda b, c: (b, c, 0, 0)),
            st_c, st_n, st_n,
            pl.BlockSpec((M_HEADS, M_DV), lambda b, c: (0, 0)),
        ],
        out_specs=[
            pl.BlockSpec((bb, L, M_WIDTH), lambda b, c: (b, c, 0)),
            st_c, st_n, st_n,
        ],
        out_shape=[
            jax.ShapeDtypeStruct((bsz, t, M_WIDTH), BF16),
            jax.ShapeDtypeStruct((bsz, M_HEADS, M_DK, M_DV), F32),
            jax.ShapeDtypeStruct((bsz, M_HEADS, M_DK), F32),
            jax.ShapeDtypeStruct((bsz, M_HEADS, LANES), F32),
        ],
        scratch_shapes=[
            pltpu.VMEM((bb, M_HEADS, M_DK, M_DV), F32),
            pltpu.VMEM((bb, M_HEADS, M_DK), F32),
            pltpu.VMEM((bb, M_HEADS, LANES), F32),
        ],
        compiler_params=_cparams(("parallel", "arbitrary"), 40),
        name="mlstm",
    )(qkvo, qkvo, qkvo, qkvo, gates, c0, n0, m0, ng)


def _proj_out_kernel(x_ref, hm_ref, cv_ref, wa_ref, wb_ref, o_ref):
    acc = jnp.dot(hm_ref[...], wa_ref[...], preferred_element_type=F32)
    acc = acc + jnp.dot(cv_ref[...], wb_ref[...], preferred_element_type=F32)
    o_ref[...] = x_ref[...] + acc


def _proj_out(x, hm, cv, wa, wb):
    n = x.shape[0]
    tm = min(TOK_TILE, n)
    return pl.pallas_call(
        _proj_out_kernel,
        grid=(n // tm,),
        in_specs=[
            pl.BlockSpec((tm, D_MODEL), lambda i: (i, 0)),
            pl.BlockSpec((tm, M_WIDTH), lambda i: (i, 0)),
            pl.BlockSpec((tm, CONV_CH), lambda i: (i, 0)),
            pl.BlockSpec((M_WIDTH, D_MODEL), lambda i: (0, 0)),
            pl.BlockSpec((CONV_CH, D_MODEL), lambda i: (0, 0)),
        ],
        out_specs=pl.BlockSpec((tm, D_MODEL), lambda i: (i, 0)),
        out_shape=jax.ShapeDtypeStruct((n, D_MODEL), F32),
        compiler_params=_cparams(("parallel",), 48),
        name="proj_out",
    )(x, hm, cv, wa, wb)


def _mem_kv_kernel(m_ref, g_ref, w_ref, k_ref, v_ref):
    mn = _rms(m_ref[...], g_ref[...]).astype(BF16)
    z = jnp.dot(mn, w_ref[...], preferred_element_type=F32)
    k_ref[...] = z[:, :CA_W]
    v_ref[...] = z[:, CA_W:]


def _mem_kv(mem, g, wkv):
    n = mem.shape[0]
    return pl.pallas_call(
        _mem_kv_kernel,
        grid=(1,),
        in_specs=[
            pl.BlockSpec((n, D_MODEL), lambda i: (0, 0)),
            pl.BlockSpec((1, D_MODEL), lambda i: (0, 0)),
            pl.BlockSpec((D_MODEL, 2 * CA_W), lambda i: (0, 0)),
        ],
        out_specs=[pl.BlockSpec((n, CA_W), lambda i: (0, 0))] * 2,
        out_shape=[jax.ShapeDtypeStruct((n, CA_W), F32)] * 2,
        compiler_params=_cparams(("arbitrary",), 40),
        name="mem_kv",
    )(mem, g, wkv)


def _attn_router_kernel(x_ref, gca_ref, wq_ref, k_ref, v_ref, wo_ref, gmoe_ref, rw_ref, rb_ref, cnt0_ref,
                        hn_in_ref, x2_ref, hn_ref, te_ref, tg_ref, tr_ref, cnt_ref, o_s, cnt_s, *, bb, tt):
    del hn_in_ref
    rows = bb * tt

    @pl.when((pl.program_id(0) == 0) & (pl.program_id(1) == 0))
    def _():
        cnt_s[...] = cnt0_ref[...]

    x = x_ref[...].reshape(rows, D_MODEL)
    h = _rms(x, gca_ref[...]).astype(BF16)
    q = jnp.dot(h, wq_ref[...], preferred_element_type=F32) * (CA_DH ** -0.5)
    q = q.astype(BF16)
    for b in range(bb):
        for hd in range(CA_HEADS):
            cols = slice(hd * CA_DH, (hd + 1) * CA_DH)
            qb = q[b * tt:(b + 1) * tt, cols]
            s = lax.dot_general(qb, k_ref[b, :, cols], (((1,), (1,)), ((), ())), preferred_element_type=F32)
            e = jnp.exp(s - jnp.max(s, axis=-1, keepdims=True))
            p = (e / jnp.sum(e, axis=-1, keepdims=True)).astype(BF16)
            o_s[b * tt:(b + 1) * tt, cols] = jnp.dot(p, v_ref[b, :, cols], preferred_element_type=F32).astype(BF16)
    x2 = x + jnp.dot(o_s[...], wo_ref[...], preferred_element_type=F32)
    x2_ref[...] = x2.reshape(bb, tt, D_MODEL)

    hn = _rms(x2, gmoe_ref[...])
    hn_bf = hn.astype(BF16)
    hn_ref[...] = _pack_pairs(hn[:, :HALF_D], hn[:, HALF_D:])
    logits = jnp.dot(hn_bf, rw_ref[...], preferred_element_type=F32) + rb_ref[...]
    lane = lax.broadcasted_iota(jnp.int32, (rows, LANES), 1)
    work = logits
    sel = []
    vals = []
    for _ in range(TOP_K):
        mx = jnp.max(work, axis=-1, keepdims=True)
        idx = jnp.min(jnp.where(work == mx, lane, LANES), axis=-1, keepdims=True)
        hit = lane == idx
        sel.append((idx, hit))
        vals.append(mx)
        work = jnp.where(hit, -jnp.inf, work)
    ex = [jnp.exp(vv - vals[0]) for vv in vals]
    tot = ex[0] + ex[1] + ex[2] + ex[3]
    assigned = jnp.zeros((rows, LANES), F32)
    for _, hit in sel:
        assigned = assigned + hit.astype(F32)
    rt = lax.broadcasted_iota(jnp.int32, (rows, rows), 0)
    cs = lax.broadcasted_iota(jnp.int32, (rows, rows), 1)
    strict = (cs < rt).astype(BF16)
    before = jnp.dot(strict, assigned.astype(BF16), preferred_element_type=F32) + cnt_s[...]
    te = jnp.zeros((rows, LANES), jnp.int32)
    tg = jnp.zeros((rows, LANES), F32)
    tr = jnp.zeros((rows, LANES), F32)
    for kk, (idx, hit) in enumerate(sel):
        rank = jnp.sum(jnp.where(hit, before, 0.0), axis=-1, keepdims=True)
        te = jnp.where(lane == kk, idx, te)
        tg = jnp.where(lane == kk, ex[kk] / tot, tg)
        tr = jnp.where(lane == kk, rank, tr)
    te_ref[...] = te
    tg_ref[...] = tg
    tr_ref[...] = tr.astype(jnp.int32)
    cnt_s[...] = cnt_s[...] + jnp.sum(assigned, axis=0, keepdims=True)
    cnt_ref[...] = cnt_s[...]


def _attn_router(x1, mem_k, mem_v, cnt0, hn_all, row0, p, bb, tt):
    bsz, t, _ = x1.shape
    rows = bb * tt
    n = bsz * t
    nb, nt = bsz // bb, t // tt
    blk0 = row0 // rows
    tok = lambda b, s: (b * nt + s, 0)
    const = lambda b, s: (0, 0)
    grid_spec = dict(
        grid=(nb, nt),
        in_specs=[
            pl.BlockSpec((bb, tt, D_MODEL), lambda b, s: (b, s, 0)),
            pl.BlockSpec((1, D_MODEL), const),
            pl.BlockSpec((D_MODEL, CA_W), const),
            pl.BlockSpec((bb, N_MEM, CA_W), lambda b, s: (b, 0, 0)),
            pl.BlockSpec((bb, N_MEM, CA_W), lambda b, s: (b, 0, 0)),
            pl.BlockSpec((CA_W, D_MODEL), const),
            pl.BlockSpec((1, D_MODEL), const),
            pl.BlockSpec((D_MODEL, LANES), const),
            pl.BlockSpec((1, LANES), const),
            pl.BlockSpec((1, LANES), const),
            pl.BlockSpec(memory_space=pl.ANY),
        ],
        out_specs=[
            pl.BlockSpec((bb, tt, D_MODEL), lambda b, s: (b, s, 0)),
            pl.BlockSpec((rows, HALF_D), lambda b, s: (blk0 + b * nt + s, 0)),
            pl.BlockSpec((rows, LANES), tok),
            pl.BlockSpec((rows, LANES), tok),
            pl.BlockSpec((rows, LANES), tok),
            pl.BlockSpec((1, LANES), const),
        ],
    )
    return pl.pallas_call(
        functools.partial(_attn_router_kernel, bb=bb, tt=tt),
        out_shape=[
            jax.ShapeDtypeStruct((bsz, t, D_MODEL), F32),
            jax.ShapeDtypeStruct(hn_all.shape, jnp.uint32),
            jax.ShapeDtypeStruct((n, LANES), jnp.int32),
            jax.ShapeDtypeStruct((n, LANES), F32),
            jax.ShapeDtypeStruct((n, LANES), jnp.int32),
            jax.ShapeDtypeStruct((1, LANES), F32),
        ],
        scratch_shapes=[pltpu.VMEM((rows, CA_W), BF16), pltpu.VMEM((1, LANES), F32)],
        input_output_aliases={10: 1},
        compiler_params=_cparams(("arbitrary", "arbitrary"), 48),
        name="attn_router",
        **grid_spec,
    )(x1, p["g_ca"], p["wq"], mem_k, mem_v, p["wo"], p["g_moe"], p["router_w"], p["router_b"], cnt0, hn_all)


def _sc_scatter_rows(rows, dest3, n_out):
    n, d = rows.shape
    nwin = dest3.shape[0]
    mesh = plsc.VectorSubcoreMesh(core_axis_name="c", subcore_axis_name="s")
    workers = mesh.num_cores * mesh.num_subcores
    assert nwin * SC_WIN == n and nwin % workers == 0, (n, nwin, workers)
    per = nwin // workers

    @functools.partial(
        pl.kernel, out_type=jax.ShapeDtypeStruct((n_out, d), rows.dtype), mesh=mesh,
        scratch_types=[pltpu.VMEM((SC_WIN, d), rows.dtype), pltpu.VMEM((TOP_K, SC_WIN), jnp.int32),
                       pltpu.SemaphoreType.DMA])
    def scatter(x_hbm, i_hbm, o_hbm, buf, idx, sem):
        wid = lax.axis_index("s") * mesh.num_cores + lax.axis_index("c")

        @pl.loop(0, per)
        def _(j):
            w = wid * per + j
            pltpu.sync_copy(x_hbm.at[pl.ds(w * SC_WIN, SC_WIN)], buf)
            pltpu.sync_copy(i_hbm.at[w], idx)
            copies = [pltpu.async_copy(buf, o_hbm.at[idx.at[kk]], sem) for kk in range(TOP_K)]
            for cp in copies:
                cp.wait()

    return scatter(rows, dest3)


def _expert_changed(te_ref):
    t = pl.program_id(1)
    return (t == 0) | (te_ref[t] != te_ref[jnp.maximum(t - 1, 0)])


def _moe_up_kernel(te_ref, nu_ref, x_ref, w1g_ref, w1l_ref, b1g_ref, b1l_ref, act_ref, wg_s, wl_s):
    @pl.when(_expert_changed(te_ref))
    def _():
        wg_s[...] = w1g_ref[0].astype(BF16)
        wl_s[...] = w1l_ref[0].astype(BF16)

    def up(words):
        lo, hi = _unpack_pairs(words)
        x = jnp.concatenate([lo.astype(BF16), hi.astype(BF16)], axis=1)
        glu = jnp.dot(x, wg_s[...], preferred_element_type=F32) + b1g_ref[0]
        lin = jnp.dot(x, wl_s[...], preferred_element_type=F32) + b1l_ref[0]
        glu = jnp.minimum(glu, SWIGLU_LIMIT)
        lin = jnp.clip(lin, -SWIGLU_LIMIT, SWIGLU_LIMIT)
        act_ref[...] = (glu * _sigmoid(SWIGLU_ALPHA * glu) * (lin + 1.0)).astype(BF16)

    t = pl.program_id(1)
    used = t < nu_ref[0]
    valid = nu_ref[1 + t]

    @pl.when(used & (valid >= MOE_TM))
    def _():
        up(x_ref[...])

    @pl.when(used & (valid < MOE_TM))
    def _():
        rowid = lax.broadcasted_iota(jnp.int32, (MOE_TM, HALF_D), 0)
        up(jnp.where(rowid < valid, x_ref[...], jnp.uint32(0)))


def _moe_down_kernel(te_ref, nu_ref, a_ref, w2_ref, b2_ref, y_ref, w2_s):
    @pl.when(_expert_changed(te_ref))
    def _():
        w2_s[...] = w2_ref[0].astype(BF16)

    @pl.when(pl.program_id(1) < nu_ref[0])
    def _():
        y = jnp.dot(a_ref[...], w2_s[...], preferred_element_type=F32) + b2_ref[0]
        y_ref[...] = _pack_pairs(y[:, :MOE_TN // 2], y[:, MOE_TN // 2:])


def _moe(tile_e, n_used, x_sorted, w1, b1, w2, b2):
    r = x_sorted.shape[0]
    n_tiles = r // MOE_TM
    nf = D_FF // MOE_TF
    nn = D_MODEL // MOE_TN

    def row(c, t, te, nu):
        return jnp.maximum(jnp.minimum(t, nu[0] - 1), 0)

    act = pl.pallas_call(
        _moe_up_kernel,
        grid_spec=pltpu.PrefetchScalarGridSpec(
            num_scalar_prefetch=2,
            grid=(nf, n_tiles),
            in_specs=[
                pl.BlockSpec((MOE_TM, HALF_D), lambda c, t, te, nu: (row(c, t, te, nu), 0)),
                pl.BlockSpec((1, D_MODEL, MOE_TF), lambda c, t, te, nu: (te[t], 0, c)),
                pl.BlockSpec((1, D_MODEL, MOE_TF), lambda c, t, te, nu: (te[t], 0, nf + c)),
                pl.BlockSpec((1, 1, MOE_TF), lambda c, t, te, nu: (te[t], 0, c)),
                pl.BlockSpec((1, 1, MOE_TF), lambda c, t, te, nu: (te[t], 0, nf + c)),
            ],
            out_specs=pl.BlockSpec((MOE_TM, MOE_TF), lambda c, t, te, nu: (row(c, t, te, nu), c)),
            scratch_shapes=[pltpu.VMEM((D_MODEL, MOE_TF), BF16), pltpu.VMEM((D_MODEL, MOE_TF), BF16)],
        ),
        out_shape=jax.ShapeDtypeStruct((r, D_FF), BF16),
        compiler_params=_cparams(("arbitrary", "arbitrary"), MOE_VMEM_MB),
        name="moe_up",
    )(tile_e, n_used, x_sorted, w1, w1, b1, b1)

    return pl.pallas_call(
        _moe_down_kernel,
        grid_spec=pltpu.PrefetchScalarGridSpec(
            num_scalar_prefetch=2,
            grid=(nn, n_tiles),
            in_specs=[
                pl.BlockSpec((MOE_TM, D_FF), lambda c, t, te, nu: (row(c, t, te, nu), 0)),
                pl.BlockSpec((1, D_FF, MOE_TN), lambda c, t, te, nu: (te[t], 0, c)),
                pl.BlockSpec((1, 1, MOE_TN), lambda c, t, te, nu: (te[t], 0, c)),
            ],
            out_specs=pl.BlockSpec((MOE_TM, MOE_TN // 2), lambda c, t, te, nu: (row(c, t, te, nu), c)),
            scratch_shapes=[pltpu.VMEM((D_FF, MOE_TN), BF16)],
        ),
        out_shape=jax.ShapeDtypeStruct((r, HALF_D), jnp.uint32),
        compiler_params=_cparams(("arbitrary", "arbitrary"), MOE_VMEM_MB),
        name="moe_down",
    )(tile_e, n_used, act, w2, b2)


def _combine_kernel(x_ref, y0_ref, y1_ref, y2_ref, y3_ref, tg_ref, g_ref, o_ref):
    tg = tg_ref[...]
    acc = x_ref[...]
    half = MOE_TN // 2
    for kk, y_ref in enumerate((y0_ref, y1_ref, y2_ref, y3_ref)):
        cols = []
        for c in range(D_MODEL // MOE_TN):
            cols += list(_unpack_pairs(y_ref[0, :, c * half:(c + 1) * half]))
        acc = acc + jnp.concatenate(cols, axis=1) * tg[:, kk:kk + 1]
    o_ref[...] = _rms(acc, g_ref[...])


def _combine(x2, yg, tg, g, row0):
    n = x2.shape[0]
    tm = min(TOK_TILE, n)
    blk0 = row0 // tm
    yspec = [pl.BlockSpec((1, tm, HALF_D), functools.partial(lambda i, kk: (kk, blk0 + i, 0), kk=kk))
             for kk in range(TOP_K)]
    return pl.pallas_call(
        _combine_kernel,
        grid=(n // tm,),
        in_specs=[pl.BlockSpec((tm, D_MODEL), lambda i: (i, 0))] + yspec + [
            pl.BlockSpec((tm, LANES), lambda i: (i, 0)),
            pl.BlockSpec((1, D_MODEL), lambda i: (0, 0)),
        ],
        out_specs=pl.BlockSpec((tm, D_MODEL), lambda i: (i, 0)),
        out_shape=jax.ShapeDtypeStruct((n, D_MODEL), F32),
        compiler_params=_cparams(("parallel",), 48),
        name="combine",
    )(x2, yg, yg, yg, yg, tg, g)


def _group_tiles(bsz, t):
    tt = min(t, TOK_TILE)
    bb = max(1, min(bsz, TOK_TILE // tt))
    return bb, tt


def _layer_group(x, c0, n0, m0, conv_state, mem_k, mem_v, cnt0, hn_all, row0, p):
    bsz, t, _ = x.shape
    n = bsz * t
    L = min(MLSTM_CHUNK, t)
    state_pad = jnp.pad(conv_state, ((0, 0), (CONV_PAD - CONV_STATE, 0), (0, 0)))
    conv_args = (state_pad, p["conv_w"], p["conv_b"], p["ln_g"], p["ln_b"])
    if t % PROJ_TM == 0:
        qkvo, gates, cv, new_conv = _proj_in_conv(x.reshape(n, D_MODEL), p["g_mix"], p["w_main"], p["wg"], p["bg"],
                                                  L, *conv_args, seq_len=t)
    else:
        qkvo, u, gates = _proj_in(x.reshape(n, D_MODEL), p["g_mix"], p["w_main"], p["wg"], p["bg"], L)
        ctt = min(CONV_TT, t)
        cbb = max(1, min(bsz, CONV_TT // ctt))
        cv, new_conv = _conv(u.reshape(bsz, t, CONV_CH), *conv_args, cbb, ctt)

    mbb = max(1, min(bsz, LANES // L)) if L < LANES else 1
    m0b = jnp.broadcast_to(m0[:, :, None], (bsz, M_HEADS, LANES))
    hm, c1, n1, m1 = _mlstm(qkvo.reshape(bsz, t, -1), gates, c0, n0, m0b, p["mh_g"], mbb, L)

    x1 = _proj_out(x.reshape(n, D_MODEL), hm.reshape(n, M_WIDTH), cv.reshape(n, CONV_CH), p["wo_a"], p["wo_b"])

    abb, att = _group_tiles(bsz, t)
    x2, hn_all, te, tg, tr, cnt = _attn_router(x1.reshape(bsz, t, D_MODEL), mem_k, mem_v, cnt0, hn_all, row0,
                                               p, abb, att)
    return x2.reshape(n, D_MODEL), hn_all, te, tg, tr, cnt, (c1, n1, m1[:, :, 0], new_conv)


def kernel(x_prompt, x_sample, mem_prompt, state_mlstm_C, state_mlstm_n, state_mlstm_m, state_conv, cache_mem_k, cache_mem_v, norm_mix_g, w_in, b_gates, mh_norm_g, conv_w, conv_b, conv_ln_g, conv_ln_b, w_out, norm_ca_g, norm_mem_g, ca_wq, ca_wk, ca_wv, ca_wo, norm_moe_g, router_w, router_b, moe_w1, moe_b1, moe_w2, moe_b2, final_norm_g):
    assert w_in.shape[0] == 1, "single layer"
    bp, tp, _ = x_prompt.shape
    bs, ts, _ = x_sample.shape
    n_p, n_s = bp * tp, bs * ts
    n_all = n_p + n_s

    wi = w_in[0]
    o0 = 0
    parts = []
    for width in (QK_W, QK_W, M_WIDTH, M_WIDTH, M_HEADS, M_HEADS, CONV_CH, CONV_CH):
        parts.append(wi[:, o0:o0 + width])
        o0 += width
    w_q, w_k, w_v, w_o, w_gi, w_gf, w_ua, w_ug = parts
    col_tiles = []
    w_qkvo = wi[:, :2 * QK_W + 2 * M_WIDTH]
    for c in range(PROJ_STEPS):
        col_tiles += [w_qkvo[:, c * QKVO_TN:(c + 1) * QKVO_TN], w_ua[:, c * GLU_TN:(c + 1) * GLU_TN],
                      w_ug[:, c * GLU_TN:(c + 1) * GLU_TN]]
    p = {
        "g_mix": norm_mix_g[0][None, :],
        "w_main": jnp.concatenate(col_tiles, axis=1).astype(BF16),
        "wg": jnp.concatenate([w_gi, w_gf], axis=1).T.astype(BF16),
        "bg": b_gates[0][:, None],
        "mh_g": mh_norm_g[0],
        "conv_w": conv_w[0],
        "conv_b": conv_b[0][None, :],
        "ln_g": conv_ln_g[0][None, :],
        "ln_b": conv_ln_b[0][None, :],
        "wo_a": w_out[0][:M_WIDTH].astype(BF16),
        "wo_b": w_out[0][M_WIDTH:].astype(BF16),
        "g_ca": norm_ca_g[0][None, :],
        "wq": ca_wq[0].astype(BF16),
        "wo": ca_wo[0].astype(BF16),
        "g_moe": norm_moe_g[0][None, :],
        "router_w": jnp.pad(router_w[0], ((0, 0), (0, LANES - N_EXPERTS))).astype(BF16),
        "router_b": jnp.concatenate([router_b[0], jnp.full((LANES - N_EXPERTS,), NEG_BIG, F32)])[None, :],
    }

    wkv = jnp.concatenate([ca_wk[0], ca_wv[0]], axis=1).astype(BF16)
    mk, mv = _mem_kv(mem_prompt.reshape(bp * N_MEM, D_MODEL), norm_mem_g[0][None, :], wkv)
    mk = mk.reshape(bp, N_MEM, CA_W)
    mv = mv.reshape(bp, N_MEM, CA_W)
    hn_all = jnp.zeros((n_all, HALF_D), jnp.uint32)
    cnt0 = jnp.zeros((1, LANES), F32)
    zc = jnp.zeros((bp, M_HEADS, M_DK, M_DV), F32)
    zn = jnp.zeros((bp, M_HEADS, M_DK), F32)
    zm = jnp.zeros((bp, M_HEADS), F32)
    zconv = jnp.zeros((bp, CONV_STATE, CONV_CH), F32)
    x2p, hn_all, te_p, tg_p, tr_p, cnt1, st_p = _layer_group(
        x_prompt, zc, zn, zm, zconv, mk.astype(BF16), mv.astype(BF16), cnt0, hn_all, 0, p)

    x2s, hn_all, te_s, tg_s, tr_s, cnt2, st_s = _layer_group(
        x_sample, state_mlstm_C[0], state_mlstm_n[0], state_mlstm_m[0], state_conv[0],
        cache_mem_k[0].reshape(bs, N_MEM, CA_W).astype(BF16), cache_mem_v[0].reshape(bs, N_MEM, CA_W).astype(BF16),
        cnt1, hn_all, n_p, p)

    te = jnp.concatenate([te_p[:, :TOP_K], te_s[:, :TOP_K]], axis=0)
    tr = jnp.concatenate([tr_p[:, :TOP_K], tr_s[:, :TOP_K]], axis=0)
    counts = cnt2[0, :N_EXPERTS].astype(jnp.int32)
    tiles_per_e = (counts + MOE_TM - 1) // MOE_TM
    tile_end = jnp.cumsum(tiles_per_e)
    row_start = (tile_end - tiles_per_e) * MOE_TM
    n_tiles = -(-(n_all * TOP_K) // MOE_TM) + N_EXPERTS
    n_used = tile_end[-1:]
    tile_ids = jnp.minimum(jnp.arange(n_tiles, dtype=jnp.int32), n_used[0] - 1)
    tile_e = jnp.minimum(jnp.sum(tile_ids[:, None] >= tile_end[None, :], axis=1), N_EXPERTS - 1).astype(jnp.int32)
    tile_first = jnp.arange(n_tiles, dtype=jnp.int32) - (tile_end - tiles_per_e)[tile_e]
    tile_valid = jnp.clip(counts[tile_e] - tile_first * MOE_TM, 0, MOE_TM)
    tile_valid = jnp.where(jnp.arange(n_tiles) < n_used[0], tile_valid, 0).astype(jnp.int32)
    moe_meta = jnp.concatenate([n_used.astype(jnp.int32), tile_valid])
    dest = row_start[te] + tr
    dest3 = dest.reshape(n_all // SC_WIN, SC_WIN, TOP_K).transpose(0, 2, 1)
    x_sorted = _sc_scatter_rows(hn_all, dest3, n_tiles * MOE_TM)

    y_sorted = _moe(tile_e, moe_meta, x_sorted, moe_w1[0], moe_b1[0][:, None, :], moe_w2[0],
                    moe_b2[0][:, None, :])

    yg = y_sorted.at[dest.T.reshape(-1)].get(mode="promise_in_bounds").reshape(TOP_K, n_all, HALF_D)
    fg = final_norm_g[None, :]
    y_prompt = _combine(x2p, yg, tg_p, fg, 0).reshape(bp, tp, D_MODEL)
    y_sample = _combine(x2s, yg, tg_s, fg, n_p).reshape(bs, ts, D_MODEL)

    c1, n1, m1, conv1 = st_p
    c2, n2, m2, conv2 = st_s
    mk4 = mk.reshape(1, bp, N_MEM, CA_HEADS, CA_DH)
    mv4 = mv.reshape(1, bp, N_MEM, CA_HEADS, CA_DH)
    return (y_prompt, y_sample, c1[None], n1[None], m1[None], conv1[None], mk4, mv4,
            c2[None], n2[None], m2[None], conv2[None])
```

```python
import functools

import jax
import jax.numpy as jnp
from jax import lax
from jax.experimental import pallas as pl
from jax.experimental.pallas import tpu as pltpu
from jax.experimental.pallas import tpu_sc as plsc

F32 = jnp.float32
BF16 = jnp.bfloat16
HIGHEST = lax.Precision.HIGHEST

D_MODEL = 2048
HALF_D = D_MODEL // 2
M_HEADS = 4
M_DV = 256
M_DK = 128
QK_W = M_HEADS * M_DK
M_WIDTH = M_HEADS * M_DV
CONV_CH = 1024
CONV_WIDTH = 31
CONV_STATE = CONV_WIDTH - 1
CONV_PAD = 32
N_MEM = 256
CA_HEADS = 4
CA_DH = 128
CA_W = CA_HEADS * CA_DH
N_EXPERTS = 32
TOP_K = 4
D_FF = 2048
SWIGLU_ALPHA = 1.702
SWIGLU_LIMIT = 7.0
EPS = 1e-6
LANES = 128
SUBLANES = 8
NEG_BIG = -1e30

PROJ_TM = 1024
PROJ_STEPS = 4
QKVO_TN = (2 * QK_W + 2 * M_WIDTH) // PROJ_STEPS
GLU_TN = CONV_CH // PROJ_STEPS
PROJ_TN = QKVO_TN + 2 * GLU_TN
CONV_SUB = PROJ_TM // PROJ_STEPS
TOK_TILE = 512
MOE_TM = 512
MOE_TF = 1024
MOE_TN = 2048
MOE_VMEM_MB = 60
MLSTM_CHUNK = 256
CONV_TT = 256
CONV_RB = 32
CONV_LC = 512
SC_WIN = 32


def _cparams(sem, vmem_mb):
    return pltpu.CompilerParams(dimension_semantics=sem, vmem_limit_bytes=vmem_mb << 20)


def _log_sigmoid(x):
    return jnp.minimum(x, 0.0) - jnp.log1p(jnp.exp(-jnp.abs(x)))


def _sigmoid(x):
    return 1.0 / (1.0 + jnp.exp(-x))


def _pack_pairs(lo, hi):
    ulo = lax.bitcast_convert_type(lo.astype(BF16).astype(F32), jnp.uint32)
    uhi = lax.bitcast_convert_type(hi.astype(BF16).astype(F32), jnp.uint32)
    return (ulo >> 16) | uhi


def _unpack_pairs(w):
    lo = lax.bitcast_convert_type(w << 16, F32)
    hi = lax.bitcast_convert_type(w & jnp.uint32(0xFFFF0000), F32)
    return lo, hi


def _rms(x, g):
    ms = jnp.mean(x * x, axis=-1, keepdims=True)
    return x * lax.rsqrt(ms + EPS) * g


def _proj_prologue(x_ref, g_ref, wg_ref, bg_ref, h_ref, gates_ref, chunk):
    tm = x_ref.shape[0]
    h = _rms(x_ref[...], g_ref[...]).astype(BF16)
    h_ref[...] = h
    gt = lax.dot_general(wg_ref[...], h, (((1,), (1,)), ((), ())), preferred_element_type=F32)
    gt = gt + bg_ref[...]
    row = lax.broadcasted_iota(jnp.int32, gt.shape, 0)
    gt = jnp.where(row < M_HEADS, gt, _log_sigmoid(gt))
    for c in range(tm // chunk):
        gates_ref[c] = gt[:, c * chunk:(c + 1) * chunk]


def _proj_step(h_ref, w_ref):
    z = jnp.dot(h_ref[...], w_ref[...], preferred_element_type=F32)
    glu = z[:, QKVO_TN:QKVO_TN + GLU_TN] * _sigmoid(z[:, QKVO_TN + GLU_TN:])
    return z[:, :QKVO_TN].astype(BF16), glu.astype(BF16)


def _proj_in_kernel(x_ref, g_ref, w_ref, wg_ref, bg_ref, qkvo_ref, u_ref, gates_ref, h_ref, *, chunk):
    @pl.when(pl.program_id(1) == 0)
    def _():
        _proj_prologue(x_ref, g_ref, wg_ref, bg_ref, h_ref, gates_ref, chunk)

    qkvo_ref[...], u_ref[...] = _proj_step(h_ref, w_ref)


def _proj_in(x, g, w_main, wg, bg, chunk):
    n = x.shape[0]
    tm = min(PROJ_TM, n)
    return pl.pallas_call(
        functools.partial(_proj_in_kernel, chunk=chunk),
        grid=(n // tm, PROJ_STEPS),
        in_specs=[
            pl.BlockSpec((tm, D_MODEL), lambda i, j: (i, 0)),
            pl.BlockSpec((1, D_MODEL), lambda i, j: (0, 0)),
            pl.BlockSpec((D_MODEL, PROJ_TN), lambda i, j: (0, j)),
            pl.BlockSpec((2 * M_HEADS, D_MODEL), lambda i, j: (0, 0)),
            pl.BlockSpec((2 * M_HEADS, 1), lambda i, j: (0, 0)),
        ],
        out_specs=[
            pl.BlockSpec((tm, QKVO_TN), lambda i, j: (i, j)),
            pl.BlockSpec((tm, GLU_TN), lambda i, j: (i, j)),
            pl.BlockSpec((tm // chunk, 2 * M_HEADS, chunk), lambda i, j: (i, 0, 0)),
        ],
        out_shape=[
            jax.ShapeDtypeStruct((n, 2 * QK_W + 2 * M_WIDTH), BF16),
            jax.ShapeDtypeStruct((n, CONV_CH), BF16),
            jax.ShapeDtypeStruct((n // chunk, 2 * M_HEADS, chunk), F32),
        ],
        scratch_shapes=[pltpu.VMEM((tm, D_MODEL), BF16)],
        compiler_params=_cparams(("parallel", "arbitrary"), 48),
        name="proj_in",
    )(x, g, w_main, wg, bg)


def _proj_in_conv_kernel(x_ref, g_ref, w_ref, wg_ref, bg_ref, st_ref, cw_ref, cb_ref, lg_ref, lb_ref,
                         qkvo_ref, gates_ref, cv_ref, ns_ref, h_ref, u_s, hist_s, e_ref, er_ref, c_ref,
                         *, chunk, tiles_per_seq):
    i = pl.program_id(0)
    j = pl.program_id(1)
    p = i - 1
    sub = CONV_SUB
    ext = sub + CONV_PAD

    @pl.when((i == 0) & (j == 0))
    def _():
        u_s[...] = jnp.zeros(u_s.shape, BF16)
        hist_s[...] = jnp.zeros(hist_s.shape, F32)
        e_ref[:, ext:, :] = jnp.zeros((1, SUBLANES, CONV_CH), F32)

    @pl.when(j == 0)
    def _():
        _proj_prologue(x_ref, g_ref, wg_ref, bg_ref, h_ref, gates_ref, chunk)

    @pl.when((j == 0) & (p >= 0) & (p % tiles_per_seq == 0))
    def _():
        hist_s[...] = st_ref[0]

    r0 = pl.multiple_of(j * sub, sub)
    e_ref[0, 0:CONV_PAD, :] = hist_s[...]
    for c in range(PROJ_STEPS):
        e_ref[0, CONV_PAD:ext, c * GLU_TN:(c + 1) * GLU_TN] = u_s[(i + 1) % 2, c, pl.ds(r0, sub), :].astype(F32)
    _conv_rows(e_ref, er_ref, c_ref, cw_ref, cb_ref, lg_ref, lb_ref, cv_ref, 0, sub)
    hist_s[...] = e_ref[0, sub:ext, :]

    qkvo_ref[...], u_s[i % 2, j] = _proj_step(h_ref, w_ref)

    @pl.when((j == PROJ_STEPS - 1) & (p >= 0) & (p % tiles_per_seq == tiles_per_seq - 1))
    def _():
        ns_ref[0] = e_ref[0, ext - CONV_STATE:ext, :]


def _proj_in_conv(x, g, w_main, wg, bg, chunk, state_pad, conv_w, conv_b, ln_g, ln_b, seq_len):
    n = x.shape[0]
    tm = PROJ_TM
    n_tiles = n // tm
    tiles_per_seq = seq_len // tm
    bsz = n // seq_len
    assert tm == PROJ_STEPS * CONV_SUB and seq_len % tm == 0
    last = n_tiles - 1

    def tile(i, j):
        return jnp.minimum(i, last)

    def seq(i, j):
        return jnp.clip((i - 1) // tiles_per_seq, 0, bsz - 1)

    const2 = lambda i, j: (0, 0)
    return pl.pallas_call(
        functools.partial(_proj_in_conv_kernel, chunk=chunk, tiles_per_seq=tiles_per_seq),
        grid=(n_tiles + 1, PROJ_STEPS),
        in_specs=[
            pl.BlockSpec((tm, D_MODEL), lambda i, j: (tile(i, j), 0)),
            pl.BlockSpec((1, D_MODEL), const2),
            pl.BlockSpec((D_MODEL, PROJ_TN), lambda i, j: (0, j)),
            pl.BlockSpec((2 * M_HEADS, D_MODEL), const2),
            pl.BlockSpec((2 * M_HEADS, 1), const2),
            pl.BlockSpec((1, CONV_PAD, CONV_CH), lambda i, j: (seq(i, j), 0, 0)),
            pl.BlockSpec((CONV_WIDTH, CONV_CH), const2),
            pl.BlockSpec((1, CONV_CH), const2),
            pl.BlockSpec((1, CONV_CH), const2),
            pl.BlockSpec((1, CONV_CH), const2),
        ],
        out_specs=[
            pl.BlockSpec((tm, QKVO_TN), lambda i, j: (tile(i, j), jnp.where(i <= last, j, PROJ_STEPS - 1))),
            pl.BlockSpec((tm // chunk, 2 * M_HEADS, chunk), lambda i, j: (tile(i, j), 0, 0)),
            pl.BlockSpec((CONV_SUB, CONV_CH), lambda i, j: (jnp.maximum((i - 1) * PROJ_STEPS + j, 0), 0)),
            pl.BlockSpec((1, CONV_STATE, CONV_CH), lambda i, j: (seq(i, j), 0, 0)),
        ],
        out_shape=[
            jax.ShapeDtypeStruct((n, 2 * QK_W + 2 * M_WIDTH), BF16),
            jax.ShapeDtypeStruct((n // chunk, 2 * M_HEADS, chunk), F32),
            jax.ShapeDtypeStruct((n, CONV_CH), BF16),
            jax.ShapeDtypeStruct((bsz, CONV_STATE, CONV_CH), F32),
        ],
        scratch_shapes=[
            pltpu.VMEM((tm, D_MODEL), BF16),
            pltpu.VMEM((2, PROJ_STEPS, tm, GLU_TN), BF16),
            pltpu.VMEM((CONV_PAD, CONV_CH), F32),
            pltpu.VMEM((1, CONV_SUB + CONV_PAD + SUBLANES, CONV_CH), F32),
            pltpu.VMEM((SUBLANES - 1, CONV_SUB + CONV_PAD, CONV_CH), F32),
            pltpu.VMEM((CONV_SUB, CONV_CH), F32),
        ],
        compiler_params=_cparams(("arbitrary", "arbitrary"), 56),
        name="proj_in_conv",
    )(x, g, w_main, wg, bg, state_pad, conv_w, conv_b, ln_g, ln_b)


def _conv_stages(e_ref, er_ref, c_ref, w_ref, cb_ref, lg_ref, lb_ref, out_ref, b, rows):
    ext = rows + CONV_PAD
    base = CONV_PAD - CONV_STATE

    def prepare():
        for r in range(1, SUBLANES):
            er_ref[r - 1] = e_ref[b, r:r + ext, :]

    def row_block(r0):
        for lc in range(CONV_CH // CONV_LC):
            ls = slice(lc * CONV_LC, (lc + 1) * CONV_LC)
            acc = jnp.zeros((CONV_RB, CONV_LC), F32) + cb_ref[:, ls]
            for j in range(CONV_WIDTH):
                off = j + base
                a, r = (off // SUBLANES) * SUBLANES, off % SUBLANES
                if r == 0:
                    tap = e_ref[b, pl.ds(r0 + a, CONV_RB), ls]
                else:
                    tap = er_ref[r - 1, pl.ds(r0 + a, CONV_RB), ls]
                acc = acc + w_ref[j:j + 1, ls] * tap
            c_ref[pl.ds(r0, CONV_RB), ls] = acc
        c = c_ref[pl.ds(r0, CONV_RB), :]
        mu = jnp.mean(c, axis=-1, keepdims=True)
        cc = c - mu
        var = jnp.mean(cc * cc, axis=-1, keepdims=True)
        cn = cc * lax.rsqrt(var + EPS) * lg_ref[...] + lb_ref[...]
        out_ref[pl.ds(r0, CONV_RB), :] = (cn * _sigmoid(cn)).astype(BF16)

    return prepare, row_block


def _conv_rows(e_ref, er_ref, c_ref, w_ref, cb_ref, lg_ref, lb_ref, out_ref, b, rows):
    prepare, row_block = _conv_stages(e_ref, er_ref, c_ref, w_ref, cb_ref, lg_ref, lb_ref, out_ref, b, rows)
    prepare()

    def body(rb, carry):
        row_block(pl.multiple_of(rb * CONV_RB, CONV_RB))
        return carry

    lax.fori_loop(0, rows // CONV_RB, body, 0)


def _conv_kernel(u_ref, st_ref, w_ref, cb_ref, lg_ref, lb_ref, cv_ref, ns_ref, e_ref, er_ref, c_ref, *, bb, tt):
    t = pl.program_id(1)
    nt = pl.num_programs(1)
    ext = tt + CONV_PAD

    @pl.when(t == 0)
    def _():
        e_ref[:, 0:CONV_PAD, :] = st_ref[...]

    @pl.when(t > 0)
    def _():
        e_ref[:, 0:CONV_PAD, :] = e_ref[:, tt:ext, :]

    e_ref[:, CONV_PAD:ext, :] = u_ref[...].astype(F32)
    e_ref[:, ext:, :] = jnp.zeros((bb, SUBLANES, CONV_CH), F32)

    for b in range(bb):
        _conv_rows(e_ref, er_ref, c_ref, w_ref, cb_ref, lg_ref, lb_ref, cv_ref.at[b], b, tt)

    @pl.when(t == nt - 1)
    def _():
        ns_ref[...] = e_ref[:, ext - CONV_STATE:ext, :]


def _conv(u, state_pad, conv_w, conv_b, ln_g, ln_b, bb, tt):
    bsz, t, _ = u.shape
    return pl.pallas_call(
        functools.partial(_conv_kernel, bb=bb, tt=tt),
        grid=(bsz // bb, t // tt),
        in_specs=[
            pl.BlockSpec((bb, tt, CONV_CH), lambda b, s: (b, s, 0)),
            pl.BlockSpec((bb, CONV_PAD, CONV_CH), lambda b, s: (b, 0, 0)),
            pl.BlockSpec((CONV_WIDTH, CONV_CH), lambda b, s: (0, 0)),
            pl.BlockSpec((1, CONV_CH), lambda b, s: (0, 0)),
            pl.BlockSpec((1, CONV_CH), lambda b, s: (0, 0)),
            pl.BlockSpec((1, CONV_CH), lambda b, s: (0, 0)),
        ],
        out_specs=[
            pl.BlockSpec((bb, tt, CONV_CH), lambda b, s: (b, s, 0)),
            pl.BlockSpec((bb, CONV_STATE, CONV_CH), lambda b, s: (b, 0, 0)),
        ],
        out_shape=[
            jax.ShapeDtypeStruct((bsz, t, CONV_CH), BF16),
            jax.ShapeDtypeStruct((bsz, CONV_STATE, CONV_CH), F32),
        ],
        scratch_shapes=[
            pltpu.VMEM((bb, tt + CONV_PAD + SUBLANES, CONV_CH), F32),
            pltpu.VMEM((SUBLANES - 1, tt + CONV_PAD, CONV_CH), F32),
            pltpu.VMEM((tt, CONV_CH), F32),
        ],
        compiler_params=_cparams(("parallel", "arbitrary"), 40),
        name="conv",
    )(u, state_pad, conv_w, conv_b, ln_g, ln_b)


def _mlstm_kernel(q_ref, k_ref, v_ref, o_ref, gt_ref, c0_ref, n0_ref, m0_ref, ng_ref,
                  hm_ref, cn_ref, nn_ref, mn_ref, c_s, n_s, m_s, *, bb, L):
    ci = pl.program_id(1)
    nc = pl.num_programs(1)

    @pl.when(ci == 0)
    def _():
        c_s[...] = c0_ref[...]
        n_s[...] = n0_ref[...]
        m_s[...] = m0_ref[...]

    rt = lax.broadcasted_iota(jnp.int32, (L, L), 0)
    cs = lax.broadcasted_iota(jnp.int32, (L, L), 1)
    causal = cs <= rt
    lower = causal.astype(F32)
    upper = (rt <= cs).astype(F32)
    eye = (rt == cs).astype(F32)
    nt_dims = (((1,), (1,)), ((), ()))
    scale = M_DK ** -0.5

    for b in range(bb):
        rows = gt_ref[b]
        cum_rows = jnp.dot(rows, upper, precision=HIGHEST, preferred_element_type=F32)
        cum_cols = lax.dot_general(lower, rows, nt_dims, precision=HIGHEST, preferred_element_type=F32)
        id_cols = lax.dot_general(eye, rows, nt_dims, precision=HIGHEST, preferred_element_type=F32)
        for h in range(M_HEADS):
            ig_row = rows[h:h + 1, :]
            ig_col = id_cols[:, h:h + 1]
            b_row = cum_rows[M_HEADS + h:M_HEADS + h + 1, :]
            b_col = cum_cols[:, M_HEADS + h:M_HEADS + h + 1]
            m0 = m_s[b, h:h + 1, 0:1]
            q = q_ref[b, :, h * M_DK:(h + 1) * M_DK]
            k = k_ref[b, :, h * M_DK:(h + 1) * M_DK]
            v = v_ref[b, :, h * M_DV:(h + 1) * M_DV]
            c0 = c_s[b, h]
            n0 = n_s[b, h:h + 1, :]

            g_col = b_col + m0
            dmat = jnp.where(causal, b_col - b_row + ig_row, -jnp.inf)
            m_col = jnp.maximum(g_col, jnp.max(dmat, axis=1, keepdims=True))
            w_state = jnp.exp(g_col - m_col) * scale
            qk = lax.dot_general(q, k, nt_dims, preferred_element_type=F32)
            s = qk * (jnp.exp(dmat - m_col) * scale)
            num = jnp.dot(s.astype(BF16), v, preferred_element_type=F32)
            num = num + w_state * jnp.dot(q, c0.astype(BF16), preferred_element_type=F32)
            qn = jnp.sum(q.astype(F32) * n0, axis=1, keepdims=True)
            den = jnp.sum(s, axis=1, keepdims=True) + w_state * qn
            hh = num / jnp.maximum(jnp.abs(den), jnp.exp(-m_col))

            hh = hh * lax.rsqrt(jnp.mean(hh * hh, axis=-1, keepdims=True) + EPS) * ng_ref[h:h + 1, :]
            og = _sigmoid(o_ref[b, :, h * M_DV:(h + 1) * M_DV].astype(F32))
            hm_ref[b, :, h * M_DV:(h + 1) * M_DV] = (og * hh).astype(BF16)

            b_last = b_row[:, L - 1:L]
            g_last = b_last + m0
            wk_row = b_last - b_row + ig_row
            m_new = jnp.maximum(g_last, jnp.max(wk_row, axis=1, keepdims=True))
            a0 = jnp.exp(g_last - m_new)
            ak_col = jnp.exp(b_last - b_col + ig_col - m_new)
            ks = k.astype(F32) * ak_col
            kv = lax.dot_general(ks.astype(BF16), v, (((0,), (0,)), ((), ())), preferred_element_type=F32)
            c_s[b, h] = a0 * c0 + kv
            n_s[b, h:h + 1, :] = a0 * n0 + jnp.sum(ks, axis=0, keepdims=True)
            m_s[b, h:h + 1, :] = jnp.broadcast_to(m_new, (1, LANES))

    @pl.when(ci == nc - 1)
    def _():
        cn_ref[...] = c_s[...]
        nn_ref[...] = n_s[...]
        mn_ref[...] = m_s[...]


def _mlstm(qkvo, gates, c0, n0, m0, ng, bb, L):
    bsz, t, _ = qkvo.shape
    nc = t // L
    gates = gates.reshape(bsz, nc, 2 * M_HEADS, L)
    st_c = pl.BlockSpec((bb, M_HEADS, M_DK, M_DV), lambda b, c: (b, 0, 0, 0))
    st_n = pl.BlockSpec((bb, M_HEADS, M_DK), lambda b, c: (b, 0, 0))
    return pl.pallas_call(
        functools.partial(_mlstm_kernel, bb=bb, L=L),
        grid=(bsz // bb, nc),
        in_specs=[
            pl.BlockSpec((bb, L, QK_W), lambda b, c: (b, c, 0)),
            pl.BlockSpec((bb, L, QK_W), lambda b, c: (b, c, 1)),
            pl.BlockSpec((bb, L, M_WIDTH), lambda b, c: (b, c, 1)),
            pl.BlockSpec((bb, L, M_WIDTH), lambda b, c: (b, c, 2)),
            pl.BlockSpec((bb, None, 2 * M_HEADS, L), lambda b, c: (b, c, 0, 0)),
            st_c, st_n, st_n,
            pl.BlockSpec((M_HEADS, M_DV), lambda b, c: (0, 0)),
        ],
        out_specs=[
            pl.BlockSpec((bb, L, M_WIDTH), lambda b, c: (b, c, 0)),
            st_c, st_n, st_n,
        ],
        out_shape=[
            jax.ShapeDtypeStruct((bsz, t, M_WIDTH), BF16),
            jax.ShapeDtypeStruct((bsz, M_HEADS, M_DK, M_DV), F32),
            jax.ShapeDtypeStruct((bsz, M_HEADS, M_DK), F32),
            jax.ShapeDtypeStruct((bsz, M_HEADS, LANES), F32),
        ],
        scratch_shapes=[
            pltpu.VMEM((bb, M_HEADS, M_DK, M_DV), F32),
            pltpu.VMEM((bb, M_HEADS, M_DK), F32),
            pltpu.VMEM((bb, M_HEADS, LANES), F32),
        ],
        compiler_params=_cparams(("parallel", "arbitrary"), 40),
        name="mlstm",
    )(qkvo, qkvo, qkvo, qkvo, gates, c0, n0, m0, ng)


def _proj_out_kernel(x_ref, hm_ref, cv_ref, wa_ref, wb_ref, o_ref):
    acc = jnp.dot(hm_ref[...], wa_ref[...], preferred_element_type=F32)
    acc = acc + jnp.dot(cv_ref[...], wb_ref[...], preferred_element_type=F32)
    o_ref[...] = x_ref[...] + acc


def _proj_out(x, hm, cv, wa, wb):
    n = x.shape[0]
    tm = min(TOK_TILE, n)
    return pl.pallas_call(
        _proj_out_kernel,
        grid=(n // tm,),
        in_specs=[
            pl.BlockSpec((tm, D_MODEL), lambda i: (i, 0)),
            pl.BlockSpec((tm, M_WIDTH), lambda i: (i, 0)),
            pl.BlockSpec((tm, CONV_CH), lambda i: (i, 0)),
            pl.BlockSpec((M_WIDTH, D_MODEL), lambda i: (0, 0)),
            pl.BlockSpec((CONV_CH, D_MODEL), lambda i: (0, 0)),
        ],
        out_specs=pl.BlockSpec((tm, D_MODEL), lambda i: (i, 0)),
        out_shape=jax.ShapeDtypeStruct((n, D_MODEL), F32),
        compiler_params=_cparams(("parallel",), 48),
        name="proj_out",
    )(x, hm, cv, wa, wb)


def _mem_kv_kernel(m_ref, g_ref, w_ref, k_ref, v_ref):
    mn = _rms(m_ref[...], g_ref[...]).astype(BF16)
    z = jnp.dot(mn, w_ref[...], preferred_element_type=F32)
    k_ref[...] = z[:, :CA_W]
    v_ref[...] = z[:, CA_W:]


def _mem_kv(mem, g, wkv):
    n = mem.shape[0]
    return pl.pallas_call(
        _mem_kv_kernel,
        grid=(1,),
        in_specs=[
            pl.BlockSpec((n, D_MODEL), lambda i: (0, 0)),
            pl.BlockSpec((1, D_MODEL), lambda i: (0, 0)),
            pl.BlockSpec((D_MODEL, 2 * CA_W), lambda i: (0, 0)),
        ],
        out_specs=[pl.BlockSpec((n, CA_W), lambda i: (0, 0))] * 2,
        out_shape=[jax.ShapeDtypeStruct((n, CA_W), F32)] * 2,
        compiler_params=_cparams(("arbitrary",), 40),
        name="mem_kv",
    )(mem, g, wkv)


def _attn_router_kernel(x_ref, gca_ref, wq_ref, k_ref, v_ref, wo_ref, gmoe_ref, rw_ref, rb_ref, cnt0_ref,
                        hn_in_ref, x2_ref, hn_ref, te_ref, tg_ref, tr_ref, cnt_ref, o_s, cnt_s, *, bb, tt):
    del hn_in_ref
    rows = bb * tt

    @pl.when((pl.program_id(0) == 0) & (pl.program_id(1) == 0))
    def _():
        cnt_s[...] = cnt0_ref[...]

    x = x_ref[...].reshape(rows, D_MODEL)
    h = _rms(x, gca_ref[...]).astype(BF16)
    q = jnp.dot(h, wq_ref[...], preferred_element_type=F32) * (CA_DH ** -0.5)
    q = q.astype(BF16)
    for b in range(bb):
        for hd in range(CA_HEADS):
            cols = slice(hd * CA_DH, (hd + 1) * CA_DH)
            qb = q[b * tt:(b + 1) * tt, cols]
            s = lax.dot_general(qb, k_ref[b, :, cols], (((1,), (1,)), ((), ())), preferred_element_type=F32)
            e = jnp.exp(s - jnp.max(s, axis=-1, keepdims=True))
            p = (e / jnp.sum(e, axis=-1, keepdims=True)).astype(BF16)
            o_s[b * tt:(b + 1) * tt, cols] = jnp.dot(p, v_ref[b, :, cols], preferred_element_type=F32).astype(BF16)
    x2 = x + jnp.dot(o_s[...], wo_ref[...], preferred_element_type=F32)
    x2_ref[...] = x2.reshape(bb, tt, D_MODEL)

    hn = _rms(x2, gmoe_ref[...])
    hn_bf = hn.astype(BF16)
    hn_ref[...] = _pack_pairs(hn[:, :HALF_D], hn[:, HALF_D:])
    logits = jnp.dot(hn_bf, rw_ref[...], preferred_element_type=F32) + rb_ref[...]
    lane = lax.broadcasted_iota(jnp.int32, (rows, LANES), 1)
    work = logits
    sel = []
    vals = []
    for _ in range(TOP_K):
        mx = jnp.max(work, axis=-1, keepdims=True)
        idx = jnp.min(jnp.where(work == mx, lane, LANES), axis=-1, keepdims=True)
        hit = lane == idx
        sel.append((idx, hit))
        vals.append(mx)
        work = jnp.where(hit, -jnp.inf, work)
    ex = [jnp.exp(vv - vals[0]) for vv in vals]
    tot = ex[0] + ex[1] + ex[2] + ex[3]
    assigned = jnp.zeros((rows, LANES), F32)
    for _, hit in sel:
        assigned = assigned + hit.astype(F32)
    rt = lax.broadcasted_iota(jnp.int32, (rows, rows), 0)
    cs = lax.broadcasted_iota(jnp.int32, (rows, rows), 1)
    strict = (cs < rt).astype(BF16)
    before = jnp.dot(strict, assigned.astype(BF16), preferred_element_type=F32) + cnt_s[...]
    te = jnp.zeros((rows, LANES), jnp.int32)
    tg = jnp.zeros((rows, LANES), F32)
    tr = jnp.zeros((rows, LANES), F32)
    for kk, (idx, hit) in enumerate(sel):
        rank = jnp.sum(jnp.where(hit, before, 0.0), axis=-1, keepdims=True)
        te = jnp.where(lane == kk, idx, te)
        tg = jnp.where(lane == kk, ex[kk] / tot, tg)
        tr = jnp.where(lane == kk, rank, tr)
    te_ref[...] = te
    tg_ref[...] = tg
    tr_ref[...] = tr.astype(jnp.int32)
    cnt_s[...] = cnt_s[...] + jnp.sum(assigned, axis=0, keepdims=True)
    cnt_ref[...] = cnt_s[...]


def _attn_router(x1, mem_k, mem_v, cnt0, hn_all, row0, p, bb, tt):
    bsz, t, _ = x1.shape
    rows = bb * tt
    n = bsz * t
    nb, nt = bsz // bb, t // tt
    blk0 = row0 // rows
    tok = lambda b, s: (b * nt + s, 0)
    const = lambda b, s: (0, 0)
    grid_spec = dict(
        grid=(nb, nt),
        in_specs=[
            pl.BlockSpec((bb, tt, D_MODEL), lambda b, s: (b, s, 0)),
            pl.BlockSpec((1, D_MODEL), const),
            pl.BlockSpec((D_MODEL, CA_W), const),
            pl.BlockSpec((bb, N_MEM, CA_W), lambda b, s: (b, 0, 0)),
            pl.BlockSpec((bb, N_MEM, CA_W), lambda b, s: (b, 0, 0)),
            pl.BlockSpec((CA_W, D_MODEL), const),
            pl.BlockSpec((1, D_MODEL), const),
            pl.BlockSpec((D_MODEL, LANES), const),
            pl.BlockSpec((1, LANES), const),
            pl.BlockSpec((1, LANES), const),
            pl.BlockSpec(memory_space=pl.ANY),
        ],
        out_specs=[
            pl.BlockSpec((bb, tt, D_MODEL), lambda b, s: (b, s, 0)),
            pl.BlockSpec((rows, HALF_D), lambda b, s: (blk0 + b * nt + s, 0)),
            pl.BlockSpec((rows, LANES), tok),
            pl.BlockSpec((rows, LANES), tok),
            pl.BlockSpec((rows, LANES), tok),
            pl.BlockSpec((1, LANES), const),
        ],
    )
    return pl.pallas_call(
        functools.partial(_attn_router_kernel, bb=bb, tt=tt),
        out_shape=[
            jax.ShapeDtypeStruct((bsz, t, D_MODEL), F32),
            jax.ShapeDtypeStruct(hn_all.shape, jnp.uint32),
            jax.ShapeDtypeStruct((n, LANES), jnp.int32),
            jax.ShapeDtypeStruct((n, LANES), F32),
            jax.ShapeDtypeStruct((n, LANES), jnp.int32),
            jax.ShapeDtypeStruct((1, LANES), F32),
        ],
        scratch_shapes=[pltpu.VMEM((rows, CA_W), BF16), pltpu.VMEM((1, LANES), F32)],
        input_output_aliases={10: 1},
        compiler_params=_cparams(("arbitrary", "arbitrary"), 48),
        name="attn_router",
        **grid_spec,
    )(x1, p["g_ca"], p["wq"], mem_k, mem_v, p["wo"], p["g_moe"], p["router_w"], p["router_b"], cnt0, hn_all)


def _sc_scatter_rows(rows, dest3, n_out):
    n, d = rows.shape
    nwin = dest3.shape[0]
    mesh = plsc.VectorSubcoreMesh(core_axis_name="c", subcore_axis_name="s")
    workers = mesh.num_cores * mesh.num_subcores
    assert nwin * SC_WIN == n and nwin % workers == 0, (n, nwin, workers)
    per = nwin // workers

    @functools.partial(
        pl.kernel, out_type=jax.ShapeDtypeStruct((n_out, d), rows.dtype), mesh=mesh,
        scratch_types=[pltpu.VMEM((SC_WIN, d), rows.dtype), pltpu.VMEM((TOP_K, SC_WIN), jnp.int32),
                       pltpu.SemaphoreType.DMA])
    def scatter(x_hbm, i_hbm, o_hbm, buf, idx, sem):
        wid = lax.axis_index("s") * mesh.num_cores + lax.axis_index("c")

        @pl.loop(0, per)
        def _(j):
            w = wid * per + j
            pltpu.sync_copy(x_hbm.at[pl.ds(w * SC_WIN, SC_WIN)], buf)
            pltpu.sync_copy(i_hbm.at[w], idx)
            copies = [pltpu.async_copy(buf, o_hbm.at[idx.at[kk]], sem) for kk in range(TOP_K)]
            for cp in copies:
                cp.wait()

    return scatter(rows, dest3)


def _expert_changed(te_ref):
    t = pl.program_id(1)
    return (t == 0) | (te_ref[t] != te_ref[jnp.maximum(t - 1, 0)])


def _moe_up_kernel(te_ref, nu_ref, x_ref, w1g_ref, w1l_ref, b1g_ref, b1l_ref, act_ref, wg_s, wl_s):
    @pl.when(_expert_changed(te_ref))
    def _():
        wg_s[...] = w1g_ref[0].astype(BF16)
        wl_s[...] = w1l_ref[0].astype(BF16)

    def up(words):
        lo, hi = _unpack_pairs(words)
        x = jnp.concatenate([lo.astype(BF16), hi.astype(BF16)], axis=1)
        glu = jnp.dot(x, wg_s[...], preferred_element_type=F32) + b1g_ref[0]
        lin = jnp.dot(x, wl_s[...], preferred_element_type=F32) + b1l_ref[0]
        glu = jnp.minimum(glu, SWIGLU_LIMIT)
        lin = jnp.clip(lin, -SWIGLU_LIMIT, SWIGLU_LIMIT)
        act_ref[...] = (glu * _sigmoid(SWIGLU_ALPHA * glu) * (lin + 1.0)).astype(BF16)

    t = pl.program_id(1)
    used = t < nu_ref[0]
    valid = nu_ref[1 + t]

    @pl.when(used & (valid >= MOE_TM))
    def _():
        up(x_ref[...])

    @pl.when(used & (valid < MOE_TM))
    def _():
        rowid = lax.broadcasted_iota(jnp.int32, (MOE_TM, HALF_D), 0)
        up(jnp.where(rowid < valid, x_ref[...], jnp.uint32(0)))


def _moe_down_kernel(te_ref, nu_ref, a_ref, w2_ref, b2_ref, y_ref, w2_s):
    @pl.when(_expert_changed(te_ref))
    def _():
        w2_s[...] = w2_ref[0].astype(BF16)

    @pl.when(pl.program_id(1) < nu_ref[0])
    def _():
        y = jnp.dot(a_ref[...], w2_s[...], preferred_element_type=F32) + b2_ref[0]
        y_ref[...] = _pack_pairs(y[:, :MOE_TN // 2], y[:, MOE_TN // 2:])


def _moe(tile_e, n_used, x_sorted, w1, b1, w2, b2):
    r = x_sorted.shape[0]
    n_tiles = r // MOE_TM
    nf = D_FF // MOE_TF
    nn = D_MODEL // MOE_TN

    def row(c, t, te, nu):
        return jnp.maximum(jnp.minimum(t, nu[0] - 1), 0)

    act = pl.pallas_call(
        _moe_up_kernel,
        grid_spec=pltpu.PrefetchScalarGridSpec(
            num_scalar_prefetch=2,
            grid=(nf, n_tiles),
            in_specs=[
                pl.BlockSpec((MOE_TM, HALF_D), lambda c, t, te, nu: (row(c, t, te, nu), 0)),
                pl.BlockSpec((1, D_MODEL, MOE_TF), lambda c, t, te, nu: (te[t], 0, c)),
                pl.BlockSpec((1, D_MODEL, MOE_TF), lambda c, t, te, nu: (te[t], 0, nf + c)),
                pl.BlockSpec((1, 1, MOE_TF), lambda c, t, te, nu: (te[t], 0, c)),
                pl.BlockSpec((1, 1, MOE_TF), lambda c, t, te, nu: (te[t], 0, nf + c)),
            ],
            out_specs=pl.BlockSpec((MOE_TM, MOE_TF), lambda c, t, te, nu: (row(c, t, te, nu), c)),
            scratch_shapes=[pltpu.VMEM((D_MODEL, MOE_TF), BF16), pltpu.VMEM((D_MODEL, MOE_TF), BF16)],
        ),
        out_shape=jax.ShapeDtypeStruct((r, D_FF), BF16),
        compiler_params=_cparams(("arbitrary", "arbitrary"), MOE_VMEM_MB),
        name="moe_up",
    )(tile_e, n_used, x_sorted, w1, w1, b1, b1)

    return pl.pallas_call(
        _moe_down_kernel,
        grid_spec=pltpu.PrefetchScalarGridSpec(
            num_scalar_prefetch=2,
            grid=(nn, n_tiles),
            in_specs=[
                pl.BlockSpec((MOE_TM, D_FF), lambda c, t, te, nu: (row(c, t, te, nu), 0)),
                pl.BlockSpec((1, D_FF, MOE_TN), lambda c, t, te, nu: (te[t], 0, c)),
                pl.BlockSpec((1, 1, MOE_TN), lambda c, t, te, nu: (te[t], 0, c)),
            ],
            out_specs=pl.BlockSpec((MOE_TM, MOE_TN // 2), lambda c, t, te, nu: (row(c, t, te, nu), c)),
            scratch_shapes=[pltpu.VMEM((D_FF, MOE_TN), BF16)],
        ),
        out_shape=jax.ShapeDtypeStruct((r, HALF_D), jnp.uint32),
        compiler_params=_cparams(("arbitrary", "arbitrary"), MOE_VMEM_MB),
        name="moe_down",
    )(tile_e, n_used, act, w2, b2)


def _combine_kernel(x_ref, y0_ref, y1_ref, y2_ref, y3_ref, tg_ref, g_ref, o_ref):
    tg = tg_ref[...]
    acc = x_ref[...]
    half = MOE_TN // 2
    for kk, y_ref in enumerate((y0_ref, y1_ref, y2_ref, y3_ref)):
        cols = []
        for c in range(D_MODEL // MOE_TN):
            cols += list(_unpack_pairs(y_ref[0, :, c * half:(c + 1) * half]))
        acc = acc + jnp.concatenate(cols, axis=1) * tg[:, kk:kk + 1]
    o_ref[...] = _rms(acc, g_ref[...])


def _combine(x2, yg, tg, g, row0):
    n = x2.shape[0]
    tm = min(TOK_TILE, n)
    blk0 = row0 // tm
    yspec = [pl.BlockSpec((1, tm, HALF_D), functools.partial(lambda i, kk: (kk, blk0 + i, 0), kk=kk))
             for kk in range(TOP_K)]
    return pl.pallas_call(
        _combine_kernel,
        grid=(n // tm,),
        in_specs=[pl.BlockSpec((tm, D_MODEL), lambda i: (i, 0))] + yspec + [
            pl.BlockSpec((tm, LANES), lambda i: (i, 0)),
            pl.BlockSpec((1, D_MODEL), lambda i: (0, 0)),
        ],
        out_specs=pl.BlockSpec((tm, D_MODEL), lambda i: (i, 0)),
        out_shape=jax.ShapeDtypeStruct((n, D_MODEL), F32),
        compiler_params=_cparams(("parallel",), 48),
        name="combine",
    )(x2, yg, yg, yg, yg, tg, g)


def _group_tiles(bsz, t):
    tt = min(t, TOK_TILE)
    bb = max(1, min(bsz, TOK_TILE // tt))
    return bb, tt


def _layer_group(x, c0, n0, m0, conv_state, mem_k, mem_v, cnt0, hn_all, row0, p):
    bsz, t, _ = x.shape
    n = bsz * t
    L = min(MLSTM_CHUNK, t)
    state_pad = jnp.pad(conv_state, ((0, 0), (CONV_PAD - CONV_STATE, 0), (0, 0)))
    conv_args = (state_pad, p["conv_w"], p["conv_b"], p["ln_g"], p["ln_b"])
    if t % PROJ_TM == 0:
        qkvo, gates, cv, new_conv = _proj_in_conv(x.reshape(n, D_MODEL), p["g_mix"], p["w_main"], p["wg"], p["bg"],
                                                  L, *conv_args, seq_len=t)
    else:
        qkvo, u, gates = _proj_in(x.reshape(n, D_MODEL), p["g_mix"], p["w_main"], p["wg"], p["bg"], L)
        ctt = min(CONV_TT, t)
        cbb = max(1, min(bsz, CONV_TT // ctt))
        cv, new_conv = _conv(u.reshape(bsz, t, CONV_CH), *conv_args, cbb, ctt)

    mbb = max(1, min(bsz, LANES // L)) if L < LANES else 1
    m0b = jnp.broadcast_to(m0[:, :, None], (bsz, M_HEADS, LANES))
    hm, c1, n1, m1 = _mlstm(qkvo.reshape(bsz, t, -1), gates, c0, n0, m0b, p["mh_g"], mbb, L)

    x1 = _proj_out(x.reshape(n, D_MODEL), hm.reshape(n, M_WIDTH), cv.reshape(n, CONV_CH), p["wo_a"], p["wo_b"])

    abb, att = _group_tiles(bsz, t)
    x2, hn_all, te, tg, tr, cnt = _attn_router(x1.reshape(bsz, t, D_MODEL), mem_k, mem_v, cnt0, hn_all, row0,
                                               p, abb, att)
    return x2.reshape(n, D_MODEL), hn_all, te, tg, tr, cnt, (c1, n1, m1[:, :, 0], new_conv)


def kernel(x_prompt, x_sample, mem_prompt, state_mlstm_C, state_mlstm_n, state_mlstm_m, state_conv, cache_mem_k, cache_mem_v, norm_mix_g, w_in, b_gates, mh_norm_g, conv_w, conv_b, conv_ln_g, conv_ln_b, w_out, norm_ca_g, norm_mem_g, ca_wq, ca_wk, ca_wv, ca_wo, norm_moe_g, router_w, router_b, moe_w1, moe_b1, moe_w2, moe_b2, final_norm_g):
    assert w_in.shape[0] == 1, "single layer"
    bp, tp, _ = x_prompt.shape
    bs, ts, _ = x_sample.shape
    n_p, n_s = bp * tp, bs * ts
    n_all = n_p + n_s

    wi = w_in[0]
    o0 = 0
    parts = []
    for width in (QK_W, QK_W, M_WIDTH, M_WIDTH, M_HEADS, M_HEADS, CONV_CH, CONV_CH):
        parts.append(wi[:, o0:o0 + width])
        o0 += width
    w_q, w_k, w_v, w_o, w_gi, w_gf, w_ua, w_ug = parts
    col_tiles = []
    w_qkvo = wi[:, :2 * QK_W + 2 * M_WIDTH]
    for c in range(PROJ_STEPS):
        col_tiles += [w_qkvo[:, c * QKVO_TN:(c + 1) * QKVO_TN], w_ua[:, c * GLU_TN:(c + 1) * GLU_TN],
                      w_ug[:, c * GLU_TN:(c + 1) * GLU_TN]]
    p = {
        "g_mix": norm_mix_g[0][None, :],
        "w_main": jnp.concatenate(col_tiles, axis=1).astype(BF16),
        "wg": jnp.concatenate([w_gi, w_gf], axis=1).T.astype(BF16),
        "bg": b_gates[0][:, None],
        "mh_g": mh_norm_g[0],
        "conv_w": conv_w[0],
        "conv_b": conv_b[0][None, :],
        "ln_g": conv_ln_g[0][None, :],
        "ln_b": conv_ln_b[0][None, :],
        "wo_a": w_out[0][:M_WIDTH].astype(BF16),
        "wo_b": w_out[0][M_WIDTH:].astype(BF16),
        "g_ca": norm_ca_g[0][None, :],
        "wq": ca_wq[0].astype(BF16),
        "wo": ca_wo[0].astype(BF16),
        "g_moe": norm_moe_g[0][None, :],
        "router_w": jnp.pad(router_w[0], ((0, 0), (0, LANES - N_EXPERTS))).astype(BF16),
        "router_b": jnp.concatenate([router_b[0], jnp.full((LANES - N_EXPERTS,), NEG_BIG, F32)])[None, :],
    }

    wkv = jnp.concatenate([ca_wk[0], ca_wv[0]], axis=1).astype(BF16)
    mk, mv = _mem_kv(mem_prompt.reshape(bp * N_MEM, D_MODEL), norm_mem_g[0][None, :], wkv)
    mk = mk.reshape(bp, N_MEM, CA_W)
    mv = mv.reshape(bp, N_MEM, CA_W)
    hn_all = jnp.zeros((n_all, HALF_D), jnp.uint32)
    cnt0 = jnp.zeros((1, LANES), F32)
    zc = jnp.zeros((bp, M_HEADS, M_DK, M_DV), F32)
    zn = jnp.zeros((bp, M_HEADS, M_DK), F32)
    zm = jnp.zeros((bp, M_HEADS), F32)
    zconv = jnp.zeros((bp, CONV_STATE, CONV_CH), F32)
    x2p, hn_all, te_p, tg_p, tr_p, cnt1, st_p = _layer_group(
        x_prompt, zc, zn, zm, zconv, mk.astype(BF16), mv.astype(BF16), cnt0, hn_all, 0, p)

    x2s, hn_all, te_s, tg_s, tr_s, cnt2, st_s = _layer_group(
        x_sample, state_mlstm_C[0], state_mlstm_n[0], state_mlstm_m[0], state_conv[0],
        cache_mem_k[0].reshape(bs, N_MEM, CA_W).astype(BF16), cache_mem_v[0].reshape(bs, N_MEM, CA_W).astype(BF16),
        cnt1, hn_all, n_p, p)

    te = jnp.concatenate([te_p[:, :TOP_K], te_s[:, :TOP_K]], axis=0)
    tr = jnp.concatenate([tr_p[:, :TOP_K], tr_s[:, :TOP_K]], axis=0)
    counts = cnt2[0, :N_EXPERTS].astype(jnp.int32)
    tiles_per_e = (counts + MOE_TM - 1) // MOE_TM
    tile_end = jnp.cumsum(tiles_per_e)
    row_start = (tile_end - tiles_per_e) * MOE_TM
    n_tiles = -(-(n_all * TOP_K) // MOE_TM) + N_EXPERTS
    n_used = tile_end[-1:]
    tile_ids = jnp.minimum(jnp.arange(n_tiles, dtype=jnp.int32), n_used[0] - 1)
    tile_e = jnp.minimum(jnp.sum(tile_ids[:, None] >= tile_end[None, :], axis=1), N_EXPERTS - 1).astype(jnp.int32)
    tile_first = jnp.arange(n_tiles, dtype=jnp.int32) - (tile_end - tiles_per_e)[tile_e]
    tile_valid = jnp.clip(counts[tile_e] - tile_first * MOE_TM, 0, MOE_TM)
    tile_valid = jnp.where(jnp.arange(n_tiles) < n_used[0], tile_valid, 0).astype(jnp.int32)
    moe_meta = jnp.concatenate([n_used.astype(jnp.int32), tile_valid])
    dest = row_start[te] + tr
    dest3 = dest.reshape(n_all // SC_WIN, SC_WIN, TOP_K).transpose(0, 2, 1)
    x_sorted = _sc_scatter_rows(hn_all, dest3, n_tiles * MOE_TM)

    y_sorted = _moe(tile_e, moe_meta, x_sorted, moe_w1[0], moe_b1[0][:, None, :], moe_w2[0],
                    moe_b2[0][:, None, :])

    yg = y_sorted.at[dest.T.reshape(-1)].get(mode="promise_in_bounds").reshape(TOP_K, n_all, HALF_D)
    fg = final_norm_g[None, :]
    y_prompt = _combine(x2p, yg, tg_p, fg, 0).reshape(bp, tp, D_MODEL)
    y_sample = _combine(x2s, yg, tg_s, fg, n_p).reshape(bs, ts, D_MODEL)

    c1, n1, m1, conv1 = st_p
    c2, n2, m2, conv2 = st_s
    mk4 = mk.reshape(1, bp, N_MEM, CA_HEADS, CA_DH)
    mv4 = mv.reshape(1, bp, N_MEM, CA_HEADS, CA_DH)
    return (y_prompt, y_sample, c1[None], n1[None], m1[None], conv1[None], mk4, mv4,
            c2[None], n2[None], m2[None], conv2[None])
```

```python
import functools

import jax
import jax.numpy as jnp
from jax import lax
from jax.experimental import pallas as pl
from jax.experimental.pallas import tpu as pltpu
from jax.experimental.pallas import tpu_sc as plsc

F32 = jnp.float32
BF16 = jnp.bfloat16
HIGHEST = lax.Precision.HIGHEST

D_MODEL = 2048
HALF_D = D_MODEL // 2
M_HEADS = 4
M_DV = 256
M_DK = 128
QK_W = M_HEADS * M_DK
M_WIDTH = M_HEADS * M_DV
CONV_CH = 1024
CONV_WIDTH = 31
CONV_STATE = CONV_WIDTH - 1
CONV_PAD = 32
N_MEM = 256
CA_HEADS = 4
CA_DH = 128
CA_W = CA_HEADS * CA_DH
N_EXPERTS = 32
TOP_K = 4
D_FF = 2048
SWIGLU_ALPHA = 1.702
SWIGLU_LIMIT = 7.0
EPS = 1e-6
LANES = 128
SUBLANES = 8
NEG_BIG = -1e30

PROJ_TM = 1024
PROJ_STEPS = 4
QKVO_TN = (2 * QK_W + 2 * M_WIDTH) // PROJ_STEPS
GLU_TN = CONV_CH // PROJ_STEPS
PROJ_TN = QKVO_TN + 2 * GLU_TN
CONV_SUB = PROJ_TM // PROJ_STEPS
TOK_TILE = 512
MOE_TM = 512
MOE_TF = 1024
MOE_TN = 2048
MOE_VMEM_MB = 60
MLSTM_CHUNK = 256
CONV_TT = 256
CONV_RB = 32
CONV_LC = 512
SC_WIN = 32


def _cparams(sem, vmem_mb):
    return pltpu.CompilerParams(dimension_semantics=sem, vmem_limit_bytes=vmem_mb << 20)


def _log_sigmoid(x):
    return jnp.minimum(x, 0.0) - jnp.log1p(jnp.exp(-jnp.abs(x)))


def _sigmoid(x):
    return 1.0 / (1.0 + jnp.exp(-x))


def _pack_pairs(lo, hi):
    return lax.bitcast_convert_type(pltpu.pack_elementwise([lo, hi], packed_dtype=BF16), jnp.uint32)


def _unpack_pairs(w):
    return tuple(pltpu.unpack_elementwise(w, index=i, packed_dtype=BF16, unpacked_dtype=F32) for i in range(2))


def _rms(x, g):
    ms = jnp.mean(x * x, axis=-1, keepdims=True)
    return x * lax.rsqrt(ms + EPS) * g


def _proj_prologue(x_ref, g_ref, wg_ref, bg_ref, h_ref, gates_ref, chunk):
    tm = x_ref.shape[0]
    h = _rms(x_ref[...], g_ref[...]).astype(BF16)
    h_ref[...] = h
    gt = lax.dot_general(wg_ref[...], h, (((1,), (1,)), ((), ())), preferred_element_type=F32)
    gt = gt + bg_ref[...]
    row = lax.broadcasted_iota(jnp.int32, gt.shape, 0)
    gt = jnp.where(row < M_HEADS, gt, _log_sigmoid(gt))
    for c in range(tm // chunk):
        gates_ref[c] = gt[:, c * chunk:(c + 1) * chunk]


def _proj_step(h_ref, w_ref):
    z = jnp.dot(h_ref[...], w_ref[...], preferred_element_type=F32)
    glu = z[:, QKVO_TN:QKVO_TN + GLU_TN] * _sigmoid(z[:, QKVO_TN + GLU_TN:])
    return z[:, :QKVO_TN].astype(BF16), glu.astype(BF16)


def _proj_in_kernel(x_ref, g_ref, w_ref, wg_ref, bg_ref, qkvo_ref, u_ref, gates_ref, h_ref, *, chunk):
    @pl.when(pl.program_id(1) == 0)
    def _():
        _proj_prologue(x_ref, g_ref, wg_ref, bg_ref, h_ref, gates_ref, chunk)

    qkvo_ref[...], u_ref[...] = _proj_step(h_ref, w_ref)


def _proj_in(x, g, w_main, wg, bg, chunk):
    n = x.shape[0]
    tm = min(PROJ_TM, n)
    return pl.pallas_call(
        functools.partial(_proj_in_kernel, chunk=chunk),
        grid=(n // tm, PROJ_STEPS),
        in_specs=[
            pl.BlockSpec((tm, D_MODEL), lambda i, j: (i, 0)),
            pl.BlockSpec((1, D_MODEL), lambda i, j: (0, 0)),
            pl.BlockSpec((D_MODEL, PROJ_TN), lambda i, j: (0, j)),
            pl.BlockSpec((2 * M_HEADS, D_MODEL), lambda i, j: (0, 0)),
            pl.BlockSpec((2 * M_HEADS, 1), lambda i, j: (0, 0)),
        ],
        out_specs=[
            pl.BlockSpec((tm, QKVO_TN), lambda i, j: (i, j)),
            pl.BlockSpec((tm, GLU_TN), lambda i, j: (i, j)),
            pl.BlockSpec((tm // chunk, 2 * M_HEADS, chunk), lambda i, j: (i, 0, 0)),
        ],
        out_shape=[
            jax.ShapeDtypeStruct((n, 2 * QK_W + 2 * M_WIDTH), BF16),
            jax.ShapeDtypeStruct((n, CONV_CH), BF16),
            jax.ShapeDtypeStruct((n // chunk, 2 * M_HEADS, chunk), F32),
        ],
        scratch_shapes=[pltpu.VMEM((tm, D_MODEL), BF16)],
        compiler_params=_cparams(("parallel", "arbitrary"), 48),
        name="proj_in",
    )(x, g, w_main, wg, bg)


def _proj_in_conv_kernel(x_ref, g_ref, w_ref, wg_ref, bg_ref, st_ref, cw_ref, cb_ref, lg_ref, lb_ref,
                         qkvo_ref, gates_ref, cv_ref, ns_ref, h_ref, u_s, hist_s, e_ref, er_ref, c_ref,
                         *, chunk, tiles_per_seq):
    i = pl.program_id(0)
    j = pl.program_id(1)
    p = i - 1
    sub = CONV_SUB
    ext = sub + CONV_PAD

    @pl.when((i == 0) & (j == 0))
    def _():
        u_s[...] = jnp.zeros(u_s.shape, BF16)
        hist_s[...] = jnp.zeros(hist_s.shape, F32)
        e_ref[:, ext:, :] = jnp.zeros((1, SUBLANES, CONV_CH), F32)

    @pl.when(j == 0)
    def _():
        _proj_prologue(x_ref, g_ref, wg_ref, bg_ref, h_ref, gates_ref, chunk)

    @pl.when((j == 0) & (p >= 0) & (p % tiles_per_seq == 0))
    def _():
        hist_s[...] = st_ref[0]

    r0 = pl.multiple_of(j * sub, sub)
    e_ref[0, 0:CONV_PAD, :] = hist_s[...]
    for c in range(PROJ_STEPS):
        e_ref[0, CONV_PAD:ext, c * GLU_TN:(c + 1) * GLU_TN] = u_s[(i + 1) % 2, c, pl.ds(r0, sub), :].astype(F32)
    prepare, row_block = _conv_stages(e_ref, er_ref, c_ref, cw_ref, cb_ref, lg_ref, lb_ref, cv_ref, 0, sub)
    prepare()
    for r0 in range(0, sub, CONV_RB):
        row_block(r0)
    hist_s[...] = e_ref[0, sub:ext, :]

    qkvo_ref[...], u_s[i % 2, j] = _proj_step(h_ref, w_ref)

    @pl.when((j == PROJ_STEPS - 1) & (p >= 0) & (p % tiles_per_seq == tiles_per_seq - 1))
    def _():
        ns_ref[0] = e_ref[0, ext - CONV_STATE:ext, :]


def _proj_in_conv(x, g, w_main, wg, bg, chunk, state_pad, conv_w, conv_b, ln_g, ln_b, seq_len):
    n = x.shape[0]
    tm = PROJ_TM
    n_tiles = n // tm
    tiles_per_seq = seq_len // tm
    bsz = n // seq_len
    assert tm == PROJ_STEPS * CONV_SUB and seq_len % tm == 0
    last = n_tiles - 1

    def tile(i, j):
        return jnp.minimum(i, last)

    def seq(i, j):
        return jnp.clip((i - 1) // tiles_per_seq, 0, bsz - 1)

    const2 = lambda i, j: (0, 0)
    return pl.pallas_call(
        functools.partial(_proj_in_conv_kernel, chunk=chunk, tiles_per_seq=tiles_per_seq),
        grid=(n_tiles + 1, PROJ_STEPS),
        in_specs=[
            pl.BlockSpec((tm, D_MODEL), lambda i, j: (tile(i, j), 0)),
            pl.BlockSpec((1, D_MODEL), const2),
            pl.BlockSpec((D_MODEL, PROJ_TN), lambda i, j: (0, j)),
            pl.BlockSpec((2 * M_HEADS, D_MODEL), const2),
            pl.BlockSpec((2 * M_HEADS, 1), const2),
            pl.BlockSpec((1, CONV_PAD, CONV_CH), lambda i, j: (seq(i, j), 0, 0)),
            pl.BlockSpec((CONV_WIDTH, CONV_CH), const2),
            pl.BlockSpec((1, CONV_CH), const2),
            pl.BlockSpec((1, CONV_CH), const2),
            pl.BlockSpec((1, CONV_CH), const2),
        ],
        out_specs=[
            pl.BlockSpec((tm, QKVO_TN), lambda i, j: (tile(i, j), jnp.where(i <= last, j, PROJ_STEPS - 1))),
            pl.BlockSpec((tm // chunk, 2 * M_HEADS, chunk), lambda i, j: (tile(i, j), 0, 0)),
            pl.BlockSpec((CONV_SUB, CONV_CH), lambda i, j: (jnp.maximum((i - 1) * PROJ_STEPS + j, 0), 0)),
            pl.BlockSpec((1, CONV_STATE, CONV_CH), lambda i, j: (seq(i, j), 0, 0)),
        ],
        out_shape=[
            jax.ShapeDtypeStruct((n, 2 * QK_W + 2 * M_WIDTH), BF16),
            jax.ShapeDtypeStruct((n // chunk, 2 * M_HEADS, chunk), F32),
            jax.ShapeDtypeStruct((n, CONV_CH), BF16),
            jax.ShapeDtypeStruct((bsz, CONV_STATE, CONV_CH), F32),
        ],
        scratch_shapes=[
            pltpu.VMEM((tm, D_MODEL), BF16),
            pltpu.VMEM((2, PROJ_STEPS, tm, GLU_TN), BF16),
            pltpu.VMEM((CONV_PAD, CONV_CH), F32),
            pltpu.VMEM((1, CONV_SUB + CONV_PAD + SUBLANES, CONV_CH), F32),
            pltpu.VMEM((SUBLANES - 1, CONV_SUB + CONV_PAD, CONV_CH), F32),
            pltpu.VMEM((CONV_SUB, CONV_CH), F32),
        ],
        compiler_params=_cparams(("arbitrary", "arbitrary"), 56),
        name="proj_in_conv",
    )(x, g, w_main, wg, bg, state_pad, conv_w, conv_b, ln_g, ln_b)


def _conv_stages(e_ref, er_ref, c_ref, w_ref, cb_ref, lg_ref, lb_ref, out_ref, b, rows):
    ext = rows + CONV_PAD
    base = CONV_PAD - CONV_STATE

    def prepare():
        for r in range(1, SUBLANES):
            er_ref[r - 1] = e_ref[b, r:r + ext, :]

    def row_block(r0):
        for lc in range(CONV_CH // CONV_LC):
            ls = slice(lc * CONV_LC, (lc + 1) * CONV_LC)
            acc = jnp.zeros((CONV_RB, CONV_LC), F32) + cb_ref[:, ls]
            for j in range(CONV_WIDTH):
                off = j + base
                a, r = (off // SUBLANES) * SUBLANES, off % SUBLANES
                if r == 0:
                    tap = e_ref[b, pl.ds(r0 + a, CONV_RB), ls]
                else:
                    tap = er_ref[r - 1, pl.ds(r0 + a, CONV_RB), ls]
                acc = acc + w_ref[j:j + 1, ls] * tap
            c_ref[pl.ds(r0, CONV_RB), ls] = acc
        c = c_ref[pl.ds(r0, CONV_RB), :]
        mu = jnp.mean(c, axis=-1, keepdims=True)
        cc = c - mu
        var = jnp.mean(cc * cc, axis=-1, keepdims=True)
        cn = cc * lax.rsqrt(var + EPS) * lg_ref[...] + lb_ref[...]
        out_ref[pl.ds(r0, CONV_RB), :] = (cn * _sigmoid(cn)).astype(BF16)

    return prepare, row_block


def _conv_rows(e_ref, er_ref, c_ref, w_ref, cb_ref, lg_ref, lb_ref, out_ref, b, rows):
    prepare, row_block = _conv_stages(e_ref, er_ref, c_ref, w_ref, cb_ref, lg_ref, lb_ref, out_ref, b, rows)
    prepare()

    def body(rb, carry):
        row_block(pl.multiple_of(rb * CONV_RB, CONV_RB))
        return carry

    lax.fori_loop(0, rows // CONV_RB, body, 0)


def _conv_kernel(u_ref, st_ref, w_ref, cb_ref, lg_ref, lb_ref, cv_ref, ns_ref, e_ref, er_ref, c_ref, *, bb, tt):
    t = pl.program_id(1)
    nt = pl.num_programs(1)
    ext = tt + CONV_PAD

    @pl.when(t == 0)
    def _():
        e_ref[:, 0:CONV_PAD, :] = st_ref[...]

    @pl.when(t > 0)
    def _():
        e_ref[:, 0:CONV_PAD, :] = e_ref[:, tt:ext, :]

    e_ref[:, CONV_PAD:ext, :] = u_ref[...].astype(F32)
    e_ref[:, ext:, :] = jnp.zeros((bb, SUBLANES, CONV_CH), F32)

    for b in range(bb):
        _conv_rows(e_ref, er_ref, c_ref, w_ref, cb_ref, lg_ref, lb_ref, cv_ref.at[b], b, tt)

    @pl.when(t == nt - 1)
    def _():
        ns_ref[...] = e_ref[:, ext - CONV_STATE:ext, :]


def _conv(u, state_pad, conv_w, conv_b, ln_g, ln_b, bb, tt):
    bsz, t, _ = u.shape
    return pl.pallas_call(
        functools.partial(_conv_kernel, bb=bb, tt=tt),
        grid=(bsz // bb, t // tt),
        in_specs=[
            pl.BlockSpec((bb, tt, CONV_CH), lambda b, s: (b, s, 0)),
            pl.BlockSpec((bb, CONV_PAD, CONV_CH), lambda b, s: (b, 0, 0)),
            pl.BlockSpec((CONV_WIDTH, CONV_CH), lambda b, s: (0, 0)),
            pl.BlockSpec((1, CONV_CH), lambda b, s: (0, 0)),
            pl.BlockSpec((1, CONV_CH), lambda b, s: (0, 0)),
            pl.BlockSpec((1, CONV_CH), lambda b, s: (0, 0)),
        ],
        out_specs=[
            pl.BlockSpec((bb, tt, CONV_CH), lambda b, s: (b, s, 0)),
            pl.BlockSpec((bb, CONV_STATE, CONV_CH), lambda b, s: (b, 0, 0)),
        ],
        out_shape=[
            jax.ShapeDtypeStruct((bsz, t, CONV_CH), BF16),
            jax.ShapeDtypeStruct((bsz, CONV_STATE, CONV_CH), F32),
        ],
        scratch_shapes=[
            pltpu.VMEM((bb, tt + CONV_PAD + SUBLANES, CONV_CH), F32),
            pltpu.VMEM((SUBLANES - 1, tt + CONV_PAD, CONV_CH), F32),
            pltpu.VMEM((tt, CONV_CH), F32),
        ],
        compiler_params=_cparams(("parallel", "arbitrary"), 40),
        name="conv",
    )(u, state_pad, conv_w, conv_b, ln_g, ln_b)


def _mlstm_kernel(q_ref, k_ref, v_ref, o_ref, gt_ref, c0_ref, n0_ref, m0_ref, ng_ref,
                  hm_ref, cn_ref, nn_ref, mn_ref, c_s, n_s, m_s, *, bb, L):
    ci = pl.program_id(1)
    nc = pl.num_programs(1)

    @pl.when(ci == 0)
    def _():
        c_s[...] = c0_ref[...]
        n_s[...] = n0_ref[...]
        m_s[...] = m0_ref[...]

    rt = lax.broadcasted_iota(jnp.int32, (L, L), 0)
    cs = lax.broadcasted_iota(jnp.int32, (L, L), 1)
    causal = cs <= rt
    lower = causal.astype(F32)
    upper = (rt <= cs).astype(F32)
    eye = (rt == cs).astype(F32)
    nt_dims = (((1,), (1,)), ((), ()))
    scale = M_DK ** -0.5

    for b in range(bb):
        rows = gt_ref[b]
        cum_rows = jnp.dot(rows, upper, precision=HIGHEST, preferred_element_type=F32)
        cum_cols = lax.dot_general(lower, rows, nt_dims, precision=HIGHEST, preferred_element_type=F32)
        id_cols = lax.dot_general(eye, rows, nt_dims, precision=HIGHEST, preferred_element_type=F32)
        for h in range(M_HEADS):
            ig_row = rows[h:h + 1, :]
            ig_col = id_cols[:, h:h + 1]
            b_row = cum_rows[M_HEADS + h:M_HEADS + h + 1, :]
            b_col = cum_cols[:, M_HEADS + h:M_HEADS + h + 1]
            m0 = m_s[b, h:h + 1, 0:1]
            q = q_ref[b, :, h * M_DK:(h + 1) * M_DK]
            k = k_ref[b, :, h * M_DK:(h + 1) * M_DK]
            v = v_ref[b, :, h * M_DV:(h + 1) * M_DV]
            c0 = c_s[b, h]
            n0 = n_s[b, h:h + 1, :]

            g_col = b_col + m0
            dmat = jnp.where(causal, b_col - b_row + ig_row, -jnp.inf)
            m_col = jnp.maximum(g_col, jnp.max(dmat, axis=1, keepdims=True))
            w_state = jnp.exp(g_col - m_col) * scale
            qk = lax.dot_general(q, k, nt_dims, preferred_element_type=F32)
            s = qk * (jnp.exp(dmat - m_col) * scale)
            num = jnp.dot(s.astype(BF16), v, preferred_element_type=F32)
            num = num + w_state * jnp.dot(q, c0.astype(BF16), preferred_element_type=F32)
            qn = jnp.sum(q.astype(F32) * n0, axis=1, keepdims=True)
            den = jnp.sum(s, axis=1, keepdims=True) + w_state * qn
            hh = num / jnp.maximum(jnp.abs(den), jnp.exp(-m_col))

            hh = hh * lax.rsqrt(jnp.mean(hh * hh, axis=-1, keepdims=True) + EPS) * ng_ref[h:h + 1, :]
            og = _sigmoid(o_ref[b, :, h * M_DV:(h + 1) * M_DV].astype(F32))
            hm_ref[b, :, h * M_DV:(h + 1) * M_DV] = (og * hh).astype(BF16)

            b_last = b_row[:, L - 1:L]
            g_last = b_last + m0
            wk_row = b_last - b_row + ig_row
            m_new = jnp.maximum(g_last, jnp.max(wk_row, axis=1, keepdims=True))
            a0 = jnp.exp(g_last - m_new)
            ak_col = jnp.exp(b_last - b_col + ig_col - m_new)
            ks = k.astype(F32) * ak_col
            kv = lax.dot_general(ks.astype(BF16), v, (((0,), (0,)), ((), ())), preferred_element_type=F32)
            c_s[b, h] = a0 * c0 + kv
            n_s[b, h:h + 1, :] = a0 * n0 + jnp.sum(ks, axis=0, keepdims=True)
            m_s[b, h:h + 1, :] = jnp.broadcast_to(m_new, (1, LANES))

    @pl.when(ci == nc - 1)
    def _():
        cn_ref[...] = c_s[...]
        nn_ref[...] = n_s[...]
        mn_ref[...] = m_s[...]


def _mlstm(qkvo, gates, c0, n0, m0, ng, bb, L):
    bsz, t, _ = qkvo.shape
    nc = t // L
    gates = gates.reshape(bsz, nc, 2 * M_HEADS, L)
    st_c = pl.BlockSpec((bb, M_HEADS, M_DK, M_DV), lambda b, c: (b, 0, 0, 0))
    st_n = pl.BlockSpec((bb, M_HEADS, M_DK), lambda b, c: (b, 0, 0))
    return pl.pallas_call(
        functools.partial(_mlstm_kernel, bb=bb, L=L),
        grid=(bsz // bb, nc),
        in_specs=[
            pl.BlockSpec((bb, L, QK_W), lambda b, c: (b, c, 0)),
            pl.BlockSpec((bb, L, QK_W), lambda b, c: (b, c, 1)),
            pl.BlockSpec((bb, L, M_WIDTH), lambda b, c: (b, c, 1)),
            pl.BlockSpec((bb, L, M_WIDTH), lambda b, c: (b, c, 2)),
            pl.BlockSpec((bb, None, 2 * M_HEADS, L), lambda b, c: (b, c, 0, 0)),
            st_c, st_n, st_n,
            pl.BlockSpec((M_HEADS, M_DV), lambda b, c: (0, 0)),
        ],
        out_specs=[
            pl.BlockSpec((bb, L, M_WIDTH), lambda b, c: (b, c, 0)),
            st_c, st_n, st_n,
        ],
        out_shape=[
            jax.ShapeDtypeStruct((bsz, t, M_WIDTH), BF16),
            jax.ShapeDtypeStruct((bsz, M_HEADS, M_DK, M_DV), F32),
            jax.ShapeDtypeStruct((bsz, M_HEADS, M_DK), F32),
            jax.ShapeDtypeStruct((bsz, M_HEADS, LANES), F32),
        ],
        scratch_shapes=[
            pltpu.VMEM((bb, M_HEADS, M_DK, M_DV), F32),
            pltpu.VMEM((bb, M_HEADS, M_DK), F32),
            pltpu.VMEM((bb, M_HEADS, LANES), F32),
        ],
        compiler_params=_cparams(("parallel", "arbitrary"), 40),
        name="mlstm",
    )(qkvo, qkvo, qkvo, qkvo, gates, c0, n0, m0, ng)


def _proj_out_kernel(x_ref, hm_ref, cv_ref, wa_ref, wb_ref, o_ref):
    acc = jnp.dot(hm_ref[...], wa_ref[...], preferred_element_type=F32)
    acc = acc + jnp.dot(cv_ref[...], wb_ref[...], preferred_element_type=F32)
    o_ref[...] = x_ref[...] + acc


def _proj_out(x, hm, cv, wa, wb):
    n = x.shape[0]
    tm = min(TOK_TILE, n)
    return pl.pallas_call(
        _proj_out_kernel,
        grid=(n // tm,),
        in_specs=[
            pl.BlockSpec((tm, D_MODEL), lambda i: (i, 0)),
            pl.BlockSpec((tm, M_WIDTH), lambda i: (i, 0)),
            pl.BlockSpec((tm, CONV_CH), lambda i: (i, 0)),
            pl.BlockSpec((M_WIDTH, D_MODEL), lambda i: (0, 0)),
            pl.BlockSpec((CONV_CH, D_MODEL), lambda i: (0, 0)),
        ],
        out_specs=pl.BlockSpec((tm, D_MODEL), lambda i: (i, 0)),
        out_shape=jax.ShapeDtypeStruct((n, D_MODEL), F32),
        compiler_params=_cparams(("parallel",), 48),
        name="proj_out",
    )(x, hm, cv, wa, wb)


def _mem_kv_kernel(m_ref, g_ref, w_ref, k_ref, v_ref):
    mn = _rms(m_ref[...], g_ref[...]).astype(BF16)
    z = jnp.dot(mn, w_ref[...], preferred_element_type=F32)
    k_ref[...] = z[:, :CA_W]
    v_ref[...] = z[:, CA_W:]


def _mem_kv(mem, g, wkv):
    n = mem.shape[0]
    return pl.pallas_call(
        _mem_kv_kernel,
        grid=(1,),
        in_specs=[
            pl.BlockSpec((n, D_MODEL), lambda i: (0, 0)),
            pl.BlockSpec((1, D_MODEL), lambda i: (0, 0)),
            pl.BlockSpec((D_MODEL, 2 * CA_W), lambda i: (0, 0)),
        ],
        out_specs=[pl.BlockSpec((n, CA_W), lambda i: (0, 0))] * 2,
        out_shape=[jax.ShapeDtypeStruct((n, CA_W), F32)] * 2,
        compiler_params=_cparams(("arbitrary",), 40),
        name="mem_kv",
    )(mem, g, wkv)


def _attn_router_kernel(x_ref, gca_ref, wq_ref, k_ref, v_ref, wo_ref, gmoe_ref, rw_ref, rb_ref, cnt0_ref,
                        hn_in_ref, x2_ref, hn_ref, te_ref, tg_ref, tr_ref, cnt_ref, o_s, cnt_s, *, bb, tt):
    del hn_in_ref
    rows = bb * tt

    @pl.when((pl.program_id(0) == 0) & (pl.program_id(1) == 0))
    def _():
        cnt_s[...] = cnt0_ref[...]

    x = x_ref[...].reshape(rows, D_MODEL)
    h = _rms(x, gca_ref[...]).astype(BF16)
    q = jnp.dot(h, wq_ref[...], preferred_element_type=F32) * (CA_DH ** -0.5)
    q = q.astype(BF16)
    for b in range(bb):
        for hd in range(CA_HEADS):
            cols = slice(hd * CA_DH, (hd + 1) * CA_DH)
            qb = q[b * tt:(b + 1) * tt, cols]
            s = lax.dot_general(qb, k_ref[b, :, cols], (((1,), (1,)), ((), ())), preferred_element_type=F32)
            e = jnp.exp(s - jnp.max(s, axis=-1, keepdims=True))
            ov = jnp.dot(e.astype(BF16), v_ref[b, :, cols], preferred_element_type=F32)
            o_s[b * tt:(b + 1) * tt, cols] = (ov / jnp.sum(e, axis=-1, keepdims=True)).astype(BF16)
    x2 = x + jnp.dot(o_s[...], wo_ref[...], preferred_element_type=F32)
    x2_ref[...] = x2.reshape(bb, tt, D_MODEL)

    hn = _rms(x2, gmoe_ref[...])
    hn_bf = hn.astype(BF16)
    hn_ref[...] = _pack_pairs(hn[:, :HALF_D], hn[:, HALF_D:])
    logits = jnp.dot(hn_bf, rw_ref[...], preferred_element_type=F32) + rb_ref[...]
    lane = lax.broadcasted_iota(jnp.int32, (rows, LANES), 1)
    lane_f = lane.astype(F32)
    work = logits
    sel = []
    vals = []
    for _ in range(TOP_K):
        mx = jnp.max(work, axis=-1, keepdims=True)
        idx = jnp.min(jnp.where(work == mx, lane_f, float(LANES)), axis=-1, keepdims=True)
        hit = lane_f == idx
        sel.append((idx, hit))
        vals.append(mx)
        work = jnp.where(hit, -jnp.inf, work)
    ex = [jnp.exp(vv - vals[0]) for vv in vals]
    tot = ex[0] + ex[1] + ex[2] + ex[3]
    assigned = jnp.zeros((rows, LANES), F32)
    for _, hit in sel:
        assigned = assigned + hit.astype(F32)
    rt = lax.broadcasted_iota(jnp.int32, (rows, rows), 0)
    cs = lax.broadcasted_iota(jnp.int32, (rows, rows), 1)
    strict = (cs < rt).astype(BF16)
    before = jnp.dot(strict, assigned.astype(BF16), preferred_element_type=F32) + cnt_s[...]
    te = jnp.zeros((rows, LANES), F32)
    tg = jnp.zeros((rows, LANES), F32)
    tr = jnp.zeros((rows, LANES), F32)
    for kk, (idx, hit) in enumerate(sel):
        rank = jnp.sum(jnp.where(hit, before, 0.0), axis=-1, keepdims=True)
        te = jnp.where(lane == kk, idx, te)
        tg = jnp.where(lane == kk, ex[kk] / tot, tg)
        tr = jnp.where(lane == kk, rank, tr)
    te_ref[...] = te.astype(jnp.int32)
    tg_ref[...] = tg
    tr_ref[...] = tr.astype(jnp.int32)
    cnt_s[...] = cnt_s[...] + jnp.sum(assigned, axis=0, keepdims=True)
    cnt_ref[...] = cnt_s[...]


def _attn_router(x1, mem_k, mem_v, cnt0, hn_all, row0, p, bb, tt):
    bsz, t, _ = x1.shape
    rows = bb * tt
    n = bsz * t
    nb, nt = bsz // bb, t // tt
    blk0 = row0 // rows
    tok = lambda b, s: (b * nt + s, 0)
    const = lambda b, s: (0, 0)
    grid_spec = dict(
        grid=(nb, nt),
        in_specs=[
            pl.BlockSpec((bb, tt, D_MODEL), lambda b, s: (b, s, 0)),
            pl.BlockSpec((1, D_MODEL), const),
            pl.BlockSpec((D_MODEL, CA_W), const),
            pl.BlockSpec((bb, N_MEM, CA_W), lambda b, s: (b, 0, 0)),
            pl.BlockSpec((bb, N_MEM, CA_W), lambda b, s: (b, 0, 0)),
            pl.BlockSpec((CA_W, D_MODEL), const),
            pl.BlockSpec((1, D_MODEL), const),
            pl.BlockSpec((D_MODEL, LANES), const),
            pl.BlockSpec((1, LANES), const),
            pl.BlockSpec((1, LANES), const),
            pl.BlockSpec(memory_space=pl.ANY),
        ],
        out_specs=[
            pl.BlockSpec((bb, tt, D_MODEL), lambda b, s: (b, s, 0)),
            pl.BlockSpec((rows, HALF_D), lambda b, s: (blk0 + b * nt + s, 0)),
            pl.BlockSpec((rows, LANES), tok),
            pl.BlockSpec((rows, LANES), tok),
            pl.BlockSpec((rows, LANES), tok),
            pl.BlockSpec((1, LANES), const),
        ],
    )
    return pl.pallas_call(
        functools.partial(_attn_router_kernel, bb=bb, tt=tt),
        out_shape=[
            jax.ShapeDtypeStruct((bsz, t, D_MODEL), F32),
            jax.ShapeDtypeStruct(hn_all.shape, jnp.uint32),
            jax.ShapeDtypeStruct((n, LANES), jnp.int32),
            jax.ShapeDtypeStruct((n, LANES), F32),
            jax.ShapeDtypeStruct((n, LANES), jnp.int32),
            jax.ShapeDtypeStruct((1, LANES), F32),
        ],
        scratch_shapes=[pltpu.VMEM((rows, CA_W), BF16), pltpu.VMEM((1, LANES), F32)],
        input_output_aliases={10: 1},
        compiler_params=_cparams(("arbitrary", "arbitrary"), 48),
        name="attn_router",
        **grid_spec,
    )(x1, p["g_ca"], p["wq"], mem_k, mem_v, p["wo"], p["g_moe"], p["router_w"], p["router_b"], cnt0, hn_all)


def _sc_scatter_rows(rows, dest3, n_out):
    n, d = rows.shape
    nwin = dest3.shape[0]
    mesh = plsc.VectorSubcoreMesh(core_axis_name="c", subcore_axis_name="s")
    workers = mesh.num_cores * mesh.num_subcores
    assert nwin * SC_WIN == n and nwin % workers == 0, (n, nwin, workers)
    per = nwin // workers

    @functools.partial(
        pl.kernel, out_type=jax.ShapeDtypeStruct((n_out, d), rows.dtype), mesh=mesh,
        scratch_types=[pltpu.VMEM((SC_WIN, d), rows.dtype), pltpu.VMEM((TOP_K, SC_WIN), jnp.int32),
                       pltpu.SemaphoreType.DMA])
    def scatter(x_hbm, i_hbm, o_hbm, buf, idx, sem):
        wid = lax.axis_index("s") * mesh.num_cores + lax.axis_index("c")

        @pl.loop(0, per)
        def _(j):
            w = wid * per + j
            pltpu.sync_copy(x_hbm.at[pl.ds(w * SC_WIN, SC_WIN)], buf)
            pltpu.sync_copy(i_hbm.at[w], idx)
            copies = [pltpu.async_copy(buf, o_hbm.at[idx.at[kk]], sem) for kk in range(TOP_K)]
            for cp in copies:
                cp.wait()

    return scatter(rows, dest3)


def _expert_changed(te_ref):
    t = pl.program_id(1)
    return (t == 0) | (te_ref[t] != te_ref[jnp.maximum(t - 1, 0)])


def _moe_up_kernel(te_ref, nu_ref, x_ref, w1g_ref, w1l_ref, b1g_ref, b1l_ref, act_ref, wg_s, wl_s):
    @pl.when(_expert_changed(te_ref))
    def _():
        wg_s[...] = w1g_ref[0].astype(BF16)
        wl_s[...] = w1l_ref[0].astype(BF16)

    def up(words):
        lo, hi = _unpack_pairs(words)
        x = jnp.concatenate([lo.astype(BF16), hi.astype(BF16)], axis=1)
        glu = jnp.dot(x, wg_s[...], preferred_element_type=F32) + b1g_ref[0]
        lin = jnp.dot(x, wl_s[...], preferred_element_type=F32) + b1l_ref[0]
        glu = jnp.minimum(glu, SWIGLU_LIMIT)
        lin = jnp.clip(lin, -SWIGLU_LIMIT, SWIGLU_LIMIT)
        act_ref[...] = (glu * _sigmoid(SWIGLU_ALPHA * glu) * (lin + 1.0)).astype(BF16)

    t = pl.program_id(1)
    used = t < nu_ref[0]
    valid = nu_ref[1 + t]

    @pl.when(used & (valid >= MOE_TM))
    def _():
        up(x_ref[...])

    @pl.when(used & (valid < MOE_TM))
    def _():
        rowid = lax.broadcasted_iota(jnp.int32, (MOE_TM, HALF_D), 0)
        up(jnp.where(rowid < valid, x_ref[...], jnp.uint32(0)))


def _moe_down_kernel(te_ref, nu_ref, a_ref, w2_ref, b2_ref, y_ref, w2_s):
    @pl.when(_expert_changed(te_ref))
    def _():
        w2_s[...] = w2_ref[0].astype(BF16)

    @pl.when(pl.program_id(1) < nu_ref[0])
    def _():
        y = jnp.dot(a_ref[...], w2_s[...], preferred_element_type=F32) + b2_ref[0]
        y_ref[...] = _pack_pairs(y[:, :MOE_TN // 2], y[:, MOE_TN // 2:])


def _moe(tile_e, n_used, x_sorted, w1, b1, w2, b2):
    r = x_sorted.shape[0]
    n_tiles = r // MOE_TM
    nf = D_FF // MOE_TF
    nn = D_MODEL // MOE_TN

    def row(c, t, te, nu):
        return jnp.maximum(jnp.minimum(t, nu[0] - 1), 0)

    act = pl.pallas_call(
        _moe_up_kernel,
        grid_spec=pltpu.PrefetchScalarGridSpec(
            num_scalar_prefetch=2,
            grid=(nf, n_tiles),
            in_specs=[
                pl.BlockSpec((MOE_TM, HALF_D), lambda c, t, te, nu: (row(c, t, te, nu), 0)),
                pl.BlockSpec((1, D_MODEL, MOE_TF), lambda c, t, te, nu: (te[t], 0, c)),
                pl.BlockSpec((1, D_MODEL, MOE_TF), lambda c, t, te, nu: (te[t], 0, nf + c)),
                pl.BlockSpec((1, 1, MOE_TF), lambda c, t, te, nu: (te[t], 0, c)),
                pl.BlockSpec((1, 1, MOE_TF), lambda c, t, te, nu: (te[t], 0, nf + c)),
            ],
            out_specs=pl.BlockSpec((MOE_TM, MOE_TF), lambda c, t, te, nu: (row(c, t, te, nu), c)),
            scratch_shapes=[pltpu.VMEM((D_MODEL, MOE_TF), BF16), pltpu.VMEM((D_MODEL, MOE_TF), BF16)],
        ),
        out_shape=jax.ShapeDtypeStruct((r, D_FF), BF16),
        compiler_params=_cparams(("arbitrary", "arbitrary"), MOE_VMEM_MB),
        name="moe_up",
    )(tile_e, n_used, x_sorted, w1, w1, b1, b1)

    return pl.pallas_call(
        _moe_down_kernel,
        grid_spec=pltpu.PrefetchScalarGridSpec(
            num_scalar_prefetch=2,
            grid=(nn, n_tiles),
            in_specs=[
                pl.BlockSpec((MOE_TM, D_FF), lambda c, t, te, nu: (row(c, t, te, nu), 0)),
                pl.BlockSpec((1, D_FF, MOE_TN), lambda c, t, te, nu: (te[t], 0, c)),
                pl.BlockSpec((1, 1, MOE_TN), lambda c, t, te, nu: (te[t], 0, c)),
            ],
            out_specs=pl.BlockSpec((MOE_TM, MOE_TN // 2), lambda c, t, te, nu: (row(c, t, te, nu), c)),
            scratch_shapes=[pltpu.VMEM((D_FF, MOE_TN), BF16)],
        ),
        out_shape=jax.ShapeDtypeStruct((r, HALF_D), jnp.uint32),
        compiler_params=_cparams(("arbitrary", "arbitrary"), MOE_VMEM_MB),
        name="moe_down",
    )(tile_e, n_used, act, w2, b2)


def _combine_kernel(x_ref, y0_ref, y1_ref, y2_ref, y3_ref, tg_ref, g_ref, o_ref):
    tg = tg_ref[...]
    acc = x_ref[...]
    half = MOE_TN // 2
    for kk, y_ref in enumerate((y0_ref, y1_ref, y2_ref, y3_ref)):
        cols = []
        for c in range(D_MODEL // MOE_TN):
            cols += list(_unpack_pairs(y_ref[0, :, c * half:(c + 1) * half]))
        acc = acc + jnp.concatenate(cols, axis=1) * tg[:, kk:kk + 1]
    o_ref[...] = _rms(acc, g_ref[...])


def _combine(x2, yg, tg, g, row0):
    n = x2.shape[0]
    tm = min(TOK_TILE, n)
    blk0 = row0 // tm
    yspec = [pl.BlockSpec((1, tm, HALF_D), functools.partial(lambda i, kk: (kk, blk0 + i, 0), kk=kk))
             for kk in range(TOP_K)]
    return pl.pallas_call(
        _combine_kernel,
        grid=(n // tm,),
        in_specs=[pl.BlockSpec((tm, D_MODEL), lambda i: (i, 0))] + yspec + [
            pl.BlockSpec((tm, LANES), lambda i: (i, 0)),
            pl.BlockSpec((1, D_MODEL), lambda i: (0, 0)),
        ],
        out_specs=pl.BlockSpec((tm, D_MODEL), lambda i: (i, 0)),
        out_shape=jax.ShapeDtypeStruct((n, D_MODEL), F32),
        compiler_params=_cparams(("parallel",), 48),
        name="combine",
    )(x2, yg, yg, yg, yg, tg, g)


def _group_tiles(bsz, t):
    tt = min(t, TOK_TILE)
    bb = max(1, min(bsz, TOK_TILE // tt))
    return bb, tt


def _layer_group(x, c0, n0, m0, conv_state, mem_k, mem_v, cnt0, hn_all, row0, p):
    bsz, t, _ = x.shape
    n = bsz * t
    L = min(MLSTM_CHUNK, t)
    state_pad = jnp.pad(conv_state, ((0, 0), (CONV_PAD - CONV_STATE, 0), (0, 0)))
    conv_args = (state_pad, p["conv_w"], p["conv_b"], p["ln_g"], p["ln_b"])
    if t % PROJ_TM == 0:
        qkvo, gates, cv, new_conv = _proj_in_conv(x.reshape(n, D_MODEL), p["g_mix"], p["w_main"], p["wg"], p["bg"],
                                                  L, *conv_args, seq_len=t)
    else:
        qkvo, u, gates = _proj_in(x.reshape(n, D_MODEL), p["g_mix"], p["w_main"], p["wg"], p["bg"], L)
        ctt = min(CONV_TT, t)
        cbb = max(1, min(bsz, CONV_TT // ctt))
        cv, new_conv = _conv(u.reshape(bsz, t, CONV_CH), *conv_args, cbb, ctt)

    mbb = max(1, min(bsz, LANES // L)) if L < LANES else 1
    m0b = jnp.broadcast_to(m0[:, :, None], (bsz, M_HEADS, LANES))
    hm, c1, n1, m1 = _mlstm(qkvo.reshape(bsz, t, -1), gates, c0, n0, m0b, p["mh_g"], mbb, L)

    x1 = _proj_out(x.reshape(n, D_MODEL), hm.reshape(n, M_WIDTH), cv.reshape(n, CONV_CH), p["wo_a"], p["wo_b"])

    abb, att = _group_tiles(bsz, t)
    x2, hn_all, te, tg, tr, cnt = _attn_router(x1.reshape(bsz, t, D_MODEL), mem_k, mem_v, cnt0, hn_all, row0,
                                               p, abb, att)
    return x2.reshape(n, D_MODEL), hn_all, te, tg, tr, cnt, (c1, n1, m1[:, :, 0], new_conv)


def kernel(x_prompt, x_sample, mem_prompt, state_mlstm_C, state_mlstm_n, state_mlstm_m, state_conv, cache_mem_k, cache_mem_v, norm_mix_g, w_in, b_gates, mh_norm_g, conv_w, conv_b, conv_ln_g, conv_ln_b, w_out, norm_ca_g, norm_mem_g, ca_wq, ca_wk, ca_wv, ca_wo, norm_moe_g, router_w, router_b, moe_w1, moe_b1, moe_w2, moe_b2, final_norm_g):
    assert w_in.shape[0] == 1, "single layer"
    bp, tp, _ = x_prompt.shape
    bs, ts, _ = x_sample.shape
    n_p, n_s = bp * tp, bs * ts
    n_all = n_p + n_s

    wi = w_in[0]
    o0 = 0
    parts = []
    for width in (QK_W, QK_W, M_WIDTH, M_WIDTH, M_HEADS, M_HEADS, CONV_CH, CONV_CH):
        parts.append(wi[:, o0:o0 + width])
        o0 += width
    w_q, w_k, w_v, w_o, w_gi, w_gf, w_ua, w_ug = parts
    col_tiles = []
    w_qkvo = wi[:, :2 * QK_W + 2 * M_WIDTH]
    for c in range(PROJ_STEPS):
        col_tiles += [w_qkvo[:, c * QKVO_TN:(c + 1) * QKVO_TN], w_ua[:, c * GLU_TN:(c + 1) * GLU_TN],
                      w_ug[:, c * GLU_TN:(c + 1) * GLU_TN]]
    p = {
        "g_mix": norm_mix_g[0][None, :],
        "w_main": jnp.concatenate(col_tiles, axis=1).astype(BF16),
        "wg": jnp.concatenate([w_gi, w_gf], axis=1).T.astype(BF16),
        "bg": b_gates[0][:, None],
        "mh_g": mh_norm_g[0],
        "conv_w": conv_w[0],
        "conv_b": conv_b[0][None, :],
        "ln_g": conv_ln_g[0][None, :],
        "ln_b": conv_ln_b[0][None, :],
        "wo_a": w_out[0][:M_WIDTH].astype(BF16),
        "wo_b": w_out[0][M_WIDTH:].astype(BF16),
        "g_ca": norm_ca_g[0][None, :],
        "wq": ca_wq[0].astype(BF16),
        "wo": ca_wo[0].astype(BF16),
        "g_moe": norm_moe_g[0][None, :],
        "router_w": jnp.pad(router_w[0], ((0, 0), (0, LANES - N_EXPERTS))).astype(BF16),
        "router_b": jnp.concatenate([router_b[0], jnp.full((LANES - N_EXPERTS,), NEG_BIG, F32)])[None, :],
    }

    wkv = jnp.concatenate([ca_wk[0], ca_wv[0]], axis=1).astype(BF16)
    mk, mv = _mem_kv(mem_prompt.reshape(bp * N_MEM, D_MODEL), norm_mem_g[0][None, :], wkv)
    mk = mk.reshape(bp, N_MEM, CA_W)
    mv = mv.reshape(bp, N_MEM, CA_W)
    hn_all = jnp.zeros((n_all, HALF_D), jnp.uint32)
    cnt0 = jnp.zeros((1, LANES), F32)
    zc = jnp.zeros((bp, M_HEADS, M_DK, M_DV), F32)
    zn = jnp.zeros((bp, M_HEADS, M_DK), F32)
    zm = jnp.zeros((bp, M_HEADS), F32)
    zconv = jnp.zeros((bp, CONV_STATE, CONV_CH), F32)
    x2p, hn_all, te_p, tg_p, tr_p, cnt1, st_p = _layer_group(
        x_prompt, zc, zn, zm, zconv, mk.astype(BF16), mv.astype(BF16), cnt0, hn_all, 0, p)

    x2s, hn_all, te_s, tg_s, tr_s, cnt2, st_s = _layer_group(
        x_sample, state_mlstm_C[0], state_mlstm_n[0], state_mlstm_m[0], state_conv[0],
        cache_mem_k[0].reshape(bs, N_MEM, CA_W).astype(BF16), cache_mem_v[0].reshape(bs, N_MEM, CA_W).astype(BF16),
        cnt1, hn_all, n_p, p)

    te = jnp.concatenate([te_p[:, :TOP_K], te_s[:, :TOP_K]], axis=0)
    tr = jnp.concatenate([tr_p[:, :TOP_K], tr_s[:, :TOP_K]], axis=0)
    counts = cnt2[0, :N_EXPERTS].astype(jnp.int32)
    tiles_per_e = (counts + MOE_TM - 1) // MOE_TM
    tile_end = jnp.cumsum(tiles_per_e)
    row_start = (tile_end - tiles_per_e) * MOE_TM
    n_tiles = -(-(n_all * TOP_K) // MOE_TM) + N_EXPERTS
    n_used = tile_end[-1:]
    tile_ids = jnp.minimum(jnp.arange(n_tiles, dtype=jnp.int32), n_used[0] - 1)
    tile_e = jnp.minimum(jnp.sum(tile_ids[:, None] >= tile_end[None, :], axis=1), N_EXPERTS - 1).astype(jnp.int32)
    tile_first = jnp.arange(n_tiles, dtype=jnp.int32) - (tile_end - tiles_per_e)[tile_e]
    tile_valid = jnp.clip(counts[tile_e] - tile_first * MOE_TM, 0, MOE_TM)
    tile_valid = jnp.where(jnp.arange(n_tiles) < n_used[0], tile_valid, 0).astype(jnp.int32)
    moe_meta = jnp.concatenate([n_used.astype(jnp.int32), tile_valid])
    dest = row_start[te] + tr
    dest3 = dest.reshape(n_all // SC_WIN, SC_WIN, TOP_K).transpose(0, 2, 1)
    x_sorted = _sc_scatter_rows(hn_all, dest3, n_tiles * MOE_TM)

    y_sorted = _moe(tile_e, moe_meta, x_sorted, moe_w1[0], moe_b1[0][:, None, :], moe_w2[0],
                    moe_b2[0][:, None, :])

    yg = y_sorted.at[dest.T.reshape(-1)].get(mode="promise_in_bounds").reshape(TOP_K, n_all, HALF_D)
    fg = final_norm_g[None, :]
    y_prompt = _combine(x2p, yg, tg_p, fg, 0).reshape(bp, tp, D_MODEL)
    y_sample = _combine(x2s, yg, tg_s, fg, n_p).reshape(bs, ts, D_MODEL)

    c1, n1, m1, conv1 = st_p
    c2, n2, m2, conv2 = st_s
    mk4 = mk.reshape(1, bp, N_MEM, CA_HEADS, CA_DH)
    mv4 = mv.reshape(1, bp, N_MEM, CA_HEADS, CA_DH)
    return (y_prompt, y_sample, c1[None], n1[None], m1[None], conv1[None], mk4, mv4,
            c2[None], n2[None], m2[None], conv2[None])
```

```python
import functools

import jax
import jax.numpy as jnp
from jax import lax
from jax.experimental import pallas as pl
from jax.experimental.pallas import tpu as pltpu
from jax.experimental.pallas import tpu_sc as plsc

F32 = jnp.float32
BF16 = jnp.bfloat16

D_MODEL = 2048
HALF_D = D_MODEL // 2
M_HEADS = 4
M_DV = 256
M_DK = 128
QK_W = M_HEADS * M_DK
M_WIDTH = M_HEADS * M_DV
CONV_CH = 1024
CONV_WIDTH = 31
CONV_STATE = CONV_WIDTH - 1
CONV_PAD = 32
N_MEM = 256
CA_HEADS = 4
CA_DH = 128
CA_W = CA_HEADS * CA_DH
N_EXPERTS = 32
TOP_K = 4
D_FF = 2048
SWIGLU_ALPHA = 1.702
SWIGLU_LIMIT = 7.0
EPS = 1e-6
LANES = 128
SUBLANES = 8
NEG_BIG = -1e30

PROJ_TM = 1024
PROJ_STEPS = 4
QKVO_TN = (2 * QK_W + 2 * M_WIDTH) // PROJ_STEPS
GLU_TN = CONV_CH // PROJ_STEPS
PROJ_TN = QKVO_TN + 2 * GLU_TN
CONV_SUB = PROJ_TM // PROJ_STEPS
TOK_TILE = 512
MOE_TM = 512
MOE_TF = 1024
MOE_TN = 2048
MOE_VMEM_MB = 60
MLSTM_CHUNK = 256
MLSTM_BB = 2
CONV_TT = 256
CONV_RB = 32
CONV_LC = 512
SC_WIN = 32


def _cparams(sem, vmem_mb):
    return pltpu.CompilerParams(dimension_semantics=sem, vmem_limit_bytes=vmem_mb << 20)


def _log_sigmoid(x):
    return jnp.minimum(x, 0.0) - jnp.log1p(jnp.exp(-jnp.abs(x)))


def _sigmoid(x):
    return 1.0 / (1.0 + jnp.exp(-x))


def _pack_pairs(lo, hi):
    return lax.bitcast_convert_type(pltpu.pack_elementwise([lo, hi], packed_dtype=BF16), jnp.uint32)


def _unpack_pairs(w):
    return tuple(pltpu.unpack_elementwise(w, index=i, packed_dtype=BF16, unpacked_dtype=F32) for i in range(2))


def _rms(x, g):
    ms = jnp.mean(x * x, axis=-1, keepdims=True)
    return x * lax.rsqrt(ms + EPS) * g


def _proj_prologue(x_ref, g_ref, wg_ref, bg_ref, h_ref, gates_ref, chunk):
    tm = x_ref.shape[0]
    h = _rms(x_ref[...], g_ref[...]).astype(BF16)
    h_ref[...] = h
    gt = lax.dot_general(wg_ref[...], h, (((1,), (1,)), ((), ())), preferred_element_type=F32)
    gt = gt + bg_ref[...]
    row = lax.broadcasted_iota(jnp.int32, gt.shape, 0)
    gt = jnp.where(row < M_HEADS, gt, _log_sigmoid(gt))
    for c in range(tm // chunk):
        gates_ref[c] = gt[:, c * chunk:(c + 1) * chunk]


def _proj_step(h_ref, w_ref):
    z = jnp.dot(h_ref[...], w_ref[...], preferred_element_type=F32)
    glu = z[:, QKVO_TN:QKVO_TN + GLU_TN] * _sigmoid(z[:, QKVO_TN + GLU_TN:])
    return z[:, :QKVO_TN].astype(BF16), glu.astype(BF16)


def _proj_in_kernel(x_ref, g_ref, w_ref, wg_ref, bg_ref, qkvo_ref, u_ref, gates_ref, h_ref, *, chunk):
    @pl.when(pl.program_id(1) == 0)
    def _():
        _proj_prologue(x_ref, g_ref, wg_ref, bg_ref, h_ref, gates_ref, chunk)

    qkvo_ref[...], u_ref[...] = _proj_step(h_ref, w_ref)


def _proj_in(x, g, w_main, wg, bg, chunk):
    n = x.shape[0]
    tm = min(PROJ_TM, n)
    return pl.pallas_call(
        functools.partial(_proj_in_kernel, chunk=chunk),
        grid=(n // tm, PROJ_STEPS),
        in_specs=[
            pl.BlockSpec((tm, D_MODEL), lambda i, j: (i, 0)),
            pl.BlockSpec((1, D_MODEL), lambda i, j: (0, 0)),
            pl.BlockSpec((D_MODEL, PROJ_TN), lambda i, j: (0, j)),
            pl.BlockSpec((2 * M_HEADS, D_MODEL), lambda i, j: (0, 0)),
            pl.BlockSpec((2 * M_HEADS, 1), lambda i, j: (0, 0)),
        ],
        out_specs=[
            pl.BlockSpec((tm, QKVO_TN), lambda i, j: (i, j)),
            pl.BlockSpec((tm, GLU_TN), lambda i, j: (i, j)),
            pl.BlockSpec((tm // chunk, 2 * M_HEADS, chunk), lambda i, j: (i, 0, 0)),
        ],
        out_shape=[
            jax.ShapeDtypeStruct((n, 2 * QK_W + 2 * M_WIDTH), BF16),
            jax.ShapeDtypeStruct((n, CONV_CH), BF16),
            jax.ShapeDtypeStruct((n // chunk, 2 * M_HEADS, chunk), F32),
        ],
        scratch_shapes=[pltpu.VMEM((tm, D_MODEL), BF16)],
        compiler_params=_cparams(("parallel", "arbitrary"), 48),
        name="proj_in",
    )(x, g, w_main, wg, bg)


def _proj_in_conv_kernel(x_ref, g_ref, w_ref, wg_ref, bg_ref, st_ref, cw_ref, cb_ref, lg_ref, lb_ref,
                         qkvo_ref, gates_ref, cv_ref, ns_ref, h_ref, u_s, hist_s, e_ref, er_ref, c_ref,
                         *, chunk, tiles_per_seq):
    i = pl.program_id(0)
    j = pl.program_id(1)
    p = i - 1
    sub = CONV_SUB
    ext = sub + CONV_PAD

    @pl.when((i == 0) & (j == 0))
    def _():
        u_s[...] = jnp.zeros(u_s.shape, BF16)
        hist_s[...] = jnp.zeros(hist_s.shape, F32)
        e_ref[:, ext:, :] = jnp.zeros((1, SUBLANES, CONV_CH), F32)

    @pl.when(j == 0)
    def _():
        _proj_prologue(x_ref, g_ref, wg_ref, bg_ref, h_ref, gates_ref, chunk)

    @pl.when((j == 0) & (p >= 0) & (p % tiles_per_seq == 0))
    def _():
        hist_s[...] = st_ref[0]

    r0 = pl.multiple_of(j * sub, sub)
    e_ref[0, 0:CONV_PAD, :] = hist_s[...]
    for c in range(PROJ_STEPS):
        e_ref[0, CONV_PAD:ext, c * GLU_TN:(c + 1) * GLU_TN] = u_s[(i + 1) % 2, c, pl.ds(r0, sub), :].astype(F32)
    prepare, row_block = _conv_stages(e_ref, er_ref, c_ref, cw_ref, cb_ref, lg_ref, lb_ref, cv_ref, 0, sub)
    prepare()
    for r0 in range(0, sub, CONV_RB):
        row_block(r0)
    hist_s[...] = e_ref[0, sub:ext, :]

    qkvo_ref[...], u_s[i % 2, j] = _proj_step(h_ref, w_ref)

    @pl.when((j == PROJ_STEPS - 1) & (p >= 0) & (p % tiles_per_seq == tiles_per_seq - 1))
    def _():
        ns_ref[0] = e_ref[0, ext - CONV_STATE:ext, :]


def _proj_in_conv(x, g, w_main, wg, bg, chunk, state_pad, conv_w, conv_b, ln_g, ln_b, seq_len):
    n = x.shape[0]
    tm = PROJ_TM
    n_tiles = n // tm
    tiles_per_seq = seq_len // tm
    bsz = n // seq_len
    assert tm == PROJ_STEPS * CONV_SUB and seq_len % tm == 0
    last = n_tiles - 1

    def tile(i, j):
        return jnp.minimum(i, last)

    def seq(i, j):
        return jnp.clip((i - 1) // tiles_per_seq, 0, bsz - 1)

    const2 = lambda i, j: (0, 0)
    return pl.pallas_call(
        functools.partial(_proj_in_conv_kernel, chunk=chunk, tiles_per_seq=tiles_per_seq),
        grid=(n_tiles + 1, PROJ_STEPS),
        in_specs=[
            pl.BlockSpec((tm, D_MODEL), lambda i, j: (tile(i, j), 0)),
            pl.BlockSpec((1, D_MODEL), const2),
            pl.BlockSpec((D_MODEL, PROJ_TN), lambda i, j: (0, j)),
            pl.BlockSpec((2 * M_HEADS, D_MODEL), const2),
            pl.BlockSpec((2 * M_HEADS, 1), const2),
            pl.BlockSpec((1, CONV_PAD, CONV_CH), lambda i, j: (seq(i, j), 0, 0)),
            pl.BlockSpec((CONV_WIDTH, CONV_CH), const2),
            pl.BlockSpec((1, CONV_CH), const2),
            pl.BlockSpec((1, CONV_CH), const2),
            pl.BlockSpec((1, CONV_CH), const2),
        ],
        out_specs=[
            pl.BlockSpec((tm, QKVO_TN), lambda i, j: (tile(i, j), jnp.where(i <= last, j, PROJ_STEPS - 1))),
            pl.BlockSpec((tm // chunk, 2 * M_HEADS, chunk), lambda i, j: (tile(i, j), 0, 0)),
            pl.BlockSpec((CONV_SUB, CONV_CH), lambda i, j: (jnp.maximum((i - 1) * PROJ_STEPS + j, 0), 0)),
            pl.BlockSpec((1, CONV_STATE, CONV_CH), lambda i, j: (seq(i, j), 0, 0)),
        ],
        out_shape=[
            jax.ShapeDtypeStruct((n, 2 * QK_W + 2 * M_WIDTH), BF16),
            jax.ShapeDtypeStruct((n // chunk, 2 * M_HEADS, chunk), F32),
            jax.ShapeDtypeStruct((n, CONV_CH), BF16),
            jax.ShapeDtypeStruct((bsz, CONV_STATE, CONV_CH), F32),
        ],
        scratch_shapes=[
            pltpu.VMEM((tm, D_MODEL), BF16),
            pltpu.VMEM((2, PROJ_STEPS, tm, GLU_TN), BF16),
            pltpu.VMEM((CONV_PAD, CONV_CH), F32),
            pltpu.VMEM((1, CONV_SUB + CONV_PAD + SUBLANES, CONV_CH), F32),
            pltpu.VMEM((SUBLANES - 1, CONV_SUB + CONV_PAD, CONV_CH), F32),
            pltpu.VMEM((CONV_SUB, CONV_CH), F32),
        ],
        compiler_params=_cparams(("arbitrary", "arbitrary"), 56),
        name="proj_in_conv",
    )(x, g, w_main, wg, bg, state_pad, conv_w, conv_b, ln_g, ln_b)


def _conv_stages(e_ref, er_ref, c_ref, w_ref, cb_ref, lg_ref, lb_ref, out_ref, b, rows):
    ext = rows + CONV_PAD
    base = CONV_PAD - CONV_STATE

    def prepare():
        for r in range(1, SUBLANES):
            er_ref[r - 1] = e_ref[b, r:r + ext, :]

    def row_block(r0):
        for lc in range(CONV_CH // CONV_LC):
            ls = slice(lc * CONV_LC, (lc + 1) * CONV_LC)
            acc = jnp.zeros((CONV_RB, CONV_LC), F32) + cb_ref[:, ls]
            for j in range(CONV_WIDTH):
                off = j + base
                a, r = (off // SUBLANES) * SUBLANES, off % SUBLANES
                if r == 0:
                    tap = e_ref[b, pl.ds(r0 + a, CONV_RB), ls]
                else:
                    tap = er_ref[r - 1, pl.ds(r0 + a, CONV_RB), ls]
                acc = acc + w_ref[j:j + 1, ls] * tap
            c_ref[pl.ds(r0, CONV_RB), ls] = acc
        c = c_ref[pl.ds(r0, CONV_RB), :]
        mu = jnp.mean(c, axis=-1, keepdims=True)
        cc = c - mu
        var = jnp.mean(cc * cc, axis=-1, keepdims=True)
        cn = cc * lax.rsqrt(var + EPS) * lg_ref[...] + lb_ref[...]
        out_ref[pl.ds(r0, CONV_RB), :] = (cn * _sigmoid(cn)).astype(BF16)

    return prepare, row_block


def _conv_rows(e_ref, er_ref, c_ref, w_ref, cb_ref, lg_ref, lb_ref, out_ref, b, rows):
    prepare, row_block = _conv_stages(e_ref, er_ref, c_ref, w_ref, cb_ref, lg_ref, lb_ref, out_ref, b, rows)
    prepare()

    def body(rb, carry):
        row_block(pl.multiple_of(rb * CONV_RB, CONV_RB))
        return carry

    lax.fori_loop(0, rows // CONV_RB, body, 0)


def _conv_kernel(u_ref, st_ref, w_ref, cb_ref, lg_ref, lb_ref, cv_ref, ns_ref, e_ref, er_ref, c_ref, *, bb, tt):
    t = pl.program_id(1)
    nt = pl.num_programs(1)
    ext = tt + CONV_PAD

    @pl.when(t == 0)
    def _():
        e_ref[:, 0:CONV_PAD, :] = st_ref[...]

    @pl.when(t > 0)
    def _():
        e_ref[:, 0:CONV_PAD, :] = e_ref[:, tt:ext, :]

    e_ref[:, CONV_PAD:ext, :] = u_ref[...].astype(F32)
    e_ref[:, ext:, :] = jnp.zeros((bb, SUBLANES, CONV_CH), F32)

    for b in range(bb):
        _conv_rows(e_ref, er_ref, c_ref, w_ref, cb_ref, lg_ref, lb_ref, cv_ref.at[b], b, tt)

    @pl.when(t == nt - 1)
    def _():
        ns_ref[...] = e_ref[:, ext - CONV_STATE:ext, :]


def _conv(u, state_pad, conv_w, conv_b, ln_g, ln_b, bb, tt):
    bsz, t, _ = u.shape
    return pl.pallas_call(
        functools.partial(_conv_kernel, bb=bb, tt=tt),
        grid=(bsz // bb, t // tt),
        in_specs=[
            pl.BlockSpec((bb, tt, CONV_CH), lambda b, s: (b, s, 0)),
            pl.BlockSpec((bb, CONV_PAD, CONV_CH), lambda b, s: (b, 0, 0)),
            pl.BlockSpec((CONV_WIDTH, CONV_CH), lambda b, s: (0, 0)),
            pl.BlockSpec((1, CONV_CH), lambda b, s: (0, 0)),
            pl.BlockSpec((1, CONV_CH), lambda b, s: (0, 0)),
            pl.BlockSpec((1, CONV_CH), lambda b, s: (0, 0)),
        ],
        out_specs=[
            pl.BlockSpec((bb, tt, CONV_CH), lambda b, s: (b, s, 0)),
            pl.BlockSpec((bb, CONV_STATE, CONV_CH), lambda b, s: (b, 0, 0)),
        ],
        out_shape=[
            jax.ShapeDtypeStruct((bsz, t, CONV_CH), BF16),
            jax.ShapeDtypeStruct((bsz, CONV_STATE, CONV_CH), F32),
        ],
        scratch_shapes=[
            pltpu.VMEM((bb, tt + CONV_PAD + SUBLANES, CONV_CH), F32),
            pltpu.VMEM((SUBLANES - 1, tt + CONV_PAD, CONV_CH), F32),
            pltpu.VMEM((tt, CONV_CH), F32),
        ],
        compiler_params=_cparams(("parallel", "arbitrary"), 40),
        name="conv",
    )(u, state_pad, conv_w, conv_b, ln_g, ln_b)


def _mlstm_kernel(q_ref, k_ref, v_ref, o_ref, gt_ref, c0_ref, n0_ref, m0_ref, ng_ref,
                  hm_ref, cn_ref, nn_ref, mn_ref, c_s, n_s, m_s, *, bb, L):
    ci = pl.program_id(1)
    nc = pl.num_programs(1)

    @pl.when(ci == 0)
    def _():
        c_s[...] = c0_ref[...]
        n_s[...] = n0_ref[...]
        m_s[...] = m0_ref[...]

    rt = lax.broadcasted_iota(jnp.int32, (L, L), 0)
    cs = lax.broadcasted_iota(jnp.int32, (L, L), 1)
    causal = cs <= rt
    lower = jnp.where(causal, 1.0, 0.0).astype(BF16)
    upper = jnp.where(rt <= cs, 1.0, 0.0).astype(BF16)
    nt_dims = (((1,), (1,)), ((), ()))
    scale = M_DK ** -0.5

    for b in range(bb):
        rows = gt_ref[b]
        cum_rows = jnp.zeros((2 * M_HEADS, L), F32)
        cum_cols = jnp.zeros((L, 2 * M_HEADS), F32)
        rest = rows
        for _ in range(3):
            piece = rest.astype(BF16)
            rest = rest - piece.astype(F32)
            cum_rows = cum_rows + jnp.dot(piece, upper, preferred_element_type=F32)
            cum_cols = cum_cols + lax.dot_general(lower, piece, nt_dims, preferred_element_type=F32)
        for h in range(M_HEADS):
            ig_row = rows[h:h + 1, :]
            b_row = cum_rows[M_HEADS + h:M_HEADS + h + 1, :]
            b_col = cum_cols[:, M_HEADS + h:M_HEADS + h + 1]
            m0 = m_s[b, h:h + 1, 0:1]
            q = q_ref[b, :, h * M_DK:(h + 1) * M_DK]
            k = k_ref[b, :, h * M_DK:(h + 1) * M_DK]
            v = v_ref[b, :, h * M_DV:(h + 1) * M_DV]
            c0 = c_s[b, h]
            n0 = n_s[b, h:h + 1, :]

            g_col = b_col + m0
            dmat = jnp.where(causal, b_col - b_row + ig_row, -jnp.inf)
            m_col = jnp.maximum(g_col, jnp.max(dmat, axis=1, keepdims=True))
            w_state = jnp.exp(g_col - m_col) * scale
            qk = lax.dot_general(q, k, nt_dims, preferred_element_type=F32)
            s = qk * (jnp.exp(dmat - m_col) * scale)
            num = jnp.dot(s.astype(BF16), v, preferred_element_type=F32)
            num = num + w_state * jnp.dot(q, c0.astype(BF16), preferred_element_type=F32)
            qn = jnp.sum(q.astype(F32) * n0, axis=1, keepdims=True)
            den = jnp.sum(s, axis=1, keepdims=True) + w_state * qn
            hh = num / jnp.maximum(jnp.abs(den), jnp.exp(-m_col))

            hh = hh * lax.rsqrt(jnp.mean(hh * hh, axis=-1, keepdims=True) + EPS) * ng_ref[h:h + 1, :]
            og = _sigmoid(o_ref[b, :, h * M_DV:(h + 1) * M_DV].astype(F32))
            hm_ref[b, :, h * M_DV:(h + 1) * M_DV] = (og * hh).astype(BF16)

            b_last = b_row[:, L - 1:L]
            g_last = b_last + m0
            wk_row = b_last - b_row + ig_row
            m_new = jnp.maximum(g_last, jnp.max(wk_row, axis=1, keepdims=True))
            a0 = jnp.exp(g_last - m_new)
            ak_row = jnp.exp(wk_row - m_new)
            kts = (k.T.astype(F32) * ak_row).astype(BF16)
            c_s[b, h] = a0 * c0 + jnp.dot(kts, v, preferred_element_type=F32)
            kn = jnp.dot(ak_row.astype(BF16), k, preferred_element_type=F32)
            n_s[b, h:h + 1, :] = a0 * n0 + kn
            m_s[b, h:h + 1, :] = jnp.broadcast_to(m_new, (1, LANES))

    @pl.when(ci == nc - 1)
    def _():
        cn_ref[...] = c_s[...]
        nn_ref[...] = n_s[...]
        mn_ref[...] = m_s[...]


def _mlstm(qkvo, gates, c0, n0, m0, ng, bb, L):
    bsz, t, _ = qkvo.shape
    nc = t // L
    gates = gates.reshape(bsz, nc, 2 * M_HEADS, L)
    st_c = pl.BlockSpec((bb, M_HEADS, M_DK, M_DV), lambda b, c: (b, 0, 0, 0))
    st_n = pl.BlockSpec((bb, M_HEADS, M_DK), lambda b, c: (b, 0, 0))
    return pl.pallas_call(
        functools.partial(_mlstm_kernel, bb=bb, L=L),
        grid=(bsz // bb, nc),
        in_specs=[
            pl.BlockSpec((bb, L, QK_W), lambda b, c: (b, c, 0)),
            pl.BlockSpec((bb, L, QK_W), lambda b, c: (b, c, 1)),
            pl.BlockSpec((bb, L, M_WIDTH), lambda b, c: (b, c, 1)),
            pl.BlockSpec((bb, L, M_WIDTH), lambda b, c: (b, c, 2)),
            pl.BlockSpec((bb, None, 2 * M_HEADS, L), lambda b, c: (b, c, 0, 0)),
            st_c, st_n, st_n,
            pl.BlockSpec((M_HEADS, M_DV), lambda b, c: (0, 0)),
        ],
        out_specs=[
            pl.BlockSpec((bb, L, M_WIDTH), lambda b, c: (b, c, 0)),
            st_c, st_n, st_n,
        ],
        out_shape=[
            jax.ShapeDtypeStruct((bsz, t, M_WIDTH), BF16),
            jax.ShapeDtypeStruct((bsz, M_HEADS, M_DK, M_DV), F32),
            jax.ShapeDtypeStruct((bsz, M_HEADS, M_DK), F32),
            jax.ShapeDtypeStruct((bsz, M_HEADS, LANES), F32),
        ],
        scratch_shapes=[
            pltpu.VMEM((bb, M_HEADS, M_DK, M_DV), F32),
            pltpu.VMEM((bb, M_HEADS, M_DK), F32),
            pltpu.VMEM((bb, M_HEADS, LANES), F32),
        ],
        compiler_params=_cparams(("parallel", "arbitrary"), 40),
        name="mlstm",
    )(qkvo, qkvo, qkvo, qkvo, gates, c0, n0, m0, ng)


def _proj_out_kernel(x_ref, hm_ref, cv_ref, wa_ref, wb_ref, o_ref):
    acc = jnp.dot(hm_ref[...], wa_ref[...], preferred_element_type=F32)
    acc = acc + jnp.dot(cv_ref[...], wb_ref[...], preferred_element_type=F32)
    o_ref[...] = x_ref[...] + acc


def _proj_out(x, hm, cv, wa, wb):
    n = x.shape[0]
    tm = min(TOK_TILE, n)
    return pl.pallas_call(
        _proj_out_kernel,
        grid=(n // tm,),
        in_specs=[
            pl.BlockSpec((tm, D_MODEL), lambda i: (i, 0)),
            pl.BlockSpec((tm, M_WIDTH), lambda i: (i, 0)),
            pl.BlockSpec((tm, CONV_CH), lambda i: (i, 0)),
            pl.BlockSpec((M_WIDTH, D_MODEL), lambda i: (0, 0)),
            pl.BlockSpec((CONV_CH, D_MODEL), lambda i: (0, 0)),
        ],
        out_specs=pl.BlockSpec((tm, D_MODEL), lambda i: (i, 0)),
        out_shape=jax.ShapeDtypeStruct((n, D_MODEL), F32),
        compiler_params=_cparams(("parallel",), 48),
        name="proj_out",
    )(x, hm, cv, wa, wb)


def _mem_kv_kernel(m_ref, g_ref, w_ref, k_ref, v_ref):
    mn = _rms(m_ref[...], g_ref[...]).astype(BF16)
    z = jnp.dot(mn, w_ref[...], preferred_element_type=F32)
    k_ref[...] = z[:, :CA_W]
    v_ref[...] = z[:, CA_W:]


def _mem_kv(mem, g, wkv):
    n = mem.shape[0]
    return pl.pallas_call(
        _mem_kv_kernel,
        grid=(1,),
        in_specs=[
            pl.BlockSpec((n, D_MODEL), lambda i: (0, 0)),
            pl.BlockSpec((1, D_MODEL), lambda i: (0, 0)),
            pl.BlockSpec((D_MODEL, 2 * CA_W), lambda i: (0, 0)),
        ],
        out_specs=[pl.BlockSpec((n, CA_W), lambda i: (0, 0))] * 2,
        out_shape=[jax.ShapeDtypeStruct((n, CA_W), F32)] * 2,
        compiler_params=_cparams(("arbitrary",), 40),
        name="mem_kv",
    )(mem, g, wkv)


def _attn_router_kernel(x_ref, gca_ref, wq_ref, k_ref, v_ref, wo_ref, gmoe_ref, rw_ref, rb_ref, cnt0_ref,
                        hn_in_ref, x2_ref, hn_ref, te_ref, tg_ref, tr_ref, cnt_ref, o_s, cnt_s, *, bb, tt):
    del hn_in_ref
    rows = bb * tt

    @pl.when((pl.program_id(0) == 0) & (pl.program_id(1) == 0))
    def _():
        cnt_s[...] = cnt0_ref[...]

    x = x_ref[...].reshape(rows, D_MODEL)
    h = _rms(x, gca_ref[...]).astype(BF16)
    q = jnp.dot(h, wq_ref[...], preferred_element_type=F32) * (CA_DH ** -0.5)
    q = q.astype(BF16)
    for b in range(bb):
        for hd in range(CA_HEADS):
            cols = slice(hd * CA_DH, (hd + 1) * CA_DH)
            qb = q[b * tt:(b + 1) * tt, cols]
            s = lax.dot_general(qb, k_ref[b, :, cols], (((1,), (1,)), ((), ())), preferred_element_type=F32)
            e = jnp.exp(s - jnp.max(s, axis=-1, keepdims=True))
            ov = jnp.dot(e.astype(BF16), v_ref[b, :, cols], preferred_element_type=F32)
            o_s[b * tt:(b + 1) * tt, cols] = (ov / jnp.sum(e, axis=-1, keepdims=True)).astype(BF16)
    x2 = x + jnp.dot(o_s[...], wo_ref[...], preferred_element_type=F32)
    x2_ref[...] = x2.reshape(bb, tt, D_MODEL)

    hn = _rms(x2, gmoe_ref[...])
    hn_bf = hn.astype(BF16)
    hn_ref[...] = _pack_pairs(hn[:, :HALF_D], hn[:, HALF_D:])
    logits = jnp.dot(hn_bf, rw_ref[...], preferred_element_type=F32) + rb_ref[...]
    lane = lax.broadcasted_iota(jnp.int32, (rows, LANES), 1)
    lane_f = lane.astype(F32)
    work = logits
    sel = []
    vals = []
    for _ in range(TOP_K):
        mx = jnp.max(work, axis=-1, keepdims=True)
        idx = jnp.min(jnp.where(work == mx, lane_f, float(LANES)), axis=-1, keepdims=True)
        hit = lane_f == idx
        sel.append((idx, hit))
        vals.append(mx)
        work = jnp.where(hit, -jnp.inf, work)
    ex = [jnp.exp(vv - vals[0]) for vv in vals]
    tot = ex[0] + ex[1] + ex[2] + ex[3]
    assigned = jnp.zeros((rows, LANES), F32)
    for _, hit in sel:
        assigned = assigned + hit.astype(F32)
    rt = lax.broadcasted_iota(jnp.int32, (rows, rows), 0)
    cs = lax.broadcasted_iota(jnp.int32, (rows, rows), 1)
    strict = (cs < rt).astype(BF16)
    before = jnp.dot(strict, assigned.astype(BF16), preferred_element_type=F32) + cnt_s[...]
    te = jnp.zeros((rows, LANES), F32)
    tg = jnp.zeros((rows, LANES), F32)
    tr = jnp.zeros((rows, LANES), F32)
    for kk, (idx, hit) in enumerate(sel):
        rank = jnp.sum(jnp.where(hit, before, 0.0), axis=-1, keepdims=True)
        te = jnp.where(lane == kk, idx, te)
        tg = jnp.where(lane == kk, ex[kk] / tot, tg)
        tr = jnp.where(lane == kk, rank, tr)
    te_ref[...] = te.astype(jnp.int32)
    tg_ref[...] = tg
    tr_ref[...] = tr.astype(jnp.int32)
    cnt_s[...] = cnt_s[...] + jnp.sum(assigned, axis=0, keepdims=True)
    cnt_ref[...] = cnt_s[...]


def _attn_router(x1, mem_k, mem_v, cnt0, hn_all, row0, p, bb, tt):
    bsz, t, _ = x1.shape
    rows = bb * tt
    n = bsz * t
    nb, nt = bsz // bb, t // tt
    blk0 = row0 // rows
    tok = lambda b, s: (b * nt + s, 0)
    const = lambda b, s: (0, 0)
    grid_spec = dict(
        grid=(nb, nt),
        in_specs=[
            pl.BlockSpec((bb, tt, D_MODEL), lambda b, s: (b, s, 0)),
            pl.BlockSpec((1, D_MODEL), const),
            pl.BlockSpec((D_MODEL, CA_W), const),
            pl.BlockSpec((bb, N_MEM, CA_W), lambda b, s: (b, 0, 0)),
            pl.BlockSpec((bb, N_MEM, CA_W), lambda b, s: (b, 0, 0)),
            pl.BlockSpec((CA_W, D_MODEL), const),
            pl.BlockSpec((1, D_MODEL), const),
            pl.BlockSpec((D_MODEL, LANES), const),
            pl.BlockSpec((1, LANES), const),
            pl.BlockSpec((1, LANES), const),
            pl.BlockSpec(memory_space=pl.ANY),
        ],
        out_specs=[
            pl.BlockSpec((bb, tt, D_MODEL), lambda b, s: (b, s, 0)),
            pl.BlockSpec((rows, HALF_D), lambda b, s: (blk0 + b * nt + s, 0)),
            pl.BlockSpec((rows, LANES), tok),
            pl.BlockSpec((rows, LANES), tok),
            pl.BlockSpec((rows, LANES), tok),
            pl.BlockSpec((1, LANES), const),
        ],
    )
    return pl.pallas_call(
        functools.partial(_attn_router_kernel, bb=bb, tt=tt),
        out_shape=[
            jax.ShapeDtypeStruct((bsz, t, D_MODEL), F32),
            jax.ShapeDtypeStruct(hn_all.shape, jnp.uint32),
            jax.ShapeDtypeStruct((n, LANES), jnp.int32),
            jax.ShapeDtypeStruct((n, LANES), F32),
            jax.ShapeDtypeStruct((n, LANES), jnp.int32),
            jax.ShapeDtypeStruct((1, LANES), F32),
        ],
        scratch_shapes=[pltpu.VMEM((rows, CA_W), BF16), pltpu.VMEM((1, LANES), F32)],
        input_output_aliases={10: 1},
        compiler_params=_cparams(("arbitrary", "arbitrary"), 48),
        name="attn_router",
        **grid_spec,
    )(x1, p["g_ca"], p["wq"], mem_k, mem_v, p["wo"], p["g_moe"], p["router_w"], p["router_b"], cnt0, hn_all)


def _sc_scatter_rows(rows, dest3, n_out):
    n, d = rows.shape
    nwin = dest3.shape[0]
    mesh = plsc.VectorSubcoreMesh(core_axis_name="c", subcore_axis_name="s")
    workers = mesh.num_cores * mesh.num_subcores
    assert nwin * SC_WIN == n and nwin % workers == 0, (n, nwin, workers)
    per = nwin // workers

    @functools.partial(
        pl.kernel, out_type=jax.ShapeDtypeStruct((n_out, d), rows.dtype), mesh=mesh,
        scratch_types=[pltpu.VMEM((SC_WIN, d), rows.dtype), pltpu.VMEM((TOP_K, SC_WIN), jnp.int32),
                       pltpu.SemaphoreType.DMA])
    def scatter(x_hbm, i_hbm, o_hbm, buf, idx, sem):
        wid = lax.axis_index("s") * mesh.num_cores + lax.axis_index("c")

        @pl.loop(0, per)
        def _(j):
            w = wid * per + j
            pltpu.sync_copy(x_hbm.at[pl.ds(w * SC_WIN, SC_WIN)], buf)
            pltpu.sync_copy(i_hbm.at[w], idx)
            copies = [pltpu.async_copy(buf, o_hbm.at[idx.at[kk]], sem) for kk in range(TOP_K)]
            for cp in copies:
                cp.wait()

    return scatter(rows, dest3)


def _expert_changed(te_ref):
    t = pl.program_id(1)
    return (t == 0) | (te_ref[t] != te_ref[jnp.maximum(t - 1, 0)])


def _moe_up_kernel(te_ref, nu_ref, x_ref, w1g_ref, w1l_ref, b1g_ref, b1l_ref, act_ref, wg_s, wl_s):
    @pl.when(_expert_changed(te_ref))
    def _():
        wg_s[...] = w1g_ref[0].astype(BF16)
        wl_s[...] = w1l_ref[0].astype(BF16)

    def up(words):
        lo, hi = _unpack_pairs(words)
        x = jnp.concatenate([lo.astype(BF16), hi.astype(BF16)], axis=1)
        glu = jnp.dot(x, wg_s[...], preferred_element_type=F32) + b1g_ref[0]
        lin = jnp.dot(x, wl_s[...], preferred_element_type=F32) + b1l_ref[0]
        glu = jnp.minimum(glu, SWIGLU_LIMIT)
        lin = jnp.clip(lin, -SWIGLU_LIMIT, SWIGLU_LIMIT)
        act_ref[...] = (glu * _sigmoid(SWIGLU_ALPHA * glu) * (lin + 1.0)).astype(BF16)

    t = pl.program_id(1)
    used = t < nu_ref[0]
    valid = nu_ref[1 + t]

    @pl.when(used & (valid >= MOE_TM))
    def _():
        up(x_ref[...])

    @pl.when(used & (valid < MOE_TM))
    def _():
        rowid = lax.broadcasted_iota(jnp.int32, (MOE_TM, HALF_D), 0)
        up(jnp.where(rowid < valid, x_ref[...], jnp.uint32(0)))


def _moe_down_kernel(te_ref, nu_ref, a_ref, w2_ref, b2_ref, y_ref, w2_s):
    @pl.when(_expert_changed(te_ref))
    def _():
        w2_s[...] = w2_ref[0].astype(BF16)

    @pl.when(pl.program_id(1) < nu_ref[0])
    def _():
        y = jnp.dot(a_ref[...], w2_s[...], preferred_element_type=F32) + b2_ref[0]
        y_ref[...] = _pack_pairs(y[:, :MOE_TN // 2], y[:, MOE_TN // 2:])


def _moe(tile_e, n_used, x_sorted, w1, b1, w2, b2):
    r = x_sorted.shape[0]
    n_tiles = r // MOE_TM
    nf = D_FF // MOE_TF
    nn = D_MODEL // MOE_TN

    def row(c, t, te, nu):
        return jnp.maximum(jnp.minimum(t, nu[0] - 1), 0)

    act = pl.pallas_call(
        _moe_up_kernel,
        grid_spec=pltpu.PrefetchScalarGridSpec(
            num_scalar_prefetch=2,
            grid=(nf, n_tiles),
            in_specs=[
                pl.BlockSpec((MOE_TM, HALF_D), lambda c, t, te, nu: (row(c, t, te, nu), 0)),
                pl.BlockSpec((1, D_MODEL, MOE_TF), lambda c, t, te, nu: (te[t], 0, c)),
                pl.BlockSpec((1, D_MODEL, MOE_TF), lambda c, t, te, nu: (te[t], 0, nf + c)),
                pl.BlockSpec((1, 1, MOE_TF), lambda c, t, te, nu: (te[t], 0, c)),
                pl.BlockSpec((1, 1, MOE_TF), lambda c, t, te, nu: (te[t], 0, nf + c)),
            ],
            out_specs=pl.BlockSpec((MOE_TM, MOE_TF), lambda c, t, te, nu: (row(c, t, te, nu), c)),
            scratch_shapes=[pltpu.VMEM((D_MODEL, MOE_TF), BF16), pltpu.VMEM((D_MODEL, MOE_TF), BF16)],
        ),
        out_shape=jax.ShapeDtypeStruct((r, D_FF), BF16),
        compiler_params=_cparams(("arbitrary", "arbitrary"), MOE_VMEM_MB),
        name="moe_up",
    )(tile_e, n_used, x_sorted, w1, w1, b1, b1)

    return pl.pallas_call(
        _moe_down_kernel,
        grid_spec=pltpu.PrefetchScalarGridSpec(
            num_scalar_prefetch=2,
            grid=(nn, n_tiles),
            in_specs=[
                pl.BlockSpec((MOE_TM, D_FF), lambda c, t, te, nu: (row(c, t, te, nu), 0)),
                pl.BlockSpec((1, D_FF, MOE_TN), lambda c, t, te, nu: (te[t], 0, c)),
                pl.BlockSpec((1, 1, MOE_TN), lambda c, t, te, nu: (te[t], 0, c)),
            ],
            out_specs=pl.BlockSpec((MOE_TM, MOE_TN // 2), lambda c, t, te, nu: (row(c, t, te, nu), c)),
            scratch_shapes=[pltpu.VMEM((D_FF, MOE_TN), BF16)],
        ),
        out_shape=jax.ShapeDtypeStruct((r, HALF_D), jnp.uint32),
        compiler_params=_cparams(("arbitrary", "arbitrary"), MOE_VMEM_MB),
        name="moe_down",
    )(tile_e, n_used, act, w2, b2)


def _combine_kernel(x_ref, y0_ref, y1_ref, y2_ref, y3_ref, tg_ref, g_ref, o_ref):
    tg = tg_ref[...]
    acc = x_ref[...]
    half = MOE_TN // 2
    for kk, y_ref in enumerate((y0_ref, y1_ref, y2_ref, y3_ref)):
        cols = []
        for c in range(D_MODEL // MOE_TN):
            cols += list(_unpack_pairs(y_ref[0, :, c * half:(c + 1) * half]))
        acc = acc + jnp.concatenate(cols, axis=1) * tg[:, kk:kk + 1]
    o_ref[...] = _rms(acc, g_ref[...])


def _combine(x2, yg, tg, g, row0):
    n = x2.shape[0]
    tm = min(TOK_TILE, n)
    blk0 = row0 // tm
    yspec = [pl.BlockSpec((1, tm, HALF_D), functools.partial(lambda i, kk: (kk, blk0 + i, 0), kk=kk))
             for kk in range(TOP_K)]
    return pl.pallas_call(
        _combine_kernel,
        grid=(n // tm,),
        in_specs=[pl.BlockSpec((tm, D_MODEL), lambda i: (i, 0))] + yspec + [
            pl.BlockSpec((tm, LANES), lambda i: (i, 0)),
            pl.BlockSpec((1, D_MODEL), lambda i: (0, 0)),
        ],
        out_specs=pl.BlockSpec((tm, D_MODEL), lambda i: (i, 0)),
        out_shape=jax.ShapeDtypeStruct((n, D_MODEL), F32),
        compiler_params=_cparams(("parallel",), 48),
        name="combine",
    )(x2, yg, yg, yg, yg, tg, g)


def _group_tiles(bsz, t):
    tt = min(t, TOK_TILE)
    bb = max(1, min(bsz, TOK_TILE // tt))
    return bb, tt


def _layer_group(x, c0, n0, m0, conv_state, mem_k, mem_v, cnt0, hn_all, row0, p):
    bsz, t, _ = x.shape
    n = bsz * t
    L = min(MLSTM_CHUNK, t)
    state_pad = jnp.pad(conv_state, ((0, 0), (CONV_PAD - CONV_STATE, 0), (0, 0)))
    conv_args = (state_pad, p["conv_w"], p["conv_b"], p["ln_g"], p["ln_b"])
    if t % PROJ_TM == 0:
        qkvo, gates, cv, new_conv = _proj_in_conv(x.reshape(n, D_MODEL), p["g_mix"], p["w_main"], p["wg"], p["bg"],
                                                  L, *conv_args, seq_len=t)
    else:
        qkvo, u, gates = _proj_in(x.reshape(n, D_MODEL), p["g_mix"], p["w_main"], p["wg"], p["bg"], L)
        ctt = min(CONV_TT, t)
        cbb = max(1, min(bsz, CONV_TT // ctt))
        cv, new_conv = _conv(u.reshape(bsz, t, CONV_CH), *conv_args, cbb, ctt)

    mbb = min(bsz, MLSTM_BB)
    m0b = jnp.broadcast_to(m0[:, :, None], (bsz, M_HEADS, LANES))
    hm, c1, n1, m1 = _mlstm(qkvo.reshape(bsz, t, -1), gates, c0, n0, m0b, p["mh_g"], mbb, L)

    x1 = _proj_out(x.reshape(n, D_MODEL), hm.reshape(n, M_WIDTH), cv.reshape(n, CONV_CH), p["wo_a"], p["wo_b"])

    abb, att = _group_tiles(bsz, t)
    x2, hn_all, te, tg, tr, cnt = _attn_router(x1.reshape(bsz, t, D_MODEL), mem_k, mem_v, cnt0, hn_all, row0,
                                               p, abb, att)
    return x2.reshape(n, D_MODEL), hn_all, te, tg, tr, cnt, (c1, n1, m1[:, :, 0], new_conv)


def kernel(x_prompt, x_sample, mem_prompt, state_mlstm_C, state_mlstm_n, state_mlstm_m, state_conv, cache_mem_k, cache_mem_v, norm_mix_g, w_in, b_gates, mh_norm_g, conv_w, conv_b, conv_ln_g, conv_ln_b, w_out, norm_ca_g, norm_mem_g, ca_wq, ca_wk, ca_wv, ca_wo, norm_moe_g, router_w, router_b, moe_w1, moe_b1, moe_w2, moe_b2, final_norm_g):
    assert w_in.shape[0] == 1, "single layer"
    bp, tp, _ = x_prompt.shape
    bs, ts, _ = x_sample.shape
    n_p, n_s = bp * tp, bs * ts
    n_all = n_p + n_s

    wi = w_in[0]
    o0 = 0
    parts = []
    for width in (QK_W, QK_W, M_WIDTH, M_WIDTH, M_HEADS, M_HEADS, CONV_CH, CONV_CH):
        parts.append(wi[:, o0:o0 + width])
        o0 += width
    w_q, w_k, w_v, w_o, w_gi, w_gf, w_ua, w_ug = parts
    col_tiles = []
    w_qkvo = wi[:, :2 * QK_W + 2 * M_WIDTH]
    for c in range(PROJ_STEPS):
        col_tiles += [w_qkvo[:, c * QKVO_TN:(c + 1) * QKVO_TN], w_ua[:, c * GLU_TN:(c + 1) * GLU_TN],
                      w_ug[:, c * GLU_TN:(c + 1) * GLU_TN]]
    p = {
        "g_mix": norm_mix_g[0][None, :],
        "w_main": jnp.concatenate(col_tiles, axis=1).astype(BF16),
        "wg": jnp.concatenate([w_gi, w_gf], axis=1).T.astype(BF16),
        "bg": b_gates[0][:, None],
        "mh_g": mh_norm_g[0],
        "conv_w": conv_w[0],
        "conv_b": conv_b[0][None, :],
        "ln_g": conv_ln_g[0][None, :],
        "ln_b": conv_ln_b[0][None, :],
        "wo_a": w_out[0][:M_WIDTH].astype(BF16),
        "wo_b": w_out[0][M_WIDTH:].astype(BF16),
        "g_ca": norm_ca_g[0][None, :],
        "wq": ca_wq[0].astype(BF16),
        "wo": ca_wo[0].astype(BF16),
        "g_moe": norm_moe_g[0][None, :],
        "router_w": jnp.pad(router_w[0], ((0, 0), (0, LANES - N_EXPERTS))).astype(BF16),
        "router_b": jnp.concatenate([router_b[0], jnp.full((LANES - N_EXPERTS,), NEG_BIG, F32)])[None, :],
    }

    wkv = jnp.concatenate([ca_wk[0], ca_wv[0]], axis=1).astype(BF16)
    mk, mv = _mem_kv(mem_prompt.reshape(bp * N_MEM, D_MODEL), norm_mem_g[0][None, :], wkv)
    mk = mk.reshape(bp, N_MEM, CA_W)
    mv = mv.reshape(bp, N_MEM, CA_W)
    hn_all = jnp.zeros((n_all, HALF_D), jnp.uint32)
    cnt0 = jnp.zeros((1, LANES), F32)
    zc = jnp.zeros((bp, M_HEADS, M_DK, M_DV), F32)
    zn = jnp.zeros((bp, M_HEADS, M_DK), F32)
    zm = jnp.zeros((bp, M_HEADS), F32)
    zconv = jnp.zeros((bp, CONV_STATE, CONV_CH), F32)
    x2p, hn_all, te_p, tg_p, tr_p, cnt1, st_p = _layer_group(
        x_prompt, zc, zn, zm, zconv, mk.astype(BF16), mv.astype(BF16), cnt0, hn_all, 0, p)

    x2s, hn_all, te_s, tg_s, tr_s, cnt2, st_s = _layer_group(
        x_sample, state_mlstm_C[0], state_mlstm_n[0], state_mlstm_m[0], state_conv[0],
        cache_mem_k[0].reshape(bs, N_MEM, CA_W).astype(BF16), cache_mem_v[0].reshape(bs, N_MEM, CA_W).astype(BF16),
        cnt1, hn_all, n_p, p)

    te = jnp.concatenate([te_p[:, :TOP_K], te_s[:, :TOP_K]], axis=0)
    tr = jnp.concatenate([tr_p[:, :TOP_K], tr_s[:, :TOP_K]], axis=0)
    counts = cnt2[0, :N_EXPERTS].astype(jnp.int32)
    tiles_per_e = (counts + MOE_TM - 1) // MOE_TM
    tile_end = jnp.cumsum(tiles_per_e)
    row_start = (tile_end - tiles_per_e) * MOE_TM
    n_tiles = -(-(n_all * TOP_K) // MOE_TM) + N_EXPERTS
    n_used = tile_end[-1:]
    tile_ids = jnp.minimum(jnp.arange(n_tiles, dtype=jnp.int32), n_used[0] - 1)
    tile_e = jnp.minimum(jnp.sum(tile_ids[:, None] >= tile_end[None, :], axis=1), N_EXPERTS - 1).astype(jnp.int32)
    tile_first = jnp.arange(n_tiles, dtype=jnp.int32) - (tile_end - tiles_per_e)[tile_e]
    tile_valid = jnp.clip(counts[tile_e] - tile_first * MOE_TM, 0, MOE_TM)
    tile_valid = jnp.where(jnp.arange(n_tiles) < n_used[0], tile_valid, 0).astype(jnp.int32)
    moe_meta = jnp.concatenate([n_used.astype(jnp.int32), tile_valid])
    dest = row_start[te] + tr
    dest3 = dest.reshape(n_all // SC_WIN, SC_WIN, TOP_K).transpose(0, 2, 1)
    x_sorted = _sc_scatter_rows(hn_all, dest3, n_tiles * MOE_TM)

    y_sorted = _moe(tile_e, moe_meta, x_sorted, moe_w1[0], moe_b1[0][:, None, :], moe_w2[0],
                    moe_b2[0][:, None, :])

    yg = y_sorted.at[dest.T.reshape(-1)].get(mode="promise_in_bounds").reshape(TOP_K, n_all, HALF_D)
    fg = final_norm_g[None, :]
    y_prompt = _combine(x2p, yg, tg_p, fg, 0).reshape(bp, tp, D_MODEL)
    y_sample = _combine(x2s, yg, tg_s, fg, n_p).reshape(bs, ts, D_MODEL)

    c1, n1, m1, conv1 = st_p
    c2, n2, m2, conv2 = st_s
    mk4 = mk.reshape(1, bp, N_MEM, CA_HEADS, CA_DH)
    mv4 = mv.reshape(1, bp, N_MEM, CA_HEADS, CA_DH)
    return (y_prompt, y_sample, c1[None], n1[None], m1[None], conv1[None], mk4, mv4,
            c2[None], n2[None], m2[None], conv2[None])
```

```python
import functools

import jax
import jax.numpy as jnp
from jax import lax
from jax.experimental import pallas as pl
from jax.experimental.pallas import tpu as pltpu
from jax.experimental.pallas import tpu_sc as plsc

F32 = jnp.float32
BF16 = jnp.bfloat16

D_MODEL = 2048
HALF_D = D_MODEL // 2
M_HEADS = 4
M_DV = 256
M_DK = 128
QK_W = M_HEADS * M_DK
M_WIDTH = M_HEADS * M_DV
CONV_CH = 1024
CONV_WIDTH = 31
CONV_STATE = CONV_WIDTH - 1
CONV_PAD = 32
N_MEM = 256
CA_HEADS = 4
CA_DH = 128
CA_W = CA_HEADS * CA_DH
N_EXPERTS = 32
TOP_K = 4
D_FF = 2048
SWIGLU_ALPHA = 1.702
SWIGLU_LIMIT = 7.0
EPS = 1e-6
LANES = 128
SUBLANES = 8
NEG_BIG = -1e30

PROJ_TM = 1024
PROJ_STEPS = 4
QKVO_TN = (2 * QK_W + 2 * M_WIDTH) // PROJ_STEPS
GLU_TN = CONV_CH // PROJ_STEPS
PROJ_TN = QKVO_TN + 2 * GLU_TN
CONV_SUB = PROJ_TM // PROJ_STEPS
TOK_TILE = 512
MOE_TM = 512
MOE_TF = 1024
MOE_TN = 2048
MOE_VMEM_MB = 60
MLSTM_CHUNK = 256
MLSTM_BB = 2
CONV_TT = 256
CONV_RB = 32
CONV_LC = 512
SC_WIN = 32


def _cparams(sem, vmem_mb):
    return pltpu.CompilerParams(dimension_semantics=sem, vmem_limit_bytes=vmem_mb << 20)


def _log_sigmoid(x):
    return jnp.minimum(x, 0.0) - jnp.log1p(jnp.exp(-jnp.abs(x)))


def _sigmoid(x):
    return 1.0 / (1.0 + jnp.exp(-x))


def _pack_pairs(lo, hi):
    return lax.bitcast_convert_type(pltpu.pack_elementwise([lo, hi], packed_dtype=BF16), jnp.uint32)


def _unpack_pairs(w):
    return tuple(pltpu.unpack_elementwise(w, index=i, packed_dtype=BF16, unpacked_dtype=F32) for i in range(2))


def _rms(x, g):
    ms = jnp.mean(x * x, axis=-1, keepdims=True)
    return x * lax.rsqrt(ms + EPS) * g


def _proj_prologue(x_ref, g_ref, wg_ref, bg_ref, h_ref, gates_ref, chunk):
    tm = x_ref.shape[0]
    h = _rms(x_ref[...], g_ref[...]).astype(BF16)
    h_ref[...] = h
    gt = lax.dot_general(wg_ref[...], h, (((1,), (1,)), ((), ())), preferred_element_type=F32)
    gt = gt + bg_ref[...]
    row = lax.broadcasted_iota(jnp.int32, gt.shape, 0)
    gt = jnp.where(row < M_HEADS, gt, _log_sigmoid(gt))
    for c in range(tm // chunk):
        gates_ref[c] = gt[:, c * chunk:(c + 1) * chunk]


def _proj_step(h_ref, w_ref):
    z = jnp.dot(h_ref[...], w_ref[...], preferred_element_type=F32)
    glu = z[:, QKVO_TN:QKVO_TN + GLU_TN] * _sigmoid(z[:, QKVO_TN + GLU_TN:])
    return z[:, :QKVO_TN].astype(BF16), glu.astype(BF16)


def _proj_in_kernel(x_ref, g_ref, w_ref, wg_ref, bg_ref, qkvo_ref, u_ref, gates_ref, h_ref, *, chunk):
    @pl.when(pl.program_id(1) == 0)
    def _():
        _proj_prologue(x_ref, g_ref, wg_ref, bg_ref, h_ref, gates_ref, chunk)

    qkvo_ref[...], u_ref[...] = _proj_step(h_ref, w_ref)


def _proj_in(x, g, w_main, wg, bg, chunk):
    n = x.shape[0]
    tm = min(PROJ_TM, n)
    return pl.pallas_call(
        functools.partial(_proj_in_kernel, chunk=chunk),
        grid=(n // tm, PROJ_STEPS),
        in_specs=[
            pl.BlockSpec((tm, D_MODEL), lambda i, j: (i, 0)),
            pl.BlockSpec((1, D_MODEL), lambda i, j: (0, 0)),
            pl.BlockSpec((D_MODEL, PROJ_TN), lambda i, j: (0, j)),
            pl.BlockSpec((2 * M_HEADS, D_MODEL), lambda i, j: (0, 0)),
            pl.BlockSpec((2 * M_HEADS, 1), lambda i, j: (0, 0)),
        ],
        out_specs=[
            pl.BlockSpec((tm, QKVO_TN), lambda i, j: (i, j)),
            pl.BlockSpec((tm, GLU_TN), lambda i, j: (i, j)),
            pl.BlockSpec((tm // chunk, 2 * M_HEADS, chunk), lambda i, j: (i, 0, 0)),
        ],
        out_shape=[
            jax.ShapeDtypeStruct((n, 2 * QK_W + 2 * M_WIDTH), BF16),
            jax.ShapeDtypeStruct((n, CONV_CH), BF16),
            jax.ShapeDtypeStruct((n // chunk, 2 * M_HEADS, chunk), F32),
        ],
        scratch_shapes=[pltpu.VMEM((tm, D_MODEL), BF16)],
        compiler_params=_cparams(("parallel", "arbitrary"), 48),
        name="proj_in",
    )(x, g, w_main, wg, bg)


def _proj_in_conv_kernel(x_ref, g_ref, w_ref, wg_ref, bg_ref, st_ref, cw_ref, cb_ref, lg_ref, lb_ref,
                         qkvo_ref, gates_ref, cv_ref, ns_ref, h_ref, u_s, hist_s, e_ref, er_ref, c_ref,
                         *, chunk, tiles_per_seq):
    i = pl.program_id(0)
    j = pl.program_id(1)
    p = i - 1
    sub = CONV_SUB
    ext = sub + CONV_PAD

    @pl.when((i == 0) & (j == 0))
    def _():
        u_s[...] = jnp.zeros(u_s.shape, BF16)
        hist_s[...] = jnp.zeros(hist_s.shape, F32)
        e_ref[:, ext:, :] = jnp.zeros((1, SUBLANES, CONV_CH), F32)

    @pl.when(j == 0)
    def _():
        _proj_prologue(x_ref, g_ref, wg_ref, bg_ref, h_ref, gates_ref, chunk)

    @pl.when((j == 0) & (p >= 0) & (p % tiles_per_seq == 0))
    def _():
        hist_s[...] = st_ref[0]

    r0 = pl.multiple_of(j * sub, sub)
    e_ref[0, 0:CONV_PAD, :] = hist_s[...]
    for c in range(PROJ_STEPS):
        e_ref[0, CONV_PAD:ext, c * GLU_TN:(c + 1) * GLU_TN] = u_s[(i + 1) % 2, c, pl.ds(r0, sub), :].astype(F32)
    prepare, row_block = _conv_stages(e_ref, er_ref, c_ref, cw_ref, cb_ref, lg_ref, lb_ref, cv_ref, 0, sub)
    prepare()
    for r0 in range(0, sub, CONV_RB):
        row_block(r0)
    hist_s[...] = e_ref[0, sub:ext, :]

    qkvo_ref[...], u_s[i % 2, j] = _proj_step(h_ref, w_ref)

    @pl.when((j == PROJ_STEPS - 1) & (p >= 0) & (p % tiles_per_seq == tiles_per_seq - 1))
    def _():
        ns_ref[0] = e_ref[0, ext - CONV_STATE:ext, :]


def _proj_in_conv(x, g, w_main, wg, bg, chunk, state_pad, conv_w, conv_b, ln_g, ln_b, seq_len):
    n = x.shape[0]
    tm = PROJ_TM
    n_tiles = n // tm
    tiles_per_seq = seq_len // tm
    bsz = n // seq_len
    assert tm == PROJ_STEPS * CONV_SUB and seq_len % tm == 0
    last = n_tiles - 1

    def tile(i, j):
        return jnp.minimum(i, last)

    def seq(i, j):
        return jnp.clip((i - 1) // tiles_per_seq, 0, bsz - 1)

    const2 = lambda i, j: (0, 0)
    return pl.pallas_call(
        functools.partial(_proj_in_conv_kernel, chunk=chunk, tiles_per_seq=tiles_per_seq),
        grid=(n_tiles + 1, PROJ_STEPS),
        in_specs=[
            pl.BlockSpec((tm, D_MODEL), lambda i, j: (tile(i, j), 0)),
            pl.BlockSpec((1, D_MODEL), const2),
            pl.BlockSpec((D_MODEL, PROJ_TN), lambda i, j: (0, j)),
            pl.BlockSpec((2 * M_HEADS, D_MODEL), const2),
            pl.BlockSpec((2 * M_HEADS, 1), const2),
            pl.BlockSpec((1, CONV_PAD, CONV_CH), lambda i, j: (seq(i, j), 0, 0)),
            pl.BlockSpec((CONV_WIDTH, CONV_CH), const2),
            pl.BlockSpec((1, CONV_CH), const2),
            pl.BlockSpec((1, CONV_CH), const2),
            pl.BlockSpec((1, CONV_CH), const2),
        ],
        out_specs=[
            pl.BlockSpec((tm, QKVO_TN), lambda i, j: (tile(i, j), jnp.where(i <= last, j, PROJ_STEPS - 1))),
            pl.BlockSpec((tm // chunk, 2 * M_HEADS, chunk), lambda i, j: (tile(i, j), 0, 0)),
            pl.BlockSpec((CONV_SUB, CONV_CH), lambda i, j: (jnp.maximum((i - 1) * PROJ_STEPS + j, 0), 0)),
            pl.BlockSpec((1, CONV_STATE, CONV_CH), lambda i, j: (seq(i, j), 0, 0)),
        ],
        out_shape=[
            jax.ShapeDtypeStruct((n, 2 * QK_W + 2 * M_WIDTH), BF16),
            jax.ShapeDtypeStruct((n // chunk, 2 * M_HEADS, chunk), F32),
            jax.ShapeDtypeStruct((n, CONV_CH), BF16),
            jax.ShapeDtypeStruct((bsz, CONV_STATE, CONV_CH), F32),
        ],
        scratch_shapes=[
            pltpu.VMEM((tm, D_MODEL), BF16),
            pltpu.VMEM((2, PROJ_STEPS, tm, GLU_TN), BF16),
            pltpu.VMEM((CONV_PAD, CONV_CH), F32),
            pltpu.VMEM((1, CONV_SUB + CONV_PAD + SUBLANES, CONV_CH), F32),
            pltpu.VMEM((SUBLANES - 1, CONV_SUB + CONV_PAD, CONV_CH), F32),
            pltpu.VMEM((CONV_SUB, CONV_CH), F32),
        ],
        compiler_params=_cparams(("arbitrary", "arbitrary"), 56),
        name="proj_in_conv",
    )(x, g, w_main, wg, bg, state_pad, conv_w, conv_b, ln_g, ln_b)


def _conv_stages(e_ref, er_ref, c_ref, w_ref, cb_ref, lg_ref, lb_ref, out_ref, b, rows):
    ext = rows + CONV_PAD
    base = CONV_PAD - CONV_STATE

    def prepare():
        for r in range(1, SUBLANES):
            er_ref[r - 1] = e_ref[b, r:r + ext, :]

    def row_block(r0):
        for lc in range(CONV_CH // CONV_LC):
            ls = slice(lc * CONV_LC, (lc + 1) * CONV_LC)
            acc = jnp.zeros((CONV_RB, CONV_LC), F32) + cb_ref[:, ls]
            for j in range(CONV_WIDTH):
                off = j + base
                a, r = (off // SUBLANES) * SUBLANES, off % SUBLANES
                if r == 0:
                    tap = e_ref[b, pl.ds(r0 + a, CONV_RB), ls]
                else:
                    tap = er_ref[r - 1, pl.ds(r0 + a, CONV_RB), ls]
                acc = acc + w_ref[j:j + 1, ls] * tap
            c_ref[pl.ds(r0, CONV_RB), ls] = acc
        c = c_ref[pl.ds(r0, CONV_RB), :]
        mu = jnp.mean(c, axis=-1, keepdims=True)
        cc = c - mu
        var = jnp.mean(cc * cc, axis=-1, keepdims=True)
        cn = cc * lax.rsqrt(var + EPS) * lg_ref[...] + lb_ref[...]
        out_ref[pl.ds(r0, CONV_RB), :] = (cn * _sigmoid(cn)).astype(BF16)

    return prepare, row_block


def _conv_rows(e_ref, er_ref, c_ref, w_ref, cb_ref, lg_ref, lb_ref, out_ref, b, rows):
    prepare, row_block = _conv_stages(e_ref, er_ref, c_ref, w_ref, cb_ref, lg_ref, lb_ref, out_ref, b, rows)
    prepare()

    def body(rb, carry):
        row_block(pl.multiple_of(rb * CONV_RB, CONV_RB))
        return carry

    lax.fori_loop(0, rows // CONV_RB, body, 0)


def _conv_kernel(u_ref, st_ref, w_ref, cb_ref, lg_ref, lb_ref, cv_ref, ns_ref, e_ref, er_ref, c_ref, *, bb, tt):
    t = pl.program_id(1)
    nt = pl.num_programs(1)
    ext = tt + CONV_PAD

    @pl.when(t == 0)
    def _():
        e_ref[:, 0:CONV_PAD, :] = st_ref[...]

    @pl.when(t > 0)
    def _():
        e_ref[:, 0:CONV_PAD, :] = e_ref[:, tt:ext, :]

    e_ref[:, CONV_PAD:ext, :] = u_ref[...].astype(F32)
    e_ref[:, ext:, :] = jnp.zeros((bb, SUBLANES, CONV_CH), F32)

    for b in range(bb):
        _conv_rows(e_ref, er_ref, c_ref, w_ref, cb_ref, lg_ref, lb_ref, cv_ref.at[b], b, tt)

    @pl.when(t == nt - 1)
    def _():
        ns_ref[...] = e_ref[:, ext - CONV_STATE:ext, :]


def _conv(u, state_pad, conv_w, conv_b, ln_g, ln_b, bb, tt):
    bsz, t, _ = u.shape
    return pl.pallas_call(
        functools.partial(_conv_kernel, bb=bb, tt=tt),
        grid=(bsz // bb, t // tt),
        in_specs=[
            pl.BlockSpec((bb, tt, CONV_CH), lambda b, s: (b, s, 0)),
            pl.BlockSpec((bb, CONV_PAD, CONV_CH), lambda b, s: (b, 0, 0)),
            pl.BlockSpec((CONV_WIDTH, CONV_CH), lambda b, s: (0, 0)),
            pl.BlockSpec((1, CONV_CH), lambda b, s: (0, 0)),
            pl.BlockSpec((1, CONV_CH), lambda b, s: (0, 0)),
            pl.BlockSpec((1, CONV_CH), lambda b, s: (0, 0)),
        ],
        out_specs=[
            pl.BlockSpec((bb, tt, CONV_CH), lambda b, s: (b, s, 0)),
            pl.BlockSpec((bb, CONV_STATE, CONV_CH), lambda b, s: (b, 0, 0)),
        ],
        out_shape=[
            jax.ShapeDtypeStruct((bsz, t, CONV_CH), BF16),
            jax.ShapeDtypeStruct((bsz, CONV_STATE, CONV_CH), F32),
        ],
        scratch_shapes=[
            pltpu.VMEM((bb, tt + CONV_PAD + SUBLANES, CONV_CH), F32),
            pltpu.VMEM((SUBLANES - 1, tt + CONV_PAD, CONV_CH), F32),
            pltpu.VMEM((tt, CONV_CH), F32),
        ],
        compiler_params=_cparams(("parallel", "arbitrary"), 40),
        name="conv",
    )(u, state_pad, conv_w, conv_b, ln_g, ln_b)


def _mlstm_kernel(q_ref, k_ref, v_ref, o_ref, gt_ref, c0_ref, n0_ref, m0_ref, ng_ref,
                  hm_ref, cn_ref, nn_ref, mn_ref, c_s, n_s, m_s, *, bb, L):
    ci = pl.program_id(1)
    nc = pl.num_programs(1)

    @pl.when(ci == 0)
    def _():
        c_s[...] = c0_ref[...]
        n_s[...] = n0_ref[...]
        m_s[...] = m0_ref[...]

    rt = lax.broadcasted_iota(jnp.int32, (L, L), 0)
    cs = lax.broadcasted_iota(jnp.int32, (L, L), 1)
    causal = cs <= rt
    lower = jnp.where(causal, 1.0, 0.0).astype(BF16)
    upper = jnp.where(rt <= cs, 1.0, 0.0).astype(BF16)
    nt_dims = (((1,), (1,)), ((), ()))
    scale = M_DK ** -0.5

    for b in range(bb):
        rows = gt_ref[b]
        cum_rows = jnp.zeros((2 * M_HEADS, L), F32)
        cum_cols = jnp.zeros((L, 2 * M_HEADS), F32)
        rest = rows
        for _ in range(3):
            piece = rest.astype(BF16)
            rest = rest - piece.astype(F32)
            cum_rows = cum_rows + jnp.dot(piece, upper, preferred_element_type=F32)
            cum_cols = cum_cols + lax.dot_general(lower, piece, nt_dims, preferred_element_type=F32)
        heads = range(M_HEADS)
        ig_row = [rows[h:h + 1, :] for h in heads]
        b_row = [cum_rows[M_HEADS + h:M_HEADS + h + 1, :] for h in heads]
        b_col = [cum_cols[:, M_HEADS + h:M_HEADS + h + 1] for h in heads]
        m0 = [m_s[b, h:h + 1, 0:1] for h in heads]
        q = [q_ref[b, :, h * M_DK:(h + 1) * M_DK] for h in heads]
        k = [k_ref[b, :, h * M_DK:(h + 1) * M_DK] for h in heads]
        v = [v_ref[b, :, h * M_DV:(h + 1) * M_DV] for h in heads]
        c0 = [c_s[b, h] for h in heads]
        n0 = [n_s[b, h:h + 1, :] for h in heads]

        g_col = [b_col[h] + m0[h] for h in heads]
        dmat = [jnp.where(causal, b_col[h] - b_row[h] + ig_row[h], -jnp.inf) for h in heads]
        m_col = [jnp.maximum(g_col[h], jnp.max(dmat[h], axis=1, keepdims=True)) for h in heads]
        w_state = [jnp.exp(g_col[h] - m_col[h]) * scale for h in heads]
        decay = [jnp.exp(dmat[h] - m_col[h]) * scale for h in heads]
        qk = [lax.dot_general(q[h], k[h], nt_dims, preferred_element_type=F32) for h in heads]
        s = [qk[h] * decay[h] for h in heads]
        qc = [jnp.dot(q[h], c0[h].astype(BF16), preferred_element_type=F32) for h in heads]
        num = [jnp.dot(s[h].astype(BF16), v[h], preferred_element_type=F32) + w_state[h] * qc[h] for h in heads]
        qn = [jnp.sum(q[h].astype(F32) * n0[h], axis=1, keepdims=True) for h in heads]
        den = [jnp.sum(s[h], axis=1, keepdims=True) + w_state[h] * qn[h] for h in heads]
        hh = [num[h] / jnp.maximum(jnp.abs(den[h]), jnp.exp(-m_col[h])) for h in heads]
        for h in heads:
            hn = hh[h] * lax.rsqrt(jnp.mean(hh[h] * hh[h], axis=-1, keepdims=True) + EPS) * ng_ref[h:h + 1, :]
            og = _sigmoid(o_ref[b, :, h * M_DV:(h + 1) * M_DV].astype(F32))
            hm_ref[b, :, h * M_DV:(h + 1) * M_DV] = (og * hn).astype(BF16)

        b_last = [b_row[h][:, L - 1:L] for h in heads]
        g_last = [b_last[h] + m0[h] for h in heads]
        wk_row = [b_last[h] - b_row[h] + ig_row[h] for h in heads]
        m_new = [jnp.maximum(g_last[h], jnp.max(wk_row[h], axis=1, keepdims=True)) for h in heads]
        a0 = [jnp.exp(g_last[h] - m_new[h]) for h in heads]
        ak_row = [jnp.exp(wk_row[h] - m_new[h]) for h in heads]
        kts = [(k[h].T.astype(F32) * ak_row[h]).astype(BF16) for h in heads]
        kv = [jnp.dot(kts[h], v[h], preferred_element_type=F32) for h in heads]
        kn = [jnp.dot(ak_row[h].astype(BF16), k[h], preferred_element_type=F32) for h in heads]
        for h in heads:
            c_s[b, h] = a0[h] * c0[h] + kv[h]
            n_s[b, h:h + 1, :] = a0[h] * n0[h] + kn[h]
            m_s[b, h:h + 1, :] = jnp.broadcast_to(m_new[h], (1, LANES))

    @pl.when(ci == nc - 1)
    def _():
        cn_ref[...] = c_s[...]
        nn_ref[...] = n_s[...]
        mn_ref[...] = m_s[...]


def _mlstm(qkvo, gates, c0, n0, m0, ng, bb, L):
    bsz, t, _ = qkvo.shape
    nc = t // L
    gates = gates.reshape(bsz, nc, 2 * M_HEADS, L)
    st_c = pl.BlockSpec((bb, M_HEADS, M_DK, M_DV), lambda b, c: (b, 0, 0, 0))
    st_n = pl.BlockSpec((bb, M_HEADS, M_DK), lambda b, c: (b, 0, 0))
    return pl.pallas_call(
        functools.partial(_mlstm_kernel, bb=bb, L=L),
        grid=(bsz // bb, nc),
        in_specs=[
            pl.BlockSpec((bb, L, QK_W), lambda b, c: (b, c, 0)),
            pl.BlockSpec((bb, L, QK_W), lambda b, c: (b, c, 1)),
            pl.BlockSpec((bb, L, M_WIDTH), lambda b, c: (b, c, 1)),
            pl.BlockSpec((bb, L, M_WIDTH), lambda b, c: (b, c, 2)),
            pl.BlockSpec((bb, None, 2 * M_HEADS, L), lambda b, c: (b, c, 0, 0)),
            st_c, st_n, st_n,
            pl.BlockSpec((M_HEADS, M_DV), lambda b, c: (0, 0)),
        ],
        out_specs=[
            pl.BlockSpec((bb, L, M_WIDTH), lambda b, c: (b, c, 0)),
            st_c, st_n, st_n,
        ],
        out_shape=[
            jax.ShapeDtypeStruct((bsz, t, M_WIDTH), BF16),
            jax.ShapeDtypeStruct((bsz, M_HEADS, M_DK, M_DV), F32),
            jax.ShapeDtypeStruct((bsz, M_HEADS, M_DK), F32),
            jax.ShapeDtypeStruct((bsz, M_HEADS, LANES), F32),
        ],
        scratch_shapes=[
            pltpu.VMEM((bb, M_HEADS, M_DK, M_DV), F32),
            pltpu.VMEM((bb, M_HEADS, M_DK), F32),
            pltpu.VMEM((bb, M_HEADS, LANES), F32),
        ],
        compiler_params=_cparams(("parallel", "arbitrary"), 40),
        name="mlstm",
    )(qkvo, qkvo, qkvo, qkvo, gates, c0, n0, m0, ng)


def _proj_out_kernel(x_ref, hm_ref, cv_ref, wa_ref, wb_ref, o_ref):
    acc = jnp.dot(hm_ref[...], wa_ref[...], preferred_element_type=F32)
    acc = acc + jnp.dot(cv_ref[...], wb_ref[...], preferred_element_type=F32)
    o_ref[...] = x_ref[...] + acc


def _proj_out(x, hm, cv, wa, wb):
    n = x.shape[0]
    tm = min(TOK_TILE, n)
    return pl.pallas_call(
        _proj_out_kernel,
        grid=(n // tm,),
        in_specs=[
            pl.BlockSpec((tm, D_MODEL), lambda i: (i, 0)),
            pl.BlockSpec((tm, M_WIDTH), lambda i: (i, 0)),
            pl.BlockSpec((tm, CONV_CH), lambda i: (i, 0)),
            pl.BlockSpec((M_WIDTH, D_MODEL), lambda i: (0, 0)),
            pl.BlockSpec((CONV_CH, D_MODEL), lambda i: (0, 0)),
        ],
        out_specs=pl.BlockSpec((tm, D_MODEL), lambda i: (i, 0)),
        out_shape=jax.ShapeDtypeStruct((n, D_MODEL), F32),
        compiler_params=_cparams(("parallel",), 48),
        name="proj_out",
    )(x, hm, cv, wa, wb)


def _mem_kv_kernel(m_ref, g_ref, w_ref, k_ref, v_ref):
    mn = _rms(m_ref[...], g_ref[...]).astype(BF16)
    z = jnp.dot(mn, w_ref[...], preferred_element_type=F32)
    k_ref[...] = z[:, :CA_W]
    v_ref[...] = z[:, CA_W:]


def _mem_kv(mem, g, wkv):
    n = mem.shape[0]
    return pl.pallas_call(
        _mem_kv_kernel,
        grid=(1,),
        in_specs=[
            pl.BlockSpec((n, D_MODEL), lambda i: (0, 0)),
            pl.BlockSpec((1, D_MODEL), lambda i: (0, 0)),
            pl.BlockSpec((D_MODEL, 2 * CA_W), lambda i: (0, 0)),
        ],
        out_specs=[pl.BlockSpec((n, CA_W), lambda i: (0, 0))] * 2,
        out_shape=[jax.ShapeDtypeStruct((n, CA_W), F32)] * 2,
        compiler_params=_cparams(("arbitrary",), 40),
        name="mem_kv",
    )(mem, g, wkv)


def _attn_router_kernel(x_ref, gca_ref, wq_ref, k_ref, v_ref, wo_ref, gmoe_ref, rw_ref, rb_ref, cnt0_ref,
                        hn_in_ref, x2_ref, hn_ref, te_ref, tg_ref, tr_ref, cnt_ref, o_s, cnt_s, *, bb, tt):
    del hn_in_ref
    rows = bb * tt

    @pl.when((pl.program_id(0) == 0) & (pl.program_id(1) == 0))
    def _():
        cnt_s[...] = cnt0_ref[...]

    x = x_ref[...].reshape(rows, D_MODEL)
    h = _rms(x, gca_ref[...]).astype(BF16)
    q = jnp.dot(h, wq_ref[...], preferred_element_type=F32) * (CA_DH ** -0.5)
    q = q.astype(BF16)
    units = [(b, slice(hd * CA_DH, (hd + 1) * CA_DH)) for b in range(bb) for hd in range(CA_HEADS)]
    nt_dims = (((1,), (1,)), ((), ()))
    s = [lax.dot_general(q[b * tt:(b + 1) * tt, cols], k_ref[b, :, cols], nt_dims, preferred_element_type=F32)
         for b, cols in units]
    e = [jnp.exp(si - jnp.max(si, axis=-1, keepdims=True)) for si in s]
    ov = [jnp.dot(ei.astype(BF16), v_ref[b, :, cols], preferred_element_type=F32) for ei, (b, cols) in zip(e, units)]
    for ei, ovi, (b, cols) in zip(e, ov, units):
        o_s[b * tt:(b + 1) * tt, cols] = (ovi / jnp.sum(ei, axis=-1, keepdims=True)).astype(BF16)
    x2 = x + jnp.dot(o_s[...], wo_ref[...], preferred_element_type=F32)
    x2_ref[...] = x2.reshape(bb, tt, D_MODEL)

    hn = _rms(x2, gmoe_ref[...])
    hn_bf = hn.astype(BF16)
    hn_ref[...] = _pack_pairs(hn[:, :HALF_D], hn[:, HALF_D:])
    logits = jnp.dot(hn_bf, rw_ref[...], preferred_element_type=F32) + rb_ref[...]
    lane = lax.broadcasted_iota(jnp.int32, (rows, LANES), 1)
    lane_f = lane.astype(F32)
    work = logits
    sel = []
    vals = []
    for _ in range(TOP_K):
        mx = jnp.max(work, axis=-1, keepdims=True)
        idx = jnp.min(jnp.where(work == mx, lane_f, float(LANES)), axis=-1, keepdims=True)
        hit = lane_f == idx
        sel.append((idx, hit))
        vals.append(mx)
        work = jnp.where(hit, -jnp.inf, work)
    ex = [jnp.exp(vv - vals[0]) for vv in vals]
    tot = ex[0] + ex[1] + ex[2] + ex[3]
    assigned = jnp.zeros((rows, LANES), F32)
    for _, hit in sel:
        assigned = assigned + hit.astype(F32)
    rt = lax.broadcasted_iota(jnp.int32, (rows, rows), 0)
    cs = lax.broadcasted_iota(jnp.int32, (rows, rows), 1)
    strict = (cs < rt).astype(BF16)
    before = jnp.dot(strict, assigned.astype(BF16), preferred_element_type=F32) + cnt_s[...]
    te = jnp.zeros((rows, LANES), F32)
    tg = jnp.zeros((rows, LANES), F32)
    tr = jnp.zeros((rows, LANES), F32)
    for kk, (idx, hit) in enumerate(sel):
        rank = jnp.sum(jnp.where(hit, before, 0.0), axis=-1, keepdims=True)
        te = jnp.where(lane == kk, idx, te)
        tg = jnp.where(lane == kk, ex[kk] / tot, tg)
        tr = jnp.where(lane == kk, rank, tr)
    te_ref[...] = te.astype(jnp.int32)
    tg_ref[...] = tg
    tr_ref[...] = tr.astype(jnp.int32)
    cnt_s[...] = cnt_s[...] + jnp.sum(assigned, axis=0, keepdims=True)
    cnt_ref[...] = cnt_s[...]


def _attn_router(x1, mem_k, mem_v, cnt0, hn_all, row0, p, bb, tt):
    bsz, t, _ = x1.shape
    rows = bb * tt
    n = bsz * t
    nb, nt = bsz // bb, t // tt
    blk0 = row0 // rows
    tok = lambda b, s: (b * nt + s, 0)
    const = lambda b, s: (0, 0)
    grid_spec = dict(
        grid=(nb, nt),
        in_specs=[
            pl.BlockSpec((bb, tt, D_MODEL), lambda b, s: (b, s, 0)),
            pl.BlockSpec((1, D_MODEL), const),
            pl.BlockSpec((D_MODEL, CA_W), const),
            pl.BlockSpec((bb, N_MEM, CA_W), lambda b, s: (b, 0, 0)),
            pl.BlockSpec((bb, N_MEM, CA_W), lambda b, s: (b, 0, 0)),
            pl.BlockSpec((CA_W, D_MODEL), const),
            pl.BlockSpec((1, D_MODEL), const),
            pl.BlockSpec((D_MODEL, LANES), const),
            pl.BlockSpec((1, LANES), const),
            pl.BlockSpec((1, LANES), const),
            pl.BlockSpec(memory_space=pl.ANY),
        ],
        out_specs=[
            pl.BlockSpec((bb, tt, D_MODEL), lambda b, s: (b, s, 0)),
            pl.BlockSpec((rows, HALF_D), lambda b, s: (blk0 + b * nt + s, 0)),
            pl.BlockSpec((rows, LANES), tok),
            pl.BlockSpec((rows, LANES), tok),
            pl.BlockSpec((rows, LANES), tok),
            pl.BlockSpec((1, LANES), const),
        ],
    )
    return pl.pallas_call(
        functools.partial(_attn_router_kernel, bb=bb, tt=tt),
        out_shape=[
            jax.ShapeDtypeStruct((bsz, t, D_MODEL), F32),
            jax.ShapeDtypeStruct(hn_all.shape, jnp.uint32),
            jax.ShapeDtypeStruct((n, LANES), jnp.int32),
            jax.ShapeDtypeStruct((n, LANES), F32),
            jax.ShapeDtypeStruct((n, LANES), jnp.int32),
            jax.ShapeDtypeStruct((1, LANES), F32),
        ],
        scratch_shapes=[pltpu.VMEM((rows, CA_W), BF16), pltpu.VMEM((1, LANES), F32)],
        input_output_aliases={10: 1},
        compiler_params=_cparams(("arbitrary", "arbitrary"), 48),
        name="attn_router",
        **grid_spec,
    )(x1, p["g_ca"], p["wq"], mem_k, mem_v, p["wo"], p["g_moe"], p["router_w"], p["router_b"], cnt0, hn_all)


def _sc_scatter_rows(rows, dest3, n_out):
    n, d = rows.shape
    nwin = dest3.shape[0]
    mesh = plsc.VectorSubcoreMesh(core_axis_name="c", subcore_axis_name="s")
    workers = mesh.num_cores * mesh.num_subcores
    assert nwin * SC_WIN == n and nwin % workers == 0, (n, nwin, workers)
    per = nwin // workers

    @functools.partial(
        pl.kernel, out_type=jax.ShapeDtypeStruct((n_out, d), rows.dtype), mesh=mesh,
        scratch_types=[pltpu.VMEM((SC_WIN, d), rows.dtype), pltpu.VMEM((TOP_K, SC_WIN), jnp.int32),
                       pltpu.SemaphoreType.DMA])
    def scatter(x_hbm, i_hbm, o_hbm, buf, idx, sem):
        wid = lax.axis_index("s") * mesh.num_cores + lax.axis_index("c")

        @pl.loop(0, per)
        def _(j):
            w = wid * per + j
            pltpu.sync_copy(x_hbm.at[pl.ds(w * SC_WIN, SC_WIN)], buf)
            pltpu.sync_copy(i_hbm.at[w], idx)
            copies = [pltpu.async_copy(buf, o_hbm.at[idx.at[kk]], sem) for kk in range(TOP_K)]
            for cp in copies:
                cp.wait()

    return scatter(rows, dest3)


def _expert_changed(te_ref):
    t = pl.program_id(1)
    return (t == 0) | (te_ref[t] != te_ref[jnp.maximum(t - 1, 0)])


def _moe_up_kernel(te_ref, nu_ref, x_ref, w1g_ref, w1l_ref, b1g_ref, b1l_ref, act_ref, wg_s, wl_s):
    @pl.when(_expert_changed(te_ref))
    def _():
        wg_s[...] = w1g_ref[0].astype(BF16)
        wl_s[...] = w1l_ref[0].astype(BF16)

    def up(words):
        lo, hi = _unpack_pairs(words)
        x = jnp.concatenate([lo.astype(BF16), hi.astype(BF16)], axis=1)
        glu = jnp.dot(x, wg_s[...], preferred_element_type=F32) + b1g_ref[0]
        lin = jnp.dot(x, wl_s[...], preferred_element_type=F32) + b1l_ref[0]
        glu = jnp.minimum(glu, SWIGLU_LIMIT)
        lin = jnp.clip(lin, -SWIGLU_LIMIT, SWIGLU_LIMIT)
        act_ref[...] = (glu * _sigmoid(SWIGLU_ALPHA * glu) * (lin + 1.0)).astype(BF16)

    t = pl.program_id(1)
    used = t < nu_ref[0]
    valid = nu_ref[1 + t]

    @pl.when(used & (valid >= MOE_TM))
    def _():
        up(x_ref[...])

    @pl.when(used & (valid < MOE_TM))
    def _():
        rowid = lax.broadcasted_iota(jnp.int32, (MOE_TM, HALF_D), 0)
        up(jnp.where(rowid < valid, x_ref[...], jnp.uint32(0)))


def _moe_down_kernel(te_ref, nu_ref, a_ref, w2_ref, b2_ref, y_ref, w2_s):
    @pl.when(_expert_changed(te_ref))
    def _():
        w2_s[...] = w2_ref[0].astype(BF16)

    @pl.when(pl.program_id(1) < nu_ref[0])
    def _():
        y = jnp.dot(a_ref[...], w2_s[...], preferred_element_type=F32) + b2_ref[0]
        y_ref[...] = _pack_pairs(y[:, :MOE_TN // 2], y[:, MOE_TN // 2:])


def _moe(tile_e, n_used, x_sorted, w1, b1, w2, b2):
    r = x_sorted.shape[0]
    n_tiles = r // MOE_TM
    nf = D_FF // MOE_TF
    nn = D_MODEL // MOE_TN

    def row(c, t, te, nu):
        return jnp.maximum(jnp.minimum(t, nu[0] - 1), 0)

    act = pl.pallas_call(
        _moe_up_kernel,
        grid_spec=pltpu.PrefetchScalarGridSpec(
            num_scalar_prefetch=2,
            grid=(nf, n_tiles),
            in_specs=[
                pl.BlockSpec((MOE_TM, HALF_D), lambda c, t, te, nu: (row(c, t, te, nu), 0)),
                pl.BlockSpec((1, D_MODEL, MOE_TF), lambda c, t, te, nu: (te[t], 0, c)),
                pl.BlockSpec((1, D_MODEL, MOE_TF), lambda c, t, te, nu: (te[t], 0, nf + c)),
                pl.BlockSpec((1, 1, MOE_TF), lambda c, t, te, nu: (te[t], 0, c)),
                pl.BlockSpec((1, 1, MOE_TF), lambda c, t, te, nu: (te[t], 0, nf + c)),
            ],
            out_specs=pl.BlockSpec((MOE_TM, MOE_TF), lambda c, t, te, nu: (row(c, t, te, nu), c)),
            scratch_shapes=[pltpu.VMEM((D_MODEL, MOE_TF), BF16), pltpu.VMEM((D_MODEL, MOE_TF), BF16)],
        ),
        out_shape=jax.ShapeDtypeStruct((r, D_FF), BF16),
        compiler_params=_cparams(("arbitrary", "arbitrary"), MOE_VMEM_MB),
        name="moe_up",
    )(tile_e, n_used, x_sorted, w1, w1, b1, b1)

    return pl.pallas_call(
        _moe_down_kernel,
        grid_spec=pltpu.PrefetchScalarGridSpec(
            num_scalar_prefetch=2,
            grid=(nn, n_tiles),
            in_specs=[
                pl.BlockSpec((MOE_TM, D_FF), lambda c, t, te, nu: (row(c, t, te, nu), 0)),
                pl.BlockSpec((1, D_FF, MOE_TN), lambda c, t, te, nu: (te[t], 0, c)),
                pl.BlockSpec((1, 1, MOE_TN), lambda c, t, te, nu: (te[t], 0, c)),
            ],
            out_specs=pl.BlockSpec((MOE_TM, MOE_TN // 2), lambda c, t, te, nu: (row(c, t, te, nu), c)),
            scratch_shapes=[pltpu.VMEM((D_FF, MOE_TN), BF16)],
        ),
        out_shape=jax.ShapeDtypeStruct((r, HALF_D), jnp.uint32),
        compiler_params=_cparams(("arbitrary", "arbitrary"), MOE_VMEM_MB),
        name="moe_down",
    )(tile_e, n_used, act, w2, b2)


def _combine_kernel(x_ref, y0_ref, y1_ref, y2_ref, y3_ref, tg_ref, g_ref, o_ref):
    tg = tg_ref[...]
    acc = x_ref[...]
    half = MOE_TN // 2
    for kk, y_ref in enumerate((y0_ref, y1_ref, y2_ref, y3_ref)):
        cols = []
        for c in range(D_MODEL // MOE_TN):
            cols += list(_unpack_pairs(y_ref[0, :, c * half:(c + 1) * half]))
        acc = acc + jnp.concatenate(cols, axis=1) * tg[:, kk:kk + 1]
    o_ref[...] = _rms(acc, g_ref[...])


def _combine(x2, yg, tg, g, row0):
    n = x2.shape[0]
    tm = min(TOK_TILE, n)
    blk0 = row0 // tm
    yspec = [pl.BlockSpec((1, tm, HALF_D), functools.partial(lambda i, kk: (kk, blk0 + i, 0), kk=kk))
             for kk in range(TOP_K)]
    return pl.pallas_call(
        _combine_kernel,
        grid=(n // tm,),
        in_specs=[pl.BlockSpec((tm, D_MODEL), lambda i: (i, 0))] + yspec + [
            pl.BlockSpec((tm, LANES), lambda i: (i, 0)),
            pl.BlockSpec((1, D_MODEL), lambda i: (0, 0)),
        ],
        out_specs=pl.BlockSpec((tm, D_MODEL), lambda i: (i, 0)),
        out_shape=jax.ShapeDtypeStruct((n, D_MODEL), F32),
        compiler_params=_cparams(("parallel",), 48),
        name="combine",
    )(x2, yg, yg, yg, yg, tg, g)


def _group_tiles(bsz, t):
    tt = min(t, TOK_TILE)
    bb = max(1, min(bsz, TOK_TILE // tt))
    return bb, tt


def _layer_group(x, c0, n0, m0, conv_state, mem_k, mem_v, cnt0, hn_all, row0, p):
    bsz, t, _ = x.shape
    n = bsz * t
    L = min(MLSTM_CHUNK, t)
    state_pad = jnp.pad(conv_state, ((0, 0), (CONV_PAD - CONV_STATE, 0), (0, 0)))
    conv_args = (state_pad, p["conv_w"], p["conv_b"], p["ln_g"], p["ln_b"])
    if t % PROJ_TM == 0:
        qkvo, gates, cv, new_conv = _proj_in_conv(x.reshape(n, D_MODEL), p["g_mix"], p["w_main"], p["wg"], p["bg"],
                                                  L, *conv_args, seq_len=t)
    else:
        qkvo, u, gates = _proj_in(x.reshape(n, D_MODEL), p["g_mix"], p["w_main"], p["wg"], p["bg"], L)
        ctt = min(CONV_TT, t)
        cbb = max(1, min(bsz, CONV_TT // ctt))
        cv, new_conv = _conv(u.reshape(bsz, t, CONV_CH), *conv_args, cbb, ctt)

    mbb = min(bsz, MLSTM_BB)
    m0b = jnp.broadcast_to(m0[:, :, None], (bsz, M_HEADS, LANES))
    hm, c1, n1, m1 = _mlstm(qkvo.reshape(bsz, t, -1), gates, c0, n0, m0b, p["mh_g"], mbb, L)

    x1 = _proj_out(x.reshape(n, D_MODEL), hm.reshape(n, M_WIDTH), cv.reshape(n, CONV_CH), p["wo_a"], p["wo_b"])

    abb, att = _group_tiles(bsz, t)
    x2, hn_all, te, tg, tr, cnt = _attn_router(x1.reshape(bsz, t, D_MODEL), mem_k, mem_v, cnt0, hn_all, row0,
                                               p, abb, att)
    return x2.reshape(n, D_MODEL), hn_all, te, tg, tr, cnt, (c1, n1, m1[:, :, 0], new_conv)


def kernel(x_prompt, x_sample, mem_prompt, state_mlstm_C, state_mlstm_n, state_mlstm_m, state_conv, cache_mem_k, cache_mem_v, norm_mix_g, w_in, b_gates, mh_norm_g, conv_w, conv_b, conv_ln_g, conv_ln_b, w_out, norm_ca_g, norm_mem_g, ca_wq, ca_wk, ca_wv, ca_wo, norm_moe_g, router_w, router_b, moe_w1, moe_b1, moe_w2, moe_b2, final_norm_g):
    assert w_in.shape[0] == 1, "single layer"
    bp, tp, _ = x_prompt.shape
    bs, ts, _ = x_sample.shape
    n_p, n_s = bp * tp, bs * ts
    n_all = n_p + n_s

    wi = w_in[0]
    o0 = 0
    parts = []
    for width in (QK_W, QK_W, M_WIDTH, M_WIDTH, M_HEADS, M_HEADS, CONV_CH, CONV_CH):
        parts.append(wi[:, o0:o0 + width])
        o0 += width
    w_q, w_k, w_v, w_o, w_gi, w_gf, w_ua, w_ug = parts
    col_tiles = []
    w_qkvo = wi[:, :2 * QK_W + 2 * M_WIDTH]
    for c in range(PROJ_STEPS):
        col_tiles += [w_qkvo[:, c * QKVO_TN:(c + 1) * QKVO_TN], w_ua[:, c * GLU_TN:(c + 1) * GLU_TN],
                      w_ug[:, c * GLU_TN:(c + 1) * GLU_TN]]
    p = {
        "g_mix": norm_mix_g[0][None, :],
        "w_main": jnp.concatenate(col_tiles, axis=1).astype(BF16),
        "wg": jnp.concatenate([w_gi, w_gf], axis=1).T.astype(BF16),
        "bg": b_gates[0][:, None],
        "mh_g": mh_norm_g[0],
        "conv_w": conv_w[0],
        "conv_b": conv_b[0][None, :],
        "ln_g": conv_ln_g[0][None, :],
        "ln_b": conv_ln_b[0][None, :],
        "wo_a": w_out[0][:M_WIDTH].astype(BF16),
        "wo_b": w_out[0][M_WIDTH:].astype(BF16),
        "g_ca": norm_ca_g[0][None, :],
        "wq": ca_wq[0].astype(BF16),
        "wo": ca_wo[0].astype(BF16),
        "g_moe": norm_moe_g[0][None, :],
        "router_w": jnp.pad(router_w[0], ((0, 0), (0, LANES - N_EXPERTS))).astype(BF16),
        "router_b": jnp.concatenate([router_b[0], jnp.full((LANES - N_EXPERTS,), NEG_BIG, F32)])[None, :],
    }

    wkv = jnp.concatenate([ca_wk[0], ca_wv[0]], axis=1).astype(BF16)
    mk, mv = _mem_kv(mem_prompt.reshape(bp * N_MEM, D_MODEL), norm_mem_g[0][None, :], wkv)
    mk = mk.reshape(bp, N_MEM, CA_W)
    mv = mv.reshape(bp, N_MEM, CA_W)
    hn_all = jnp.zeros((n_all, HALF_D), jnp.uint32)
    cnt0 = jnp.zeros((1, LANES), F32)
    zc = jnp.zeros((bp, M_HEADS, M_DK, M_DV), F32)
    zn = jnp.zeros((bp, M_HEADS, M_DK), F32)
    zm = jnp.zeros((bp, M_HEADS), F32)
    zconv = jnp.zeros((bp, CONV_STATE, CONV_CH), F32)
    x2p, hn_all, te_p, tg_p, tr_p, cnt1, st_p = _layer_group(
        x_prompt, zc, zn, zm, zconv, mk.astype(BF16), mv.astype(BF16), cnt0, hn_all, 0, p)

    x2s, hn_all, te_s, tg_s, tr_s, cnt2, st_s = _layer_group(
        x_sample, state_mlstm_C[0], state_mlstm_n[0], state_mlstm_m[0], state_conv[0],
        cache_mem_k[0].reshape(bs, N_MEM, CA_W).astype(BF16), cache_mem_v[0].reshape(bs, N_MEM, CA_W).astype(BF16),
        cnt1, hn_all, n_p, p)

    te = jnp.concatenate([te_p[:, :TOP_K], te_s[:, :TOP_K]], axis=0)
    tr = jnp.concatenate([tr_p[:, :TOP_K], tr_s[:, :TOP_K]], axis=0)
    counts = cnt2[0, :N_EXPERTS].astype(jnp.int32)
    tiles_per_e = (counts + MOE_TM - 1) // MOE_TM
    tile_end = jnp.cumsum(tiles_per_e)
    row_start = (tile_end - tiles_per_e) * MOE_TM
    n_tiles = -(-(n_all * TOP_K) // MOE_TM) + N_EXPERTS
    n_used = tile_end[-1:]
    tile_ids = jnp.minimum(jnp.arange(n_tiles, dtype=jnp.int32), n_used[0] - 1)
    tile_e = jnp.minimum(jnp.sum(tile_ids[:, None] >= tile_end[None, :], axis=1), N_EXPERTS - 1).astype(jnp.int32)
    tile_first = jnp.arange(n_tiles, dtype=jnp.int32) - (tile_end - tiles_per_e)[tile_e]
    tile_valid = jnp.clip(counts[tile_e] - tile_first * MOE_TM, 0, MOE_TM)
    tile_valid = jnp.where(jnp.arange(n_tiles) < n_used[0], tile_valid, 0).astype(jnp.int32)
    moe_meta = jnp.concatenate([n_used.astype(jnp.int32), tile_valid])
    dest = row_start[te] + tr
    dest3 = dest.reshape(n_all // SC_WIN, SC_WIN, TOP_K).transpose(0, 2, 1)
    x_sorted = _sc_scatter_rows(hn_all, dest3, n_tiles * MOE_TM)

    y_sorted = _moe(tile_e, moe_meta, x_sorted, moe_w1[0], moe_b1[0][:, None, :], moe_w2[0],
                    moe_b2[0][:, None, :])

    yg = y_sorted.at[dest.T.reshape(-1)].get(mode="promise_in_bounds").reshape(TOP_K, n_all, HALF_D)
    fg = final_norm_g[None, :]
    y_prompt = _combine(x2p, yg, tg_p, fg, 0).reshape(bp, tp, D_MODEL)
    y_sample = _combine(x2s, yg, tg_s, fg, n_p).reshape(bs, ts, D_MODEL)

    c1, n1, m1, conv1 = st_p
    c2, n2, m2, conv2 = st_s
    mk4 = mk.reshape(1, bp, N_MEM, CA_HEADS, CA_DH)
    mv4 = mv.reshape(1, bp, N_MEM, CA_HEADS, CA_DH)
    return (y_prompt, y_sample, c1[None], n1[None], m1[None], conv1[None], mk4, mv4,
            c2[None], n2[None], m2[None], conv2[None])
```

```python
import functools

import jax
import jax.numpy as jnp
from jax import lax
from jax.experimental import pallas as pl
from jax.experimental.pallas import tpu as pltpu
from jax.experimental.pallas import tpu_sc as plsc

F32 = jnp.float32
BF16 = jnp.bfloat16

D_MODEL = 2048
HALF_D = D_MODEL // 2
M_HEADS = 4
M_DV = 256
M_DK = 128
QK_W = M_HEADS * M_DK
M_WIDTH = M_HEADS * M_DV
CONV_CH = 1024
CONV_WIDTH = 31
CONV_STATE = CONV_WIDTH - 1
CONV_PAD = 32
N_MEM = 256
CA_HEADS = 4
CA_DH = 128
CA_W = CA_HEADS * CA_DH
N_EXPERTS = 32
TOP_K = 4
D_FF = 2048
SWIGLU_ALPHA = 1.702
SWIGLU_LIMIT = 7.0
EPS = 1e-6
LANES = 128
SUBLANES = 8
NEG_BIG = -1e30

PROJ_TM = 1024
PROJ_STEPS = 4
QKVO_TN = (2 * QK_W + 2 * M_WIDTH) // PROJ_STEPS
GLU_TN = CONV_CH // PROJ_STEPS
PROJ_TN = QKVO_TN + 2 * GLU_TN
CONV_SUB = PROJ_TM // PROJ_STEPS
TOK_TILE = 512
MOE_TM = 512
MOE_TF = 1024
MOE_TN = 2048
MOE_VMEM_MB = 60
MLSTM_CHUNK = 256
MLSTM_BB = 2
CONV_TT = 256
CONV_RB = 32
CONV_LC = 512
SC_WIN = 32
SC_GWIN = 8


def _cparams(sem, vmem_mb):
    return pltpu.CompilerParams(dimension_semantics=sem, vmem_limit_bytes=vmem_mb << 20)


def _log_sigmoid(x):
    return jnp.minimum(x, 0.0) - jnp.log1p(jnp.exp(-jnp.abs(x)))


def _sigmoid(x):
    return 1.0 / (1.0 + jnp.exp(-x))


def _pack_pairs(lo, hi):
    return lax.bitcast_convert_type(pltpu.pack_elementwise([lo, hi], packed_dtype=BF16), jnp.uint32)


def _unpack_pairs(w):
    return tuple(pltpu.unpack_elementwise(w, index=i, packed_dtype=BF16, unpacked_dtype=F32) for i in range(2))


def _rms(x, g):
    ms = jnp.mean(x * x, axis=-1, keepdims=True)
    return x * lax.rsqrt(ms + EPS) * g


def _proj_prologue(x_ref, g_ref, wg_ref, bg_ref, h_ref, gates_ref, chunk):
    tm = x_ref.shape[0]
    h = _rms(x_ref[...], g_ref[...]).astype(BF16)
    h_ref[...] = h
    gt = lax.dot_general(wg_ref[...], h, (((1,), (1,)), ((), ())), preferred_element_type=F32)
    gt = gt + bg_ref[...]
    row = lax.broadcasted_iota(jnp.int32, gt.shape, 0)
    gt = jnp.where(row < M_HEADS, gt, _log_sigmoid(gt))
    for c in range(tm // chunk):
        gates_ref[c] = gt[:, c * chunk:(c + 1) * chunk]


def _proj_step(h_ref, w_ref):
    z = jnp.dot(h_ref[...], w_ref[...], preferred_element_type=F32)
    glu = z[:, QKVO_TN:QKVO_TN + GLU_TN] * _sigmoid(z[:, QKVO_TN + GLU_TN:])
    return z[:, :QKVO_TN].astype(BF16), glu.astype(BF16)


def _proj_in_kernel(x_ref, g_ref, w_ref, wg_ref, bg_ref, qkvo_ref, u_ref, gates_ref, h_ref, *, chunk):
    @pl.when(pl.program_id(1) == 0)
    def _():
        _proj_prologue(x_ref, g_ref, wg_ref, bg_ref, h_ref, gates_ref, chunk)

    qkvo_ref[...], u_ref[...] = _proj_step(h_ref, w_ref)


def _proj_in(x, g, w_main, wg, bg, chunk):
    n = x.shape[0]
    tm = min(PROJ_TM, n)
    return pl.pallas_call(
        functools.partial(_proj_in_kernel, chunk=chunk),
        grid=(n // tm, PROJ_STEPS),
        in_specs=[
            pl.BlockSpec((tm, D_MODEL), lambda i, j: (i, 0)),
            pl.BlockSpec((1, D_MODEL), lambda i, j: (0, 0)),
            pl.BlockSpec((D_MODEL, PROJ_TN), lambda i, j: (0, j)),
            pl.BlockSpec((2 * M_HEADS, D_MODEL), lambda i, j: (0, 0)),
            pl.BlockSpec((2 * M_HEADS, 1), lambda i, j: (0, 0)),
        ],
        out_specs=[
            pl.BlockSpec((tm, QKVO_TN), lambda i, j: (i, j)),
            pl.BlockSpec((tm, GLU_TN), lambda i, j: (i, j)),
            pl.BlockSpec((tm // chunk, 2 * M_HEADS, chunk), lambda i, j: (i, 0, 0)),
        ],
        out_shape=[
            jax.ShapeDtypeStruct((n, 2 * QK_W + 2 * M_WIDTH), BF16),
            jax.ShapeDtypeStruct((n, CONV_CH), BF16),
            jax.ShapeDtypeStruct((n // chunk, 2 * M_HEADS, chunk), F32),
        ],
        scratch_shapes=[pltpu.VMEM((tm, D_MODEL), BF16)],
        compiler_params=_cparams(("parallel", "arbitrary"), 48),
        name="proj_in",
    )(x, g, w_main, wg, bg)


def _proj_in_conv_kernel(x_ref, g_ref, w_ref, wg_ref, bg_ref, st_ref, cw_ref, cb_ref, lg_ref, lb_ref,
                         qkvo_ref, gates_ref, cv_ref, ns_ref, h_ref, u_s, hist_s, e_ref, er_ref, c_ref,
                         *, chunk, tiles_per_seq):
    i = pl.program_id(0)
    j = pl.program_id(1)
    p = i - 1
    sub = CONV_SUB
    ext = sub + CONV_PAD

    @pl.when((i == 0) & (j == 0))
    def _():
        u_s[...] = jnp.zeros(u_s.shape, BF16)
        hist_s[...] = jnp.zeros(hist_s.shape, F32)
        e_ref[:, ext:, :] = jnp.zeros((1, SUBLANES, CONV_CH), F32)

    @pl.when(j == 0)
    def _():
        _proj_prologue(x_ref, g_ref, wg_ref, bg_ref, h_ref, gates_ref, chunk)

    @pl.when((j == 0) & (p >= 0) & (p % tiles_per_seq == 0))
    def _():
        hist_s[...] = st_ref[0]

    r0 = pl.multiple_of(j * sub, sub)
    e_ref[0, 0:CONV_PAD, :] = hist_s[...]
    for c in range(PROJ_STEPS):
        e_ref[0, CONV_PAD:ext, c * GLU_TN:(c + 1) * GLU_TN] = u_s[(i + 1) % 2, c, pl.ds(r0, sub), :].astype(F32)
    prepare, row_block = _conv_stages(e_ref, er_ref, c_ref, cw_ref, cb_ref, lg_ref, lb_ref, cv_ref, 0, sub)
    prepare()
    for r0 in range(0, sub, CONV_RB):
        row_block(r0)
    hist_s[...] = e_ref[0, sub:ext, :]

    qkvo_ref[...], u_s[i % 2, j] = _proj_step(h_ref, w_ref)

    @pl.when((j == PROJ_STEPS - 1) & (p >= 0) & (p % tiles_per_seq == tiles_per_seq - 1))
    def _():
        ns_ref[0] = e_ref[0, ext - CONV_STATE:ext, :]


def _proj_in_conv(x, g, w_main, wg, bg, chunk, state_pad, conv_w, conv_b, ln_g, ln_b, seq_len):
    n = x.shape[0]
    tm = PROJ_TM
    n_tiles = n // tm
    tiles_per_seq = seq_len // tm
    bsz = n // seq_len
    assert tm == PROJ_STEPS * CONV_SUB and seq_len % tm == 0
    last = n_tiles - 1

    def tile(i, j):
        return jnp.minimum(i, last)

    def seq(i, j):
        return jnp.clip((i - 1) // tiles_per_seq, 0, bsz - 1)

    const2 = lambda i, j: (0, 0)
    return pl.pallas_call(
        functools.partial(_proj_in_conv_kernel, chunk=chunk, tiles_per_seq=tiles_per_seq),
        grid=(n_tiles + 1, PROJ_STEPS),
        in_specs=[
            pl.BlockSpec((tm, D_MODEL), lambda i, j: (tile(i, j), 0)),
            pl.BlockSpec((1, D_MODEL), const2),
            pl.BlockSpec((D_MODEL, PROJ_TN), lambda i, j: (0, j)),
            pl.BlockSpec((2 * M_HEADS, D_MODEL), const2),
            pl.BlockSpec((2 * M_HEADS, 1), const2),
            pl.BlockSpec((1, CONV_PAD, CONV_CH), lambda i, j: (seq(i, j), 0, 0)),
            pl.BlockSpec((CONV_WIDTH, CONV_CH), const2),
            pl.BlockSpec((1, CONV_CH), const2),
            pl.BlockSpec((1, CONV_CH), const2),
            pl.BlockSpec((1, CONV_CH), const2),
        ],
        out_specs=[
            pl.BlockSpec((tm, QKVO_TN), lambda i, j: (tile(i, j), jnp.where(i <= last, j, PROJ_STEPS - 1))),
            pl.BlockSpec((tm // chunk, 2 * M_HEADS, chunk), lambda i, j: (tile(i, j), 0, 0)),
            pl.BlockSpec((CONV_SUB, CONV_CH), lambda i, j: (jnp.maximum((i - 1) * PROJ_STEPS + j, 0), 0)),
            pl.BlockSpec((1, CONV_STATE, CONV_CH), lambda i, j: (seq(i, j), 0, 0)),
        ],
        out_shape=[
            jax.ShapeDtypeStruct((n, 2 * QK_W + 2 * M_WIDTH), BF16),
            jax.ShapeDtypeStruct((n // chunk, 2 * M_HEADS, chunk), F32),
            jax.ShapeDtypeStruct((n, CONV_CH), BF16),
            jax.ShapeDtypeStruct((bsz, CONV_STATE, CONV_CH), F32),
        ],
        scratch_shapes=[
            pltpu.VMEM((tm, D_MODEL), BF16),
            pltpu.VMEM((2, PROJ_STEPS, tm, GLU_TN), BF16),
            pltpu.VMEM((CONV_PAD, CONV_CH), F32),
            pltpu.VMEM((1, CONV_SUB + CONV_PAD + SUBLANES, CONV_CH), F32),
            pltpu.VMEM((SUBLANES - 1, CONV_SUB + CONV_PAD, CONV_CH), F32),
            pltpu.VMEM((CONV_SUB, CONV_CH), F32),
        ],
        compiler_params=_cparams(("arbitrary", "arbitrary"), 56),
        name="proj_in_conv",
    )(x, g, w_main, wg, bg, state_pad, conv_w, conv_b, ln_g, ln_b)


def _conv_stages(e_ref, er_ref, c_ref, w_ref, cb_ref, lg_ref, lb_ref, out_ref, b, rows):
    ext = rows + CONV_PAD
    base = CONV_PAD - CONV_STATE

    def prepare():
        for r in range(1, SUBLANES):
            er_ref[r - 1] = e_ref[b, r:r + ext, :]

    def row_block(r0):
        for lc in range(CONV_CH // CONV_LC):
            ls = slice(lc * CONV_LC, (lc + 1) * CONV_LC)
            acc = jnp.zeros((CONV_RB, CONV_LC), F32) + cb_ref[:, ls]
            for j in range(CONV_WIDTH):
                off = j + base
                a, r = (off // SUBLANES) * SUBLANES, off % SUBLANES
                if r == 0:
                    tap = e_ref[b, pl.ds(r0 + a, CONV_RB), ls]
                else:
                    tap = er_ref[r - 1, pl.ds(r0 + a, CONV_RB), ls]
                acc = acc + w_ref[j:j + 1, ls] * tap
            c_ref[pl.ds(r0, CONV_RB), ls] = acc
        c = c_ref[pl.ds(r0, CONV_RB), :]
        mu = jnp.mean(c, axis=-1, keepdims=True)
        cc = c - mu
        var = jnp.mean(cc * cc, axis=-1, keepdims=True)
        cn = cc * lax.rsqrt(var + EPS) * lg_ref[...] + lb_ref[...]
        out_ref[pl.ds(r0, CONV_RB), :] = (cn * _sigmoid(cn)).astype(BF16)

    return prepare, row_block


def _conv_rows(e_ref, er_ref, c_ref, w_ref, cb_ref, lg_ref, lb_ref, out_ref, b, rows):
    prepare, row_block = _conv_stages(e_ref, er_ref, c_ref, w_ref, cb_ref, lg_ref, lb_ref, out_ref, b, rows)
    prepare()

    def body(rb, carry):
        row_block(pl.multiple_of(rb * CONV_RB, CONV_RB))
        return carry

    lax.fori_loop(0, rows // CONV_RB, body, 0)


def _conv_kernel(u_ref, st_ref, w_ref, cb_ref, lg_ref, lb_ref, cv_ref, ns_ref, e_ref, er_ref, c_ref, *, bb, tt):
    t = pl.program_id(1)
    nt = pl.num_programs(1)
    ext = tt + CONV_PAD

    @pl.when(t == 0)
    def _():
        e_ref[:, 0:CONV_PAD, :] = st_ref[...]

    @pl.when(t > 0)
    def _():
        e_ref[:, 0:CONV_PAD, :] = e_ref[:, tt:ext, :]

    e_ref[:, CONV_PAD:ext, :] = u_ref[...].astype(F32)
    e_ref[:, ext:, :] = jnp.zeros((bb, SUBLANES, CONV_CH), F32)

    for b in range(bb):
        _conv_rows(e_ref, er_ref, c_ref, w_ref, cb_ref, lg_ref, lb_ref, cv_ref.at[b], b, tt)

    @pl.when(t == nt - 1)
    def _():
        ns_ref[...] = e_ref[:, ext - CONV_STATE:ext, :]


def _conv(u, state_pad, conv_w, conv_b, ln_g, ln_b, bb, tt):
    bsz, t, _ = u.shape
    return pl.pallas_call(
        functools.partial(_conv_kernel, bb=bb, tt=tt),
        grid=(bsz // bb, t // tt),
        in_specs=[
            pl.BlockSpec((bb, tt, CONV_CH), lambda b, s: (b, s, 0)),
            pl.BlockSpec((bb, CONV_PAD, CONV_CH), lambda b, s: (b, 0, 0)),
            pl.BlockSpec((CONV_WIDTH, CONV_CH), lambda b, s: (0, 0)),
            pl.BlockSpec((1, CONV_CH), lambda b, s: (0, 0)),
            pl.BlockSpec((1, CONV_CH), lambda b, s: (0, 0)),
            pl.BlockSpec((1, CONV_CH), lambda b, s: (0, 0)),
        ],
        out_specs=[
            pl.BlockSpec((bb, tt, CONV_CH), lambda b, s: (b, s, 0)),
            pl.BlockSpec((bb, CONV_STATE, CONV_CH), lambda b, s: (b, 0, 0)),
        ],
        out_shape=[
            jax.ShapeDtypeStruct((bsz, t, CONV_CH), BF16),
            jax.ShapeDtypeStruct((bsz, CONV_STATE, CONV_CH), F32),
        ],
        scratch_shapes=[
            pltpu.VMEM((bb, tt + CONV_PAD + SUBLANES, CONV_CH), F32),
            pltpu.VMEM((SUBLANES - 1, tt + CONV_PAD, CONV_CH), F32),
            pltpu.VMEM((tt, CONV_CH), F32),
        ],
        compiler_params=_cparams(("parallel", "arbitrary"), 40),
        name="conv",
    )(u, state_pad, conv_w, conv_b, ln_g, ln_b)


def _mlstm_kernel(q_ref, k_ref, v_ref, o_ref, gt_ref, c0_ref, n0_ref, m0_ref, ng_ref,
                  hm_ref, cn_ref, nn_ref, mn_ref, c_s, n_s, m_s, *, bb, L):
    ci = pl.program_id(1)
    nc = pl.num_programs(1)

    @pl.when(ci == 0)
    def _():
        c_s[...] = c0_ref[...]
        n_s[...] = n0_ref[...]
        m_s[...] = m0_ref[...]

    rt = lax.broadcasted_iota(jnp.int32, (L, L), 0)
    cs = lax.broadcasted_iota(jnp.int32, (L, L), 1)
    causal = cs <= rt
    lower = jnp.where(causal, 1.0, 0.0).astype(BF16)
    upper = jnp.where(rt <= cs, 1.0, 0.0).astype(BF16)
    nt_dims = (((1,), (1,)), ((), ()))
    scale = M_DK ** -0.5

    for b in range(bb):
        rows = gt_ref[b]
        cum_rows = jnp.zeros((2 * M_HEADS, L), F32)
        cum_cols = jnp.zeros((L, 2 * M_HEADS), F32)
        rest = rows
        for _ in range(3):
            piece = rest.astype(BF16)
            rest = rest - piece.astype(F32)
            cum_rows = cum_rows + jnp.dot(piece, upper, preferred_element_type=F32)
            cum_cols = cum_cols + lax.dot_general(lower, piece, nt_dims, preferred_element_type=F32)
        heads = range(M_HEADS)
        ig_row = [rows[h:h + 1, :] for h in heads]
        b_row = [cum_rows[M_HEADS + h:M_HEADS + h + 1, :] for h in heads]
        b_col = [cum_cols[:, M_HEADS + h:M_HEADS + h + 1] for h in heads]
        m0 = [m_s[b, h:h + 1, 0:1] for h in heads]
        q = [q_ref[b, :, h * M_DK:(h + 1) * M_DK] for h in heads]
        k = [k_ref[b, :, h * M_DK:(h + 1) * M_DK] for h in heads]
        v = [v_ref[b, :, h * M_DV:(h + 1) * M_DV] for h in heads]
        c0 = [c_s[b, h] for h in heads]
        n0 = [n_s[b, h:h + 1, :] for h in heads]

        g_col = [b_col[h] + m0[h] for h in heads]
        dmat = [jnp.where(causal, b_col[h] - b_row[h] + ig_row[h], -jnp.inf) for h in heads]
        m_col = [jnp.maximum(g_col[h], jnp.max(dmat[h], axis=1, keepdims=True)) for h in heads]
        w_state = [jnp.exp(g_col[h] - m_col[h]) * scale for h in heads]
        decay = [jnp.exp(dmat[h] - m_col[h]) * scale for h in heads]
        qk = [lax.dot_general(q[h], k[h], nt_dims, preferred_element_type=F32) for h in heads]
        s = [qk[h] * decay[h] for h in heads]
        qc = [jnp.dot(q[h], c0[h].astype(BF16), preferred_element_type=F32) for h in heads]
        num = [jnp.dot(s[h].astype(BF16), v[h], preferred_element_type=F32) + w_state[h] * qc[h] for h in heads]
        qn = [jnp.sum(q[h].astype(F32) * n0[h], axis=1, keepdims=True) for h in heads]
        den = [jnp.sum(s[h], axis=1, keepdims=True) + w_state[h] * qn[h] for h in heads]
        hh = [num[h] / jnp.maximum(jnp.abs(den[h]), jnp.exp(-m_col[h])) for h in heads]
        for h in heads:
            hn = hh[h] * lax.rsqrt(jnp.mean(hh[h] * hh[h], axis=-1, keepdims=True) + EPS) * ng_ref[h:h + 1, :]
            og = _sigmoid(o_ref[b, :, h * M_DV:(h + 1) * M_DV].astype(F32))
            hm_ref[b, :, h * M_DV:(h + 1) * M_DV] = (og * hn).astype(BF16)

        b_last = [b_row[h][:, L - 1:L] for h in heads]
        g_last = [b_last[h] + m0[h] for h in heads]
        wk_row = [b_last[h] - b_row[h] + ig_row[h] for h in heads]
        m_new = [jnp.maximum(g_last[h], jnp.max(wk_row[h], axis=1, keepdims=True)) for h in heads]
        a0 = [jnp.exp(g_last[h] - m_new[h]) for h in heads]
        ak_row = [jnp.exp(wk_row[h] - m_new[h]) for h in heads]
        kts = [(k[h].T.astype(F32) * ak_row[h]).astype(BF16) for h in heads]
        kv = [jnp.dot(kts[h], v[h], preferred_element_type=F32) for h in heads]
        kn = [jnp.dot(ak_row[h].astype(BF16), k[h], preferred_element_type=F32) for h in heads]
        for h in heads:
            c_s[b, h] = a0[h] * c0[h] + kv[h]
            n_s[b, h:h + 1, :] = a0[h] * n0[h] + kn[h]
            m_s[b, h:h + 1, :] = jnp.broadcast_to(m_new[h], (1, LANES))

    @pl.when(ci == nc - 1)
    def _():
        cn_ref[...] = c_s[...]
        nn_ref[...] = n_s[...]
        mn_ref[...] = m_s[...]


def _mlstm(qkvo, gates, c0, n0, m0, ng, bb, L):
    bsz, t, _ = qkvo.shape
    nc = t // L
    gates = gates.reshape(bsz, nc, 2 * M_HEADS, L)
    st_c = pl.BlockSpec((bb, M_HEADS, M_DK, M_DV), lambda b, c: (b, 0, 0, 0))
    st_n = pl.BlockSpec((bb, M_HEADS, M_DK), lambda b, c: (b, 0, 0))
    return pl.pallas_call(
        functools.partial(_mlstm_kernel, bb=bb, L=L),
        grid=(bsz // bb, nc),
        in_specs=[
            pl.BlockSpec((bb, L, QK_W), lambda b, c: (b, c, 0)),
            pl.BlockSpec((bb, L, QK_W), lambda b, c: (b, c, 1)),
            pl.BlockSpec((bb, L, M_WIDTH), lambda b, c: (b, c, 1)),
            pl.BlockSpec((bb, L, M_WIDTH), lambda b, c: (b, c, 2)),
            pl.BlockSpec((bb, None, 2 * M_HEADS, L), lambda b, c: (b, c, 0, 0)),
            st_c, st_n, st_n,
            pl.BlockSpec((M_HEADS, M_DV), lambda b, c: (0, 0)),
        ],
        out_specs=[
            pl.BlockSpec((bb, L, M_WIDTH), lambda b, c: (b, c, 0)),
            st_c, st_n, st_n,
        ],
        out_shape=[
            jax.ShapeDtypeStruct((bsz, t, M_WIDTH), BF16),
            jax.ShapeDtypeStruct((bsz, M_HEADS, M_DK, M_DV), F32),
            jax.ShapeDtypeStruct((bsz, M_HEADS, M_DK), F32),
            jax.ShapeDtypeStruct((bsz, M_HEADS, LANES), F32),
        ],
        scratch_shapes=[
            pltpu.VMEM((bb, M_HEADS, M_DK, M_DV), F32),
            pltpu.VMEM((bb, M_HEADS, M_DK), F32),
            pltpu.VMEM((bb, M_HEADS, LANES), F32),
        ],
        compiler_params=_cparams(("parallel", "arbitrary"), 40),
        name="mlstm",
    )(qkvo, qkvo, qkvo, qkvo, gates, c0, n0, m0, ng)


def _proj_out_kernel(x_ref, hm_ref, cv_ref, wa_ref, wb_ref, o_ref):
    acc = jnp.dot(hm_ref[...], wa_ref[...], preferred_element_type=F32)
    acc = acc + jnp.dot(cv_ref[...], wb_ref[...], preferred_element_type=F32)
    o_ref[...] = x_ref[...] + acc


def _proj_out(x, hm, cv, wa, wb):
    n = x.shape[0]
    tm = min(TOK_TILE, n)
    return pl.pallas_call(
        _proj_out_kernel,
        grid=(n // tm,),
        in_specs=[
            pl.BlockSpec((tm, D_MODEL), lambda i: (i, 0)),
            pl.BlockSpec((tm, M_WIDTH), lambda i: (i, 0)),
            pl.BlockSpec((tm, CONV_CH), lambda i: (i, 0)),
            pl.BlockSpec((M_WIDTH, D_MODEL), lambda i: (0, 0)),
            pl.BlockSpec((CONV_CH, D_MODEL), lambda i: (0, 0)),
        ],
        out_specs=pl.BlockSpec((tm, D_MODEL), lambda i: (i, 0)),
        out_shape=jax.ShapeDtypeStruct((n, D_MODEL), F32),
        compiler_params=_cparams(("parallel",), 48),
        name="proj_out",
    )(x, hm, cv, wa, wb)


def _mem_kv_kernel(m_ref, g_ref, w_ref, k_ref, v_ref):
    mn = _rms(m_ref[...], g_ref[...]).astype(BF16)
    z = jnp.dot(mn, w_ref[...], preferred_element_type=F32)
    k_ref[...] = z[:, :CA_W]
    v_ref[...] = z[:, CA_W:]


def _mem_kv(mem, g, wkv):
    n = mem.shape[0]
    return pl.pallas_call(
        _mem_kv_kernel,
        grid=(1,),
        in_specs=[
            pl.BlockSpec((n, D_MODEL), lambda i: (0, 0)),
            pl.BlockSpec((1, D_MODEL), lambda i: (0, 0)),
            pl.BlockSpec((D_MODEL, 2 * CA_W), lambda i: (0, 0)),
        ],
        out_specs=[pl.BlockSpec((n, CA_W), lambda i: (0, 0))] * 2,
        out_shape=[jax.ShapeDtypeStruct((n, CA_W), F32)] * 2,
        compiler_params=_cparams(("arbitrary",), 40),
        name="mem_kv",
    )(mem, g, wkv)


def _attn_router_kernel(x_ref, gca_ref, wq_ref, k_ref, v_ref, wo_ref, gmoe_ref, rw_ref, rb_ref, cnt0_ref,
                        hn_in_ref, x2_ref, hn_ref, te_ref, tg_ref, tr_ref, cnt_ref, o_s, cnt_s, *, bb, tt):
    del hn_in_ref
    rows = bb * tt

    @pl.when((pl.program_id(0) == 0) & (pl.program_id(1) == 0))
    def _():
        cnt_s[...] = cnt0_ref[...]

    x = x_ref[...].reshape(rows, D_MODEL)
    h = _rms(x, gca_ref[...]).astype(BF16)
    q = jnp.dot(h, wq_ref[...], preferred_element_type=F32) * (CA_DH ** -0.5)
    q = q.astype(BF16)
    units = [(b, slice(hd * CA_DH, (hd + 1) * CA_DH)) for b in range(bb) for hd in range(CA_HEADS)]
    nt_dims = (((1,), (1,)), ((), ()))
    s = [lax.dot_general(q[b * tt:(b + 1) * tt, cols], k_ref[b, :, cols], nt_dims, preferred_element_type=F32)
         for b, cols in units]
    e = [jnp.exp(si - jnp.max(si, axis=-1, keepdims=True)) for si in s]
    ov = [jnp.dot(ei.astype(BF16), v_ref[b, :, cols], preferred_element_type=F32) for ei, (b, cols) in zip(e, units)]
    for ei, ovi, (b, cols) in zip(e, ov, units):
        o_s[b * tt:(b + 1) * tt, cols] = (ovi / jnp.sum(ei, axis=-1, keepdims=True)).astype(BF16)
    x2 = x + jnp.dot(o_s[...], wo_ref[...], preferred_element_type=F32)
    x2_ref[...] = x2.reshape(bb, tt, D_MODEL)

    hn = _rms(x2, gmoe_ref[...])
    hn_bf = hn.astype(BF16)
    hn_ref[...] = _pack_pairs(hn[:, :HALF_D], hn[:, HALF_D:])
    logits = jnp.dot(hn_bf, rw_ref[...], preferred_element_type=F32) + rb_ref[...]
    lane = lax.broadcasted_iota(jnp.int32, (rows, LANES), 1)
    lane_f = lane.astype(F32)
    work = logits
    sel = []
    vals = []
    for _ in range(TOP_K):
        mx = jnp.max(work, axis=-1, keepdims=True)
        idx = jnp.min(jnp.where(work == mx, lane_f, float(LANES)), axis=-1, keepdims=True)
        hit = lane_f == idx
        sel.append((idx, hit))
        vals.append(mx)
        work = jnp.where(hit, -jnp.inf, work)
    ex = [jnp.exp(vv - vals[0]) for vv in vals]
    tot = ex[0] + ex[1] + ex[2] + ex[3]
    assigned = jnp.zeros((rows, LANES), F32)
    for _, hit in sel:
        assigned = assigned + hit.astype(F32)
    rt = lax.broadcasted_iota(jnp.int32, (rows, rows), 0)
    cs = lax.broadcasted_iota(jnp.int32, (rows, rows), 1)
    strict = (cs < rt).astype(BF16)
    before = jnp.dot(strict, assigned.astype(BF16), preferred_element_type=F32) + cnt_s[...]
    te = jnp.zeros((rows, LANES), F32)
    tg = jnp.zeros((rows, LANES), F32)
    tr = jnp.zeros((rows, LANES), F32)
    for kk, (idx, hit) in enumerate(sel):
        rank = jnp.sum(jnp.where(hit, before, 0.0), axis=-1, keepdims=True)
        te = jnp.where(lane == kk, idx, te)
        tg = jnp.where(lane == kk, ex[kk] / tot, tg)
        tr = jnp.where(lane == kk, rank, tr)
    te_ref[...] = te.astype(jnp.int32)
    tg_ref[...] = tg
    tr_ref[...] = tr.astype(jnp.int32)
    cnt_s[...] = cnt_s[...] + jnp.sum(assigned, axis=0, keepdims=True)
    cnt_ref[...] = cnt_s[...]


def _attn_router(x1, mem_k, mem_v, cnt0, hn_all, row0, p, bb, tt):
    bsz, t, _ = x1.shape
    rows = bb * tt
    n = bsz * t
    nb, nt = bsz // bb, t // tt
    blk0 = row0 // rows
    tok = lambda b, s: (b * nt + s, 0)
    const = lambda b, s: (0, 0)
    grid_spec = dict(
        grid=(nb, nt),
        in_specs=[
            pl.BlockSpec((bb, tt, D_MODEL), lambda b, s: (b, s, 0)),
            pl.BlockSpec((1, D_MODEL), const),
            pl.BlockSpec((D_MODEL, CA_W), const),
            pl.BlockSpec((bb, N_MEM, CA_W), lambda b, s: (b, 0, 0)),
            pl.BlockSpec((bb, N_MEM, CA_W), lambda b, s: (b, 0, 0)),
            pl.BlockSpec((CA_W, D_MODEL), const),
            pl.BlockSpec((1, D_MODEL), const),
            pl.BlockSpec((D_MODEL, LANES), const),
            pl.BlockSpec((1, LANES), const),
            pl.BlockSpec((1, LANES), const),
            pl.BlockSpec(memory_space=pl.ANY),
        ],
        out_specs=[
            pl.BlockSpec((bb, tt, D_MODEL), lambda b, s: (b, s, 0)),
            pl.BlockSpec((rows, HALF_D), lambda b, s: (blk0 + b * nt + s, 0)),
            pl.BlockSpec((rows, LANES), tok),
            pl.BlockSpec((rows, LANES), tok),
            pl.BlockSpec((rows, LANES), tok),
            pl.BlockSpec((1, LANES), const),
        ],
    )
    return pl.pallas_call(
        functools.partial(_attn_router_kernel, bb=bb, tt=tt),
        out_shape=[
            jax.ShapeDtypeStruct((bsz, t, D_MODEL), F32),
            jax.ShapeDtypeStruct(hn_all.shape, jnp.uint32),
            jax.ShapeDtypeStruct((n, LANES), jnp.int32),
            jax.ShapeDtypeStruct((n, LANES), F32),
            jax.ShapeDtypeStruct((n, LANES), jnp.int32),
            jax.ShapeDtypeStruct((1, LANES), F32),
        ],
        scratch_shapes=[pltpu.VMEM((rows, CA_W), BF16), pltpu.VMEM((1, LANES), F32)],
        input_output_aliases={10: 1},
        compiler_params=_cparams(("arbitrary", "arbitrary"), 48),
        name="attn_router",
        **grid_spec,
    )(x1, p["g_ca"], p["wq"], mem_k, mem_v, p["wo"], p["g_moe"], p["router_w"], p["router_b"], cnt0, hn_all)


def _sc_scatter_rows(rows, dest3, n_out):
    n, d = rows.shape
    nwin = dest3.shape[0]
    mesh = plsc.VectorSubcoreMesh(core_axis_name="c", subcore_axis_name="s")
    workers = mesh.num_cores * mesh.num_subcores
    assert nwin * SC_WIN == n and nwin % workers == 0, (n, nwin, workers)
    per = nwin // workers

    @functools.partial(
        pl.kernel, out_type=jax.ShapeDtypeStruct((n_out, d), rows.dtype), mesh=mesh,
        scratch_types=[pltpu.VMEM((SC_WIN, d), rows.dtype), pltpu.VMEM((TOP_K, SC_WIN), jnp.int32),
                       pltpu.SemaphoreType.DMA])
    def scatter(x_hbm, i_hbm, o_hbm, buf, idx, sem):
        wid = lax.axis_index("s") * mesh.num_cores + lax.axis_index("c")

        @pl.loop(0, per)
        def _(j):
            w = wid * per + j
            pltpu.sync_copy(x_hbm.at[pl.ds(w * SC_WIN, SC_WIN)], buf)
            pltpu.sync_copy(i_hbm.at[w], idx)
            copies = [pltpu.async_copy(buf, o_hbm.at[idx.at[kk]], sem) for kk in range(TOP_K)]
            for cp in copies:
                cp.wait()

    return scatter(rows, dest3)


def _sc_gather_rows(y, dest3):
    d = y.shape[1]
    nwin = dest3.shape[0]
    n = nwin * SC_GWIN
    mesh = plsc.VectorSubcoreMesh(core_axis_name="c", subcore_axis_name="s")
    workers = mesh.num_cores * mesh.num_subcores
    assert nwin % (2 * workers) == 0, (nwin, workers)
    per = nwin // workers

    @functools.partial(
        pl.kernel, out_type=jax.ShapeDtypeStruct((TOP_K, n, d), y.dtype), mesh=mesh,
        scratch_types=[pltpu.VMEM((2 * TOP_K, SC_GWIN, d), y.dtype), pltpu.VMEM((2, TOP_K, SC_GWIN), jnp.int32),
                       pltpu.SemaphoreType.DMA, pltpu.SemaphoreType.DMA, pltpu.SemaphoreType.DMA])
    def gather(y_hbm, i_hbm, o_hbm, buf, idx, sem_gather, sem_write, sem_idx):
        base = (lax.axis_index("s") * mesh.num_cores + lax.axis_index("c")) * per

        def index_copy(w, s):
            return pltpu.make_async_copy(i_hbm.at[w], idx.at[s], sem_idx)

        def write_copy(w, s, kk):
            return pltpu.make_async_copy(buf.at[s * TOP_K + kk], o_hbm.at[kk, pl.ds(w * SC_GWIN, SC_GWIN)], sem_write)

        index_copy(base, 0).start()

        @pl.loop(0, per // 2)
        def _(g):
            for s in range(2):
                w = base + 2 * g + s
                index_copy(w, s).wait()
                if s == 0:
                    index_copy(w + 1, 1).start()
                else:
                    @pl.when(g + 1 < per // 2)
                    def _():
                        index_copy(w + 1, 0).start()

                @pl.when(g > 0)
                def _():
                    for kk in range(TOP_K):
                        write_copy(w, s, kk).wait()

                gathers = [pltpu.make_async_copy(y_hbm.at[idx.at[s, kk]], buf.at[s * TOP_K + kk], sem_gather)
                           for kk in range(TOP_K)]
                for cp in gathers:
                    cp.start()
                for kk in range(TOP_K):
                    gathers[kk].wait()
                    write_copy(w, s, kk).start()

        for s in range(2):
            for kk in range(TOP_K):
                write_copy(base, s, kk).wait()

    return gather(y, dest3)


def _expert_changed(te_ref):
    t = pl.program_id(1)
    return (t == 0) | (te_ref[t] != te_ref[jnp.maximum(t - 1, 0)])


def _moe_up_kernel(te_ref, nu_ref, x_ref, w1g_ref, w1l_ref, b1g_ref, b1l_ref, act_ref, wg_s, wl_s):
    @pl.when(_expert_changed(te_ref))
    def _():
        wg_s[...] = w1g_ref[0].astype(BF16)
        wl_s[...] = w1l_ref[0].astype(BF16)

    def up(words):
        lo, hi = _unpack_pairs(words)
        x = jnp.concatenate([lo.astype(BF16), hi.astype(BF16)], axis=1)
        glu = jnp.dot(x, wg_s[...], preferred_element_type=F32) + b1g_ref[0]
        lin = jnp.dot(x, wl_s[...], preferred_element_type=F32) + b1l_ref[0]
        glu = jnp.minimum(glu, SWIGLU_LIMIT)
        lin = jnp.clip(lin, -SWIGLU_LIMIT, SWIGLU_LIMIT)
        act_ref[...] = (glu * _sigmoid(SWIGLU_ALPHA * glu) * (lin + 1.0)).astype(BF16)

    t = pl.program_id(1)
    used = t < nu_ref[0]
    valid = nu_ref[1 + t]

    @pl.when(used & (valid >= MOE_TM))
    def _():
        up(x_ref[...])

    @pl.when(used & (valid < MOE_TM))
    def _():
        rowid = lax.broadcasted_iota(jnp.int32, (MOE_TM, HALF_D), 0)
        up(jnp.where(rowid < valid, x_ref[...], jnp.uint32(0)))


def _moe_down_kernel(te_ref, nu_ref, a_ref, w2_ref, b2_ref, y_ref, w2_s):
    @pl.when(_expert_changed(te_ref))
    def _():
        w2_s[...] = w2_ref[0].astype(BF16)

    @pl.when(pl.program_id(1) < nu_ref[0])
    def _():
        y = jnp.dot(a_ref[...], w2_s[...], preferred_element_type=F32) + b2_ref[0]
        y_ref[...] = _pack_pairs(y[:, :MOE_TN // 2], y[:, MOE_TN // 2:])


def _moe(tile_e, n_used, x_sorted, w1, b1, w2, b2):
    r = x_sorted.shape[0]
    n_tiles = r // MOE_TM
    nf = D_FF // MOE_TF
    nn = D_MODEL // MOE_TN

    def row(c, t, te, nu):
        return jnp.maximum(jnp.minimum(t, nu[0] - 1), 0)

    act = pl.pallas_call(
        _moe_up_kernel,
        grid_spec=pltpu.PrefetchScalarGridSpec(
            num_scalar_prefetch=2,
            grid=(nf, n_tiles),
            in_specs=[
                pl.BlockSpec((MOE_TM, HALF_D), lambda c, t, te, nu: (row(c, t, te, nu), 0)),
                pl.BlockSpec((1, D_MODEL, MOE_TF), lambda c, t, te, nu: (te[t], 0, c)),
                pl.BlockSpec((1, D_MODEL, MOE_TF), lambda c, t, te, nu: (te[t], 0, nf + c)),
                pl.BlockSpec((1, 1, MOE_TF), lambda c, t, te, nu: (te[t], 0, c)),
                pl.BlockSpec((1, 1, MOE_TF), lambda c, t, te, nu: (te[t], 0, nf + c)),
            ],
            out_specs=pl.BlockSpec((MOE_TM, MOE_TF), lambda c, t, te, nu: (row(c, t, te, nu), c)),
            scratch_shapes=[pltpu.VMEM((D_MODEL, MOE_TF), BF16), pltpu.VMEM((D_MODEL, MOE_TF), BF16)],
        ),
        out_shape=jax.ShapeDtypeStruct((r, D_FF), BF16),
        compiler_params=_cparams(("arbitrary", "arbitrary"), MOE_VMEM_MB),
        name="moe_up",
    )(tile_e, n_used, x_sorted, w1, w1, b1, b1)

    return pl.pallas_call(
        _moe_down_kernel,
        grid_spec=pltpu.PrefetchScalarGridSpec(
            num_scalar_prefetch=2,
            grid=(nn, n_tiles),
            in_specs=[
                pl.BlockSpec((MOE_TM, D_FF), lambda c, t, te, nu: (row(c, t, te, nu), 0)),
                pl.BlockSpec((1, D_FF, MOE_TN), lambda c, t, te, nu: (te[t], 0, c)),
                pl.BlockSpec((1, 1, MOE_TN), lambda c, t, te, nu: (te[t], 0, c)),
            ],
            out_specs=pl.BlockSpec((MOE_TM, MOE_TN // 2), lambda c, t, te, nu: (row(c, t, te, nu), c)),
            scratch_shapes=[pltpu.VMEM((D_FF, MOE_TN), BF16)],
        ),
        out_shape=jax.ShapeDtypeStruct((r, HALF_D), jnp.uint32),
        compiler_params=_cparams(("arbitrary", "arbitrary"), MOE_VMEM_MB),
        name="moe_down",
    )(tile_e, n_used, act, w2, b2)


def _combine_kernel(x_ref, y0_ref, y1_ref, y2_ref, y3_ref, tg_ref, g_ref, o_ref):
    tg = tg_ref[...]
    acc = x_ref[...]
    half = MOE_TN // 2
    for kk, y_ref in enumerate((y0_ref, y1_ref, y2_ref, y3_ref)):
        cols = []
        for c in range(D_MODEL // MOE_TN):
            cols += list(_unpack_pairs(y_ref[0, :, c * half:(c + 1) * half]))
        acc = acc + jnp.concatenate(cols, axis=1) * tg[:, kk:kk + 1]
    o_ref[...] = _rms(acc, g_ref[...])


def _combine(x2, yg, tg, g, row0):
    n = x2.shape[0]
    tm = min(TOK_TILE, n)
    blk0 = row0 // tm
    yspec = [pl.BlockSpec((1, tm, HALF_D), functools.partial(lambda i, kk: (kk, blk0 + i, 0), kk=kk))
             for kk in range(TOP_K)]
    return pl.pallas_call(
        _combine_kernel,
        grid=(n // tm,),
        in_specs=[pl.BlockSpec((tm, D_MODEL), lambda i: (i, 0))] + yspec + [
            pl.BlockSpec((tm, LANES), lambda i: (i, 0)),
            pl.BlockSpec((1, D_MODEL), lambda i: (0, 0)),
        ],
        out_specs=pl.BlockSpec((tm, D_MODEL), lambda i: (i, 0)),
        out_shape=jax.ShapeDtypeStruct((n, D_MODEL), F32),
        compiler_params=_cparams(("parallel",), 48),
        name="combine",
    )(x2, yg, yg, yg, yg, tg, g)


def _group_tiles(bsz, t):
    tt = min(t, TOK_TILE)
    bb = max(1, min(bsz, TOK_TILE // tt))
    return bb, tt


def _layer_group(x, c0, n0, m0, conv_state, mem_k, mem_v, cnt0, hn_all, row0, p):
    bsz, t, _ = x.shape
    n = bsz * t
    L = min(MLSTM_CHUNK, t)
    state_pad = jnp.pad(conv_state, ((0, 0), (CONV_PAD - CONV_STATE, 0), (0, 0)))
    conv_args = (state_pad, p["conv_w"], p["conv_b"], p["ln_g"], p["ln_b"])
    if t % PROJ_TM == 0:
        qkvo, gates, cv, new_conv = _proj_in_conv(x.reshape(n, D_MODEL), p["g_mix"], p["w_main"], p["wg"], p["bg"],
                                                  L, *conv_args, seq_len=t)
    else:
        qkvo, u, gates = _proj_in(x.reshape(n, D_MODEL), p["g_mix"], p["w_main"], p["wg"], p["bg"], L)
        ctt = min(CONV_TT, t)
        cbb = max(1, min(bsz, CONV_TT // ctt))
        cv, new_conv = _conv(u.reshape(bsz, t, CONV_CH), *conv_args, cbb, ctt)

    mbb = min(bsz, MLSTM_BB)
    m0b = jnp.broadcast_to(m0[:, :, None], (bsz, M_HEADS, LANES))
    hm, c1, n1, m1 = _mlstm(qkvo.reshape(bsz, t, -1), gates, c0, n0, m0b, p["mh_g"], mbb, L)

    x1 = _proj_out(x.reshape(n, D_MODEL), hm.reshape(n, M_WIDTH), cv.reshape(n, CONV_CH), p["wo_a"], p["wo_b"])

    abb, att = _group_tiles(bsz, t)
    x2, hn_all, te, tg, tr, cnt = _attn_router(x1.reshape(bsz, t, D_MODEL), mem_k, mem_v, cnt0, hn_all, row0,
                                               p, abb, att)
    return x2.reshape(n, D_MODEL), hn_all, te, tg, tr, cnt, (c1, n1, m1[:, :, 0], new_conv)


def kernel(x_prompt, x_sample, mem_prompt, state_mlstm_C, state_mlstm_n, state_mlstm_m, state_conv, cache_mem_k, cache_mem_v, norm_mix_g, w_in, b_gates, mh_norm_g, conv_w, conv_b, conv_ln_g, conv_ln_b, w_out, norm_ca_g, norm_mem_g, ca_wq, ca_wk, ca_wv, ca_wo, norm_moe_g, router_w, router_b, moe_w1, moe_b1, moe_w2, moe_b2, final_norm_g):
    assert w_in.shape[0] == 1, "single layer"
    bp, tp, _ = x_prompt.shape
    bs, ts, _ = x_sample.shape
    n_p, n_s = bp * tp, bs * ts
    n_all = n_p + n_s

    wi = w_in[0]
    o0 = 0
    parts = []
    for width in (QK_W, QK_W, M_WIDTH, M_WIDTH, M_HEADS, M_HEADS, CONV_CH, CONV_CH):
        parts.append(wi[:, o0:o0 + width])
        o0 += width
    w_q, w_k, w_v, w_o, w_gi, w_gf, w_ua, w_ug = parts
    col_tiles = []
    w_qkvo = wi[:, :2 * QK_W + 2 * M_WIDTH]
    for c in range(PROJ_STEPS):
        col_tiles += [w_qkvo[:, c * QKVO_TN:(c + 1) * QKVO_TN], w_ua[:, c * GLU_TN:(c + 1) * GLU_TN],
                      w_ug[:, c * GLU_TN:(c + 1) * GLU_TN]]
    p = {
        "g_mix": norm_mix_g[0][None, :],
        "w_main": jnp.concatenate(col_tiles, axis=1).astype(BF16),
        "wg": jnp.concatenate([w_gi, w_gf], axis=1).T.astype(BF16),
        "bg": b_gates[0][:, None],
        "mh_g": mh_norm_g[0],
        "conv_w": conv_w[0],
        "conv_b": conv_b[0][None, :],
        "ln_g": conv_ln_g[0][None, :],
        "ln_b": conv_ln_b[0][None, :],
        "wo_a": w_out[0][:M_WIDTH].astype(BF16),
        "wo_b": w_out[0][M_WIDTH:].astype(BF16),
        "g_ca": norm_ca_g[0][None, :],
        "wq": ca_wq[0].astype(BF16),
        "wo": ca_wo[0].astype(BF16),
        "g_moe": norm_moe_g[0][None, :],
        "router_w": jnp.pad(router_w[0], ((0, 0), (0, LANES - N_EXPERTS))).astype(BF16),
        "router_b": jnp.concatenate([router_b[0], jnp.full((LANES - N_EXPERTS,), NEG_BIG, F32)])[None, :],
    }

    wkv = jnp.concatenate([ca_wk[0], ca_wv[0]], axis=1).astype(BF16)
    mk, mv = _mem_kv(mem_prompt.reshape(bp * N_MEM, D_MODEL), norm_mem_g[0][None, :], wkv)
    mk = mk.reshape(bp, N_MEM, CA_W)
    mv = mv.reshape(bp, N_MEM, CA_W)
    hn_all = jnp.zeros((n_all, HALF_D), jnp.uint32)
    cnt0 = jnp.zeros((1, LANES), F32)
    zc = jnp.zeros((bp, M_HEADS, M_DK, M_DV), F32)
    zn = jnp.zeros((bp, M_HEADS, M_DK), F32)
    zm = jnp.zeros((bp, M_HEADS), F32)
    zconv = jnp.zeros((bp, CONV_STATE, CONV_CH), F32)
    x2p, hn_all, te_p, tg_p, tr_p, cnt1, st_p = _layer_group(
        x_prompt, zc, zn, zm, zconv, mk.astype(BF16), mv.astype(BF16), cnt0, hn_all, 0, p)

    x2s, hn_all, te_s, tg_s, tr_s, cnt2, st_s = _layer_group(
        x_sample, state_mlstm_C[0], state_mlstm_n[0], state_mlstm_m[0], state_conv[0],
        cache_mem_k[0].reshape(bs, N_MEM, CA_W).astype(BF16), cache_mem_v[0].reshape(bs, N_MEM, CA_W).astype(BF16),
        cnt1, hn_all, n_p, p)

    te = jnp.concatenate([te_p[:, :TOP_K], te_s[:, :TOP_K]], axis=0)
    tr = jnp.concatenate([tr_p[:, :TOP_K], tr_s[:, :TOP_K]], axis=0)
    counts = cnt2[0, :N_EXPERTS].astype(jnp.int32)
    tiles_per_e = (counts + MOE_TM - 1) // MOE_TM
    tile_end = jnp.cumsum(tiles_per_e)
    row_start = (tile_end - tiles_per_e) * MOE_TM
    n_tiles = -(-(n_all * TOP_K) // MOE_TM) + N_EXPERTS
    n_used = tile_end[-1:]
    tile_ids = jnp.minimum(jnp.arange(n_tiles, dtype=jnp.int32), n_used[0] - 1)
    tile_e = jnp.minimum(jnp.sum(tile_ids[:, None] >= tile_end[None, :], axis=1), N_EXPERTS - 1).astype(jnp.int32)
    tile_first = jnp.arange(n_tiles, dtype=jnp.int32) - (tile_end - tiles_per_e)[tile_e]
    tile_valid = jnp.clip(counts[tile_e] - tile_first * MOE_TM, 0, MOE_TM)
    tile_valid = jnp.where(jnp.arange(n_tiles) < n_used[0], tile_valid, 0).astype(jnp.int32)
    moe_meta = jnp.concatenate([n_used.astype(jnp.int32), tile_valid])
    dest = row_start[te] + tr
    dest3 = dest.reshape(n_all // SC_WIN, SC_WIN, TOP_K).transpose(0, 2, 1)
    x_sorted = _sc_scatter_rows(hn_all, dest3, n_tiles * MOE_TM)

    y_sorted = _moe(tile_e, moe_meta, x_sorted, moe_w1[0], moe_b1[0][:, None, :], moe_w2[0],
                    moe_b2[0][:, None, :])

    yg = _sc_gather_rows(y_sorted, dest.reshape(n_all // SC_GWIN, SC_GWIN, TOP_K).transpose(0, 2, 1))
    fg = final_norm_g[None, :]
    y_prompt = _combine(x2p, yg, tg_p, fg, 0).reshape(bp, tp, D_MODEL)
    y_sample = _combine(x2s, yg, tg_s, fg, n_p).reshape(bs, ts, D_MODEL)

    c1, n1, m1, conv1 = st_p
    c2, n2, m2, conv2 = st_s
    mk4 = mk.reshape(1, bp, N_MEM, CA_HEADS, CA_DH)
    mv4 = mv.reshape(1, bp, N_MEM, CA_HEADS, CA_DH)
    return (y_prompt, y_sample, c1[None], n1[None], m1[None], conv1[None], mk4, mv4,
            c2[None], n2[None], m2[None], conv2[None])
```

```python
import functools

import jax
import jax.numpy as jnp
from jax import lax
from jax.experimental import pallas as pl
from jax.experimental.pallas import tpu as pltpu
from jax.experimental.pallas import tpu_sc as plsc

F32 = jnp.float32
BF16 = jnp.bfloat16

D_MODEL = 2048
HALF_D = D_MODEL // 2
M_HEADS = 4
M_DV = 256
M_DK = 128
QK_W = M_HEADS * M_DK
M_WIDTH = M_HEADS * M_DV
CONV_CH = 1024
CONV_WIDTH = 31
CONV_STATE = CONV_WIDTH - 1
CONV_PAD = 32
N_MEM = 256
CA_HEADS = 4
CA_DH = 128
CA_W = CA_HEADS * CA_DH
N_EXPERTS = 32
TOP_K = 4
D_FF = 2048
SWIGLU_ALPHA = 1.702
SWIGLU_LIMIT = 7.0
EPS = 1e-6
LANES = 128
SUBLANES = 8
NEG_BIG = -1e30

PROJ_TM = 1024
PROJ_STEPS = 4
QKVO_TN = (2 * QK_W + 2 * M_WIDTH) // PROJ_STEPS
GLU_TN = CONV_CH // PROJ_STEPS
PROJ_TN = QKVO_TN + 2 * GLU_TN
CONV_SUB = PROJ_TM // PROJ_STEPS
TOK_TILE = 512
ATTN_SPLIT = 2
MOE_TM = 512
MOE_TF = 1024
MOE_TN = 2048
MOE_VMEM_MB = 60
MLSTM_CHUNK = 256
MLSTM_BB = 2
CONV_TT = 256
CONV_RB = 32
CONV_LC = 512
SC_WIN = 32
SC_GWIN = 8


def _cparams(sem, vmem_mb):
    return pltpu.CompilerParams(dimension_semantics=sem, vmem_limit_bytes=vmem_mb << 20)


def _log_sigmoid(x):
    return jnp.minimum(x, 0.0) - jnp.log1p(jnp.exp(-jnp.abs(x)))


def _sigmoid(x):
    return 1.0 / (1.0 + jnp.exp(-x))


def _pack_pairs(lo, hi):
    return lax.bitcast_convert_type(pltpu.pack_elementwise([lo, hi], packed_dtype=BF16), jnp.uint32)


def _unpack_pairs(w):
    return tuple(pltpu.unpack_elementwise(w, index=i, packed_dtype=BF16, unpacked_dtype=F32) for i in range(2))


def _rms(x, g):
    ms = jnp.mean(x * x, axis=-1, keepdims=True)
    return x * lax.rsqrt(ms + EPS) * g


def _proj_prologue(x_ref, g_ref, wg_ref, bg_ref, h_ref, gates_ref, chunk):
    tm = x_ref.shape[0]
    h = _rms(x_ref[...], g_ref[...]).astype(BF16)
    h_ref[...] = h
    gt = lax.dot_general(wg_ref[...], h, (((1,), (1,)), ((), ())), preferred_element_type=F32)
    gt = gt + bg_ref[...]
    row = lax.broadcasted_iota(jnp.int32, gt.shape, 0)
    gt = jnp.where(row < M_HEADS, gt, _log_sigmoid(gt))
    for c in range(tm // chunk):
        gates_ref[c] = gt[:, c * chunk:(c + 1) * chunk]


def _proj_step(h_ref, w_ref):
    z = jnp.dot(h_ref[...], w_ref[...], preferred_element_type=F32)
    glu = z[:, QKVO_TN:QKVO_TN + GLU_TN] * _sigmoid(z[:, QKVO_TN + GLU_TN:])
    return z[:, :QKVO_TN].astype(BF16), glu.astype(BF16)


def _proj_in_kernel(x_ref, g_ref, w_ref, wg_ref, bg_ref, qkvo_ref, u_ref, gates_ref, h_ref, *, chunk):
    @pl.when(pl.program_id(1) == 0)
    def _():
        _proj_prologue(x_ref, g_ref, wg_ref, bg_ref, h_ref, gates_ref, chunk)

    qkvo_ref[...], u_ref[...] = _proj_step(h_ref, w_ref)


def _proj_in(x, g, w_main, wg, bg, chunk):
    n = x.shape[0]
    tm = min(PROJ_TM, n)
    return pl.pallas_call(
        functools.partial(_proj_in_kernel, chunk=chunk),
        grid=(n // tm, PROJ_STEPS),
        in_specs=[
            pl.BlockSpec((tm, D_MODEL), lambda i, j: (i, 0)),
            pl.BlockSpec((1, D_MODEL), lambda i, j: (0, 0)),
            pl.BlockSpec((D_MODEL, PROJ_TN), lambda i, j: (0, j)),
            pl.BlockSpec((2 * M_HEADS, D_MODEL), lambda i, j: (0, 0)),
            pl.BlockSpec((2 * M_HEADS, 1), lambda i, j: (0, 0)),
        ],
        out_specs=[
            pl.BlockSpec((tm, QKVO_TN), lambda i, j: (i, j)),
            pl.BlockSpec((tm, GLU_TN), lambda i, j: (i, j)),
            pl.BlockSpec((tm // chunk, 2 * M_HEADS, chunk), lambda i, j: (i, 0, 0)),
        ],
        out_shape=[
            jax.ShapeDtypeStruct((n, 2 * QK_W + 2 * M_WIDTH), BF16),
            jax.ShapeDtypeStruct((n, CONV_CH), BF16),
            jax.ShapeDtypeStruct((n // chunk, 2 * M_HEADS, chunk), F32),
        ],
        scratch_shapes=[pltpu.VMEM((tm, D_MODEL), BF16)],
        compiler_params=_cparams(("parallel", "arbitrary"), 48),
        name="proj_in",
    )(x, g, w_main, wg, bg)


def _proj_in_conv_kernel(x_ref, g_ref, w_ref, wg_ref, bg_ref, st_ref, cw_ref, cb_ref, lg_ref, lb_ref,
                         qkvo_ref, gates_ref, cv_ref, ns_ref, h_ref, u_s, hist_s, e_ref, er_ref, c_ref,
                         *, chunk, tiles_per_seq):
    i = pl.program_id(0)
    j = pl.program_id(1)
    p = i - 1
    sub = CONV_SUB
    ext = sub + CONV_PAD

    @pl.when((i == 0) & (j == 0))
    def _():
        u_s[...] = jnp.zeros(u_s.shape, BF16)
        hist_s[...] = jnp.zeros(hist_s.shape, F32)
        e_ref[:, ext:, :] = jnp.zeros((1, SUBLANES, CONV_CH), F32)

    @pl.when(j == 0)
    def _():
        _proj_prologue(x_ref, g_ref, wg_ref, bg_ref, h_ref, gates_ref, chunk)

    @pl.when((j == 0) & (p >= 0) & (p % tiles_per_seq == 0))
    def _():
        hist_s[...] = st_ref[0]

    r0 = pl.multiple_of(j * sub, sub)
    e_ref[0, 0:CONV_PAD, :] = hist_s[...]
    for c in range(PROJ_STEPS):
        e_ref[0, CONV_PAD:ext, c * GLU_TN:(c + 1) * GLU_TN] = u_s[(i + 1) % 2, c, pl.ds(r0, sub), :].astype(F32)
    prepare, row_block = _conv_stages(e_ref, er_ref, c_ref, cw_ref, cb_ref, lg_ref, lb_ref, cv_ref, 0, sub)
    prepare()
    for r0 in range(0, sub, CONV_RB):
        row_block(r0)
    hist_s[...] = e_ref[0, sub:ext, :]

    qkvo_ref[...], u_s[i % 2, j] = _proj_step(h_ref, w_ref)

    @pl.when((j == PROJ_STEPS - 1) & (p >= 0) & (p % tiles_per_seq == tiles_per_seq - 1))
    def _():
        ns_ref[0] = e_ref[0, ext - CONV_STATE:ext, :]


def _proj_in_conv(x, g, w_main, wg, bg, chunk, state_pad, conv_w, conv_b, ln_g, ln_b, seq_len):
    n = x.shape[0]
    tm = PROJ_TM
    n_tiles = n // tm
    tiles_per_seq = seq_len // tm
    bsz = n // seq_len
    assert tm == PROJ_STEPS * CONV_SUB and seq_len % tm == 0
    last = n_tiles - 1

    def tile(i, j):
        return jnp.minimum(i, last)

    def seq(i, j):
        return jnp.clip((i - 1) // tiles_per_seq, 0, bsz - 1)

    const2 = lambda i, j: (0, 0)
    return pl.pallas_call(
        functools.partial(_proj_in_conv_kernel, chunk=chunk, tiles_per_seq=tiles_per_seq),
        grid=(n_tiles + 1, PROJ_STEPS),
        in_specs=[
            pl.BlockSpec((tm, D_MODEL), lambda i, j: (tile(i, j), 0)),
            pl.BlockSpec((1, D_MODEL), const2),
            pl.BlockSpec((D_MODEL, PROJ_TN), lambda i, j: (0, j)),
            pl.BlockSpec((2 * M_HEADS, D_MODEL), const2),
            pl.BlockSpec((2 * M_HEADS, 1), const2),
            pl.BlockSpec((1, CONV_PAD, CONV_CH), lambda i, j: (seq(i, j), 0, 0)),
            pl.BlockSpec((CONV_WIDTH, CONV_CH), const2),
            pl.BlockSpec((1, CONV_CH), const2),
            pl.BlockSpec((1, CONV_CH), const2),
            pl.BlockSpec((1, CONV_CH), const2),
        ],
        out_specs=[
            pl.BlockSpec((tm, QKVO_TN), lambda i, j: (tile(i, j), jnp.where(i <= last, j, PROJ_STEPS - 1))),
            pl.BlockSpec((tm // chunk, 2 * M_HEADS, chunk), lambda i, j: (tile(i, j), 0, 0)),
            pl.BlockSpec((CONV_SUB, CONV_CH), lambda i, j: (jnp.maximum((i - 1) * PROJ_STEPS + j, 0), 0)),
            pl.BlockSpec((1, CONV_STATE, CONV_CH), lambda i, j: (seq(i, j), 0, 0)),
        ],
        out_shape=[
            jax.ShapeDtypeStruct((n, 2 * QK_W + 2 * M_WIDTH), BF16),
            jax.ShapeDtypeStruct((n // chunk, 2 * M_HEADS, chunk), F32),
            jax.ShapeDtypeStruct((n, CONV_CH), BF16),
            jax.ShapeDtypeStruct((bsz, CONV_STATE, CONV_CH), F32),
        ],
        scratch_shapes=[
            pltpu.VMEM((tm, D_MODEL), BF16),
            pltpu.VMEM((2, PROJ_STEPS, tm, GLU_TN), BF16),
            pltpu.VMEM((CONV_PAD, CONV_CH), F32),
            pltpu.VMEM((1, CONV_SUB + CONV_PAD + SUBLANES, CONV_CH), F32),
            pltpu.VMEM((SUBLANES - 1, CONV_SUB + CONV_PAD, CONV_CH), F32),
            pltpu.VMEM((CONV_SUB, CONV_CH), F32),
        ],
        compiler_params=_cparams(("arbitrary", "arbitrary"), 56),
        name="proj_in_conv",
    )(x, g, w_main, wg, bg, state_pad, conv_w, conv_b, ln_g, ln_b)


def _conv_stages(e_ref, er_ref, c_ref, w_ref, cb_ref, lg_ref, lb_ref, out_ref, b, rows):
    ext = rows + CONV_PAD
    base = CONV_PAD - CONV_STATE

    def prepare():
        for r in range(1, SUBLANES):
            er_ref[r - 1] = e_ref[b, r:r + ext, :]

    def row_block(r0):
        for lc in range(CONV_CH // CONV_LC):
            ls = slice(lc * CONV_LC, (lc + 1) * CONV_LC)
            acc = jnp.zeros((CONV_RB, CONV_LC), F32) + cb_ref[:, ls]
            for j in range(CONV_WIDTH):
                off = j + base
                a, r = (off // SUBLANES) * SUBLANES, off % SUBLANES
                if r == 0:
                    tap = e_ref[b, pl.ds(r0 + a, CONV_RB), ls]
                else:
                    tap = er_ref[r - 1, pl.ds(r0 + a, CONV_RB), ls]
                acc = acc + w_ref[j:j + 1, ls] * tap
            c_ref[pl.ds(r0, CONV_RB), ls] = acc
        c = c_ref[pl.ds(r0, CONV_RB), :]
        mu = jnp.mean(c, axis=-1, keepdims=True)
        cc = c - mu
        var = jnp.mean(cc * cc, axis=-1, keepdims=True)
        cn = cc * lax.rsqrt(var + EPS) * lg_ref[...] + lb_ref[...]
        out_ref[pl.ds(r0, CONV_RB), :] = (cn * _sigmoid(cn)).astype(BF16)

    return prepare, row_block


def _conv_rows(e_ref, er_ref, c_ref, w_ref, cb_ref, lg_ref, lb_ref, out_ref, b, rows):
    prepare, row_block = _conv_stages(e_ref, er_ref, c_ref, w_ref, cb_ref, lg_ref, lb_ref, out_ref, b, rows)
    prepare()

    def body(rb, carry):
        row_block(pl.multiple_of(rb * CONV_RB, CONV_RB))
        return carry

    lax.fori_loop(0, rows // CONV_RB, body, 0)


def _conv_kernel(u_ref, st_ref, w_ref, cb_ref, lg_ref, lb_ref, cv_ref, ns_ref, e_ref, er_ref, c_ref, *, bb, tt):
    t = pl.program_id(1)
    nt = pl.num_programs(1)
    ext = tt + CONV_PAD

    @pl.when(t == 0)
    def _():
        e_ref[:, 0:CONV_PAD, :] = st_ref[...]

    @pl.when(t > 0)
    def _():
        e_ref[:, 0:CONV_PAD, :] = e_ref[:, tt:ext, :]

    e_ref[:, CONV_PAD:ext, :] = u_ref[...].astype(F32)
    e_ref[:, ext:, :] = jnp.zeros((bb, SUBLANES, CONV_CH), F32)

    for b in range(bb):
        _conv_rows(e_ref, er_ref, c_ref, w_ref, cb_ref, lg_ref, lb_ref, cv_ref.at[b], b, tt)

    @pl.when(t == nt - 1)
    def _():
        ns_ref[...] = e_ref[:, ext - CONV_STATE:ext, :]


def _conv(u, state_pad, conv_w, conv_b, ln_g, ln_b, bb, tt):
    bsz, t, _ = u.shape
    return pl.pallas_call(
        functools.partial(_conv_kernel, bb=bb, tt=tt),
        grid=(bsz // bb, t // tt),
        in_specs=[
            pl.BlockSpec((bb, tt, CONV_CH), lambda b, s: (b, s, 0)),
            pl.BlockSpec((bb, CONV_PAD, CONV_CH), lambda b, s: (b, 0, 0)),
            pl.BlockSpec((CONV_WIDTH, CONV_CH), lambda b, s: (0, 0)),
            pl.BlockSpec((1, CONV_CH), lambda b, s: (0, 0)),
            pl.BlockSpec((1, CONV_CH), lambda b, s: (0, 0)),
            pl.BlockSpec((1, CONV_CH), lambda b, s: (0, 0)),
        ],
        out_specs=[
            pl.BlockSpec((bb, tt, CONV_CH), lambda b, s: (b, s, 0)),
            pl.BlockSpec((bb, CONV_STATE, CONV_CH), lambda b, s: (b, 0, 0)),
        ],
        out_shape=[
            jax.ShapeDtypeStruct((bsz, t, CONV_CH), BF16),
            jax.ShapeDtypeStruct((bsz, CONV_STATE, CONV_CH), F32),
        ],
        scratch_shapes=[
            pltpu.VMEM((bb, tt + CONV_PAD + SUBLANES, CONV_CH), F32),
            pltpu.VMEM((SUBLANES - 1, tt + CONV_PAD, CONV_CH), F32),
            pltpu.VMEM((tt, CONV_CH), F32),
        ],
        compiler_params=_cparams(("parallel", "arbitrary"), 40),
        name="conv",
    )(u, state_pad, conv_w, conv_b, ln_g, ln_b)


def _mlstm_kernel(q_ref, k_ref, v_ref, o_ref, gt_ref, c0_ref, n0_ref, m0_ref, ng_ref,
                  hm_ref, cn_ref, nn_ref, mn_ref, c_s, n_s, m_s, *, bb, L):
    ci = pl.program_id(1)
    nc = pl.num_programs(1)

    @pl.when(ci == 0)
    def _():
        c_s[...] = c0_ref[...]
        n_s[...] = n0_ref[...]
        m_s[...] = m0_ref[...]

    rt = lax.broadcasted_iota(jnp.int32, (L, L), 0)
    cs = lax.broadcasted_iota(jnp.int32, (L, L), 1)
    causal = cs <= rt
    lower = jnp.where(causal, 1.0, 0.0).astype(BF16)
    upper = jnp.where(rt <= cs, 1.0, 0.0).astype(BF16)
    nt_dims = (((1,), (1,)), ((), ()))
    scale = M_DK ** -0.5

    for b in range(bb):
        rows = gt_ref[b]
        cum_rows = jnp.zeros((2 * M_HEADS, L), F32)
        cum_cols = jnp.zeros((L, 2 * M_HEADS), F32)
        rest = rows
        for _ in range(3):
            piece = rest.astype(BF16)
            rest = rest - piece.astype(F32)
            cum_rows = cum_rows + jnp.dot(piece, upper, preferred_element_type=F32)
            cum_cols = cum_cols + lax.dot_general(lower, piece, nt_dims, preferred_element_type=F32)
        heads = range(M_HEADS)
        ig_row = [rows[h:h + 1, :] for h in heads]
        b_row = [cum_rows[M_HEADS + h:M_HEADS + h + 1, :] for h in heads]
        b_col = [cum_cols[:, M_HEADS + h:M_HEADS + h + 1] for h in heads]
        m0 = [m_s[b, h:h + 1, 0:1] for h in heads]
        q = [q_ref[b, :, h * M_DK:(h + 1) * M_DK] for h in heads]
        k = [k_ref[b, :, h * M_DK:(h + 1) * M_DK] for h in heads]
        v = [v_ref[b, :, h * M_DV:(h + 1) * M_DV] for h in heads]
        c0 = [c_s[b, h] for h in heads]
        n0 = [n_s[b, h:h + 1, :] for h in heads]

        g_col = [b_col[h] + m0[h] for h in heads]
        dmat = [jnp.where(causal, b_col[h] - b_row[h] + ig_row[h], -jnp.inf) for h in heads]
        m_col = [jnp.maximum(g_col[h], jnp.max(dmat[h], axis=1, keepdims=True)) for h in heads]
        w_state = [jnp.exp(g_col[h] - m_col[h]) * scale for h in heads]
        decay = [jnp.exp(dmat[h] - m_col[h]) * scale for h in heads]
        qk = [lax.dot_general(q[h], k[h], nt_dims, preferred_element_type=F32) for h in heads]
        s = [qk[h] * decay[h] for h in heads]
        qc = [jnp.dot(q[h], c0[h].astype(BF16), preferred_element_type=F32) for h in heads]
        num = [jnp.dot(s[h].astype(BF16), v[h], preferred_element_type=F32) + w_state[h] * qc[h] for h in heads]
        qn = [jnp.sum(q[h].astype(F32) * n0[h], axis=1, keepdims=True) for h in heads]
        den = [jnp.sum(s[h], axis=1, keepdims=True) + w_state[h] * qn[h] for h in heads]
        hh = [num[h] / jnp.maximum(jnp.abs(den[h]), jnp.exp(-m_col[h])) for h in heads]
        for h in heads:
            hn = hh[h] * lax.rsqrt(jnp.mean(hh[h] * hh[h], axis=-1, keepdims=True) + EPS) * ng_ref[h:h + 1, :]
            og = _sigmoid(o_ref[b, :, h * M_DV:(h + 1) * M_DV].astype(F32))
            hm_ref[b, :, h * M_DV:(h + 1) * M_DV] = (og * hn).astype(BF16)

        b_last = [b_row[h][:, L - 1:L] for h in heads]
        g_last = [b_last[h] + m0[h] for h in heads]
        wk_row = [b_last[h] - b_row[h] + ig_row[h] for h in heads]
        m_new = [jnp.maximum(g_last[h], jnp.max(wk_row[h], axis=1, keepdims=True)) for h in heads]
        a0 = [jnp.exp(g_last[h] - m_new[h]) for h in heads]
        ak_row = [jnp.exp(wk_row[h] - m_new[h]) for h in heads]
        kts = [(k[h].T.astype(F32) * ak_row[h]).astype(BF16) for h in heads]
        kv = [jnp.dot(kts[h], v[h], preferred_element_type=F32) for h in heads]
        kn = [jnp.dot(ak_row[h].astype(BF16), k[h], preferred_element_type=F32) for h in heads]
        for h in heads:
            c_s[b, h] = a0[h] * c0[h] + kv[h]
            n_s[b, h:h + 1, :] = a0[h] * n0[h] + kn[h]
            m_s[b, h:h + 1, :] = jnp.broadcast_to(m_new[h], (1, LANES))

    @pl.when(ci == nc - 1)
    def _():
        cn_ref[...] = c_s[...]
        nn_ref[...] = n_s[...]
        mn_ref[...] = m_s[...]


def _mlstm(qkvo, gates, c0, n0, m0, ng, bb, L):
    bsz, t, _ = qkvo.shape
    nc = t // L
    gates = gates.reshape(bsz, nc, 2 * M_HEADS, L)
    st_c = pl.BlockSpec((bb, M_HEADS, M_DK, M_DV), lambda b, c: (b, 0, 0, 0))
    st_n = pl.BlockSpec((bb, M_HEADS, M_DK), lambda b, c: (b, 0, 0))
    return pl.pallas_call(
        functools.partial(_mlstm_kernel, bb=bb, L=L),
        grid=(bsz // bb, nc),
        in_specs=[
            pl.BlockSpec((bb, L, QK_W), lambda b, c: (b, c, 0)),
            pl.BlockSpec((bb, L, QK_W), lambda b, c: (b, c, 1)),
            pl.BlockSpec((bb, L, M_WIDTH), lambda b, c: (b, c, 1)),
            pl.BlockSpec((bb, L, M_WIDTH), lambda b, c: (b, c, 2)),
            pl.BlockSpec((bb, None, 2 * M_HEADS, L), lambda b, c: (b, c, 0, 0)),
            st_c, st_n, st_n,
            pl.BlockSpec((M_HEADS, M_DV), lambda b, c: (0, 0)),
        ],
        out_specs=[
            pl.BlockSpec((bb, L, M_WIDTH), lambda b, c: (b, c, 0)),
            st_c, st_n, st_n,
        ],
        out_shape=[
            jax.ShapeDtypeStruct((bsz, t, M_WIDTH), BF16),
            jax.ShapeDtypeStruct((bsz, M_HEADS, M_DK, M_DV), F32),
            jax.ShapeDtypeStruct((bsz, M_HEADS, M_DK), F32),
            jax.ShapeDtypeStruct((bsz, M_HEADS, LANES), F32),
        ],
        scratch_shapes=[
            pltpu.VMEM((bb, M_HEADS, M_DK, M_DV), F32),
            pltpu.VMEM((bb, M_HEADS, M_DK), F32),
            pltpu.VMEM((bb, M_HEADS, LANES), F32),
        ],
        compiler_params=_cparams(("parallel", "arbitrary"), 40),
        name="mlstm",
    )(qkvo, qkvo, qkvo, qkvo, gates, c0, n0, m0, ng)


def _proj_out_kernel(x_ref, hm_ref, cv_ref, wa_ref, wb_ref, o_ref):
    acc = jnp.dot(hm_ref[...], wa_ref[...], preferred_element_type=F32)
    acc = acc + jnp.dot(cv_ref[...], wb_ref[...], preferred_element_type=F32)
    o_ref[...] = x_ref[...] + acc


def _proj_out(x, hm, cv, wa, wb):
    n = x.shape[0]
    tm = min(TOK_TILE, n)
    return pl.pallas_call(
        _proj_out_kernel,
        grid=(n // tm,),
        in_specs=[
            pl.BlockSpec((tm, D_MODEL), lambda i: (i, 0)),
            pl.BlockSpec((tm, M_WIDTH), lambda i: (i, 0)),
            pl.BlockSpec((tm, CONV_CH), lambda i: (i, 0)),
            pl.BlockSpec((M_WIDTH, D_MODEL), lambda i: (0, 0)),
            pl.BlockSpec((CONV_CH, D_MODEL), lambda i: (0, 0)),
        ],
        out_specs=pl.BlockSpec((tm, D_MODEL), lambda i: (i, 0)),
        out_shape=jax.ShapeDtypeStruct((n, D_MODEL), F32),
        compiler_params=_cparams(("parallel",), 48),
        name="proj_out",
    )(x, hm, cv, wa, wb)


def _mem_kv_kernel(m_ref, g_ref, w_ref, k_ref, v_ref):
    mn = _rms(m_ref[...], g_ref[...]).astype(BF16)
    z = jnp.dot(mn, w_ref[...], preferred_element_type=F32)
    k_ref[...] = z[:, :CA_W]
    v_ref[...] = z[:, CA_W:]


def _mem_kv(mem, g, wkv):
    n = mem.shape[0]
    return pl.pallas_call(
        _mem_kv_kernel,
        grid=(1,),
        in_specs=[
            pl.BlockSpec((n, D_MODEL), lambda i: (0, 0)),
            pl.BlockSpec((1, D_MODEL), lambda i: (0, 0)),
            pl.BlockSpec((D_MODEL, 2 * CA_W), lambda i: (0, 0)),
        ],
        out_specs=[pl.BlockSpec((n, CA_W), lambda i: (0, 0))] * 2,
        out_shape=[jax.ShapeDtypeStruct((n, CA_W), F32)] * 2,
        compiler_params=_cparams(("arbitrary",), 40),
        name="mem_kv",
    )(mem, g, wkv)


def _attn_router_kernel(x_ref, gca_ref, wq_ref, k_ref, v_ref, wo_ref, gmoe_ref, rw_ref, rb_ref, cnt0_ref,
                        hn_in_ref, x2_ref, hn_ref, te_ref, tg_ref, tr_ref, cnt_ref, o_s, cnt_s, *, bb, tt):
    del hn_in_ref
    rows = bb * tt

    @pl.when((pl.program_id(0) == 0) & (pl.program_id(1) == 0))
    def _():
        cnt_s[...] = cnt0_ref[...]

    parts = range(ATTN_SPLIT)
    pr = rows // ATTN_SPLIT
    bpp = bb // ATTN_SPLIT

    def part_ref(ref, i):
        return ref.at[0, i * pr:(i + 1) * pr, :] if bb == 1 else ref.at[i * bpp:(i + 1) * bpp]

    seqs = [[(0, pr, 0)] if bb == 1 else [(j * tt, (j + 1) * tt, i * bpp + j) for j in range(bpp)] for i in parts]

    x = [part_ref(x_ref, i)[...].reshape(pr, D_MODEL) for i in parts]
    h = [_rms(x[i], gca_ref[...]).astype(BF16) for i in parts]
    q = [(jnp.dot(h[i], wq_ref[...], preferred_element_type=F32) * (CA_DH ** -0.5)).astype(BF16) for i in parts]
    units = [(i, r0, r1, b, slice(hd * CA_DH, (hd + 1) * CA_DH))
             for i in parts for r0, r1, b in seqs[i] for hd in range(CA_HEADS)]
    nt_dims = (((1,), (1,)), ((), ()))
    s = [lax.dot_general(q[i][r0:r1, cols], k_ref[b, :, cols], nt_dims, preferred_element_type=F32)
         for i, r0, r1, b, cols in units]
    e = [jnp.exp(si - jnp.max(si, axis=-1, keepdims=True)) for si in s]
    ov = [jnp.dot(ei.astype(BF16), v_ref[b, :, cols], preferred_element_type=F32)
          for ei, (i, r0, r1, b, cols) in zip(e, units)]
    for ei, ovi, (i, r0, r1, b, cols) in zip(e, ov, units):
        o_s[i * pr + r0:i * pr + r1, cols] = (ovi / jnp.sum(ei, axis=-1, keepdims=True)).astype(BF16)
    x2 = [x[i] + jnp.dot(o_s[i * pr:(i + 1) * pr, :], wo_ref[...], preferred_element_type=F32) for i in parts]
    for i in parts:
        out = part_ref(x2_ref, i)
        out[...] = x2[i].reshape(out.shape)

    hn = [_rms(x2[i], gmoe_ref[...]) for i in parts]
    for i in parts:
        hn_ref[i * pr:(i + 1) * pr, :] = _pack_pairs(hn[i][:, :HALF_D], hn[i][:, HALF_D:])
    work = [jnp.dot(hn[i].astype(BF16), rw_ref[...], preferred_element_type=F32) + rb_ref[...] for i in parts]
    lane = lax.broadcasted_iota(jnp.int32, (pr, LANES), 1)
    lane_f = lane.astype(F32)
    sel = [[] for _ in parts]
    vals = [[] for _ in parts]
    for _ in range(TOP_K):
        mx = [jnp.max(work[i], axis=-1, keepdims=True) for i in parts]
        idx = [jnp.min(jnp.where(work[i] == mx[i], lane_f, float(LANES)), axis=-1, keepdims=True) for i in parts]
        hit = [lane_f == idx[i] for i in parts]
        work = [jnp.where(hit[i], -jnp.inf, work[i]) for i in parts]
        for i in parts:
            sel[i].append((idx[i], hit[i]))
            vals[i].append(mx[i])
    ex = [[jnp.exp(vv - vals[i][0]) for vv in vals[i]] for i in parts]
    tot = [ex[i][0] + ex[i][1] + ex[i][2] + ex[i][3] for i in parts]
    assigned = []
    for i in parts:
        a = jnp.zeros((pr, LANES), F32)
        for _, hit_k in sel[i]:
            a = a + hit_k.astype(F32)
        assigned.append(a)
    rt = lax.broadcasted_iota(jnp.int32, (pr, pr), 0)
    cs = lax.broadcasted_iota(jnp.int32, (pr, pr), 1)
    strict = (cs < rt).astype(BF16)
    inside = [jnp.dot(strict, assigned[i].astype(BF16), preferred_element_type=F32) for i in parts]
    count = cnt_s[...]
    for i in parts:
        before = inside[i] + count
        count = count + jnp.sum(assigned[i], axis=0, keepdims=True)
        te = jnp.zeros((pr, LANES), F32)
        tg = jnp.zeros((pr, LANES), F32)
        tr = jnp.zeros((pr, LANES), F32)
        for kk, (idx_k, hit_k) in enumerate(sel[i]):
            rank = jnp.sum(jnp.where(hit_k, before, 0.0), axis=-1, keepdims=True)
            te = jnp.where(lane == kk, idx_k, te)
            tg = jnp.where(lane == kk, ex[i][kk] / tot[i], tg)
            tr = jnp.where(lane == kk, rank, tr)
        te_ref[i * pr:(i + 1) * pr, :] = te.astype(jnp.int32)
        tg_ref[i * pr:(i + 1) * pr, :] = tg
        tr_ref[i * pr:(i + 1) * pr, :] = tr.astype(jnp.int32)
    cnt_s[...] = count
    cnt_ref[...] = count


def _attn_router(x1, mem_k, mem_v, cnt0, hn_all, row0, p, bb, tt):
    bsz, t, _ = x1.shape
    rows = bb * tt
    n = bsz * t
    nb, nt = bsz // bb, t // tt
    blk0 = row0 // rows
    tok = lambda b, s: (b * nt + s, 0)
    const = lambda b, s: (0, 0)
    grid_spec = dict(
        grid=(nb, nt),
        in_specs=[
            pl.BlockSpec((bb, tt, D_MODEL), lambda b, s: (b, s, 0)),
            pl.BlockSpec((1, D_MODEL), const),
            pl.BlockSpec((D_MODEL, CA_W), const),
            pl.BlockSpec((bb, N_MEM, CA_W), lambda b, s: (b, 0, 0)),
            pl.BlockSpec((bb, N_MEM, CA_W), lambda b, s: (b, 0, 0)),
            pl.BlockSpec((CA_W, D_MODEL), const),
            pl.BlockSpec((1, D_MODEL), const),
            pl.BlockSpec((D_MODEL, LANES), const),
            pl.BlockSpec((1, LANES), const),
            pl.BlockSpec((1, LANES), const),
            pl.BlockSpec(memory_space=pl.ANY),
        ],
        out_specs=[
            pl.BlockSpec((bb, tt, D_MODEL), lambda b, s: (b, s, 0)),
            pl.BlockSpec((rows, HALF_D), lambda b, s: (blk0 + b * nt + s, 0)),
            pl.BlockSpec((rows, LANES), tok),
            pl.BlockSpec((rows, LANES), tok),
            pl.BlockSpec((rows, LANES), tok),
            pl.BlockSpec((1, LANES), const),
        ],
    )
    return pl.pallas_call(
        functools.partial(_attn_router_kernel, bb=bb, tt=tt),
        out_shape=[
            jax.ShapeDtypeStruct((bsz, t, D_MODEL), F32),
            jax.ShapeDtypeStruct(hn_all.shape, jnp.uint32),
            jax.ShapeDtypeStruct((n, LANES), jnp.int32),
            jax.ShapeDtypeStruct((n, LANES), F32),
            jax.ShapeDtypeStruct((n, LANES), jnp.int32),
            jax.ShapeDtypeStruct((1, LANES), F32),
        ],
        scratch_shapes=[pltpu.VMEM((rows, CA_W), BF16), pltpu.VMEM((1, LANES), F32)],
        input_output_aliases={10: 1},
        compiler_params=_cparams(("arbitrary", "arbitrary"), 48),
        name="attn_router",
        **grid_spec,
    )(x1, p["g_ca"], p["wq"], mem_k, mem_v, p["wo"], p["g_moe"], p["router_w"], p["router_b"], cnt0, hn_all)


def _sc_scatter_rows(rows, dest3, n_out):
    n, d = rows.shape
    nwin = dest3.shape[0]
    mesh = plsc.VectorSubcoreMesh(core_axis_name="c", subcore_axis_name="s")
    workers = mesh.num_cores * mesh.num_subcores
    assert nwin * SC_WIN == n and nwin % workers == 0, (n, nwin, workers)
    per = nwin // workers

    @functools.partial(
        pl.kernel, out_type=jax.ShapeDtypeStruct((n_out, d), rows.dtype), mesh=mesh,
        scratch_types=[pltpu.VMEM((SC_WIN, d), rows.dtype), pltpu.VMEM((TOP_K, SC_WIN), jnp.int32),
                       pltpu.SemaphoreType.DMA])
    def scatter(x_hbm, i_hbm, o_hbm, buf, idx, sem):
        wid = lax.axis_index("s") * mesh.num_cores + lax.axis_index("c")

        @pl.loop(0, per)
        def _(j):
            w = wid * per + j
            pltpu.sync_copy(x_hbm.at[pl.ds(w * SC_WIN, SC_WIN)], buf)
            pltpu.sync_copy(i_hbm.at[w], idx)
            copies = [pltpu.async_copy(buf, o_hbm.at[idx.at[kk]], sem) for kk in range(TOP_K)]
            for cp in copies:
                cp.wait()

    return scatter(rows, dest3)


def _sc_gather_rows(y, dest3):
    d = y.shape[1]
    nwin = dest3.shape[0]
    n = nwin * SC_GWIN
    mesh = plsc.VectorSubcoreMesh(core_axis_name="c", subcore_axis_name="s")
    workers = mesh.num_cores * mesh.num_subcores
    assert nwin % (2 * workers) == 0, (nwin, workers)
    per = nwin // workers

    @functools.partial(
        pl.kernel, out_type=jax.ShapeDtypeStruct((TOP_K, n, d), y.dtype), mesh=mesh,
        scratch_types=[pltpu.VMEM((2 * TOP_K, SC_GWIN, d), y.dtype), pltpu.VMEM((2, TOP_K, SC_GWIN), jnp.int32),
                       pltpu.SemaphoreType.DMA, pltpu.SemaphoreType.DMA, pltpu.SemaphoreType.DMA])
    def gather(y_hbm, i_hbm, o_hbm, buf, idx, sem_gather, sem_write, sem_idx):
        base = (lax.axis_index("s") * mesh.num_cores + lax.axis_index("c")) * per

        def index_copy(w, s):
            return pltpu.make_async_copy(i_hbm.at[w], idx.at[s], sem_idx)

        def write_copy(w, s, kk):
            return pltpu.make_async_copy(buf.at[s * TOP_K + kk], o_hbm.at[kk, pl.ds(w * SC_GWIN, SC_GWIN)], sem_write)

        index_copy(base, 0).start()

        @pl.loop(0, per // 2)
        def _(g):
            for s in range(2):
                w = base + 2 * g + s
                index_copy(w, s).wait()
                if s == 0:
                    index_copy(w + 1, 1).start()
                else:
                    @pl.when(g + 1 < per // 2)
                    def _():
                        index_copy(w + 1, 0).start()

                @pl.when(g > 0)
                def _():
                    for kk in range(TOP_K):
                        write_copy(w, s, kk).wait()

                gathers = [pltpu.make_async_copy(y_hbm.at[idx.at[s, kk]], buf.at[s * TOP_K + kk], sem_gather)
                           for kk in range(TOP_K)]
                for cp in gathers:
                    cp.start()
                for kk in range(TOP_K):
                    gathers[kk].wait()
                    write_copy(w, s, kk).start()

        for s in range(2):
            for kk in range(TOP_K):
                write_copy(base, s, kk).wait()

    return gather(y, dest3)


def _expert_changed(te_ref):
    t = pl.program_id(1)
    return (t == 0) | (te_ref[t] != te_ref[jnp.maximum(t - 1, 0)])


def _moe_up_kernel(te_ref, nu_ref, x_ref, w1g_ref, w1l_ref, b1g_ref, b1l_ref, act_ref, wg_s, wl_s):
    @pl.when(_expert_changed(te_ref))
    def _():
        wg_s[...] = w1g_ref[0].astype(BF16)
        wl_s[...] = w1l_ref[0].astype(BF16)

    def up(words):
        lo, hi = _unpack_pairs(words)
        x = jnp.concatenate([lo.astype(BF16), hi.astype(BF16)], axis=1)
        glu = jnp.dot(x, wg_s[...], preferred_element_type=F32) + b1g_ref[0]
        lin = jnp.dot(x, wl_s[...], preferred_element_type=F32) + b1l_ref[0]
        glu = jnp.minimum(glu, SWIGLU_LIMIT)
        lin = jnp.clip(lin, -SWIGLU_LIMIT, SWIGLU_LIMIT)
        act_ref[...] = (glu * _sigmoid(SWIGLU_ALPHA * glu) * (lin + 1.0)).astype(BF16)

    t = pl.program_id(1)
    used = t < nu_ref[0]
    valid = nu_ref[1 + t]

    @pl.when(used & (valid >= MOE_TM))
    def _():
        up(x_ref[...])

    @pl.when(used & (valid < MOE_TM))
    def _():
        rowid = lax.broadcasted_iota(jnp.int32, (MOE_TM, HALF_D), 0)
        up(jnp.where(rowid < valid, x_ref[...], jnp.uint32(0)))


def _moe_down_kernel(te_ref, nu_ref, a_ref, w2_ref, b2_ref, y_ref, w2_s):
    @pl.when(_expert_changed(te_ref))
    def _():
        w2_s[...] = w2_ref[0].astype(BF16)

    @pl.when(pl.program_id(1) < nu_ref[0])
    def _():
        y = jnp.dot(a_ref[...], w2_s[...], preferred_element_type=F32) + b2_ref[0]
        y_ref[...] = _pack_pairs(y[:, :MOE_TN // 2], y[:, MOE_TN // 2:])


def _moe(tile_e, n_used, x_sorted, w1, b1, w2, b2):
    r = x_sorted.shape[0]
    n_tiles = r // MOE_TM
    nf = D_FF // MOE_TF
    nn = D_MODEL // MOE_TN

    def row(c, t, te, nu):
        return jnp.maximum(jnp.minimum(t, nu[0] - 1), 0)

    act = pl.pallas_call(
        _moe_up_kernel,
        grid_spec=pltpu.PrefetchScalarGridSpec(
            num_scalar_prefetch=2,
            grid=(nf, n_tiles),
            in_specs=[
                pl.BlockSpec((MOE_TM, HALF_D), lambda c, t, te, nu: (row(c, t, te, nu), 0)),
                pl.BlockSpec((1, D_MODEL, MOE_TF), lambda c, t, te, nu: (te[t], 0, c)),
                pl.BlockSpec((1, D_MODEL, MOE_TF), lambda c, t, te, nu: (te[t], 0, nf + c)),
                pl.BlockSpec((1, 1, MOE_TF), lambda c, t, te, nu: (te[t], 0, c)),
                pl.BlockSpec((1, 1, MOE_TF), lambda c, t, te, nu: (te[t], 0, nf + c)),
            ],
            out_specs=pl.BlockSpec((MOE_TM, MOE_TF), lambda c, t, te, nu: (row(c, t, te, nu), c)),
            scratch_shapes=[pltpu.VMEM((D_MODEL, MOE_TF), BF16), pltpu.VMEM((D_MODEL, MOE_TF), BF16)],
        ),
        out_shape=jax.ShapeDtypeStruct((r, D_FF), BF16),
        compiler_params=_cparams(("arbitrary", "arbitrary"), MOE_VMEM_MB),
        name="moe_up",
    )(tile_e, n_used, x_sorted, w1, w1, b1, b1)

    return pl.pallas_call(
        _moe_down_kernel,
        grid_spec=pltpu.PrefetchScalarGridSpec(
            num_scalar_prefetch=2,
            grid=(nn, n_tiles),
            in_specs=[
                pl.BlockSpec((MOE_TM, D_FF), lambda c, t, te, nu: (row(c, t, te, nu), 0)),
                pl.BlockSpec((1, D_FF, MOE_TN), lambda c, t, te, nu: (te[t], 0, c)),
                pl.BlockSpec((1, 1, MOE_TN), lambda c, t, te, nu: (te[t], 0, c)),
            ],
            out_specs=pl.BlockSpec((MOE_TM, MOE_TN // 2), lambda c, t, te, nu: (row(c, t, te, nu), c)),
            scratch_shapes=[pltpu.VMEM((D_FF, MOE_TN), BF16)],
        ),
        out_shape=jax.ShapeDtypeStruct((r, HALF_D), jnp.uint32),
        compiler_params=_cparams(("arbitrary", "arbitrary"), MOE_VMEM_MB),
        name="moe_down",
    )(tile_e, n_used, act, w2, b2)


def _combine_kernel(x_ref, y0_ref, y1_ref, y2_ref, y3_ref, tg_ref, g_ref, o_ref):
    tg = tg_ref[...]
    acc = x_ref[...]
    half = MOE_TN // 2
    for kk, y_ref in enumerate((y0_ref, y1_ref, y2_ref, y3_ref)):
        cols = []
        for c in range(D_MODEL // MOE_TN):
            cols += list(_unpack_pairs(y_ref[0, :, c * half:(c + 1) * half]))
        acc = acc + jnp.concatenate(cols, axis=1) * tg[:, kk:kk + 1]
    o_ref[...] = _rms(acc, g_ref[...])


def _combine(x2, yg, tg, g, row0):
    n = x2.shape[0]
    tm = min(TOK_TILE, n)
    blk0 = row0 // tm
    yspec = [pl.BlockSpec((1, tm, HALF_D), functools.partial(lambda i, kk: (kk, blk0 + i, 0), kk=kk))
             for kk in range(TOP_K)]
    return pl.pallas_call(
        _combine_kernel,
        grid=(n // tm,),
        in_specs=[pl.BlockSpec((tm, D_MODEL), lambda i: (i, 0))] + yspec + [
            pl.BlockSpec((tm, LANES), lambda i: (i, 0)),
            pl.BlockSpec((1, D_MODEL), lambda i: (0, 0)),
        ],
        out_specs=pl.BlockSpec((tm, D_MODEL), lambda i: (i, 0)),
        out_shape=jax.ShapeDtypeStruct((n, D_MODEL), F32),
        compiler_params=_cparams(("parallel",), 48),
        name="combine",
    )(x2, yg, yg, yg, yg, tg, g)


def _group_tiles(bsz, t):
    tt = min(t, TOK_TILE)
    bb = max(1, min(bsz, TOK_TILE // tt))
    return bb, tt


def _layer_group(x, c0, n0, m0, conv_state, mem_k, mem_v, cnt0, hn_all, row0, p):
    bsz, t, _ = x.shape
    n = bsz * t
    L = min(MLSTM_CHUNK, t)
    state_pad = jnp.pad(conv_state, ((0, 0), (CONV_PAD - CONV_STATE, 0), (0, 0)))
    conv_args = (state_pad, p["conv_w"], p["conv_b"], p["ln_g"], p["ln_b"])
    if t % PROJ_TM == 0:
        qkvo, gates, cv, new_conv = _proj_in_conv(x.reshape(n, D_MODEL), p["g_mix"], p["w_main"], p["wg"], p["bg"],
                                                  L, *conv_args, seq_len=t)
    else:
        qkvo, u, gates = _proj_in(x.reshape(n, D_MODEL), p["g_mix"], p["w_main"], p["wg"], p["bg"], L)
        ctt = min(CONV_TT, t)
        cbb = max(1, min(bsz, CONV_TT // ctt))
        cv, new_conv = _conv(u.reshape(bsz, t, CONV_CH), *conv_args, cbb, ctt)

    mbb = min(bsz, MLSTM_BB)
    m0b = jnp.broadcast_to(m0[:, :, None], (bsz, M_HEADS, LANES))
    hm, c1, n1, m1 = _mlstm(qkvo.reshape(bsz, t, -1), gates, c0, n0, m0b, p["mh_g"], mbb, L)

    x1 = _proj_out(x.reshape(n, D_MODEL), hm.reshape(n, M_WIDTH), cv.reshape(n, CONV_CH), p["wo_a"], p["wo_b"])

    abb, att = _group_tiles(bsz, t)
    x2, hn_all, te, tg, tr, cnt = _attn_router(x1.reshape(bsz, t, D_MODEL), mem_k, mem_v, cnt0, hn_all, row0,
                                               p, abb, att)
    return x2.reshape(n, D_MODEL), hn_all, te, tg, tr, cnt, (c1, n1, m1[:, :, 0], new_conv)


def kernel(x_prompt, x_sample, mem_prompt, state_mlstm_C, state_mlstm_n, state_mlstm_m, state_conv, cache_mem_k, cache_mem_v, norm_mix_g, w_in, b_gates, mh_norm_g, conv_w, conv_b, conv_ln_g, conv_ln_b, w_out, norm_ca_g, norm_mem_g, ca_wq, ca_wk, ca_wv, ca_wo, norm_moe_g, router_w, router_b, moe_w1, moe_b1, moe_w2, moe_b2, final_norm_g):
    assert w_in.shape[0] == 1, "single layer"
    bp, tp, _ = x_prompt.shape
    bs, ts, _ = x_sample.shape
    n_p, n_s = bp * tp, bs * ts
    n_all = n_p + n_s

    wi = w_in[0]
    o0 = 0
    parts = []
    for width in (QK_W, QK_W, M_WIDTH, M_WIDTH, M_HEADS, M_HEADS, CONV_CH, CONV_CH):
        parts.append(wi[:, o0:o0 + width])
        o0 += width
    w_q, w_k, w_v, w_o, w_gi, w_gf, w_ua, w_ug = parts
    col_tiles = []
    w_qkvo = wi[:, :2 * QK_W + 2 * M_WIDTH]
    for c in range(PROJ_STEPS):
        col_tiles += [w_qkvo[:, c * QKVO_TN:(c + 1) * QKVO_TN], w_ua[:, c * GLU_TN:(c + 1) * GLU_TN],
                      w_ug[:, c * GLU_TN:(c + 1) * GLU_TN]]
    p = {
        "g_mix": norm_mix_g[0][None, :],
        "w_main": jnp.concatenate(col_tiles, axis=1).astype(BF16),
        "wg": jnp.concatenate([w_gi, w_gf], axis=1).T.astype(BF16),
        "bg": b_gates[0][:, None],
        "mh_g": mh_norm_g[0],
        "conv_w": conv_w[0],
        "conv_b": conv_b[0][None, :],
        "ln_g": conv_ln_g[0][None, :],
        "ln_b": conv_ln_b[0][None, :],
        "wo_a": w_out[0][:M_WIDTH].astype(BF16),
        "wo_b": w_out[0][M_WIDTH:].astype(BF16),
        "g_ca": norm_ca_g[0][None, :],
        "wq": ca_wq[0].astype(BF16),
        "wo": ca_wo[0].astype(BF16),
        "g_moe": norm_moe_g[0][None, :],
        "router_w": jnp.pad(router_w[0], ((0, 0), (0, LANES - N_EXPERTS))).astype(BF16),
        "router_b": jnp.concatenate([router_b[0], jnp.full((LANES - N_EXPERTS,), NEG_BIG, F32)])[None, :],
    }

    wkv = jnp.concatenate([ca_wk[0], ca_wv[0]], axis=1).astype(BF16)
    mk, mv = _mem_kv(mem_prompt.reshape(bp * N_MEM, D_MODEL), norm_mem_g[0][None, :], wkv)
    mk = mk.reshape(bp, N_MEM, CA_W)
    mv = mv.reshape(bp, N_MEM, CA_W)
    hn_all = jnp.zeros((n_all, HALF_D), jnp.uint32)
    cnt0 = jnp.zeros((1, LANES), F32)
    zc = jnp.zeros((bp, M_HEADS, M_DK, M_DV), F32)
    zn = jnp.zeros((bp, M_HEADS, M_DK), F32)
    zm = jnp.zeros((bp, M_HEADS), F32)
    zconv = jnp.zeros((bp, CONV_STATE, CONV_CH), F32)
    x2p, hn_all, te_p, tg_p, tr_p, cnt1, st_p = _layer_group(
        x_prompt, zc, zn, zm, zconv, mk.astype(BF16), mv.astype(BF16), cnt0, hn_all, 0, p)

    x2s, hn_all, te_s, tg_s, tr_s, cnt2, st_s = _layer_group(
        x_sample, state_mlstm_C[0], state_mlstm_n[0], state_mlstm_m[0], state_conv[0],
        cache_mem_k[0].reshape(bs, N_MEM, CA_W).astype(BF16), cache_mem_v[0].reshape(bs, N_MEM, CA_W).astype(BF16),
        cnt1, hn_all, n_p, p)

    te = jnp.concatenate([te_p[:, :TOP_K], te_s[:, :TOP_K]], axis=0)
    tr = jnp.concatenate([tr_p[:, :TOP_K], tr_s[:, :TOP_K]], axis=0)
    counts = cnt2[0, :N_EXPERTS].astype(jnp.int32)
    tiles_per_e = (counts + MOE_TM - 1) // MOE_TM
    tile_end = jnp.cumsum(tiles_per_e)
    row_start = (tile_end - tiles_per_e) * MOE_TM
    n_tiles = -(-(n_all * TOP_K) // MOE_TM) + N_EXPERTS
    n_used = tile_end[-1:]
    tile_ids = jnp.minimum(jnp.arange(n_tiles, dtype=jnp.int32), n_used[0] - 1)
    tile_e = jnp.minimum(jnp.sum(tile_ids[:, None] >= tile_end[None, :], axis=1), N_EXPERTS - 1).astype(jnp.int32)
    tile_first = jnp.arange(n_tiles, dtype=jnp.int32) - (tile_end - tiles_per_e)[tile_e]
    tile_valid = jnp.clip(counts[tile_e] - tile_first * MOE_TM, 0, MOE_TM)
    tile_valid = jnp.where(jnp.arange(n_tiles) < n_used[0], tile_valid, 0).astype(jnp.int32)
    moe_meta = jnp.concatenate([n_used.astype(jnp.int32), tile_valid])
    dest = row_start[te] + tr
    dest3 = dest.reshape(n_all // SC_WIN, SC_WIN, TOP_K).transpose(0, 2, 1)
    x_sorted = _sc_scatter_rows(hn_all, dest3, n_tiles * MOE_TM)

    y_sorted = _moe(tile_e, moe_meta, x_sorted, moe_w1[0], moe_b1[0][:, None, :], moe_w2[0],
                    moe_b2[0][:, None, :])

    yg = _sc_gather_rows(y_sorted, dest.reshape(n_all // SC_GWIN, SC_GWIN, TOP_K).transpose(0, 2, 1))
    fg = final_norm_g[None, :]
    y_prompt = _combine(x2p, yg, tg_p, fg, 0).reshape(bp, tp, D_MODEL)
    y_sample = _combine(x2s, yg, tg_s, fg, n_p).reshape(bs, ts, D_MODEL)

    c1, n1, m1, conv1 = st_p
    c2, n2, m2, conv2 = st_s
    mk4 = mk.reshape(1, bp, N_MEM, CA_HEADS, CA_DH)
    mv4 = mv.reshape(1, bp, N_MEM, CA_HEADS, CA_DH)
    return (y_prompt, y_sample, c1[None], n1[None], m1[None], conv1[None], mk4, mv4,
            c2[None], n2[None], m2[None], conv2[None])
```

```python
import functools

import jax
import jax.numpy as jnp
from jax import lax
from jax.experimental import pallas as pl
from jax.experimental.pallas import tpu as pltpu
from jax.experimental.pallas import tpu_sc as plsc

F32 = jnp.float32
BF16 = jnp.bfloat16

D_MODEL = 2048
HALF_D = D_MODEL // 2
M_HEADS = 4
M_DV = 256
M_DK = 128
QK_W = M_HEADS * M_DK
M_WIDTH = M_HEADS * M_DV
CONV_CH = 1024
CONV_WIDTH = 31
CONV_STATE = CONV_WIDTH - 1
CONV_PAD = 32
N_MEM = 256
CA_HEADS = 4
CA_DH = 128
CA_W = CA_HEADS * CA_DH
N_EXPERTS = 32
TOP_K = 4
D_FF = 2048
SWIGLU_ALPHA = 1.702
SWIGLU_LIMIT = 7.0
EPS = 1e-6
LANES = 128
SUBLANES = 8
NEG_BIG = -1e30

PROJ_TM = 1024
PROJ_STEPS = 4
QKVO_TN = (2 * QK_W + 2 * M_WIDTH) // PROJ_STEPS
GLU_TN = CONV_CH // PROJ_STEPS
PROJ_TN = QKVO_TN + 2 * GLU_TN
CONV_SUB = PROJ_TM // PROJ_STEPS
TOK_TILE = 512
ATTN_SPLIT = 2
MOE_TM = 512
MOE_TF = 1024
MOE_TN = 2048
MOE_VMEM_MB = 60
MLSTM_CHUNK = 256
MLSTM_BB = 2
CONV_TT = 256
CONV_RB = 32
CONV_LC = 512
SC_WIN = 32
SC_GWIN = 8


def _cparams(sem, vmem_mb):
    return pltpu.CompilerParams(dimension_semantics=sem, vmem_limit_bytes=vmem_mb << 20)


def _log_sigmoid(x):
    return jnp.minimum(x, 0.0) - jnp.log1p(jnp.exp(-jnp.abs(x)))


def _sigmoid(x):
    return 1.0 / (1.0 + jnp.exp(-x))


def _pack_pairs(lo, hi):
    return lax.bitcast_convert_type(pltpu.pack_elementwise([lo, hi], packed_dtype=BF16), jnp.uint32)


def _unpack_pairs(w):
    return tuple(pltpu.unpack_elementwise(w, index=i, packed_dtype=BF16, unpacked_dtype=F32) for i in range(2))


def _rms(x, g):
    ms = jnp.mean(x * x, axis=-1, keepdims=True)
    return x * lax.rsqrt(ms + EPS) * g


def _proj_prologue(x_ref, g_ref, wg_ref, bg_ref, h_ref, gates_ref, chunk):
    tm = x_ref.shape[0]
    h = _rms(x_ref[...], g_ref[...]).astype(BF16)
    h_ref[...] = h
    gt = lax.dot_general(wg_ref[...], h, (((1,), (1,)), ((), ())), preferred_element_type=F32)
    gt = gt + bg_ref[...]
    row = lax.broadcasted_iota(jnp.int32, gt.shape, 0)
    gt = jnp.where(row < M_HEADS, gt, _log_sigmoid(gt))
    for c in range(tm // chunk):
        gates_ref[c] = gt[:, c * chunk:(c + 1) * chunk]


def _proj_step(h_ref, w_ref):
    z = jnp.dot(h_ref[...], w_ref[...], preferred_element_type=F32)
    glu = z[:, QKVO_TN:QKVO_TN + GLU_TN] * _sigmoid(z[:, QKVO_TN + GLU_TN:])
    return z[:, :QKVO_TN].astype(BF16), glu.astype(BF16)


def _proj_in_kernel(x_ref, g_ref, w_ref, wg_ref, bg_ref, qkvo_ref, u_ref, gates_ref, h_ref, *, chunk):
    @pl.when(pl.program_id(1) == 0)
    def _():
        _proj_prologue(x_ref, g_ref, wg_ref, bg_ref, h_ref, gates_ref, chunk)

    qkvo_ref[...], u_ref[...] = _proj_step(h_ref, w_ref)


def _proj_in(x, g, w_main, wg, bg, chunk):
    n = x.shape[0]
    tm = min(PROJ_TM, n)
    return pl.pallas_call(
        functools.partial(_proj_in_kernel, chunk=chunk),
        grid=(n // tm, PROJ_STEPS),
        in_specs=[
            pl.BlockSpec((tm, D_MODEL), lambda i, j: (i, 0)),
            pl.BlockSpec((1, D_MODEL), lambda i, j: (0, 0)),
            pl.BlockSpec((D_MODEL, PROJ_TN), lambda i, j: (0, j)),
            pl.BlockSpec((2 * M_HEADS, D_MODEL), lambda i, j: (0, 0)),
            pl.BlockSpec((2 * M_HEADS, 1), lambda i, j: (0, 0)),
        ],
        out_specs=[
            pl.BlockSpec((tm, QKVO_TN), lambda i, j: (i, j)),
            pl.BlockSpec((tm, GLU_TN), lambda i, j: (i, j)),
            pl.BlockSpec((tm // chunk, 2 * M_HEADS, chunk), lambda i, j: (i, 0, 0)),
        ],
        out_shape=[
            jax.ShapeDtypeStruct((n, 2 * QK_W + 2 * M_WIDTH), BF16),
            jax.ShapeDtypeStruct((n, CONV_CH), BF16),
            jax.ShapeDtypeStruct((n // chunk, 2 * M_HEADS, chunk), F32),
        ],
        scratch_shapes=[pltpu.VMEM((tm, D_MODEL), BF16)],
        compiler_params=_cparams(("parallel", "arbitrary"), 48),
        name="proj_in",
    )(x, g, w_main, wg, bg)


def _proj_in_conv_kernel(x_ref, g_ref, w_ref, wg_ref, bg_ref, st_ref, cw_ref, cb_ref, lg_ref, lb_ref,
                         qkvo_ref, gates_ref, cv_ref, ns_ref, h_ref, u_s, hist_s, e_ref, er_ref, c_ref,
                         *, chunk, tiles_per_seq):
    i = pl.program_id(0)
    j = pl.program_id(1)
    p = i - 1
    sub = CONV_SUB
    ext = sub + CONV_PAD

    @pl.when((i == 0) & (j == 0))
    def _():
        u_s[...] = jnp.zeros(u_s.shape, BF16)
        hist_s[...] = jnp.zeros(hist_s.shape, F32)
        e_ref[:, ext:, :] = jnp.zeros((1, SUBLANES, CONV_CH), F32)

    @pl.when(j == 0)
    def _():
        _proj_prologue(x_ref, g_ref, wg_ref, bg_ref, h_ref, gates_ref, chunk)

    @pl.when((j == 0) & (p >= 0) & (p % tiles_per_seq == 0))
    def _():
        hist_s[...] = st_ref[0]

    r0 = pl.multiple_of(j * sub, sub)
    e_ref[0, 0:CONV_PAD, :] = hist_s[...]
    for c in range(PROJ_STEPS):
        e_ref[0, CONV_PAD:ext, c * GLU_TN:(c + 1) * GLU_TN] = u_s[(i + 1) % 2, c, pl.ds(r0, sub), :].astype(F32)
    prepare, row_block = _conv_stages(e_ref, er_ref, c_ref, cw_ref, cb_ref, lg_ref, lb_ref, cv_ref, 0, sub)
    prepare()
    for r0 in range(0, sub, CONV_RB):
        row_block(r0)
    hist_s[...] = e_ref[0, sub:ext, :]

    qkvo_ref[...], u_s[i % 2, j] = _proj_step(h_ref, w_ref)

    @pl.when((j == PROJ_STEPS - 1) & (p >= 0) & (p % tiles_per_seq == tiles_per_seq - 1))
    def _():
        ns_ref[0] = e_ref[0, ext - CONV_STATE:ext, :]


def _proj_in_conv(x, g, w_main, wg, bg, chunk, state_pad, conv_w, conv_b, ln_g, ln_b, seq_len):
    n = x.shape[0]
    tm = PROJ_TM
    n_tiles = n // tm
    tiles_per_seq = seq_len // tm
    bsz = n // seq_len
    assert tm == PROJ_STEPS * CONV_SUB and seq_len % tm == 0
    last = n_tiles - 1

    def tile(i, j):
        return jnp.minimum(i, last)

    def seq(i, j):
        return jnp.clip((i - 1) // tiles_per_seq, 0, bsz - 1)

    const2 = lambda i, j: (0, 0)
    return pl.pallas_call(
        functools.partial(_proj_in_conv_kernel, chunk=chunk, tiles_per_seq=tiles_per_seq),
        grid=(n_tiles + 1, PROJ_STEPS),
        in_specs=[
            pl.BlockSpec((tm, D_MODEL), lambda i, j: (tile(i, j), 0)),
            pl.BlockSpec((1, D_MODEL), const2),
            pl.BlockSpec((D_MODEL, PROJ_TN), lambda i, j: (0, j)),
            pl.BlockSpec((2 * M_HEADS, D_MODEL), const2),
            pl.BlockSpec((2 * M_HEADS, 1), const2),
            pl.BlockSpec((1, CONV_PAD, CONV_CH), lambda i, j: (seq(i, j), 0, 0)),
            pl.BlockSpec((CONV_WIDTH, CONV_CH), const2),
            pl.BlockSpec((1, CONV_CH), const2),
            pl.BlockSpec((1, CONV_CH), const2),
            pl.BlockSpec((1, CONV_CH), const2),
        ],
        out_specs=[
            pl.BlockSpec((tm, QKVO_TN), lambda i, j: (tile(i, j), jnp.where(i <= last, j, PROJ_STEPS - 1))),
            pl.BlockSpec((tm // chunk, 2 * M_HEADS, chunk), lambda i, j: (tile(i, j), 0, 0)),
            pl.BlockSpec((CONV_SUB, CONV_CH), lambda i, j: (jnp.maximum((i - 1) * PROJ_STEPS + j, 0), 0)),
            pl.BlockSpec((1, CONV_STATE, CONV_CH), lambda i, j: (seq(i, j), 0, 0)),
        ],
        out_shape=[
            jax.ShapeDtypeStruct((n, 2 * QK_W + 2 * M_WIDTH), BF16),
            jax.ShapeDtypeStruct((n // chunk, 2 * M_HEADS, chunk), F32),
            jax.ShapeDtypeStruct((n, CONV_CH), BF16),
            jax.ShapeDtypeStruct((bsz, CONV_STATE, CONV_CH), F32),
        ],
        scratch_shapes=[
            pltpu.VMEM((tm, D_MODEL), BF16),
            pltpu.VMEM((2, PROJ_STEPS, tm, GLU_TN), BF16),
            pltpu.VMEM((CONV_PAD, CONV_CH), F32),
            pltpu.VMEM((1, CONV_SUB + CONV_PAD + SUBLANES, CONV_CH), F32),
            pltpu.VMEM((SUBLANES - 1, CONV_SUB + CONV_PAD, CONV_CH), F32),
            pltpu.VMEM((CONV_SUB, CONV_CH), F32),
        ],
        compiler_params=_cparams(("arbitrary", "arbitrary"), 56),
        name="proj_in_conv",
    )(x, g, w_main, wg, bg, state_pad, conv_w, conv_b, ln_g, ln_b)


def _conv_stages(e_ref, er_ref, c_ref, w_ref, cb_ref, lg_ref, lb_ref, out_ref, b, rows):
    ext = rows + CONV_PAD
    base = CONV_PAD - CONV_STATE

    def prepare():
        for r in range(1, SUBLANES):
            er_ref[r - 1] = e_ref[b, r:r + ext, :]

    def row_block(r0):
        for lc in range(CONV_CH // CONV_LC):
            ls = slice(lc * CONV_LC, (lc + 1) * CONV_LC)
            acc = jnp.zeros((CONV_RB, CONV_LC), F32) + cb_ref[:, ls]
            for j in range(CONV_WIDTH):
                off = j + base
                a, r = (off // SUBLANES) * SUBLANES, off % SUBLANES
                if r == 0:
                    tap = e_ref[b, pl.ds(r0 + a, CONV_RB), ls]
                else:
                    tap = er_ref[r - 1, pl.ds(r0 + a, CONV_RB), ls]
                acc = acc + w_ref[j:j + 1, ls] * tap
            c_ref[pl.ds(r0, CONV_RB), ls] = acc
        c = c_ref[pl.ds(r0, CONV_RB), :]
        mu = jnp.mean(c, axis=-1, keepdims=True)
        cc = c - mu
        var = jnp.mean(cc * cc, axis=-1, keepdims=True)
        cn = cc * lax.rsqrt(var + EPS) * lg_ref[...] + lb_ref[...]
        out_ref[pl.ds(r0, CONV_RB), :] = (cn * _sigmoid(cn)).astype(BF16)

    return prepare, row_block


def _conv_rows(e_ref, er_ref, c_ref, w_ref, cb_ref, lg_ref, lb_ref, out_ref, b, rows):
    prepare, row_block = _conv_stages(e_ref, er_ref, c_ref, w_ref, cb_ref, lg_ref, lb_ref, out_ref, b, rows)
    prepare()

    def body(rb, carry):
        row_block(pl.multiple_of(rb * CONV_RB, CONV_RB))
        return carry

    lax.fori_loop(0, rows // CONV_RB, body, 0)


def _conv_kernel(u_ref, st_ref, w_ref, cb_ref, lg_ref, lb_ref, cv_ref, ns_ref, e_ref, er_ref, c_ref, *, bb, tt):
    t = pl.program_id(1)
    nt = pl.num_programs(1)
    ext = tt + CONV_PAD

    @pl.when(t == 0)
    def _():
        e_ref[:, 0:CONV_PAD, :] = st_ref[...]

    @pl.when(t > 0)
    def _():
        e_ref[:, 0:CONV_PAD, :] = e_ref[:, tt:ext, :]

    e_ref[:, CONV_PAD:ext, :] = u_ref[...].astype(F32)
    e_ref[:, ext:, :] = jnp.zeros((bb, SUBLANES, CONV_CH), F32)

    for b in range(bb):
        _conv_rows(e_ref, er_ref, c_ref, w_ref, cb_ref, lg_ref, lb_ref, cv_ref.at[b], b, tt)

    @pl.when(t == nt - 1)
    def _():
        ns_ref[...] = e_ref[:, ext - CONV_STATE:ext, :]


def _conv(u, state_pad, conv_w, conv_b, ln_g, ln_b, bb, tt):
    bsz, t, _ = u.shape
    return pl.pallas_call(
        functools.partial(_conv_kernel, bb=bb, tt=tt),
        grid=(bsz // bb, t // tt),
        in_specs=[
            pl.BlockSpec((bb, tt, CONV_CH), lambda b, s: (b, s, 0)),
            pl.BlockSpec((bb, CONV_PAD, CONV_CH), lambda b, s: (b, 0, 0)),
            pl.BlockSpec((CONV_WIDTH, CONV_CH), lambda b, s: (0, 0)),
            pl.BlockSpec((1, CONV_CH), lambda b, s: (0, 0)),
            pl.BlockSpec((1, CONV_CH), lambda b, s: (0, 0)),
            pl.BlockSpec((1, CONV_CH), lambda b, s: (0, 0)),
        ],
        out_specs=[
            pl.BlockSpec((bb, tt, CONV_CH), lambda b, s: (b, s, 0)),
            pl.BlockSpec((bb, CONV_STATE, CONV_CH), lambda b, s: (b, 0, 0)),
        ],
        out_shape=[
            jax.ShapeDtypeStruct((bsz, t, CONV_CH), BF16),
            jax.ShapeDtypeStruct((bsz, CONV_STATE, CONV_CH), F32),
        ],
        scratch_shapes=[
            pltpu.VMEM((bb, tt + CONV_PAD + SUBLANES, CONV_CH), F32),
            pltpu.VMEM((SUBLANES - 1, tt + CONV_PAD, CONV_CH), F32),
            pltpu.VMEM((tt, CONV_CH), F32),
        ],
        compiler_params=_cparams(("parallel", "arbitrary"), 40),
        name="conv",
    )(u, state_pad, conv_w, conv_b, ln_g, ln_b)


def _mlstm_kernel(q_ref, k_ref, v_ref, o_ref, gt_ref, c0_ref, n0_ref, m0_ref, ng_ref,
                  hm_ref, cn_ref, nn_ref, mn_ref, c_s, n_s, m_s, *, bb, L):
    ci = pl.program_id(1)
    nc = pl.num_programs(1)

    @pl.when(ci == 0)
    def _():
        c_s[...] = c0_ref[...]
        n_s[...] = n0_ref[...]
        m_s[...] = m0_ref[...]

    rt = lax.broadcasted_iota(jnp.int32, (L, L), 0)
    cs = lax.broadcasted_iota(jnp.int32, (L, L), 1)
    causal = cs <= rt
    lower = jnp.where(causal, 1.0, 0.0).astype(BF16)
    upper = jnp.where(rt <= cs, 1.0, 0.0).astype(BF16)
    nt_dims = (((1,), (1,)), ((), ()))
    scale = M_DK ** -0.5

    for b in range(bb):
        rows = gt_ref[b]
        cum_rows = jnp.zeros((2 * M_HEADS, L), F32)
        cum_cols = jnp.zeros((L, 2 * M_HEADS), F32)
        rest = rows
        for _ in range(3):
            piece = rest.astype(BF16)
            rest = rest - piece.astype(F32)
            cum_rows = cum_rows + jnp.dot(piece, upper, preferred_element_type=F32)
            cum_cols = cum_cols + lax.dot_general(lower, piece, nt_dims, preferred_element_type=F32)
        heads = range(M_HEADS)
        ig_row = [rows[h:h + 1, :] for h in heads]
        b_row = [cum_rows[M_HEADS + h:M_HEADS + h + 1, :] for h in heads]
        b_col = [cum_cols[:, M_HEADS + h:M_HEADS + h + 1] for h in heads]
        m0 = [m_s[b, h:h + 1, 0:1] for h in heads]
        q = [q_ref[b, :, h * M_DK:(h + 1) * M_DK] for h in heads]
        k = [k_ref[b, :, h * M_DK:(h + 1) * M_DK] for h in heads]
        v = [v_ref[b, :, h * M_DV:(h + 1) * M_DV] for h in heads]
        c0 = [c_s[b, h] for h in heads]
        n0 = [n_s[b, h:h + 1, :] for h in heads]

        g_col = [b_col[h] + m0[h] for h in heads]
        dmat = [jnp.where(causal, b_col[h] - b_row[h] + ig_row[h], -jnp.inf) for h in heads]
        m_col = [jnp.maximum(g_col[h], jnp.max(dmat[h], axis=1, keepdims=True)) for h in heads]
        w_state = [jnp.exp(g_col[h] - m_col[h]) * scale for h in heads]
        decay = [jnp.exp(dmat[h] - m_col[h]) * scale for h in heads]
        qk = [lax.dot_general(q[h], k[h], nt_dims, preferred_element_type=F32) for h in heads]
        s = [qk[h] * decay[h] for h in heads]
        qc = [jnp.dot(q[h], c0[h].astype(BF16), preferred_element_type=F32) for h in heads]
        num = [jnp.dot(s[h].astype(BF16), v[h], preferred_element_type=F32) + w_state[h] * qc[h] for h in heads]
        qn = [jnp.sum(q[h].astype(F32) * n0[h], axis=1, keepdims=True) for h in heads]
        den = [jnp.sum(s[h], axis=1, keepdims=True) + w_state[h] * qn[h] for h in heads]
        hh = [num[h] / jnp.maximum(jnp.abs(den[h]), jnp.exp(-m_col[h])) for h in heads]
        for h in heads:
            hn = hh[h] * lax.rsqrt(jnp.mean(hh[h] * hh[h], axis=-1, keepdims=True) + EPS) * ng_ref[h:h + 1, :]
            og = _sigmoid(o_ref[b, :, h * M_DV:(h + 1) * M_DV].astype(F32))
            hm_ref[b, :, h * M_DV:(h + 1) * M_DV] = (og * hn).astype(BF16)

        b_last = [b_row[h][:, L - 1:L] for h in heads]
        g_last = [b_last[h] + m0[h] for h in heads]
        wk_row = [b_last[h] - b_row[h] + ig_row[h] for h in heads]
        m_new = [jnp.maximum(g_last[h], jnp.max(wk_row[h], axis=1, keepdims=True)) for h in heads]
        a0 = [jnp.exp(g_last[h] - m_new[h]) for h in heads]
        ak_row = [jnp.exp(wk_row[h] - m_new[h]) for h in heads]
        kts = [(k[h].T.astype(F32) * ak_row[h]).astype(BF16) for h in heads]
        kv = [jnp.dot(kts[h], v[h], preferred_element_type=F32) for h in heads]
        kn = [jnp.dot(ak_row[h].astype(BF16), k[h], preferred_element_type=F32) for h in heads]
        for h in heads:
            c_s[b, h] = a0[h] * c0[h] + kv[h]
            n_s[b, h:h + 1, :] = a0[h] * n0[h] + kn[h]
            m_s[b, h:h + 1, :] = jnp.broadcast_to(m_new[h], (1, LANES))

    @pl.when(ci == nc - 1)
    def _():
        cn_ref[...] = c_s[...]
        nn_ref[...] = n_s[...]
        mn_ref[...] = m_s[...]


def _mlstm(qkvo, gates, c0, n0, m0, ng, bb, L):
    bsz, t, _ = qkvo.shape
    nc = t // L
    gates = gates.reshape(bsz, nc, 2 * M_HEADS, L)
    st_c = pl.BlockSpec((bb, M_HEADS, M_DK, M_DV), lambda b, c: (b, 0, 0, 0))
    st_n = pl.BlockSpec((bb, M_HEADS, M_DK), lambda b, c: (b, 0, 0))
    return pl.pallas_call(
        functools.partial(_mlstm_kernel, bb=bb, L=L),
        grid=(bsz // bb, nc),
        in_specs=[
            pl.BlockSpec((bb, L, QK_W), lambda b, c: (b, c, 0)),
            pl.BlockSpec((bb, L, QK_W), lambda b, c: (b, c, 1)),
            pl.BlockSpec((bb, L, M_WIDTH), lambda b, c: (b, c, 1)),
            pl.BlockSpec((bb, L, M_WIDTH), lambda b, c: (b, c, 2)),
            pl.BlockSpec((bb, None, 2 * M_HEADS, L), lambda b, c: (b, c, 0, 0)),
            st_c, st_n, st_n,
            pl.BlockSpec((M_HEADS, M_DV), lambda b, c: (0, 0)),
        ],
        out_specs=[
            pl.BlockSpec((bb, L, M_WIDTH), lambda b, c: (b, c, 0)),
            st_c, st_n, st_n,
        ],
        out_shape=[
            jax.ShapeDtypeStruct((bsz, t, M_WIDTH), BF16),
            jax.ShapeDtypeStruct((bsz, M_HEADS, M_DK, M_DV), F32),
            jax.ShapeDtypeStruct((bsz, M_HEADS, M_DK), F32),
            jax.ShapeDtypeStruct((bsz, M_HEADS, LANES), F32),
        ],
        scratch_shapes=[
            pltpu.VMEM((bb, M_HEADS, M_DK, M_DV), F32),
            pltpu.VMEM((bb, M_HEADS, M_DK), F32),
            pltpu.VMEM((bb, M_HEADS, LANES), F32),
        ],
        compiler_params=_cparams(("parallel", "arbitrary"), 40),
        name="mlstm",
    )(qkvo, qkvo, qkvo, qkvo, gates, c0, n0, m0, ng)


def _proj_out_kernel(x_ref, hm_ref, cv_ref, wa_ref, wb_ref, o_ref):
    acc = jnp.dot(hm_ref[...], wa_ref[...], preferred_element_type=F32)
    acc = acc + jnp.dot(cv_ref[...], wb_ref[...], preferred_element_type=F32)
    o_ref[...] = x_ref[...] + acc


def _proj_out(x, hm, cv, wa, wb):
    n = x.shape[0]
    tm = min(TOK_TILE, n)
    return pl.pallas_call(
        _proj_out_kernel,
        grid=(n // tm,),
        in_specs=[
            pl.BlockSpec((tm, D_MODEL), lambda i: (i, 0)),
            pl.BlockSpec((tm, M_WIDTH), lambda i: (i, 0)),
            pl.BlockSpec((tm, CONV_CH), lambda i: (i, 0)),
            pl.BlockSpec((M_WIDTH, D_MODEL), lambda i: (0, 0)),
            pl.BlockSpec((CONV_CH, D_MODEL), lambda i: (0, 0)),
        ],
        out_specs=pl.BlockSpec((tm, D_MODEL), lambda i: (i, 0)),
        out_shape=jax.ShapeDtypeStruct((n, D_MODEL), F32),
        compiler_params=_cparams(("parallel",), 48),
        name="proj_out",
    )(x, hm, cv, wa, wb)


def _mem_kv_kernel(m_ref, g_ref, w_ref, k_ref, v_ref):
    mn = _rms(m_ref[...], g_ref[...]).astype(BF16)
    z = jnp.dot(mn, w_ref[...], preferred_element_type=F32)
    k_ref[...] = z[:, :CA_W]
    v_ref[...] = z[:, CA_W:]


def _mem_kv(mem, g, wkv):
    n = mem.shape[0]
    return pl.pallas_call(
        _mem_kv_kernel,
        grid=(1,),
        in_specs=[
            pl.BlockSpec((n, D_MODEL), lambda i: (0, 0)),
            pl.BlockSpec((1, D_MODEL), lambda i: (0, 0)),
            pl.BlockSpec((D_MODEL, 2 * CA_W), lambda i: (0, 0)),
        ],
        out_specs=[pl.BlockSpec((n, CA_W), lambda i: (0, 0))] * 2,
        out_shape=[jax.ShapeDtypeStruct((n, CA_W), F32)] * 2,
        compiler_params=_cparams(("arbitrary",), 40),
        name="mem_kv",
    )(mem, g, wkv)


def _attn_router_kernel(x_ref, gca_ref, wq_ref, k_ref, v_ref, wo_ref, gmoe_ref, rw_ref, rb_ref, cnt0_ref,
                        hn_in_ref, x2_ref, hn_ref, te_ref, tg_ref, tr_ref, cnt_ref, o_s, cnt_s, *, bb, tt):
    del hn_in_ref
    rows = bb * tt

    @pl.when((pl.program_id(0) == 0) & (pl.program_id(1) == 0))
    def _():
        cnt_s[...] = cnt0_ref[...]

    parts = range(ATTN_SPLIT)
    pr = rows // ATTN_SPLIT
    bpp = bb // ATTN_SPLIT

    def part_ref(ref, i):
        return ref.at[0, i * pr:(i + 1) * pr, :] if bb == 1 else ref.at[i * bpp:(i + 1) * bpp]

    seqs = [[(0, pr, 0)] if bb == 1 else [(j * tt, (j + 1) * tt, i * bpp + j) for j in range(bpp)] for i in parts]

    x = [part_ref(x_ref, i)[...].reshape(pr, D_MODEL) for i in parts]
    h = [_rms(x[i], gca_ref[...]).astype(BF16) for i in parts]
    q = [(jnp.dot(h[i], wq_ref[...], preferred_element_type=F32) * (CA_DH ** -0.5)).astype(BF16) for i in parts]
    units = [(i, r0, r1, b, slice(hd * CA_DH, (hd + 1) * CA_DH))
             for i in parts for r0, r1, b in seqs[i] for hd in range(CA_HEADS)]
    nt_dims = (((1,), (1,)), ((), ()))
    s = [lax.dot_general(q[i][r0:r1, cols], k_ref[b, :, cols], nt_dims, preferred_element_type=F32)
         for i, r0, r1, b, cols in units]
    e = [jnp.exp(si - jnp.max(si, axis=-1, keepdims=True)) for si in s]
    ov = [jnp.dot(ei.astype(BF16), v_ref[b, :, cols], preferred_element_type=F32)
          for ei, (i, r0, r1, b, cols) in zip(e, units)]
    for ei, ovi, (i, r0, r1, b, cols) in zip(e, ov, units):
        o_s[i * pr + r0:i * pr + r1, cols] = (ovi / jnp.sum(ei, axis=-1, keepdims=True)).astype(BF16)
    x2 = [x[i] + jnp.dot(o_s[i * pr:(i + 1) * pr, :], wo_ref[...], preferred_element_type=F32) for i in parts]
    for i in parts:
        out = part_ref(x2_ref, i)
        out[...] = x2[i].reshape(out.shape)

    hn = [_rms(x2[i], gmoe_ref[...]) for i in parts]
    for i in parts:
        hn_ref[i * pr:(i + 1) * pr, :] = _pack_pairs(hn[i][:, :HALF_D], hn[i][:, HALF_D:])
    work = [jnp.dot(hn[i].astype(BF16), rw_ref[...], preferred_element_type=F32) + rb_ref[...] for i in parts]
    lane = lax.broadcasted_iota(jnp.int32, (pr, LANES), 1)
    lane_f = lane.astype(F32)
    sel = [[] for _ in parts]
    vals = [[] for _ in parts]
    for _ in range(TOP_K):
        mx = [jnp.max(work[i], axis=-1, keepdims=True) for i in parts]
        idx = [jnp.min(jnp.where(work[i] == mx[i], lane_f, float(LANES)), axis=-1, keepdims=True) for i in parts]
        hit = [lane_f == idx[i] for i in parts]
        work = [jnp.where(hit[i], -jnp.inf, work[i]) for i in parts]
        for i in parts:
            sel[i].append((idx[i], hit[i]))
            vals[i].append(mx[i])
    ex = [[jnp.exp(vv - vals[i][0]) for vv in vals[i]] for i in parts]
    tot = [ex[i][0] + ex[i][1] + ex[i][2] + ex[i][3] for i in parts]
    assigned = []
    for i in parts:
        a = jnp.zeros((pr, LANES), F32)
        for _, hit_k in sel[i]:
            a = a + hit_k.astype(F32)
        assigned.append(a)
    rt = lax.broadcasted_iota(jnp.int32, (pr, pr), 0)
    cs = lax.broadcasted_iota(jnp.int32, (pr, pr), 1)
    strict = (cs < rt).astype(BF16)
    inside = [jnp.dot(strict, assigned[i].astype(BF16), preferred_element_type=F32) for i in parts]
    count = cnt_s[...]
    for i in parts:
        before = inside[i] + count
        count = count + jnp.sum(assigned[i], axis=0, keepdims=True)
        te = jnp.zeros((pr, LANES), F32)
        tg = jnp.zeros((pr, LANES), F32)
        tr = jnp.zeros((pr, LANES), F32)
        for kk, (idx_k, hit_k) in enumerate(sel[i]):
            rank = jnp.sum(jnp.where(hit_k, before, 0.0), axis=-1, keepdims=True)
            te = jnp.where(lane == kk, idx_k, te)
            tg = jnp.where(lane == kk, ex[i][kk] / tot[i], tg)
            tr = jnp.where(lane == kk, rank, tr)
        te_ref[i * pr:(i + 1) * pr, :] = te.astype(jnp.int32)
        tg_ref[i * pr:(i + 1) * pr, :] = tg
        tr_ref[i * pr:(i + 1) * pr, :] = tr.astype(jnp.int32)
    cnt_s[...] = count
    cnt_ref[...] = count


def _attn_router(x1, mem_k, mem_v, cnt0, hn_all, row0, p, bb, tt):
    bsz, t, _ = x1.shape
    rows = bb * tt
    n = bsz * t
    nb, nt = bsz // bb, t // tt
    blk0 = row0 // rows
    tok = lambda b, s: (b * nt + s, 0)
    const = lambda b, s: (0, 0)
    grid_spec = dict(
        grid=(nb, nt),
        in_specs=[
            pl.BlockSpec((bb, tt, D_MODEL), lambda b, s: (b, s, 0)),
            pl.BlockSpec((1, D_MODEL), const),
            pl.BlockSpec((D_MODEL, CA_W), const),
            pl.BlockSpec((bb, N_MEM, CA_W), lambda b, s: (b, 0, 0)),
            pl.BlockSpec((bb, N_MEM, CA_W), lambda b, s: (b, 0, 0)),
            pl.BlockSpec((CA_W, D_MODEL), const),
            pl.BlockSpec((1, D_MODEL), const),
            pl.BlockSpec((D_MODEL, LANES), const),
            pl.BlockSpec((1, LANES), const),
            pl.BlockSpec((1, LANES), const),
            pl.BlockSpec(memory_space=pl.ANY),
        ],
        out_specs=[
            pl.BlockSpec((bb, tt, D_MODEL), lambda b, s: (b, s, 0)),
            pl.BlockSpec((rows, HALF_D), lambda b, s: (blk0 + b * nt + s, 0)),
            pl.BlockSpec((rows, LANES), tok),
            pl.BlockSpec((rows, LANES), tok),
            pl.BlockSpec((rows, LANES), tok),
            pl.BlockSpec((1, LANES), const),
        ],
    )
    return pl.pallas_call(
        functools.partial(_attn_router_kernel, bb=bb, tt=tt),
        out_shape=[
            jax.ShapeDtypeStruct((bsz, t, D_MODEL), F32),
            jax.ShapeDtypeStruct(hn_all.shape, jnp.uint32),
            jax.ShapeDtypeStruct((n, LANES), jnp.int32),
            jax.ShapeDtypeStruct((n, LANES), F32),
            jax.ShapeDtypeStruct((n, LANES), jnp.int32),
            jax.ShapeDtypeStruct((1, LANES), F32),
        ],
        scratch_shapes=[pltpu.VMEM((rows, CA_W), BF16), pltpu.VMEM((1, LANES), F32)],
        input_output_aliases={10: 1},
        compiler_params=_cparams(("arbitrary", "arbitrary"), 48),
        name="attn_router",
        **grid_spec,
    )(x1, p["g_ca"], p["wq"], mem_k, mem_v, p["wo"], p["g_moe"], p["router_w"], p["router_b"], cnt0, hn_all)


def _sc_scatter_rows(rows, dest3, n_out):
    n, d = rows.shape
    nwin = dest3.shape[0]
    mesh = plsc.VectorSubcoreMesh(core_axis_name="c", subcore_axis_name="s")
    workers = mesh.num_cores * mesh.num_subcores
    assert nwin * SC_WIN == n and nwin % workers == 0, (n, nwin, workers)
    per = nwin // workers

    @functools.partial(
        pl.kernel, out_type=jax.ShapeDtypeStruct((n_out, d), rows.dtype), mesh=mesh,
        scratch_types=[pltpu.VMEM((SC_WIN, d), rows.dtype), pltpu.VMEM((TOP_K, SC_WIN), jnp.int32),
                       pltpu.SemaphoreType.DMA])
    def scatter(x_hbm, i_hbm, o_hbm, buf, idx, sem):
        wid = lax.axis_index("s") * mesh.num_cores + lax.axis_index("c")

        @pl.loop(0, per)
        def _(j):
            w = wid * per + j
            pltpu.sync_copy(x_hbm.at[pl.ds(w * SC_WIN, SC_WIN)], buf)
            pltpu.sync_copy(i_hbm.at[w], idx)
            copies = [pltpu.async_copy(buf, o_hbm.at[idx.at[kk]], sem) for kk in range(TOP_K)]
            for cp in copies:
                cp.wait()

    return scatter(rows, dest3)


def _sc_gather_rows(y, dest3):
    d = y.shape[1]
    nwin = dest3.shape[0]
    n = nwin * SC_GWIN
    mesh = plsc.VectorSubcoreMesh(core_axis_name="c", subcore_axis_name="s")
    workers = mesh.num_cores * mesh.num_subcores
    assert nwin % (2 * workers) == 0, (nwin, workers)
    per = nwin // workers

    @functools.partial(
        pl.kernel, out_type=jax.ShapeDtypeStruct((TOP_K, n, d), y.dtype), mesh=mesh,
        scratch_types=[pltpu.VMEM((2 * TOP_K, SC_GWIN, d), y.dtype), pltpu.VMEM((2, TOP_K, SC_GWIN), jnp.int32),
                       pltpu.SemaphoreType.DMA, pltpu.SemaphoreType.DMA, pltpu.SemaphoreType.DMA])
    def gather(y_hbm, i_hbm, o_hbm, buf, idx, sem_gather, sem_write, sem_idx):
        base = (lax.axis_index("s") * mesh.num_cores + lax.axis_index("c")) * per

        def index_copy(w, s):
            return pltpu.make_async_copy(i_hbm.at[w], idx.at[s], sem_idx)

        def write_copy(w, s, kk):
            return pltpu.make_async_copy(buf.at[s * TOP_K + kk], o_hbm.at[kk, pl.ds(w * SC_GWIN, SC_GWIN)], sem_write)

        index_copy(base, 0).start()

        @pl.loop(0, per // 2)
        def _(g):
            for s in range(2):
                w = base + 2 * g + s
                index_copy(w, s).wait()
                if s == 0:
                    index_copy(w + 1, 1).start()
                else:
                    @pl.when(g + 1 < per // 2)
                    def _():
                        index_copy(w + 1, 0).start()

                @pl.when(g > 0)
                def _():
                    for kk in range(TOP_K):
                        write_copy(w, s, kk).wait()

                gathers = [pltpu.make_async_copy(y_hbm.at[idx.at[s, kk]], buf.at[s * TOP_K + kk], sem_gather)
                           for kk in range(TOP_K)]
                for cp in gathers:
                    cp.start()
                for kk in range(TOP_K):
                    gathers[kk].wait()
                    write_copy(w, s, kk).start()

        for s in range(2):
            for kk in range(TOP_K):
                write_copy(base, s, kk).wait()

    return gather(y, dest3)


def _expert_changed(te_ref):
    t = pl.program_id(1)
    return (t == 0) | (te_ref[t] != te_ref[jnp.maximum(t - 1, 0)])


def _moe_up_kernel(te_ref, nu_ref, x_ref, w1g_ref, w1l_ref, b1g_ref, b1l_ref, act_ref, wg_s, wl_s):
    @pl.when(_expert_changed(te_ref))
    def _():
        wg_s[...] = w1g_ref[0].astype(BF16)
        wl_s[...] = w1l_ref[0].astype(BF16)

    def up(words):
        lo, hi = _unpack_pairs(words)
        x = jnp.concatenate([lo.astype(BF16), hi.astype(BF16)], axis=1)
        glu = jnp.dot(x, wg_s[...], preferred_element_type=F32) + b1g_ref[0]
        lin = jnp.dot(x, wl_s[...], preferred_element_type=F32) + b1l_ref[0]
        glu = jnp.minimum(glu, SWIGLU_LIMIT)
        lin = jnp.clip(lin, -SWIGLU_LIMIT, SWIGLU_LIMIT)
        act_ref[...] = (glu * _sigmoid(SWIGLU_ALPHA * glu) * (lin + 1.0)).astype(BF16)

    t = pl.program_id(1)
    used = t < nu_ref[0]
    valid = nu_ref[1 + t]

    @pl.when(used & (valid >= MOE_TM))
    def _():
        up(x_ref[...])

    @pl.when(used & (valid < MOE_TM))
    def _():
        rowid = lax.broadcasted_iota(jnp.int32, (MOE_TM, HALF_D), 0)
        up(jnp.where(rowid < valid, x_ref[...], jnp.uint32(0)))


def _moe_down_kernel(te_ref, nu_ref, a_ref, w2_ref, b2_ref, y_ref, w2_s):
    @pl.when(_expert_changed(te_ref))
    def _():
        w2_s[...] = w2_ref[0].astype(BF16)

    @pl.when(pl.program_id(1) < nu_ref[0])
    def _():
        y = jnp.dot(a_ref[...], w2_s[...], preferred_element_type=F32) + b2_ref[0]
        y_ref[...] = _pack_pairs(y[:, :MOE_TN // 2], y[:, MOE_TN // 2:])


def _moe(tile_e, n_used, x_sorted, w1, b1, w2, b2):
    r = x_sorted.shape[0]
    n_tiles = r // MOE_TM
    nf = D_FF // MOE_TF
    nn = D_MODEL // MOE_TN

    def row(c, t, te, nu):
        return jnp.maximum(jnp.minimum(t, nu[0] - 1), 0)

    act = pl.pallas_call(
        _moe_up_kernel,
        grid_spec=pltpu.PrefetchScalarGridSpec(
            num_scalar_prefetch=2,
            grid=(nf, n_tiles),
            in_specs=[
                pl.BlockSpec((MOE_TM, HALF_D), lambda c, t, te, nu: (row(c, t, te, nu), 0)),
                pl.BlockSpec((1, D_MODEL, MOE_TF), lambda c, t, te, nu: (te[t], 0, c)),
                pl.BlockSpec((1, D_MODEL, MOE_TF), lambda c, t, te, nu: (te[t], 0, nf + c)),
                pl.BlockSpec((1, 1, MOE_TF), lambda c, t, te, nu: (te[t], 0, c)),
                pl.BlockSpec((1, 1, MOE_TF), lambda c, t, te, nu: (te[t], 0, nf + c)),
            ],
            out_specs=pl.BlockSpec((MOE_TM, MOE_TF), lambda c, t, te, nu: (row(c, t, te, nu), c)),
            scratch_shapes=[pltpu.VMEM((D_MODEL, MOE_TF), BF16), pltpu.VMEM((D_MODEL, MOE_TF), BF16)],
        ),
        out_shape=jax.ShapeDtypeStruct((r, D_FF), BF16),
        compiler_params=_cparams(("arbitrary", "arbitrary"), MOE_VMEM_MB),
        name="moe_up",
    )(tile_e, n_used, x_sorted, w1, w1, b1, b1)

    return pl.pallas_call(
        _moe_down_kernel,
        grid_spec=pltpu.PrefetchScalarGridSpec(
            num_scalar_prefetch=2,
            grid=(nn, n_tiles),
            in_specs=[
                pl.BlockSpec((MOE_TM, D_FF), lambda c, t, te, nu: (row(c, t, te, nu), 0)),
                pl.BlockSpec((1, D_FF, MOE_TN), lambda c, t, te, nu: (te[t], 0, c)),
                pl.BlockSpec((1, 1, MOE_TN), lambda c, t, te, nu: (te[t], 0, c)),
            ],
            out_specs=pl.BlockSpec((MOE_TM, MOE_TN // 2), lambda c, t, te, nu: (row(c, t, te, nu), c)),
            scratch_shapes=[pltpu.VMEM((D_FF, MOE_TN), BF16)],
        ),
        out_shape=jax.ShapeDtypeStruct((r, HALF_D), jnp.uint32),
        compiler_params=_cparams(("arbitrary", "arbitrary"), MOE_VMEM_MB),
        name="moe_down",
    )(tile_e, n_used, act, w2, b2)


def _combine_kernel(x_ref, y0_ref, y1_ref, y2_ref, y3_ref, tg_ref, g_ref, *rest):
    o_ref = rest[-1]
    tg = tg_ref[...]
    acc = x_ref[...]
    half = MOE_TN // 2
    for kk, y_ref in enumerate((y0_ref, y1_ref, y2_ref, y3_ref)):
        cols = []
        for c in range(D_MODEL // MOE_TN):
            cols += list(_unpack_pairs(y_ref[0, :, c * half:(c + 1) * half]))
        acc = acc + jnp.concatenate(cols, axis=1) * tg[:, kk:kk + 1]
    o_ref[...] = _rms(acc, g_ref[...])


def _combine(x2, yg, tg, g, row0, rows, yg_row0, prev=None):
    n = x2.shape[0]
    tm = min(TOK_TILE, n)
    blk0, yblk0 = row0 // tm, yg_row0 // tm
    here = lambda i: (blk0 + i, 0)
    yspec = [pl.BlockSpec((1, tm, HALF_D), functools.partial(lambda i, kk: (kk, yblk0 + i, 0), kk=kk))
             for kk in range(TOP_K)]
    extra = () if prev is None else (prev,)
    return pl.pallas_call(
        _combine_kernel,
        grid=(rows // tm,),
        in_specs=[pl.BlockSpec((tm, D_MODEL), here)] + yspec + [
            pl.BlockSpec((tm, LANES), here),
            pl.BlockSpec((1, D_MODEL), lambda i: (0, 0)),
        ] + [pl.BlockSpec(memory_space=pl.ANY)] * len(extra),
        out_specs=pl.BlockSpec((tm, D_MODEL), here),
        out_shape=jax.ShapeDtypeStruct((n, D_MODEL), F32),
        input_output_aliases={7: 0} if extra else {},
        compiler_params=_cparams(("parallel",), 48),
        name="combine",
    )(x2, yg, yg, yg, yg, tg, g, *extra)


def _group_tiles(bsz, t):
    tt = min(t, TOK_TILE)
    bb = max(1, min(bsz, TOK_TILE // tt))
    return bb, tt


def _layer_group(x, c0, n0, m0, conv_state, mem_k, mem_v, cnt0, hn_all, row0, p):
    bsz, t, _ = x.shape
    n = bsz * t
    L = min(MLSTM_CHUNK, t)
    state_pad = jnp.pad(conv_state, ((0, 0), (CONV_PAD - CONV_STATE, 0), (0, 0)))
    conv_args = (state_pad, p["conv_w"], p["conv_b"], p["ln_g"], p["ln_b"])
    if t % PROJ_TM == 0:
        qkvo, gates, cv, new_conv = _proj_in_conv(x.reshape(n, D_MODEL), p["g_mix"], p["w_main"], p["wg"], p["bg"],
                                                  L, *conv_args, seq_len=t)
    else:
        qkvo, u, gates = _proj_in(x.reshape(n, D_MODEL), p["g_mix"], p["w_main"], p["wg"], p["bg"], L)
        ctt = min(CONV_TT, t)
        cbb = max(1, min(bsz, CONV_TT // ctt))
        cv, new_conv = _conv(u.reshape(bsz, t, CONV_CH), *conv_args, cbb, ctt)

    mbb = min(bsz, MLSTM_BB)
    m0b = jnp.broadcast_to(m0[:, :, None], (bsz, M_HEADS, LANES))
    hm, c1, n1, m1 = _mlstm(qkvo.reshape(bsz, t, -1), gates, c0, n0, m0b, p["mh_g"], mbb, L)

    x1 = _proj_out(x.reshape(n, D_MODEL), hm.reshape(n, M_WIDTH), cv.reshape(n, CONV_CH), p["wo_a"], p["wo_b"])

    abb, att = _group_tiles(bsz, t)
    x2, hn_all, te, tg, tr, cnt = _attn_router(x1.reshape(bsz, t, D_MODEL), mem_k, mem_v, cnt0, hn_all, row0,
                                               p, abb, att)
    return x2.reshape(n, D_MODEL), hn_all, te, tg, tr, cnt, (c1, n1, m1[:, :, 0], new_conv)


def kernel(x_prompt, x_sample, mem_prompt, state_mlstm_C, state_mlstm_n, state_mlstm_m, state_conv, cache_mem_k, cache_mem_v, norm_mix_g, w_in, b_gates, mh_norm_g, conv_w, conv_b, conv_ln_g, conv_ln_b, w_out, norm_ca_g, norm_mem_g, ca_wq, ca_wk, ca_wv, ca_wo, norm_moe_g, router_w, router_b, moe_w1, moe_b1, moe_w2, moe_b2, final_norm_g):
    assert w_in.shape[0] == 1, "single layer"
    bp, tp, _ = x_prompt.shape
    bs, ts, _ = x_sample.shape
    n_p, n_s = bp * tp, bs * ts
    n_all = n_p + n_s

    wi = w_in[0]
    o0 = 0
    parts = []
    for width in (QK_W, QK_W, M_WIDTH, M_WIDTH, M_HEADS, M_HEADS, CONV_CH, CONV_CH):
        parts.append(wi[:, o0:o0 + width])
        o0 += width
    w_q, w_k, w_v, w_o, w_gi, w_gf, w_ua, w_ug = parts
    col_tiles = []
    w_qkvo = wi[:, :2 * QK_W + 2 * M_WIDTH]
    for c in range(PROJ_STEPS):
        col_tiles += [w_qkvo[:, c * QKVO_TN:(c + 1) * QKVO_TN], w_ua[:, c * GLU_TN:(c + 1) * GLU_TN],
                      w_ug[:, c * GLU_TN:(c + 1) * GLU_TN]]
    p = {
        "g_mix": norm_mix_g[0][None, :],
        "w_main": jnp.concatenate(col_tiles, axis=1).astype(BF16),
        "wg": jnp.concatenate([w_gi, w_gf], axis=1).T.astype(BF16),
        "bg": b_gates[0][:, None],
        "mh_g": mh_norm_g[0],
        "conv_w": conv_w[0],
        "conv_b": conv_b[0][None, :],
        "ln_g": conv_ln_g[0][None, :],
        "ln_b": conv_ln_b[0][None, :],
        "wo_a": w_out[0][:M_WIDTH].astype(BF16),
        "wo_b": w_out[0][M_WIDTH:].astype(BF16),
        "g_ca": norm_ca_g[0][None, :],
        "wq": ca_wq[0].astype(BF16),
        "wo": ca_wo[0].astype(BF16),
        "g_moe": norm_moe_g[0][None, :],
        "router_w": jnp.pad(router_w[0], ((0, 0), (0, LANES - N_EXPERTS))).astype(BF16),
        "router_b": jnp.concatenate([router_b[0], jnp.full((LANES - N_EXPERTS,), NEG_BIG, F32)])[None, :],
    }

    wkv = jnp.concatenate([ca_wk[0], ca_wv[0]], axis=1).astype(BF16)
    mk, mv = _mem_kv(mem_prompt.reshape(bp * N_MEM, D_MODEL), norm_mem_g[0][None, :], wkv)
    mk = mk.reshape(bp, N_MEM, CA_W)
    mv = mv.reshape(bp, N_MEM, CA_W)
    hn_all = jnp.zeros((n_all, HALF_D), jnp.uint32)
    cnt0 = jnp.zeros((1, LANES), F32)
    zc = jnp.zeros((bp, M_HEADS, M_DK, M_DV), F32)
    zn = jnp.zeros((bp, M_HEADS, M_DK), F32)
    zm = jnp.zeros((bp, M_HEADS), F32)
    zconv = jnp.zeros((bp, CONV_STATE, CONV_CH), F32)
    x2p, hn_all, te_p, tg_p, tr_p, cnt1, st_p = _layer_group(
        x_prompt, zc, zn, zm, zconv, mk.astype(BF16), mv.astype(BF16), cnt0, hn_all, 0, p)

    x2s, hn_all, te_s, tg_s, tr_s, cnt2, st_s = _layer_group(
        x_sample, state_mlstm_C[0], state_mlstm_n[0], state_mlstm_m[0], state_conv[0],
        cache_mem_k[0].reshape(bs, N_MEM, CA_W).astype(BF16), cache_mem_v[0].reshape(bs, N_MEM, CA_W).astype(BF16),
        cnt1, hn_all, n_p, p)

    te = jnp.concatenate([te_p[:, :TOP_K], te_s[:, :TOP_K]], axis=0)
    tr = jnp.concatenate([tr_p[:, :TOP_K], tr_s[:, :TOP_K]], axis=0)
    counts = cnt2[0, :N_EXPERTS].astype(jnp.int32)
    tiles_per_e = (counts + MOE_TM - 1) // MOE_TM
    tile_end = jnp.cumsum(tiles_per_e)
    row_start = (tile_end - tiles_per_e) * MOE_TM
    n_tiles = -(-(n_all * TOP_K) // MOE_TM) + N_EXPERTS
    n_used = tile_end[-1:]
    tile_ids = jnp.minimum(jnp.arange(n_tiles, dtype=jnp.int32), n_used[0] - 1)
    tile_e = jnp.minimum(jnp.sum(tile_ids[:, None] >= tile_end[None, :], axis=1), N_EXPERTS - 1).astype(jnp.int32)
    tile_first = jnp.arange(n_tiles, dtype=jnp.int32) - (tile_end - tiles_per_e)[tile_e]
    tile_valid = jnp.clip(counts[tile_e] - tile_first * MOE_TM, 0, MOE_TM)
    tile_valid = jnp.where(jnp.arange(n_tiles) < n_used[0], tile_valid, 0).astype(jnp.int32)
    moe_meta = jnp.concatenate([n_used.astype(jnp.int32), tile_valid])
    dest = row_start[te] + tr
    dest3 = dest.reshape(n_all // SC_WIN, SC_WIN, TOP_K).transpose(0, 2, 1)
    x_sorted = _sc_scatter_rows(hn_all, dest3, n_tiles * MOE_TM)

    y_sorted = _moe(tile_e, moe_meta, x_sorted, moe_w1[0], moe_b1[0][:, None, :], moe_w2[0],
                    moe_b2[0][:, None, :])

    n_a = (bp // 2) * tp if bp > 1 else n_p
    dest3g = dest.reshape(n_all // SC_GWIN, SC_GWIN, TOP_K).transpose(0, 2, 1)
    yg_a = _sc_gather_rows(y_sorted, dest3g[:n_a // SC_GWIN])
    yg_b = _sc_gather_rows(y_sorted, dest3g[n_a // SC_GWIN:])
    fg = final_norm_g[None, :]
    y_prompt = _combine(x2p, yg_a, tg_p, fg, 0, n_a, 0)
    if n_a < n_p:
        y_prompt = _combine(x2p, yg_b, tg_p, fg, n_a, n_p - n_a, 0, prev=y_prompt)
    y_prompt = y_prompt.reshape(bp, tp, D_MODEL)
    y_sample = _combine(x2s, yg_b, tg_s, fg, 0, n_s, n_p - n_a).reshape(bs, ts, D_MODEL)

    c1, n1, m1, conv1 = st_p
    c2, n2, m2, conv2 = st_s
    mk4 = mk.reshape(1, bp, N_MEM, CA_HEADS, CA_DH)
    mv4 = mv.reshape(1, bp, N_MEM, CA_HEADS, CA_DH)
    return (y_prompt, y_sample, c1[None], n1[None], m1[None], conv1[None], mk4, mv4,
            c2[None], n2[None], m2[None], conv2[None])
```

```python
import functools

import jax
import jax.numpy as jnp
from jax import lax
from jax.experimental import pallas as pl
from jax.experimental.pallas import tpu as pltpu
from jax.experimental.pallas import tpu_sc as plsc

F32 = jnp.float32
BF16 = jnp.bfloat16

D_MODEL = 2048
HALF_D = D_MODEL // 2
M_HEADS = 4
M_DV = 256
M_DK = 128
QK_W = M_HEADS * M_DK
M_WIDTH = M_HEADS * M_DV
CONV_CH = 1024
CONV_WIDTH = 31
CONV_STATE = CONV_WIDTH - 1
CONV_PAD = 32
N_MEM = 256
CA_HEADS = 4
CA_DH = 128
CA_W = CA_HEADS * CA_DH
N_EXPERTS = 32
TOP_K = 4
D_FF = 2048
SWIGLU_ALPHA = 1.702
SWIGLU_LIMIT = 7.0
EPS = 1e-6
LANES = 128
SUBLANES = 8
NEG_BIG = -1e30

PROJ_TM = 1024
PROJ_STEPS = 4
QKVO_TN = (2 * QK_W + 2 * M_WIDTH) // PROJ_STEPS
GLU_TN = CONV_CH // PROJ_STEPS
PROJ_TN = QKVO_TN + 2 * GLU_TN
CONV_SUB = PROJ_TM // PROJ_STEPS
TOK_TILE = 512
ATTN_SPLIT = 2
MOE_TM = 512
MOE_TF = 1024
MOE_TN = 2048
MOE_VMEM_MB = 60
MLSTM_CHUNK = 256
MLSTM_BB = 2
CONV_TT = 256
CONV_RB = 32
CONV_LC = 512
SC_WIN = 32
SC_GWIN = 8


def _cparams(sem, vmem_mb):
    return pltpu.CompilerParams(dimension_semantics=sem, vmem_limit_bytes=vmem_mb << 20)


def _log_sigmoid(x):
    return jnp.minimum(x, 0.0) - jnp.log1p(jnp.exp(-jnp.abs(x)))


def _sigmoid(x):
    return 1.0 / (1.0 + jnp.exp(-x))


def _pack_pairs(lo, hi):
    return lax.bitcast_convert_type(pltpu.pack_elementwise([lo, hi], packed_dtype=BF16), jnp.uint32)


def _unpack_pairs(w):
    return tuple(pltpu.unpack_elementwise(w, index=i, packed_dtype=BF16, unpacked_dtype=F32) for i in range(2))


def _unwritten(shape, dtype):
    return pl.pallas_call(lambda o_ref: None, out_shape=jax.ShapeDtypeStruct(shape, dtype),
                          out_specs=pl.BlockSpec(memory_space=pl.ANY), name="unwritten_buffer")()


def _rms(x, g):
    ms = jnp.mean(x * x, axis=-1, keepdims=True)
    return x * lax.rsqrt(ms + EPS) * g


def _proj_prologue(x_ref, g_ref, wg_ref, bg_ref, h_ref, gates_ref, chunk):
    tm = x_ref.shape[0]
    h = _rms(x_ref[...], g_ref[...]).astype(BF16)
    h_ref[...] = h
    gt = lax.dot_general(wg_ref[...], h, (((1,), (1,)), ((), ())), preferred_element_type=F32)
    gt = gt + bg_ref[...]
    row = lax.broadcasted_iota(jnp.int32, gt.shape, 0)
    gt = jnp.where(row < M_HEADS, gt, _log_sigmoid(gt))
    for c in range(tm // chunk):
        gates_ref[c] = gt[:, c * chunk:(c + 1) * chunk]


def _proj_step(h_ref, w_ref):
    z = jnp.dot(h_ref[...], w_ref[...], preferred_element_type=F32)
    glu = z[:, QKVO_TN:QKVO_TN + GLU_TN] * _sigmoid(z[:, QKVO_TN + GLU_TN:])
    return z[:, :QKVO_TN].astype(BF16), glu.astype(BF16)


def _proj_in_kernel(x_ref, g_ref, w_ref, wg_ref, bg_ref, qkvo_ref, u_ref, gates_ref, h_ref, *, chunk):
    @pl.when(pl.program_id(1) == 0)
    def _():
        _proj_prologue(x_ref, g_ref, wg_ref, bg_ref, h_ref, gates_ref, chunk)

    qkvo_ref[...], u_ref[...] = _proj_step(h_ref, w_ref)


def _proj_in(x, g, w_main, wg, bg, chunk):
    n = x.shape[0]
    tm = min(PROJ_TM, n)
    return pl.pallas_call(
        functools.partial(_proj_in_kernel, chunk=chunk),
        grid=(n // tm, PROJ_STEPS),
        in_specs=[
            pl.BlockSpec((tm, D_MODEL), lambda i, j: (i, 0)),
            pl.BlockSpec((1, D_MODEL), lambda i, j: (0, 0)),
            pl.BlockSpec((D_MODEL, PROJ_TN), lambda i, j: (0, j)),
            pl.BlockSpec((2 * M_HEADS, D_MODEL), lambda i, j: (0, 0)),
            pl.BlockSpec((2 * M_HEADS, 1), lambda i, j: (0, 0)),
        ],
        out_specs=[
            pl.BlockSpec((tm, QKVO_TN), lambda i, j: (i, j)),
            pl.BlockSpec((tm, GLU_TN), lambda i, j: (i, j)),
            pl.BlockSpec((tm // chunk, 2 * M_HEADS, chunk), lambda i, j: (i, 0, 0)),
        ],
        out_shape=[
            jax.ShapeDtypeStruct((n, 2 * QK_W + 2 * M_WIDTH), BF16),
            jax.ShapeDtypeStruct((n, CONV_CH), BF16),
            jax.ShapeDtypeStruct((n // chunk, 2 * M_HEADS, chunk), F32),
        ],
        scratch_shapes=[pltpu.VMEM((tm, D_MODEL), BF16)],
        compiler_params=_cparams(("parallel", "arbitrary"), 48),
        name="proj_in",
    )(x, g, w_main, wg, bg)


def _proj_in_conv_kernel(x_ref, g_ref, w_ref, wg_ref, bg_ref, st_ref, cw_ref, cb_ref, lg_ref, lb_ref,
                         qkvo_ref, gates_ref, cv_ref, ns_ref, h_ref, u_s, hist_s, e_ref, er_ref, c_ref,
                         *, chunk, tiles_per_seq):
    i = pl.program_id(0)
    j = pl.program_id(1)
    p = i - 1
    sub = CONV_SUB
    ext = sub + CONV_PAD

    @pl.when((i == 0) & (j == 0))
    def _():
        u_s[...] = jnp.zeros(u_s.shape, BF16)
        hist_s[...] = jnp.zeros(hist_s.shape, F32)
        e_ref[:, ext:, :] = jnp.zeros((1, SUBLANES, CONV_CH), F32)

    @pl.when(j == 0)
    def _():
        _proj_prologue(x_ref, g_ref, wg_ref, bg_ref, h_ref, gates_ref, chunk)

    @pl.when((j == 0) & (p >= 0) & (p % tiles_per_seq == 0))
    def _():
        hist_s[...] = st_ref[0]

    r0 = pl.multiple_of(j * sub, sub)
    e_ref[0, 0:CONV_PAD, :] = hist_s[...]
    for c in range(PROJ_STEPS):
        e_ref[0, CONV_PAD:ext, c * GLU_TN:(c + 1) * GLU_TN] = u_s[(i + 1) % 2, c, pl.ds(r0, sub), :].astype(F32)
    prepare, row_block = _conv_stages(e_ref, er_ref, c_ref, cw_ref, cb_ref, lg_ref, lb_ref, cv_ref, 0, sub)
    prepare()
    for r0 in range(0, sub, CONV_RB):
        row_block(r0)
    hist_s[...] = e_ref[0, sub:ext, :]

    qkvo_ref[...], u_s[i % 2, j] = _proj_step(h_ref, w_ref)

    @pl.when((j == PROJ_STEPS - 1) & (p >= 0) & (p % tiles_per_seq == tiles_per_seq - 1))
    def _():
        ns_ref[0] = e_ref[0, ext - CONV_STATE:ext, :]


def _proj_in_conv(x, g, w_main, wg, bg, chunk, state_pad, conv_w, conv_b, ln_g, ln_b, seq_len):
    n = x.shape[0]
    tm = PROJ_TM
    n_tiles = n // tm
    tiles_per_seq = seq_len // tm
    bsz = n // seq_len
    assert tm == PROJ_STEPS * CONV_SUB and seq_len % tm == 0
    last = n_tiles - 1

    def tile(i, j):
        return jnp.minimum(i, last)

    def seq(i, j):
        return jnp.clip((i - 1) // tiles_per_seq, 0, bsz - 1)

    const2 = lambda i, j: (0, 0)
    return pl.pallas_call(
        functools.partial(_proj_in_conv_kernel, chunk=chunk, tiles_per_seq=tiles_per_seq),
        grid=(n_tiles + 1, PROJ_STEPS),
        in_specs=[
            pl.BlockSpec((tm, D_MODEL), lambda i, j: (tile(i, j), 0)),
            pl.BlockSpec((1, D_MODEL), const2),
            pl.BlockSpec((D_MODEL, PROJ_TN), lambda i, j: (0, j)),
            pl.BlockSpec((2 * M_HEADS, D_MODEL), const2),
            pl.BlockSpec((2 * M_HEADS, 1), const2),
            pl.BlockSpec((1, CONV_PAD, CONV_CH), lambda i, j: (seq(i, j), 0, 0)),
            pl.BlockSpec((CONV_WIDTH, CONV_CH), const2),
            pl.BlockSpec((1, CONV_CH), const2),
            pl.BlockSpec((1, CONV_CH), const2),
            pl.BlockSpec((1, CONV_CH), const2),
        ],
        out_specs=[
            pl.BlockSpec((tm, QKVO_TN), lambda i, j: (tile(i, j), jnp.where(i <= last, j, PROJ_STEPS - 1))),
            pl.BlockSpec((tm // chunk, 2 * M_HEADS, chunk), lambda i, j: (tile(i, j), 0, 0)),
            pl.BlockSpec((CONV_SUB, CONV_CH), lambda i, j: (jnp.maximum((i - 1) * PROJ_STEPS + j, 0), 0)),
            pl.BlockSpec((1, CONV_STATE, CONV_CH), lambda i, j: (seq(i, j), 0, 0)),
        ],
        out_shape=[
            jax.ShapeDtypeStruct((n, 2 * QK_W + 2 * M_WIDTH), BF16),
            jax.ShapeDtypeStruct((n // chunk, 2 * M_HEADS, chunk), F32),
            jax.ShapeDtypeStruct((n, CONV_CH), BF16),
            jax.ShapeDtypeStruct((bsz, CONV_STATE, CONV_CH), F32),
        ],
        scratch_shapes=[
            pltpu.VMEM((tm, D_MODEL), BF16),
            pltpu.VMEM((2, PROJ_STEPS, tm, GLU_TN), BF16),
            pltpu.VMEM((CONV_PAD, CONV_CH), F32),
            pltpu.VMEM((1, CONV_SUB + CONV_PAD + SUBLANES, CONV_CH), F32),
            pltpu.VMEM((SUBLANES - 1, CONV_SUB + CONV_PAD, CONV_CH), F32),
            pltpu.VMEM((CONV_SUB, CONV_CH), F32),
        ],
        compiler_params=_cparams(("arbitrary", "arbitrary"), 56),
        name="proj_in_conv",
    )(x, g, w_main, wg, bg, state_pad, conv_w, conv_b, ln_g, ln_b)


def _conv_stages(e_ref, er_ref, c_ref, w_ref, cb_ref, lg_ref, lb_ref, out_ref, b, rows):
    ext = rows + CONV_PAD
    base = CONV_PAD - CONV_STATE

    def prepare():
        for r in range(1, SUBLANES):
            er_ref[r - 1] = e_ref[b, r:r + ext, :]

    def row_block(r0):
        for lc in range(CONV_CH // CONV_LC):
            ls = slice(lc * CONV_LC, (lc + 1) * CONV_LC)
            acc = jnp.zeros((CONV_RB, CONV_LC), F32) + cb_ref[:, ls]
            for j in range(CONV_WIDTH):
                off = j + base
                a, r = (off // SUBLANES) * SUBLANES, off % SUBLANES
                if r == 0:
                    tap = e_ref[b, pl.ds(r0 + a, CONV_RB), ls]
                else:
                    tap = er_ref[r - 1, pl.ds(r0 + a, CONV_RB), ls]
                acc = acc + w_ref[j:j + 1, ls] * tap
            c_ref[pl.ds(r0, CONV_RB), ls] = acc
        c = c_ref[pl.ds(r0, CONV_RB), :]
        mu = jnp.mean(c, axis=-1, keepdims=True)
        cc = c - mu
        var = jnp.mean(cc * cc, axis=-1, keepdims=True)
        cn = cc * lax.rsqrt(var + EPS) * lg_ref[...] + lb_ref[...]
        out_ref[pl.ds(r0, CONV_RB), :] = (cn * _sigmoid(cn)).astype(BF16)

    return prepare, row_block


def _conv_rows(e_ref, er_ref, c_ref, w_ref, cb_ref, lg_ref, lb_ref, out_ref, b, rows):
    prepare, row_block = _conv_stages(e_ref, er_ref, c_ref, w_ref, cb_ref, lg_ref, lb_ref, out_ref, b, rows)
    prepare()

    def body(rb, carry):
        row_block(pl.multiple_of(rb * CONV_RB, CONV_RB))
        return carry

    lax.fori_loop(0, rows // CONV_RB, body, 0)


def _conv_kernel(u_ref, st_ref, w_ref, cb_ref, lg_ref, lb_ref, cv_ref, ns_ref, e_ref, er_ref, c_ref, *, bb, tt):
    t = pl.program_id(1)
    nt = pl.num_programs(1)
    ext = tt + CONV_PAD

    @pl.when(t == 0)
    def _():
        e_ref[:, 0:CONV_PAD, :] = st_ref[...]

    @pl.when(t > 0)
    def _():
        e_ref[:, 0:CONV_PAD, :] = e_ref[:, tt:ext, :]

    e_ref[:, CONV_PAD:ext, :] = u_ref[...].astype(F32)
    e_ref[:, ext:, :] = jnp.zeros((bb, SUBLANES, CONV_CH), F32)

    for b in range(bb):
        _conv_rows(e_ref, er_ref, c_ref, w_ref, cb_ref, lg_ref, lb_ref, cv_ref.at[b], b, tt)

    @pl.when(t == nt - 1)
    def _():
        ns_ref[...] = e_ref[:, ext - CONV_STATE:ext, :]


def _conv(u, state_pad, conv_w, conv_b, ln_g, ln_b, bb, tt):
    bsz, t, _ = u.shape
    return pl.pallas_call(
        functools.partial(_conv_kernel, bb=bb, tt=tt),
        grid=(bsz // bb, t // tt),
        in_specs=[
            pl.BlockSpec((bb, tt, CONV_CH), lambda b, s: (b, s, 0)),
            pl.BlockSpec((bb, CONV_PAD, CONV_CH), lambda b, s: (b, 0, 0)),
            pl.BlockSpec((CONV_WIDTH, CONV_CH), lambda b, s: (0, 0)),
            pl.BlockSpec((1, CONV_CH), lambda b, s: (0, 0)),
            pl.BlockSpec((1, CONV_CH), lambda b, s: (0, 0)),
            pl.BlockSpec((1, CONV_CH), lambda b, s: (0, 0)),
        ],
        out_specs=[
            pl.BlockSpec((bb, tt, CONV_CH), lambda b, s: (b, s, 0)),
            pl.BlockSpec((bb, CONV_STATE, CONV_CH), lambda b, s: (b, 0, 0)),
        ],
        out_shape=[
            jax.ShapeDtypeStruct((bsz, t, CONV_CH), BF16),
            jax.ShapeDtypeStruct((bsz, CONV_STATE, CONV_CH), F32),
        ],
        scratch_shapes=[
            pltpu.VMEM((bb, tt + CONV_PAD + SUBLANES, CONV_CH), F32),
            pltpu.VMEM((SUBLANES - 1, tt + CONV_PAD, CONV_CH), F32),
            pltpu.VMEM((tt, CONV_CH), F32),
        ],
        compiler_params=_cparams(("parallel", "arbitrary"), 40),
        name="conv",
    )(u, state_pad, conv_w, conv_b, ln_g, ln_b)


def _mlstm_kernel(q_ref, k_ref, v_ref, o_ref, gt_ref, c0_ref, n0_ref, m0_ref, ng_ref,
                  hm_ref, cn_ref, nn_ref, mn_ref, c_s, n_s, m_s, *, bb, L):
    ci = pl.program_id(1)
    nc = pl.num_programs(1)

    @pl.when(ci == 0)
    def _():
        c_s[...] = c0_ref[...]
        n_s[...] = n0_ref[...]
        m_s[...] = m0_ref[...]

    rt = lax.broadcasted_iota(jnp.int32, (L, L), 0)
    cs = lax.broadcasted_iota(jnp.int32, (L, L), 1)
    causal = cs <= rt
    lower = jnp.where(causal, 1.0, 0.0).astype(BF16)
    upper = jnp.where(rt <= cs, 1.0, 0.0).astype(BF16)
    nt_dims = (((1,), (1,)), ((), ()))
    scale = M_DK ** -0.5

    for b in range(bb):
        rows = gt_ref[b]
        cum_rows = jnp.zeros((2 * M_HEADS, L), F32)
        cum_cols = jnp.zeros((L, 2 * M_HEADS), F32)
        rest = rows
        for _ in range(3):
            piece = rest.astype(BF16)
            rest = rest - piece.astype(F32)
            cum_rows = cum_rows + jnp.dot(piece, upper, preferred_element_type=F32)
            cum_cols = cum_cols + lax.dot_general(lower, piece, nt_dims, preferred_element_type=F32)
        heads = range(M_HEADS)
        ig_row = [rows[h:h + 1, :] for h in heads]
        b_row = [cum_rows[M_HEADS + h:M_HEADS + h + 1, :] for h in heads]
        b_col = [cum_cols[:, M_HEADS + h:M_HEADS + h + 1] for h in heads]
        m0 = [m_s[b, h:h + 1, 0:1] for h in heads]
        q = [q_ref[b, :, h * M_DK:(h + 1) * M_DK] for h in heads]
        k = [k_ref[b, :, h * M_DK:(h + 1) * M_DK] for h in heads]
        v = [v_ref[b, :, h * M_DV:(h + 1) * M_DV] for h in heads]
        c0 = [c_s[b, h] for h in heads]
        n0 = [n_s[b, h:h + 1, :] for h in heads]

        g_col = [b_col[h] + m0[h] for h in heads]
        dmat = [jnp.where(causal, b_col[h] - b_row[h] + ig_row[h], -jnp.inf) for h in heads]
        m_col = [jnp.maximum(g_col[h], jnp.max(dmat[h], axis=1, keepdims=True)) for h in heads]
        w_state = [jnp.exp(g_col[h] - m_col[h]) * scale for h in heads]
        decay = [jnp.exp(dmat[h] - m_col[h]) * scale for h in heads]
        qk = [lax.dot_general(q[h], k[h], nt_dims, preferred_element_type=F32) for h in heads]
        s = [qk[h] * decay[h] for h in heads]
        qc = [jnp.dot(q[h], c0[h].astype(BF16), preferred_element_type=F32) for h in heads]
        num = [jnp.dot(s[h].astype(BF16), v[h], preferred_element_type=F32) + w_state[h] * qc[h] for h in heads]
        qn = [jnp.sum(q[h].astype(F32) * n0[h], axis=1, keepdims=True) for h in heads]
        den = [jnp.sum(s[h], axis=1, keepdims=True) + w_state[h] * qn[h] for h in heads]
        hh = [num[h] / jnp.maximum(jnp.abs(den[h]), jnp.exp(-m_col[h])) for h in heads]
        for h in heads:
            hn = hh[h] * lax.rsqrt(jnp.mean(hh[h] * hh[h], axis=-1, keepdims=True) + EPS) * ng_ref[h:h + 1, :]
            og = _sigmoid(o_ref[b, :, h * M_DV:(h + 1) * M_DV].astype(F32))
            hm_ref[b, :, h * M_DV:(h + 1) * M_DV] = (og * hn).astype(BF16)

        b_last = [b_row[h][:, L - 1:L] for h in heads]
        g_last = [b_last[h] + m0[h] for h in heads]
        wk_row = [b_last[h] - b_row[h] + ig_row[h] for h in heads]
        m_new = [jnp.maximum(g_last[h], jnp.max(wk_row[h], axis=1, keepdims=True)) for h in heads]
        a0 = [jnp.exp(g_last[h] - m_new[h]) for h in heads]
        ak_row = [jnp.exp(wk_row[h] - m_new[h]) for h in heads]
        kts = [(k[h].T.astype(F32) * ak_row[h]).astype(BF16) for h in heads]
        kv = [jnp.dot(kts[h], v[h], preferred_element_type=F32) for h in heads]
        kn = [jnp.dot(ak_row[h].astype(BF16), k[h], preferred_element_type=F32) for h in heads]
        for h in heads:
            c_s[b, h] = a0[h] * c0[h] + kv[h]
            n_s[b, h:h + 1, :] = a0[h] * n0[h] + kn[h]
            m_s[b, h:h + 1, :] = jnp.broadcast_to(m_new[h], (1, LANES))

    @pl.when(ci == nc - 1)
    def _():
        cn_ref[...] = c_s[...]
        nn_ref[...] = n_s[...]
        mn_ref[...] = m_s[...]


def _mlstm(qkvo, gates, c0, n0, m0, ng, bb, L):
    bsz, t, _ = qkvo.shape
    nc = t // L
    gates = gates.reshape(bsz, nc, 2 * M_HEADS, L)
    st_c = pl.BlockSpec((bb, M_HEADS, M_DK, M_DV), lambda b, c: (b, 0, 0, 0))
    st_n = pl.BlockSpec((bb, M_HEADS, M_DK), lambda b, c: (b, 0, 0))
    return pl.pallas_call(
        functools.partial(_mlstm_kernel, bb=bb, L=L),
        grid=(bsz // bb, nc),
        in_specs=[
            pl.BlockSpec((bb, L, QK_W), lambda b, c: (b, c, 0)),
            pl.BlockSpec((bb, L, QK_W), lambda b, c: (b, c, 1)),
            pl.BlockSpec((bb, L, M_WIDTH), lambda b, c: (b, c, 1)),
            pl.BlockSpec((bb, L, M_WIDTH), lambda b, c: (b, c, 2)),
            pl.BlockSpec((bb, None, 2 * M_HEADS, L), lambda b, c: (b, c, 0, 0)),
            st_c, st_n, st_n,
            pl.BlockSpec((M_HEADS, M_DV), lambda b, c: (0, 0)),
        ],
        out_specs=[
            pl.BlockSpec((bb, L, M_WIDTH), lambda b, c: (b, c, 0)),
            st_c, st_n, st_n,
        ],
        out_shape=[
            jax.ShapeDtypeStruct((bsz, t, M_WIDTH), BF16),
            jax.ShapeDtypeStruct((bsz, M_HEADS, M_DK, M_DV), F32),
            jax.ShapeDtypeStruct((bsz, M_HEADS, M_DK), F32),
            jax.ShapeDtypeStruct((bsz, M_HEADS, LANES), F32),
        ],
        scratch_shapes=[
            pltpu.VMEM((bb, M_HEADS, M_DK, M_DV), F32),
            pltpu.VMEM((bb, M_HEADS, M_DK), F32),
            pltpu.VMEM((bb, M_HEADS, LANES), F32),
        ],
        compiler_params=_cparams(("parallel", "arbitrary"), 40),
        name="mlstm",
    )(qkvo, qkvo, qkvo, qkvo, gates, c0, n0, m0, ng)


def _proj_out_kernel(x_ref, hm_ref, cv_ref, wa_ref, wb_ref, o_ref):
    acc = jnp.dot(hm_ref[...], wa_ref[...], preferred_element_type=F32)
    acc = acc + jnp.dot(cv_ref[...], wb_ref[...], preferred_element_type=F32)
    o_ref[...] = x_ref[...] + acc


def _proj_out(x, hm, cv, wa, wb):
    n = x.shape[0]
    tm = min(TOK_TILE, n)
    return pl.pallas_call(
        _proj_out_kernel,
        grid=(n // tm,),
        in_specs=[
            pl.BlockSpec((tm, D_MODEL), lambda i: (i, 0)),
            pl.BlockSpec((tm, M_WIDTH), lambda i: (i, 0)),
            pl.BlockSpec((tm, CONV_CH), lambda i: (i, 0)),
            pl.BlockSpec((M_WIDTH, D_MODEL), lambda i: (0, 0)),
            pl.BlockSpec((CONV_CH, D_MODEL), lambda i: (0, 0)),
        ],
        out_specs=pl.BlockSpec((tm, D_MODEL), lambda i: (i, 0)),
        out_shape=jax.ShapeDtypeStruct((n, D_MODEL), F32),
        compiler_params=_cparams(("parallel",), 48),
        name="proj_out",
    )(x, hm, cv, wa, wb)


def _mem_kv_kernel(m_ref, g_ref, w_ref, k_ref, v_ref):
    mn = _rms(m_ref[...], g_ref[...]).astype(BF16)
    z = jnp.dot(mn, w_ref[...], preferred_element_type=F32)
    k_ref[...] = z[:, :CA_W]
    v_ref[...] = z[:, CA_W:]


def _mem_kv(mem, g, wkv):
    n = mem.shape[0]
    return pl.pallas_call(
        _mem_kv_kernel,
        grid=(1,),
        in_specs=[
            pl.BlockSpec((n, D_MODEL), lambda i: (0, 0)),
            pl.BlockSpec((1, D_MODEL), lambda i: (0, 0)),
            pl.BlockSpec((D_MODEL, 2 * CA_W), lambda i: (0, 0)),
        ],
        out_specs=[pl.BlockSpec((n, CA_W), lambda i: (0, 0))] * 2,
        out_shape=[jax.ShapeDtypeStruct((n, CA_W), F32)] * 2,
        compiler_params=_cparams(("arbitrary",), 40),
        name="mem_kv",
    )(mem, g, wkv)


def _attn_router_kernel(x_ref, gca_ref, wq_ref, k_ref, v_ref, wo_ref, gmoe_ref, rw_ref, rb_ref, cnt0_ref,
                        hn_in_ref, x2_ref, hn_ref, te_ref, tg_ref, tr_ref, cnt_ref, o_s, cnt_s, *, bb, tt):
    del hn_in_ref
    rows = bb * tt

    @pl.when((pl.program_id(0) == 0) & (pl.program_id(1) == 0))
    def _():
        cnt_s[...] = cnt0_ref[...]

    parts = range(ATTN_SPLIT)
    pr = rows // ATTN_SPLIT
    bpp = bb // ATTN_SPLIT

    def part_ref(ref, i):
        return ref.at[0, i * pr:(i + 1) * pr, :] if bb == 1 else ref.at[i * bpp:(i + 1) * bpp]

    seqs = [[(0, pr, 0)] if bb == 1 else [(j * tt, (j + 1) * tt, i * bpp + j) for j in range(bpp)] for i in parts]

    x = [part_ref(x_ref, i)[...].reshape(pr, D_MODEL) for i in parts]
    h = [_rms(x[i], gca_ref[...]).astype(BF16) for i in parts]
    q = [(jnp.dot(h[i], wq_ref[...], preferred_element_type=F32) * (CA_DH ** -0.5)).astype(BF16) for i in parts]
    units = [(i, r0, r1, b, slice(hd * CA_DH, (hd + 1) * CA_DH))
             for i in parts for r0, r1, b in seqs[i] for hd in range(CA_HEADS)]
    nt_dims = (((1,), (1,)), ((), ()))
    s = [lax.dot_general(q[i][r0:r1, cols], k_ref[b, :, cols], nt_dims, preferred_element_type=F32)
         for i, r0, r1, b, cols in units]
    e = [jnp.exp(si - jnp.max(si, axis=-1, keepdims=True)) for si in s]
    ov = [jnp.dot(ei.astype(BF16), v_ref[b, :, cols], preferred_element_type=F32)
          for ei, (i, r0, r1, b, cols) in zip(e, units)]
    for ei, ovi, (i, r0, r1, b, cols) in zip(e, ov, units):
        o_s[i * pr + r0:i * pr + r1, cols] = (ovi / jnp.sum(ei, axis=-1, keepdims=True)).astype(BF16)
    x2 = [x[i] + jnp.dot(o_s[i * pr:(i + 1) * pr, :], wo_ref[...], preferred_element_type=F32) for i in parts]
    for i in parts:
        out = part_ref(x2_ref, i)
        out[...] = x2[i].reshape(out.shape)

    hn = [_rms(x2[i], gmoe_ref[...]) for i in parts]
    for i in parts:
        hn_ref[i * pr:(i + 1) * pr, :] = _pack_pairs(hn[i][:, :HALF_D], hn[i][:, HALF_D:])
    work = [jnp.dot(hn[i].astype(BF16), rw_ref[...], preferred_element_type=F32) + rb_ref[...] for i in parts]
    lane = lax.broadcasted_iota(jnp.int32, (pr, LANES), 1)
    lane_f = lane.astype(F32)
    sel = [[] for _ in parts]
    vals = [[] for _ in parts]
    for _ in range(TOP_K):
        mx = [jnp.max(work[i], axis=-1, keepdims=True) for i in parts]
        idx = [jnp.min(jnp.where(work[i] == mx[i], lane_f, float(LANES)), axis=-1, keepdims=True) for i in parts]
        hit = [lane_f == idx[i] for i in parts]
        work = [jnp.where(hit[i], -jnp.inf, work[i]) for i in parts]
        for i in parts:
            sel[i].append((idx[i], hit[i]))
            vals[i].append(mx[i])
    ex = [[jnp.exp(vv - vals[i][0]) for vv in vals[i]] for i in parts]
    tot = [ex[i][0] + ex[i][1] + ex[i][2] + ex[i][3] for i in parts]
    assigned = []
    for i in parts:
        a = jnp.zeros((pr, LANES), F32)
        for _, hit_k in sel[i]:
            a = a + hit_k.astype(F32)
        assigned.append(a)
    rt = lax.broadcasted_iota(jnp.int32, (pr, pr), 0)
    cs = lax.broadcasted_iota(jnp.int32, (pr, pr), 1)
    strict = (cs < rt).astype(BF16)
    inside = [jnp.dot(strict, assigned[i].astype(BF16), preferred_element_type=F32) for i in parts]
    count = cnt_s[...]
    for i in parts:
        before = inside[i] + count
        count = count + jnp.sum(assigned[i], axis=0, keepdims=True)
        te = jnp.zeros((pr, LANES), F32)
        tg = jnp.zeros((pr, LANES), F32)
        tr = jnp.zeros((pr, LANES), F32)
        for kk, (idx_k, hit_k) in enumerate(sel[i]):
            rank = jnp.sum(jnp.where(hit_k, before, 0.0), axis=-1, keepdims=True)
            te = jnp.where(lane == kk, idx_k, te)
            tg = jnp.where(lane == kk, ex[i][kk] / tot[i], tg)
            tr = jnp.where(lane == kk, rank, tr)
        te_ref[i * pr:(i + 1) * pr, :] = te.astype(jnp.int32)
        tg_ref[i * pr:(i + 1) * pr, :] = tg
        tr_ref[i * pr:(i + 1) * pr, :] = tr.astype(jnp.int32)
    cnt_s[...] = count
    cnt_ref[...] = count


def _attn_router(x1, mem_k, mem_v, cnt0, hn_all, row0, p, bb, tt):
    bsz, t, _ = x1.shape
    rows = bb * tt
    n = bsz * t
    nb, nt = bsz // bb, t // tt
    blk0 = row0 // rows
    tok = lambda b, s: (b * nt + s, 0)
    const = lambda b, s: (0, 0)
    grid_spec = dict(
        grid=(nb, nt),
        in_specs=[
            pl.BlockSpec((bb, tt, D_MODEL), lambda b, s: (b, s, 0)),
            pl.BlockSpec((1, D_MODEL), const),
            pl.BlockSpec((D_MODEL, CA_W), const),
            pl.BlockSpec((bb, N_MEM, CA_W), lambda b, s: (b, 0, 0)),
            pl.BlockSpec((bb, N_MEM, CA_W), lambda b, s: (b, 0, 0)),
            pl.BlockSpec((CA_W, D_MODEL), const),
            pl.BlockSpec((1, D_MODEL), const),
            pl.BlockSpec((D_MODEL, LANES), const),
            pl.BlockSpec((1, LANES), const),
            pl.BlockSpec((1, LANES), const),
            pl.BlockSpec(memory_space=pl.ANY),
        ],
        out_specs=[
            pl.BlockSpec((bb, tt, D_MODEL), lambda b, s: (b, s, 0)),
            pl.BlockSpec((rows, HALF_D), lambda b, s: (blk0 + b * nt + s, 0)),
            pl.BlockSpec((rows, LANES), tok),
            pl.BlockSpec((rows, LANES), tok),
            pl.BlockSpec((rows, LANES), tok),
            pl.BlockSpec((1, LANES), const),
        ],
    )
    return pl.pallas_call(
        functools.partial(_attn_router_kernel, bb=bb, tt=tt),
        out_shape=[
            jax.ShapeDtypeStruct((bsz, t, D_MODEL), F32),
            jax.ShapeDtypeStruct(hn_all.shape, jnp.uint32),
            jax.ShapeDtypeStruct((n, LANES), jnp.int32),
            jax.ShapeDtypeStruct((n, LANES), F32),
            jax.ShapeDtypeStruct((n, LANES), jnp.int32),
            jax.ShapeDtypeStruct((1, LANES), F32),
        ],
        scratch_shapes=[pltpu.VMEM((rows, CA_W), BF16), pltpu.VMEM((1, LANES), F32)],
        input_output_aliases={10: 1},
        compiler_params=_cparams(("arbitrary", "arbitrary"), 48),
        name="attn_router",
        **grid_spec,
    )(x1, p["g_ca"], p["wq"], mem_k, mem_v, p["wo"], p["g_moe"], p["router_w"], p["router_b"], cnt0, hn_all)


def _sc_scatter_rows(rows, dest3, n_out):
    n, d = rows.shape
    nwin = dest3.shape[0]
    mesh = plsc.VectorSubcoreMesh(core_axis_name="c", subcore_axis_name="s")
    workers = mesh.num_cores * mesh.num_subcores
    assert nwin * SC_WIN == n and nwin % workers == 0, (n, nwin, workers)
    per = nwin // workers

    @functools.partial(
        pl.kernel, out_type=jax.ShapeDtypeStruct((n_out, d), rows.dtype), mesh=mesh,
        scratch_types=[pltpu.VMEM((SC_WIN, d), rows.dtype), pltpu.VMEM((TOP_K, SC_WIN), jnp.int32),
                       pltpu.SemaphoreType.DMA])
    def scatter(x_hbm, i_hbm, o_hbm, buf, idx, sem):
        wid = lax.axis_index("s") * mesh.num_cores + lax.axis_index("c")

        @pl.loop(0, per)
        def _(j):
            w = wid * per + j
            pltpu.sync_copy(x_hbm.at[pl.ds(w * SC_WIN, SC_WIN)], buf)
            pltpu.sync_copy(i_hbm.at[w], idx)
            copies = [pltpu.async_copy(buf, o_hbm.at[idx.at[kk]], sem) for kk in range(TOP_K)]
            for cp in copies:
                cp.wait()

    return scatter(rows, dest3)


def _sc_gather_rows(y, dest3):
    d = y.shape[1]
    nwin = dest3.shape[0]
    n = nwin * SC_GWIN
    mesh = plsc.VectorSubcoreMesh(core_axis_name="c", subcore_axis_name="s")
    workers = mesh.num_cores * mesh.num_subcores
    assert nwin % (2 * workers) == 0, (nwin, workers)
    per = nwin // workers

    @functools.partial(
        pl.kernel, out_type=jax.ShapeDtypeStruct((TOP_K, n, d), y.dtype), mesh=mesh,
        scratch_types=[pltpu.VMEM((2 * TOP_K, SC_GWIN, d), y.dtype), pltpu.VMEM((2, TOP_K, SC_GWIN), jnp.int32),
                       pltpu.SemaphoreType.DMA, pltpu.SemaphoreType.DMA, pltpu.SemaphoreType.DMA])
    def gather(y_hbm, i_hbm, o_hbm, buf, idx, sem_gather, sem_write, sem_idx):
        base = (lax.axis_index("s") * mesh.num_cores + lax.axis_index("c")) * per

        def index_copy(w, s):
            return pltpu.make_async_copy(i_hbm.at[w], idx.at[s], sem_idx)

        def write_copy(w, s, kk):
            return pltpu.make_async_copy(buf.at[s * TOP_K + kk], o_hbm.at[kk, pl.ds(w * SC_GWIN, SC_GWIN)], sem_write)

        index_copy(base, 0).start()

        @pl.loop(0, per // 2)
        def _(g):
            for s in range(2):
                w = base + 2 * g + s
                index_copy(w, s).wait()
                if s == 0:
                    index_copy(w + 1, 1).start()
                else:
                    @pl.when(g + 1 < per // 2)
                    def _():
                        index_copy(w + 1, 0).start()

                @pl.when(g > 0)
                def _():
                    for kk in range(TOP_K):
                        write_copy(w, s, kk).wait()

                gathers = [pltpu.make_async_copy(y_hbm.at[idx.at[s, kk]], buf.at[s * TOP_K + kk], sem_gather)
                           for kk in range(TOP_K)]
                for cp in gathers:
                    cp.start()
                for kk in range(TOP_K):
                    gathers[kk].wait()
                    write_copy(w, s, kk).start()

        for s in range(2):
            for kk in range(TOP_K):
                write_copy(base, s, kk).wait()

    return gather(y, dest3)


def _expert_changed(te_ref):
    t = pl.program_id(1)
    return (t == 0) | (te_ref[t] != te_ref[jnp.maximum(t - 1, 0)])


def _moe_up_kernel(te_ref, nu_ref, x_ref, w1g_ref, w1l_ref, b1g_ref, b1l_ref, act_ref, wg_s, wl_s):
    @pl.when(_expert_changed(te_ref))
    def _():
        wg_s[...] = w1g_ref[0].astype(BF16)
        wl_s[...] = w1l_ref[0].astype(BF16)

    def up(words):
        lo, hi = _unpack_pairs(words)
        x = jnp.concatenate([lo.astype(BF16), hi.astype(BF16)], axis=1)
        glu = jnp.dot(x, wg_s[...], preferred_element_type=F32) + b1g_ref[0]
        lin = jnp.dot(x, wl_s[...], preferred_element_type=F32) + b1l_ref[0]
        glu = jnp.minimum(glu, SWIGLU_LIMIT)
        lin = jnp.clip(lin, -SWIGLU_LIMIT, SWIGLU_LIMIT)
        act_ref[...] = (glu * _sigmoid(SWIGLU_ALPHA * glu) * (lin + 1.0)).astype(BF16)

    t = pl.program_id(1)
    used = t < nu_ref[0]
    valid = nu_ref[1 + t]

    @pl.when(used & (valid >= MOE_TM))
    def _():
        up(x_ref[...])

    @pl.when(used & (valid < MOE_TM))
    def _():
        rowid = lax.broadcasted_iota(jnp.int32, (MOE_TM, HALF_D), 0)
        up(jnp.where(rowid < valid, x_ref[...], jnp.uint32(0)))


def _moe_down_kernel(te_ref, nu_ref, a_ref, w2_ref, b2_ref, y_ref, w2_s):
    @pl.when(_expert_changed(te_ref))
    def _():
        w2_s[...] = w2_ref[0].astype(BF16)

    @pl.when(pl.program_id(1) < nu_ref[0])
    def _():
        y = jnp.dot(a_ref[...], w2_s[...], preferred_element_type=F32) + b2_ref[0]
        y_ref[...] = _pack_pairs(y[:, :MOE_TN // 2], y[:, MOE_TN // 2:])


def _moe(tile_e, n_used, x_sorted, w1, b1, w2, b2):
    r = x_sorted.shape[0]
    n_tiles = r // MOE_TM
    nf = D_FF // MOE_TF
    nn = D_MODEL // MOE_TN

    def row(c, t, te, nu):
        return jnp.maximum(jnp.minimum(t, nu[0] - 1), 0)

    act = pl.pallas_call(
        _moe_up_kernel,
        grid_spec=pltpu.PrefetchScalarGridSpec(
            num_scalar_prefetch=2,
            grid=(nf, n_tiles),
            in_specs=[
                pl.BlockSpec((MOE_TM, HALF_D), lambda c, t, te, nu: (row(c, t, te, nu), 0)),
                pl.BlockSpec((1, D_MODEL, MOE_TF), lambda c, t, te, nu: (te[t], 0, c)),
                pl.BlockSpec((1, D_MODEL, MOE_TF), lambda c, t, te, nu: (te[t], 0, nf + c)),
                pl.BlockSpec((1, 1, MOE_TF), lambda c, t, te, nu: (te[t], 0, c)),
                pl.BlockSpec((1, 1, MOE_TF), lambda c, t, te, nu: (te[t], 0, nf + c)),
            ],
            out_specs=pl.BlockSpec((MOE_TM, MOE_TF), lambda c, t, te, nu: (row(c, t, te, nu), c)),
            scratch_shapes=[pltpu.VMEM((D_MODEL, MOE_TF), BF16), pltpu.VMEM((D_MODEL, MOE_TF), BF16)],
        ),
        out_shape=jax.ShapeDtypeStruct((r, D_FF), BF16),
        compiler_params=_cparams(("arbitrary", "arbitrary"), MOE_VMEM_MB),
        name="moe_up",
    )(tile_e, n_used, x_sorted, w1, w1, b1, b1)

    return pl.pallas_call(
        _moe_down_kernel,
        grid_spec=pltpu.PrefetchScalarGridSpec(
            num_scalar_prefetch=2,
            grid=(nn, n_tiles),
            in_specs=[
                pl.BlockSpec((MOE_TM, D_FF), lambda c, t, te, nu: (row(c, t, te, nu), 0)),
                pl.BlockSpec((1, D_FF, MOE_TN), lambda c, t, te, nu: (te[t], 0, c)),
                pl.BlockSpec((1, 1, MOE_TN), lambda c, t, te, nu: (te[t], 0, c)),
            ],
            out_specs=pl.BlockSpec((MOE_TM, MOE_TN // 2), lambda c, t, te, nu: (row(c, t, te, nu), c)),
            scratch_shapes=[pltpu.VMEM((D_FF, MOE_TN), BF16)],
        ),
        out_shape=jax.ShapeDtypeStruct((r, HALF_D), jnp.uint32),
        compiler_params=_cparams(("arbitrary", "arbitrary"), MOE_VMEM_MB),
        name="moe_down",
    )(tile_e, n_used, act, w2, b2)


def _combine_kernel(x_ref, y0_ref, y1_ref, y2_ref, y3_ref, tg_ref, g_ref, o_ref):
    tg = tg_ref[...]
    acc = x_ref[...]
    half = MOE_TN // 2
    for kk, y_ref in enumerate((y0_ref, y1_ref, y2_ref, y3_ref)):
        cols = []
        for c in range(D_MODEL // MOE_TN):
            cols += list(_unpack_pairs(y_ref[0, :, c * half:(c + 1) * half]))
        acc = acc + jnp.concatenate(cols, axis=1) * tg[:, kk:kk + 1]
    o_ref[...] = _rms(acc, g_ref[...])


def _combine(x2, yg, tg, g, row0):
    n = x2.shape[0]
    tm = min(TOK_TILE, n)
    blk0 = row0 // tm
    yspec = [pl.BlockSpec((1, tm, HALF_D), functools.partial(lambda i, kk: (kk, blk0 + i, 0), kk=kk))
             for kk in range(TOP_K)]
    return pl.pallas_call(
        _combine_kernel,
        grid=(n // tm,),
        in_specs=[pl.BlockSpec((tm, D_MODEL), lambda i: (i, 0))] + yspec + [
            pl.BlockSpec((tm, LANES), lambda i: (i, 0)),
            pl.BlockSpec((1, D_MODEL), lambda i: (0, 0)),
        ],
        out_specs=pl.BlockSpec((tm, D_MODEL), lambda i: (i, 0)),
        out_shape=jax.ShapeDtypeStruct((n, D_MODEL), F32),
        compiler_params=_cparams(("parallel",), 48),
        name="combine",
    )(x2, yg, yg, yg, yg, tg, g)


def _group_tiles(bsz, t):
    tt = min(t, TOK_TILE)
    bb = max(1, min(bsz, TOK_TILE // tt))
    return bb, tt


def _layer_group(x, c0, n0, m0, conv_state, mem_k, mem_v, cnt0, hn_all, row0, p):
    bsz, t, _ = x.shape
    n = bsz * t
    L = min(MLSTM_CHUNK, t)
    state_pad = jnp.pad(conv_state, ((0, 0), (CONV_PAD - CONV_STATE, 0), (0, 0)))
    conv_args = (state_pad, p["conv_w"], p["conv_b"], p["ln_g"], p["ln_b"])
    if t % PROJ_TM == 0:
        qkvo, gates, cv, new_conv = _proj_in_conv(x.reshape(n, D_MODEL), p["g_mix"], p["w_main"], p["wg"], p["bg"],
                                                  L, *conv_args, seq_len=t)
    else:
        qkvo, u, gates = _proj_in(x.reshape(n, D_MODEL), p["g_mix"], p["w_main"], p["wg"], p["bg"], L)
        ctt = min(CONV_TT, t)
        cbb = max(1, min(bsz, CONV_TT // ctt))
        cv, new_conv = _conv(u.reshape(bsz, t, CONV_CH), *conv_args, cbb, ctt)

    mbb = min(bsz, MLSTM_BB)
    m0b = jnp.broadcast_to(m0[:, :, None], (bsz, M_HEADS, LANES))
    hm, c1, n1, m1 = _mlstm(qkvo.reshape(bsz, t, -1), gates, c0, n0, m0b, p["mh_g"], mbb, L)

    x1 = _proj_out(x.reshape(n, D_MODEL), hm.reshape(n, M_WIDTH), cv.reshape(n, CONV_CH), p["wo_a"], p["wo_b"])

    abb, att = _group_tiles(bsz, t)
    x2, hn_all, te, tg, tr, cnt = _attn_router(x1.reshape(bsz, t, D_MODEL), mem_k, mem_v, cnt0, hn_all, row0,
                                               p, abb, att)
    return x2.reshape(n, D_MODEL), hn_all, te, tg, tr, cnt, (c1, n1, m1[:, :, 0], new_conv)


def kernel(x_prompt, x_sample, mem_prompt, state_mlstm_C, state_mlstm_n, state_mlstm_m, state_conv, cache_mem_k, cache_mem_v, norm_mix_g, w_in, b_gates, mh_norm_g, conv_w, conv_b, conv_ln_g, conv_ln_b, w_out, norm_ca_g, norm_mem_g, ca_wq, ca_wk, ca_wv, ca_wo, norm_moe_g, router_w, router_b, moe_w1, moe_b1, moe_w2, moe_b2, final_norm_g):
    assert w_in.shape[0] == 1, "single layer"
    bp, tp, _ = x_prompt.shape
    bs, ts, _ = x_sample.shape
    n_p, n_s = bp * tp, bs * ts
    n_all = n_p + n_s

    wi = w_in[0]
    o0 = 0
    parts = []
    for width in (QK_W, QK_W, M_WIDTH, M_WIDTH, M_HEADS, M_HEADS, CONV_CH, CONV_CH):
        parts.append(wi[:, o0:o0 + width])
        o0 += width
    w_q, w_k, w_v, w_o, w_gi, w_gf, w_ua, w_ug = parts
    col_tiles = []
    w_qkvo = wi[:, :2 * QK_W + 2 * M_WIDTH]
    for c in range(PROJ_STEPS):
        col_tiles += [w_qkvo[:, c * QKVO_TN:(c + 1) * QKVO_TN], w_ua[:, c * GLU_TN:(c + 1) * GLU_TN],
                      w_ug[:, c * GLU_TN:(c + 1) * GLU_TN]]
    p = {
        "g_mix": norm_mix_g[0][None, :],
        "w_main": jnp.concatenate(col_tiles, axis=1).astype(BF16),
        "wg": jnp.concatenate([w_gi, w_gf], axis=1).T.astype(BF16),
        "bg": b_gates[0][:, None],
        "mh_g": mh_norm_g[0],
        "conv_w": conv_w[0],
        "conv_b": conv_b[0][None, :],
        "ln_g": conv_ln_g[0][None, :],
        "ln_b": conv_ln_b[0][None, :],
        "wo_a": w_out[0][:M_WIDTH].astype(BF16),
        "wo_b": w_out[0][M_WIDTH:].astype(BF16),
        "g_ca": norm_ca_g[0][None, :],
        "wq": ca_wq[0].astype(BF16),
        "wo": ca_wo[0].astype(BF16),
        "g_moe": norm_moe_g[0][None, :],
        "router_w": jnp.pad(router_w[0], ((0, 0), (0, LANES - N_EXPERTS))).astype(BF16),
        "router_b": jnp.concatenate([router_b[0], jnp.full((LANES - N_EXPERTS,), NEG_BIG, F32)])[None, :],
    }

    wkv = jnp.concatenate([ca_wk[0], ca_wv[0]], axis=1).astype(BF16)
    mk, mv = _mem_kv(mem_prompt.reshape(bp * N_MEM, D_MODEL), norm_mem_g[0][None, :], wkv)
    mk = mk.reshape(bp, N_MEM, CA_W)
    mv = mv.reshape(bp, N_MEM, CA_W)
    hn_all = _unwritten((n_all, HALF_D), jnp.uint32)
    cnt0 = jnp.zeros((1, LANES), F32)
    zc = jnp.zeros((bp, M_HEADS, M_DK, M_DV), F32)
    zn = jnp.zeros((bp, M_HEADS, M_DK), F32)
    zm = jnp.zeros((bp, M_HEADS), F32)
    zconv = jnp.zeros((bp, CONV_STATE, CONV_CH), F32)
    x2p, hn_all, te_p, tg_p, tr_p, cnt1, st_p = _layer_group(
        x_prompt, zc, zn, zm, zconv, mk.astype(BF16), mv.astype(BF16), cnt0, hn_all, 0, p)

    x2s, hn_all, te_s, tg_s, tr_s, cnt2, st_s = _layer_group(
        x_sample, state_mlstm_C[0], state_mlstm_n[0], state_mlstm_m[0], state_conv[0],
        cache_mem_k[0].reshape(bs, N_MEM, CA_W).astype(BF16), cache_mem_v[0].reshape(bs, N_MEM, CA_W).astype(BF16),
        cnt1, hn_all, n_p, p)

    te = jnp.concatenate([te_p[:, :TOP_K], te_s[:, :TOP_K]], axis=0)
    tr = jnp.concatenate([tr_p[:, :TOP_K], tr_s[:, :TOP_K]], axis=0)
    counts = cnt2[0, :N_EXPERTS].astype(jnp.int32)
    tiles_per_e = (counts + MOE_TM - 1) // MOE_TM
    tile_end = jnp.cumsum(tiles_per_e)
    row_start = (tile_end - tiles_per_e) * MOE_TM
    n_tiles = -(-(n_all * TOP_K) // MOE_TM) + N_EXPERTS
    n_used = tile_end[-1:]
    tile_ids = jnp.minimum(jnp.arange(n_tiles, dtype=jnp.int32), n_used[0] - 1)
    tile_e = jnp.minimum(jnp.sum(tile_ids[:, None] >= tile_end[None, :], axis=1), N_EXPERTS - 1).astype(jnp.int32)
    tile_first = jnp.arange(n_tiles, dtype=jnp.int32) - (tile_end - tiles_per_e)[tile_e]
    tile_valid = jnp.clip(counts[tile_e] - tile_first * MOE_TM, 0, MOE_TM)
    tile_valid = jnp.where(jnp.arange(n_tiles) < n_used[0], tile_valid, 0).astype(jnp.int32)
    moe_meta = jnp.concatenate([n_used.astype(jnp.int32), tile_valid])
    dest = row_start[te] + tr
    dest3 = dest.reshape(n_all // SC_WIN, SC_WIN, TOP_K).transpose(0, 2, 1)
    x_sorted = _sc_scatter_rows(hn_all, dest3, n_tiles * MOE_TM)

    y_sorted = _moe(tile_e, moe_meta, x_sorted, moe_w1[0], moe_b1[0][:, None, :], moe_w2[0],
                    moe_b2[0][:, None, :])

    yg = _sc_gather_rows(y_sorted, dest.reshape(n_all // SC_GWIN, SC_GWIN, TOP_K).transpose(0, 2, 1))
    fg = final_norm_g[None, :]
    y_prompt = _combine(x2p, yg, tg_p, fg, 0).reshape(bp, tp, D_MODEL)
    y_sample = _combine(x2s, yg, tg_s, fg, n_p).reshape(bs, ts, D_MODEL)

    c1, n1, m1, conv1 = st_p
    c2, n2, m2, conv2 = st_s
    mk4 = mk.reshape(1, bp, N_MEM, CA_HEADS, CA_DH)
    mv4 = mv.reshape(1, bp, N_MEM, CA_HEADS, CA_DH)
    return (y_prompt, y_sample, c1[None], n1[None], m1[None], conv1[None], mk4, mv4,
            c2[None], n2[None], m2[None], conv2[None])
```

```python
import functools

import jax
import jax.numpy as jnp
from jax import lax
from jax.experimental import pallas as pl
from jax.experimental.pallas import tpu as pltpu
from jax.experimental.pallas import tpu_sc as plsc

F32 = jnp.float32
BF16 = jnp.bfloat16

D_MODEL = 2048
HALF_D = D_MODEL // 2
M_HEADS = 4
M_DV = 256
M_DK = 128
QK_W = M_HEADS * M_DK
M_WIDTH = M_HEADS * M_DV
CONV_CH = 1024
CONV_WIDTH = 31
CONV_STATE = CONV_WIDTH - 1
CONV_PAD = 32
N_MEM = 256
CA_HEADS = 4
CA_DH = 128
CA_W = CA_HEADS * CA_DH
N_EXPERTS = 32
TOP_K = 4
D_FF = 2048
SWIGLU_ALPHA = 1.702
SWIGLU_LIMIT = 7.0
EPS = 1e-6
LANES = 128
SUBLANES = 8
NEG_BIG = -1e30

PROJ_TM = 1024
PROJ_STEPS = 4
QKVO_TN = (2 * QK_W + 2 * M_WIDTH) // PROJ_STEPS
GLU_TN = CONV_CH // PROJ_STEPS
PROJ_TN = QKVO_TN + 2 * GLU_TN
CONV_SUB = PROJ_TM // PROJ_STEPS
TOK_TILE = 512
ATTN_SPLIT = 2
MOE_TM = 512
MOE_TF = 1024
MOE_TN = 2048
MOE_VMEM_MB = 60
MLSTM_CHUNK = 256
MLSTM_BB = 2
CONV_TT = 256
CONV_RB = 32
CONV_LC = 512
SC_WIN = 32
SC_GWIN = 8


def _cparams(sem, vmem_mb):
    return pltpu.CompilerParams(dimension_semantics=sem, vmem_limit_bytes=vmem_mb << 20)


def _log_sigmoid(x):
    return jnp.minimum(x, 0.0) - jnp.log1p(jnp.exp(-jnp.abs(x)))


def _sigmoid(x):
    return 1.0 / (1.0 + jnp.exp(-x))


def _pack_pairs(lo, hi):
    return lax.bitcast_convert_type(pltpu.pack_elementwise([lo, hi], packed_dtype=BF16), jnp.uint32)


def _unpack_pairs(w):
    return tuple(pltpu.unpack_elementwise(w, index=i, packed_dtype=BF16, unpacked_dtype=F32) for i in range(2))


def _rms(x, g):
    ms = jnp.mean(x * x, axis=-1, keepdims=True)
    return x * lax.rsqrt(ms + EPS) * g


def _proj_prologue(x_ref, g_ref, wg_ref, bg_ref, h_ref, gates_ref, chunk):
    tm = x_ref.shape[0]
    h = _rms(x_ref[...], g_ref[...]).astype(BF16)
    h_ref[...] = h
    gt = lax.dot_general(wg_ref[...], h, (((1,), (1,)), ((), ())), preferred_element_type=F32)
    gt = gt + bg_ref[...]
    row = lax.broadcasted_iota(jnp.int32, gt.shape, 0)
    gt = jnp.where(row < M_HEADS, gt, _log_sigmoid(gt))
    for c in range(tm // chunk):
        gates_ref[c] = gt[:, c * chunk:(c + 1) * chunk]


def _proj_step(h_ref, w_ref):
    z = jnp.dot(h_ref[...], w_ref[...], preferred_element_type=F32)
    glu = z[:, QKVO_TN:QKVO_TN + GLU_TN] * _sigmoid(z[:, QKVO_TN + GLU_TN:])
    return z[:, :QKVO_TN].astype(BF16), glu.astype(BF16)


def _proj_in_kernel(x_ref, g_ref, w_ref, wg_ref, bg_ref, qkvo_ref, u_ref, gates_ref, h_ref, *, chunk):
    @pl.when(pl.program_id(1) == 0)
    def _():
        _proj_prologue(x_ref, g_ref, wg_ref, bg_ref, h_ref, gates_ref, chunk)

    qkvo_ref[...], u_ref[...] = _proj_step(h_ref, w_ref)


def _proj_in(x, g, w_main, wg, bg, chunk):
    n = x.shape[0]
    tm = min(PROJ_TM, n)
    return pl.pallas_call(
        functools.partial(_proj_in_kernel, chunk=chunk),
        grid=(n // tm, PROJ_STEPS),
        in_specs=[
            pl.BlockSpec((tm, D_MODEL), lambda i, j: (i, 0)),
            pl.BlockSpec((1, D_MODEL), lambda i, j: (0, 0)),
            pl.BlockSpec((D_MODEL, PROJ_TN), lambda i, j: (0, j)),
            pl.BlockSpec((2 * M_HEADS, D_MODEL), lambda i, j: (0, 0)),
            pl.BlockSpec((2 * M_HEADS, 1), lambda i, j: (0, 0)),
        ],
        out_specs=[
            pl.BlockSpec((tm, QKVO_TN), lambda i, j: (i, j)),
            pl.BlockSpec((tm, GLU_TN), lambda i, j: (i, j)),
            pl.BlockSpec((tm // chunk, 2 * M_HEADS, chunk), lambda i, j: (i, 0, 0)),
        ],
        out_shape=[
            jax.ShapeDtypeStruct((n, 2 * QK_W + 2 * M_WIDTH), BF16),
            jax.ShapeDtypeStruct((n, CONV_CH), BF16),
            jax.ShapeDtypeStruct((n // chunk, 2 * M_HEADS, chunk), F32),
        ],
        scratch_shapes=[pltpu.VMEM((tm, D_MODEL), BF16)],
        compiler_params=_cparams(("parallel", "arbitrary"), 48),
        name="proj_in",
    )(x, g, w_main, wg, bg)


def _proj_in_conv_kernel(x_ref, g_ref, w_ref, wg_ref, bg_ref, st_ref, cw_ref, cb_ref, lg_ref, lb_ref,
                         qkvo_ref, gates_ref, cv_ref, ns_ref, h_ref, u_s, hist_s, e_ref, er_ref, c_ref,
                         *, chunk, tiles_per_seq):
    i = pl.program_id(0)
    j = pl.program_id(1)
    p = i - 1
    sub = CONV_SUB
    ext = sub + CONV_PAD

    @pl.when((i == 0) & (j == 0))
    def _():
        u_s[...] = jnp.zeros(u_s.shape, BF16)
        hist_s[...] = jnp.zeros(hist_s.shape, F32)
        e_ref[:, ext:, :] = jnp.zeros((1, SUBLANES, CONV_CH), F32)

    @pl.when(j == 0)
    def _():
        _proj_prologue(x_ref, g_ref, wg_ref, bg_ref, h_ref, gates_ref, chunk)

    @pl.when((j == 0) & (p >= 0) & (p % tiles_per_seq == 0))
    def _():
        hist_s[...] = st_ref[0]

    r0 = pl.multiple_of(j * sub, sub)
    e_ref[0, 0:CONV_PAD, :] = hist_s[...]
    for c in range(PROJ_STEPS):
        e_ref[0, CONV_PAD:ext, c * GLU_TN:(c + 1) * GLU_TN] = u_s[(i + 1) % 2, c, pl.ds(r0, sub), :].astype(F32)
    prepare, row_block = _conv_stages(e_ref, er_ref, c_ref, cw_ref, cb_ref, lg_ref, lb_ref, cv_ref, 0, sub)
    prepare()
    for r0 in range(0, sub, CONV_RB):
        row_block(r0)
    hist_s[...] = e_ref[0, sub:ext, :]

    qkvo_ref[...], u_s[i % 2, j] = _proj_step(h_ref, w_ref)

    @pl.when((j == PROJ_STEPS - 1) & (p >= 0) & (p % tiles_per_seq == tiles_per_seq - 1))
    def _():
        ns_ref[0] = e_ref[0, ext - CONV_STATE:ext, :]


def _proj_in_conv(x, g, w_main, wg, bg, chunk, state_pad, conv_w, conv_b, ln_g, ln_b, seq_len):
    n = x.shape[0]
    tm = PROJ_TM
    n_tiles = n // tm
    tiles_per_seq = seq_len // tm
    bsz = n // seq_len
    assert tm == PROJ_STEPS * CONV_SUB and seq_len % tm == 0
    last = n_tiles - 1

    def tile(i, j):
        return jnp.minimum(i, last)

    def seq(i, j):
        return jnp.clip((i - 1) // tiles_per_seq, 0, bsz - 1)

    const2 = lambda i, j: (0, 0)
    return pl.pallas_call(
        functools.partial(_proj_in_conv_kernel, chunk=chunk, tiles_per_seq=tiles_per_seq),
        grid=(n_tiles + 1, PROJ_STEPS),
        in_specs=[
            pl.BlockSpec((tm, D_MODEL), lambda i, j: (tile(i, j), 0)),
            pl.BlockSpec((1, D_MODEL), const2),
            pl.BlockSpec((D_MODEL, PROJ_TN), lambda i, j: (0, j)),
            pl.BlockSpec((2 * M_HEADS, D_MODEL), const2),
            pl.BlockSpec((2 * M_HEADS, 1), const2),
            pl.BlockSpec((1, CONV_PAD, CONV_CH), lambda i, j: (seq(i, j), 0, 0)),
            pl.BlockSpec((CONV_WIDTH, CONV_CH), const2),
            pl.BlockSpec((1, CONV_CH), const2),
            pl.BlockSpec((1, CONV_CH), const2),
            pl.BlockSpec((1, CONV_CH), const2),
        ],
        out_specs=[
            pl.BlockSpec((tm, QKVO_TN), lambda i, j: (tile(i, j), jnp.where(i <= last, j, PROJ_STEPS - 1))),
            pl.BlockSpec((tm // chunk, 2 * M_HEADS, chunk), lambda i, j: (tile(i, j), 0, 0)),
            pl.BlockSpec((CONV_SUB, CONV_CH), lambda i, j: (jnp.maximum((i - 1) * PROJ_STEPS + j, 0), 0)),
            pl.BlockSpec((1, CONV_STATE, CONV_CH), lambda i, j: (seq(i, j), 0, 0)),
        ],
        out_shape=[
            jax.ShapeDtypeStruct((n, 2 * QK_W + 2 * M_WIDTH), BF16),
            jax.ShapeDtypeStruct((n // chunk, 2 * M_HEADS, chunk), F32),
            jax.ShapeDtypeStruct((n, CONV_CH), BF16),
            jax.ShapeDtypeStruct((bsz, CONV_STATE, CONV_CH), F32),
        ],
        scratch_shapes=[
            pltpu.VMEM((tm, D_MODEL), BF16),
            pltpu.VMEM((2, PROJ_STEPS, tm, GLU_TN), BF16),
            pltpu.VMEM((CONV_PAD, CONV_CH), F32),
            pltpu.VMEM((1, CONV_SUB + CONV_PAD + SUBLANES, CONV_CH), F32),
            pltpu.VMEM((SUBLANES - 1, CONV_SUB + CONV_PAD, CONV_CH), F32),
            pltpu.VMEM((CONV_SUB, CONV_CH), F32),
        ],
        compiler_params=_cparams(("arbitrary", "arbitrary"), 56),
        name="proj_in_conv",
    )(x, g, w_main, wg, bg, state_pad, conv_w, conv_b, ln_g, ln_b)


def _conv_stages(e_ref, er_ref, c_ref, w_ref, cb_ref, lg_ref, lb_ref, out_ref, b, rows):
    ext = rows + CONV_PAD
    base = CONV_PAD - CONV_STATE

    def prepare():
        for r in range(1, SUBLANES):
            er_ref[r - 1] = e_ref[b, r:r + ext, :]

    def row_block(r0):
        for lc in range(CONV_CH // CONV_LC):
            ls = slice(lc * CONV_LC, (lc + 1) * CONV_LC)
            acc = jnp.zeros((CONV_RB, CONV_LC), F32) + cb_ref[:, ls]
            for j in range(CONV_WIDTH):
                off = j + base
                a, r = (off // SUBLANES) * SUBLANES, off % SUBLANES
                if r == 0:
                    tap = e_ref[b, pl.ds(r0 + a, CONV_RB), ls]
                else:
                    tap = er_ref[r - 1, pl.ds(r0 + a, CONV_RB), ls]
                acc = acc + w_ref[j:j + 1, ls] * tap
            c_ref[pl.ds(r0, CONV_RB), ls] = acc
        c = c_ref[pl.ds(r0, CONV_RB), :]
        mu = jnp.mean(c, axis=-1, keepdims=True)
        cc = c - mu
        var = jnp.mean(cc * cc, axis=-1, keepdims=True)
        cn = cc * lax.rsqrt(var + EPS) * lg_ref[...] + lb_ref[...]
        out_ref[pl.ds(r0, CONV_RB), :] = (cn * _sigmoid(cn)).astype(BF16)

    return prepare, row_block


def _conv_rows(e_ref, er_ref, c_ref, w_ref, cb_ref, lg_ref, lb_ref, out_ref, b, rows):
    prepare, row_block = _conv_stages(e_ref, er_ref, c_ref, w_ref, cb_ref, lg_ref, lb_ref, out_ref, b, rows)
    prepare()

    def body(rb, carry):
        row_block(pl.multiple_of(rb * CONV_RB, CONV_RB))
        return carry

    lax.fori_loop(0, rows // CONV_RB, body, 0)


def _conv_kernel(u_ref, st_ref, w_ref, cb_ref, lg_ref, lb_ref, cv_ref, ns_ref, e_ref, er_ref, c_ref, *, bb, tt):
    t = pl.program_id(1)
    nt = pl.num_programs(1)
    ext = tt + CONV_PAD

    @pl.when(t == 0)
    def _():
        e_ref[:, 0:CONV_PAD, :] = st_ref[...]

    @pl.when(t > 0)
    def _():
        e_ref[:, 0:CONV_PAD, :] = e_ref[:, tt:ext, :]

    e_ref[:, CONV_PAD:ext, :] = u_ref[...].astype(F32)
    e_ref[:, ext:, :] = jnp.zeros((bb, SUBLANES, CONV_CH), F32)

    for b in range(bb):
        _conv_rows(e_ref, er_ref, c_ref, w_ref, cb_ref, lg_ref, lb_ref, cv_ref.at[b], b, tt)

    @pl.when(t == nt - 1)
    def _():
        ns_ref[...] = e_ref[:, ext - CONV_STATE:ext, :]


def _conv(u, state_pad, conv_w, conv_b, ln_g, ln_b, bb, tt):
    bsz, t, _ = u.shape
    return pl.pallas_call(
        functools.partial(_conv_kernel, bb=bb, tt=tt),
        grid=(bsz // bb, t // tt),
        in_specs=[
            pl.BlockSpec((bb, tt, CONV_CH), lambda b, s: (b, s, 0)),
            pl.BlockSpec((bb, CONV_PAD, CONV_CH), lambda b, s: (b, 0, 0)),
            pl.BlockSpec((CONV_WIDTH, CONV_CH), lambda b, s: (0, 0)),
            pl.BlockSpec((1, CONV_CH), lambda b, s: (0, 0)),
            pl.BlockSpec((1, CONV_CH), lambda b, s: (0, 0)),
            pl.BlockSpec((1, CONV_CH), lambda b, s: (0, 0)),
        ],
        out_specs=[
            pl.BlockSpec((bb, tt, CONV_CH), lambda b, s: (b, s, 0)),
            pl.BlockSpec((bb, CONV_STATE, CONV_CH), lambda b, s: (b, 0, 0)),
        ],
        out_shape=[
            jax.ShapeDtypeStruct((bsz, t, CONV_CH), BF16),
            jax.ShapeDtypeStruct((bsz, CONV_STATE, CONV_CH), F32),
        ],
        scratch_shapes=[
            pltpu.VMEM((bb, tt + CONV_PAD + SUBLANES, CONV_CH), F32),
            pltpu.VMEM((SUBLANES - 1, tt + CONV_PAD, CONV_CH), F32),
            pltpu.VMEM((tt, CONV_CH), F32),
        ],
        compiler_params=_cparams(("parallel", "arbitrary"), 40),
        name="conv",
    )(u, state_pad, conv_w, conv_b, ln_g, ln_b)


def _mlstm_kernel(q_ref, k_ref, v_ref, o_ref, gt_ref, c0_ref, n0_ref, m0_ref, ng_ref,
                  hm_ref, cn_ref, nn_ref, mn_ref, c_s, n_s, m_s, *, bb, L):
    ci = pl.program_id(1)
    nc = pl.num_programs(1)

    @pl.when(ci == 0)
    def _():
        c_s[...] = c0_ref[...]
        n_s[...] = n0_ref[...]
        m_s[...] = m0_ref[...]

    rt = lax.broadcasted_iota(jnp.int32, (L, L), 0)
    cs = lax.broadcasted_iota(jnp.int32, (L, L), 1)
    causal = cs <= rt
    lower = jnp.where(causal, 1.0, 0.0).astype(BF16)
    upper = jnp.where(rt <= cs, 1.0, 0.0).astype(BF16)
    nt_dims = (((1,), (1,)), ((), ()))
    scale = M_DK ** -0.5

    for b in range(bb):
        rows = gt_ref[b]
        cum_rows = jnp.zeros((2 * M_HEADS, L), F32)
        cum_cols = jnp.zeros((L, 2 * M_HEADS), F32)
        rest = rows
        for _ in range(3):
            piece = rest.astype(BF16)
            rest = rest - piece.astype(F32)
            cum_rows = cum_rows + jnp.dot(piece, upper, preferred_element_type=F32)
            cum_cols = cum_cols + lax.dot_general(lower, piece, nt_dims, preferred_element_type=F32)
        heads = range(M_HEADS)
        ig_row = [rows[h:h + 1, :] for h in heads]
        b_row = [cum_rows[M_HEADS + h:M_HEADS + h + 1, :] for h in heads]
        b_col = [cum_cols[:, M_HEADS + h:M_HEADS + h + 1] for h in heads]
        m0 = [m_s[b, h:h + 1, 0:1] for h in heads]
        q = [q_ref[b, :, h * M_DK:(h + 1) * M_DK] for h in heads]
        k = [k_ref[b, :, h * M_DK:(h + 1) * M_DK] for h in heads]
        v = [v_ref[b, :, h * M_DV:(h + 1) * M_DV] for h in heads]
        c0 = [c_s[b, h] for h in heads]
        n0 = [n_s[b, h:h + 1, :] for h in heads]

        g_col = [b_col[h] + m0[h] for h in heads]
        dmat = [jnp.where(causal, b_col[h] - b_row[h] + ig_row[h], -jnp.inf) for h in heads]
        m_col = [jnp.maximum(g_col[h], jnp.max(dmat[h], axis=1, keepdims=True)) for h in heads]
        w_state = [jnp.exp(g_col[h] - m_col[h]) * scale for h in heads]
        decay = [jnp.exp(dmat[h] - m_col[h]) * scale for h in heads]
        qk = [lax.dot_general(q[h], k[h], nt_dims, preferred_element_type=F32) for h in heads]
        s = [qk[h] * decay[h] for h in heads]
        qc = [jnp.dot(q[h], c0[h].astype(BF16), preferred_element_type=F32) for h in heads]
        num = [jnp.dot(s[h].astype(BF16), v[h], preferred_element_type=F32) + w_state[h] * qc[h] for h in heads]
        qn = [jnp.sum(q[h].astype(F32) * n0[h], axis=1, keepdims=True) for h in heads]
        den = [jnp.sum(s[h], axis=1, keepdims=True) + w_state[h] * qn[h] for h in heads]
        hh = [num[h] / jnp.maximum(jnp.abs(den[h]), jnp.exp(-m_col[h])) for h in heads]
        for h in heads:
            hn = hh[h] * lax.rsqrt(jnp.mean(hh[h] * hh[h], axis=-1, keepdims=True) + EPS) * ng_ref[h:h + 1, :]
            og = _sigmoid(o_ref[b, :, h * M_DV:(h + 1) * M_DV].astype(F32))
            hm_ref[b, :, h * M_DV:(h + 1) * M_DV] = (og * hn).astype(BF16)

        b_last = [b_row[h][:, L - 1:L] for h in heads]
        g_last = [b_last[h] + m0[h] for h in heads]
        wk_row = [b_last[h] - b_row[h] + ig_row[h] for h in heads]
        m_new = [jnp.maximum(g_last[h], jnp.max(wk_row[h], axis=1, keepdims=True)) for h in heads]
        a0 = [jnp.exp(g_last[h] - m_new[h]) for h in heads]
        ak_row = [jnp.exp(wk_row[h] - m_new[h]) for h in heads]
        kts = [(k[h].T.astype(F32) * ak_row[h]).astype(BF16) for h in heads]
        kv = [jnp.dot(kts[h], v[h], preferred_element_type=F32) for h in heads]
        kn = [jnp.dot(ak_row[h].astype(BF16), k[h], preferred_element_type=F32) for h in heads]
        for h in heads:
            c_s[b, h] = a0[h] * c0[h] + kv[h]
            n_s[b, h:h + 1, :] = a0[h] * n0[h] + kn[h]
            m_s[b, h:h + 1, :] = jnp.broadcast_to(m_new[h], (1, LANES))

    @pl.when(ci == nc - 1)
    def _():
        cn_ref[...] = c_s[...]
        nn_ref[...] = n_s[...]
        mn_ref[...] = m_s[...]


def _mlstm(qkvo, gates, c0, n0, m0, ng, bb, L):
    bsz, t, _ = qkvo.shape
    nc = t // L
    gates = gates.reshape(bsz, nc, 2 * M_HEADS, L)
    st_c = pl.BlockSpec((bb, M_HEADS, M_DK, M_DV), lambda b, c: (b, 0, 0, 0))
    st_n = pl.BlockSpec((bb, M_HEADS, M_DK), lambda b, c: (b, 0, 0))
    return pl.pallas_call(
        functools.partial(_mlstm_kernel, bb=bb, L=L),
        grid=(bsz // bb, nc),
        in_specs=[
            pl.BlockSpec((bb, L, QK_W), lambda b, c: (b, c, 0)),
            pl.BlockSpec((bb, L, QK_W), lambda b, c: (b, c, 1)),
            pl.BlockSpec((bb, L, M_WIDTH), lambda b, c: (b, c, 1)),
            pl.BlockSpec((bb, L, M_WIDTH), lambda b, c: (b, c, 2)),
            pl.BlockSpec((bb, None, 2 * M_HEADS, L), lambda b, c: (b, c, 0, 0)),
            st_c, st_n, st_n,
            pl.BlockSpec((M_HEADS, M_DV), lambda b, c: (0, 0)),
        ],
        out_specs=[
            pl.BlockSpec((bb, L, M_WIDTH), lambda b, c: (b, c, 0)),
            st_c, st_n, st_n,
        ],
        out_shape=[
            jax.ShapeDtypeStruct((bsz, t, M_WIDTH), BF16),
            jax.ShapeDtypeStruct((bsz, M_HEADS, M_DK, M_DV), F32),
            jax.ShapeDtypeStruct((bsz, M_HEADS, M_DK), F32),
            jax.ShapeDtypeStruct((bsz, M_HEADS, LANES), F32),
        ],
        scratch_shapes=[
            pltpu.VMEM((bb, M_HEADS, M_DK, M_DV), F32),
            pltpu.VMEM((bb, M_HEADS, M_DK), F32),
            pltpu.VMEM((bb, M_HEADS, LANES), F32),
        ],
        compiler_params=_cparams(("parallel", "arbitrary"), 40),
        name="mlstm",
    )(qkvo, qkvo, qkvo, qkvo, gates, c0, n0, m0, ng)


def _proj_out_kernel(x_ref, hm_ref, cv_ref, wa_ref, wb_ref, o_ref):
    acc = jnp.dot(hm_ref[...], wa_ref[...], preferred_element_type=F32)
    acc = acc + jnp.dot(cv_ref[...], wb_ref[...], preferred_element_type=F32)
    o_ref[...] = x_ref[...] + acc


def _proj_out(x, hm, cv, wa, wb):
    n = x.shape[0]
    tm = min(TOK_TILE, n)
    return pl.pallas_call(
        _proj_out_kernel,
        grid=(n // tm,),
        in_specs=[
            pl.BlockSpec((tm, D_MODEL), lambda i: (i, 0)),
            pl.BlockSpec((tm, M_WIDTH), lambda i: (i, 0)),
            pl.BlockSpec((tm, CONV_CH), lambda i: (i, 0)),
            pl.BlockSpec((M_WIDTH, D_MODEL), lambda i: (0, 0)),
            pl.BlockSpec((CONV_CH, D_MODEL), lambda i: (0, 0)),
        ],
        out_specs=pl.BlockSpec((tm, D_MODEL), lambda i: (i, 0)),
        out_shape=jax.ShapeDtypeStruct((n, D_MODEL), F32),
        compiler_params=_cparams(("parallel",), 48),
        name="proj_out",
    )(x, hm, cv, wa, wb)


def _mem_kv_kernel(m_ref, g_ref, w_ref, k_ref, v_ref):
    mn = _rms(m_ref[...], g_ref[...]).astype(BF16)
    z = jnp.dot(mn, w_ref[...], preferred_element_type=F32)
    k_ref[...] = z[:, :CA_W]
    v_ref[...] = z[:, CA_W:]


def _mem_kv(mem, g, wkv):
    n = mem.shape[0]
    return pl.pallas_call(
        _mem_kv_kernel,
        grid=(1,),
        in_specs=[
            pl.BlockSpec((n, D_MODEL), lambda i: (0, 0)),
            pl.BlockSpec((1, D_MODEL), lambda i: (0, 0)),
            pl.BlockSpec((D_MODEL, 2 * CA_W), lambda i: (0, 0)),
        ],
        out_specs=[pl.BlockSpec((n, CA_W), lambda i: (0, 0))] * 2,
        out_shape=[jax.ShapeDtypeStruct((n, CA_W), F32)] * 2,
        compiler_params=_cparams(("arbitrary",), 40),
        name="mem_kv",
    )(mem, g, wkv)


def _attn_router_kernel(x_ref, gca_ref, wq_ref, k_ref, v_ref, wo_ref, gmoe_ref, rw_ref, rb_ref, cnt0_ref,
                        x2_ref, hn_ref, te_ref, tg_ref, tr_ref, cnt_ref, o_s, cnt_s, *, bb, tt):
    rows = bb * tt

    @pl.when((pl.program_id(0) == 0) & (pl.program_id(1) == 0))
    def _():
        cnt_s[...] = cnt0_ref[...]

    parts = range(ATTN_SPLIT)
    pr = rows // ATTN_SPLIT
    bpp = bb // ATTN_SPLIT

    def part_ref(ref, i):
        return ref.at[0, i * pr:(i + 1) * pr, :] if bb == 1 else ref.at[i * bpp:(i + 1) * bpp]

    seqs = [[(0, pr, 0)] if bb == 1 else [(j * tt, (j + 1) * tt, i * bpp + j) for j in range(bpp)] for i in parts]

    x = [part_ref(x_ref, i)[...].reshape(pr, D_MODEL) for i in parts]
    h = [_rms(x[i], gca_ref[...]).astype(BF16) for i in parts]
    q = [(jnp.dot(h[i], wq_ref[...], preferred_element_type=F32) * (CA_DH ** -0.5)).astype(BF16) for i in parts]
    units = [(i, r0, r1, b, slice(hd * CA_DH, (hd + 1) * CA_DH))
             for i in parts for r0, r1, b in seqs[i] for hd in range(CA_HEADS)]
    nt_dims = (((1,), (1,)), ((), ()))
    s = [lax.dot_general(q[i][r0:r1, cols], k_ref[b, :, cols], nt_dims, preferred_element_type=F32)
         for i, r0, r1, b, cols in units]
    e = [jnp.exp(si - jnp.max(si, axis=-1, keepdims=True)) for si in s]
    ov = [jnp.dot(ei.astype(BF16), v_ref[b, :, cols], preferred_element_type=F32)
          for ei, (i, r0, r1, b, cols) in zip(e, units)]
    for ei, ovi, (i, r0, r1, b, cols) in zip(e, ov, units):
        o_s[i * pr + r0:i * pr + r1, cols] = (ovi / jnp.sum(ei, axis=-1, keepdims=True)).astype(BF16)
    x2 = [x[i] + jnp.dot(o_s[i * pr:(i + 1) * pr, :], wo_ref[...], preferred_element_type=F32) for i in parts]
    for i in parts:
        out = part_ref(x2_ref, i)
        out[...] = x2[i].reshape(out.shape)

    hn = [_rms(x2[i], gmoe_ref[...]) for i in parts]
    for i in parts:
        hn_ref[i * pr:(i + 1) * pr, :] = _pack_pairs(hn[i][:, :HALF_D], hn[i][:, HALF_D:])
    work = [jnp.dot(hn[i].astype(BF16), rw_ref[...], preferred_element_type=F32) + rb_ref[...] for i in parts]
    lane = lax.broadcasted_iota(jnp.int32, (pr, LANES), 1)
    lane_f = lane.astype(F32)
    sel = [[] for _ in parts]
    vals = [[] for _ in parts]
    for _ in range(TOP_K):
        mx = [jnp.max(work[i], axis=-1, keepdims=True) for i in parts]
        idx = [jnp.min(jnp.where(work[i] == mx[i], lane_f, float(LANES)), axis=-1, keepdims=True) for i in parts]
        hit = [lane_f == idx[i] for i in parts]
        work = [jnp.where(hit[i], -jnp.inf, work[i]) for i in parts]
        for i in parts:
            sel[i].append((idx[i], hit[i]))
            vals[i].append(mx[i])
    ex = [[jnp.exp(vv - vals[i][0]) for vv in vals[i]] for i in parts]
    tot = [ex[i][0] + ex[i][1] + ex[i][2] + ex[i][3] for i in parts]
    assigned = []
    for i in parts:
        a = jnp.zeros((pr, LANES), F32)
        for _, hit_k in sel[i]:
            a = a + hit_k.astype(F32)
        assigned.append(a)
    rt = lax.broadcasted_iota(jnp.int32, (pr, pr), 0)
    cs = lax.broadcasted_iota(jnp.int32, (pr, pr), 1)
    strict = (cs < rt).astype(BF16)
    inside = [jnp.dot(strict, assigned[i].astype(BF16), preferred_element_type=F32) for i in parts]
    count = cnt_s[...]
    for i in parts:
        before = inside[i] + count
        count = count + jnp.sum(assigned[i], axis=0, keepdims=True)
        te = jnp.zeros((pr, LANES), F32)
        tg = jnp.zeros((pr, LANES), F32)
        tr = jnp.zeros((pr, LANES), F32)
        for kk, (idx_k, hit_k) in enumerate(sel[i]):
            rank = jnp.sum(jnp.where(hit_k, before, 0.0), axis=-1, keepdims=True)
            te = jnp.where(lane == kk, idx_k, te)
            tg = jnp.where(lane == kk, ex[i][kk] / tot[i], tg)
            tr = jnp.where(lane == kk, rank, tr)
        te_ref[i * pr:(i + 1) * pr, :] = te.astype(jnp.int32)
        tg_ref[i * pr:(i + 1) * pr, :] = tg
        tr_ref[i * pr:(i + 1) * pr, :] = tr.astype(jnp.int32)
    cnt_s[...] = count
    cnt_ref[...] = count


def _attn_router(x1, mem_k, mem_v, cnt0, p, bb, tt):
    bsz, t, _ = x1.shape
    rows = bb * tt
    n = bsz * t
    nb, nt = bsz // bb, t // tt
    tok = lambda b, s: (b * nt + s, 0)
    const = lambda b, s: (0, 0)
    grid_spec = dict(
        grid=(nb, nt),
        in_specs=[
            pl.BlockSpec((bb, tt, D_MODEL), lambda b, s: (b, s, 0)),
            pl.BlockSpec((1, D_MODEL), const),
            pl.BlockSpec((D_MODEL, CA_W), const),
            pl.BlockSpec((bb, N_MEM, CA_W), lambda b, s: (b, 0, 0)),
            pl.BlockSpec((bb, N_MEM, CA_W), lambda b, s: (b, 0, 0)),
            pl.BlockSpec((CA_W, D_MODEL), const),
            pl.BlockSpec((1, D_MODEL), const),
            pl.BlockSpec((D_MODEL, LANES), const),
            pl.BlockSpec((1, LANES), const),
            pl.BlockSpec((1, LANES), const),
        ],
        out_specs=[
            pl.BlockSpec((bb, tt, D_MODEL), lambda b, s: (b, s, 0)),
            pl.BlockSpec((rows, HALF_D), tok),
            pl.BlockSpec((rows, LANES), tok),
            pl.BlockSpec((rows, LANES), tok),
            pl.BlockSpec((rows, LANES), tok),
            pl.BlockSpec((1, LANES), const),
        ],
    )
    return pl.pallas_call(
        functools.partial(_attn_router_kernel, bb=bb, tt=tt),
        out_shape=[
            jax.ShapeDtypeStruct((bsz, t, D_MODEL), F32),
            jax.ShapeDtypeStruct((n, HALF_D), jnp.uint32),
            jax.ShapeDtypeStruct((n, LANES), jnp.int32),
            jax.ShapeDtypeStruct((n, LANES), F32),
            jax.ShapeDtypeStruct((n, LANES), jnp.int32),
            jax.ShapeDtypeStruct((1, LANES), F32),
        ],
        scratch_shapes=[pltpu.VMEM((rows, CA_W), BF16), pltpu.VMEM((1, LANES), F32)],
        compiler_params=_cparams(("arbitrary", "arbitrary"), 48),
        name="attn_router",
        **grid_spec,
    )(x1, p["g_ca"], p["wq"], mem_k, mem_v, p["wo"], p["g_moe"], p["router_w"], p["router_b"], cnt0)


def _sc_scatter_rows(rows_a, rows_b, dest3, n_out):
    d = rows_a.shape[1]
    nwin = dest3.shape[0]
    nwin_a = rows_a.shape[0] // SC_WIN
    mesh = plsc.VectorSubcoreMesh(core_axis_name="c", subcore_axis_name="s")
    workers = mesh.num_cores * mesh.num_subcores
    assert nwin * SC_WIN == rows_a.shape[0] + rows_b.shape[0] and nwin_a * SC_WIN == rows_a.shape[0]
    assert nwin % workers == 0, (nwin, workers)
    per = nwin // workers

    @functools.partial(
        pl.kernel, out_type=jax.ShapeDtypeStruct((n_out, d), rows_a.dtype), mesh=mesh,
        scratch_types=[pltpu.VMEM((SC_WIN, d), rows_a.dtype), pltpu.VMEM((TOP_K, SC_WIN), jnp.int32),
                       pltpu.SemaphoreType.DMA])
    def scatter(xa_hbm, xb_hbm, i_hbm, o_hbm, buf, idx, sem):
        wid = lax.axis_index("s") * mesh.num_cores + lax.axis_index("c")

        @pl.loop(0, per)
        def _(j):
            w = wid * per + j

            @pl.when(w < nwin_a)
            def _():
                pltpu.sync_copy(xa_hbm.at[pl.ds(w * SC_WIN, SC_WIN)], buf)

            @pl.when(w >= nwin_a)
            def _():
                pltpu.sync_copy(xb_hbm.at[pl.ds((w - nwin_a) * SC_WIN, SC_WIN)], buf)

            pltpu.sync_copy(i_hbm.at[w], idx)
            copies = [pltpu.async_copy(buf, o_hbm.at[idx.at[kk]], sem) for kk in range(TOP_K)]
            for cp in copies:
                cp.wait()

    return scatter(rows_a, rows_b, dest3)


def _sc_gather_rows(y, dest3):
    d = y.shape[1]
    nwin = dest3.shape[0]
    n = nwin * SC_GWIN
    mesh = plsc.VectorSubcoreMesh(core_axis_name="c", subcore_axis_name="s")
    workers = mesh.num_cores * mesh.num_subcores
    assert nwin % (2 * workers) == 0, (nwin, workers)
    per = nwin // workers

    @functools.partial(
        pl.kernel, out_type=jax.ShapeDtypeStruct((TOP_K, n, d), y.dtype), mesh=mesh,
        scratch_types=[pltpu.VMEM((2 * TOP_K, SC_GWIN, d), y.dtype), pltpu.VMEM((2, TOP_K, SC_GWIN), jnp.int32),
                       pltpu.SemaphoreType.DMA, pltpu.SemaphoreType.DMA, pltpu.SemaphoreType.DMA])
    def gather(y_hbm, i_hbm, o_hbm, buf, idx, sem_gather, sem_write, sem_idx):
        base = (lax.axis_index("s") * mesh.num_cores + lax.axis_index("c")) * per

        def index_copy(w, s):
            return pltpu.make_async_copy(i_hbm.at[w], idx.at[s], sem_idx)

        def write_copy(w, s, kk):
            return pltpu.make_async_copy(buf.at[s * TOP_K + kk], o_hbm.at[kk, pl.ds(w * SC_GWIN, SC_GWIN)], sem_write)

        index_copy(base, 0).start()

        @pl.loop(0, per // 2)
        def _(g):
            for s in range(2):
                w = base + 2 * g + s
                index_copy(w, s).wait()
                if s == 0:
                    index_copy(w + 1, 1).start()
                else:
                    @pl.when(g + 1 < per // 2)
                    def _():
                        index_copy(w + 1, 0).start()

                @pl.when(g > 0)
                def _():
                    for kk in range(TOP_K):
                        write_copy(w, s, kk).wait()

                gathers = [pltpu.make_async_copy(y_hbm.at[idx.at[s, kk]], buf.at[s * TOP_K + kk], sem_gather)
                           for kk in range(TOP_K)]
                for cp in gathers:
                    cp.start()
                for kk in range(TOP_K):
                    gathers[kk].wait()
                    write_copy(w, s, kk).start()

        for s in range(2):
            for kk in range(TOP_K):
                write_copy(base, s, kk).wait()

    return gather(y, dest3)


def _expert_changed(te_ref):
    t = pl.program_id(1)
    return (t == 0) | (te_ref[t] != te_ref[jnp.maximum(t - 1, 0)])


def _moe_up_kernel(te_ref, nu_ref, x_ref, w1g_ref, w1l_ref, b1g_ref, b1l_ref, act_ref, wg_s, wl_s):
    @pl.when(_expert_changed(te_ref))
    def _():
        wg_s[...] = w1g_ref[0].astype(BF16)
        wl_s[...] = w1l_ref[0].astype(BF16)

    def up(words):
        lo, hi = _unpack_pairs(words)
        x = jnp.concatenate([lo.astype(BF16), hi.astype(BF16)], axis=1)
        glu = jnp.dot(x, wg_s[...], preferred_element_type=F32) + b1g_ref[0]
        lin = jnp.dot(x, wl_s[...], preferred_element_type=F32) + b1l_ref[0]
        glu = jnp.minimum(glu, SWIGLU_LIMIT)
        lin = jnp.clip(lin, -SWIGLU_LIMIT, SWIGLU_LIMIT)
        act_ref[...] = (glu * _sigmoid(SWIGLU_ALPHA * glu) * (lin + 1.0)).astype(BF16)

    t = pl.program_id(1)
    used = t < nu_ref[0]
    valid = nu_ref[1 + t]

    @pl.when(used & (valid >= MOE_TM))
    def _():
        up(x_ref[...])

    @pl.when(used & (valid < MOE_TM))
    def _():
        rowid = lax.broadcasted_iota(jnp.int32, (MOE_TM, HALF_D), 0)
        up(jnp.where(rowid < valid, x_ref[...], jnp.uint32(0)))


def _moe_down_kernel(te_ref, nu_ref, a_ref, w2_ref, b2_ref, y_ref, w2_s):
    @pl.when(_expert_changed(te_ref))
    def _():
        w2_s[...] = w2_ref[0].astype(BF16)

    @pl.when(pl.program_id(1) < nu_ref[0])
    def _():
        y = jnp.dot(a_ref[...], w2_s[...], preferred_element_type=F32) + b2_ref[0]
        y_ref[...] = _pack_pairs(y[:, :MOE_TN // 2], y[:, MOE_TN // 2:])


def _moe(tile_e, n_used, x_sorted, w1, b1, w2, b2):
    r = x_sorted.shape[0]
    n_tiles = r // MOE_TM
    nf = D_FF // MOE_TF
    nn = D_MODEL // MOE_TN

    def row(c, t, te, nu):
        return jnp.maximum(jnp.minimum(t, nu[0] - 1), 0)

    act = pl.pallas_call(
        _moe_up_kernel,
        grid_spec=pltpu.PrefetchScalarGridSpec(
            num_scalar_prefetch=2,
            grid=(nf, n_tiles),
            in_specs=[
                pl.BlockSpec((MOE_TM, HALF_D), lambda c, t, te, nu: (row(c, t, te, nu), 0)),
                pl.BlockSpec((1, D_MODEL, MOE_TF), lambda c, t, te, nu: (te[t], 0, c)),
                pl.BlockSpec((1, D_MODEL, MOE_TF), lambda c, t, te, nu: (te[t], 0, nf + c)),
                pl.BlockSpec((1, 1, MOE_TF), lambda c, t, te, nu: (te[t], 0, c)),
                pl.BlockSpec((1, 1, MOE_TF), lambda c, t, te, nu: (te[t], 0, nf + c)),
            ],
            out_specs=pl.BlockSpec((MOE_TM, MOE_TF), lambda c, t, te, nu: (row(c, t, te, nu), c)),
            scratch_shapes=[pltpu.VMEM((D_MODEL, MOE_TF), BF16), pltpu.VMEM((D_MODEL, MOE_TF), BF16)],
        ),
        out_shape=jax.ShapeDtypeStruct((r, D_FF), BF16),
        compiler_params=_cparams(("arbitrary", "arbitrary"), MOE_VMEM_MB),
        name="moe_up",
    )(tile_e, n_used, x_sorted, w1, w1, b1, b1)

    return pl.pallas_call(
        _moe_down_kernel,
        grid_spec=pltpu.PrefetchScalarGridSpec(
            num_scalar_prefetch=2,
            grid=(nn, n_tiles),
            in_specs=[
                pl.BlockSpec((MOE_TM, D_FF), lambda c, t, te, nu: (row(c, t, te, nu), 0)),
                pl.BlockSpec((1, D_FF, MOE_TN), lambda c, t, te, nu: (te[t], 0, c)),
                pl.BlockSpec((1, 1, MOE_TN), lambda c, t, te, nu: (te[t], 0, c)),
            ],
            out_specs=pl.BlockSpec((MOE_TM, MOE_TN // 2), lambda c, t, te, nu: (row(c, t, te, nu), c)),
            scratch_shapes=[pltpu.VMEM((D_FF, MOE_TN), BF16)],
        ),
        out_shape=jax.ShapeDtypeStruct((r, HALF_D), jnp.uint32),
        compiler_params=_cparams(("arbitrary", "arbitrary"), MOE_VMEM_MB),
        name="moe_down",
    )(tile_e, n_used, act, w2, b2)


def _combine_kernel(x_ref, y0_ref, y1_ref, y2_ref, y3_ref, tg_ref, g_ref, o_ref):
    tg = tg_ref[...]
    acc = x_ref[...]
    half = MOE_TN // 2
    for kk, y_ref in enumerate((y0_ref, y1_ref, y2_ref, y3_ref)):
        cols = []
        for c in range(D_MODEL // MOE_TN):
            cols += list(_unpack_pairs(y_ref[0, :, c * half:(c + 1) * half]))
        acc = acc + jnp.concatenate(cols, axis=1) * tg[:, kk:kk + 1]
    o_ref[...] = _rms(acc, g_ref[...])


def _combine(x2, yg, tg, g, row0):
    n = x2.shape[0]
    tm = min(TOK_TILE, n)
    blk0 = row0 // tm
    yspec = [pl.BlockSpec((1, tm, HALF_D), functools.partial(lambda i, kk: (kk, blk0 + i, 0), kk=kk))
             for kk in range(TOP_K)]
    return pl.pallas_call(
        _combine_kernel,
        grid=(n // tm,),
        in_specs=[pl.BlockSpec((tm, D_MODEL), lambda i: (i, 0))] + yspec + [
            pl.BlockSpec((tm, LANES), lambda i: (i, 0)),
            pl.BlockSpec((1, D_MODEL), lambda i: (0, 0)),
        ],
        out_specs=pl.BlockSpec((tm, D_MODEL), lambda i: (i, 0)),
        out_shape=jax.ShapeDtypeStruct((n, D_MODEL), F32),
        compiler_params=_cparams(("parallel",), 48),
        name="combine",
    )(x2, yg, yg, yg, yg, tg, g)


def _group_tiles(bsz, t):
    tt = min(t, TOK_TILE)
    bb = max(1, min(bsz, TOK_TILE // tt))
    return bb, tt


def _layer_group(x, c0, n0, m0, conv_state, mem_k, mem_v, cnt0, p):
    bsz, t, _ = x.shape
    n = bsz * t
    L = min(MLSTM_CHUNK, t)
    state_pad = jnp.pad(conv_state, ((0, 0), (CONV_PAD - CONV_STATE, 0), (0, 0)))
    conv_args = (state_pad, p["conv_w"], p["conv_b"], p["ln_g"], p["ln_b"])
    if t % PROJ_TM == 0:
        qkvo, gates, cv, new_conv = _proj_in_conv(x.reshape(n, D_MODEL), p["g_mix"], p["w_main"], p["wg"], p["bg"],
                                                  L, *conv_args, seq_len=t)
    else:
        qkvo, u, gates = _proj_in(x.reshape(n, D_MODEL), p["g_mix"], p["w_main"], p["wg"], p["bg"], L)
        ctt = min(CONV_TT, t)
        cbb = max(1, min(bsz, CONV_TT // ctt))
        cv, new_conv = _conv(u.reshape(bsz, t, CONV_CH), *conv_args, cbb, ctt)

    mbb = min(bsz, MLSTM_BB)
    m0b = jnp.broadcast_to(m0[:, :, None], (bsz, M_HEADS, LANES))
    hm, c1, n1, m1 = _mlstm(qkvo.reshape(bsz, t, -1), gates, c0, n0, m0b, p["mh_g"], mbb, L)

    x1 = _proj_out(x.reshape(n, D_MODEL), hm.reshape(n, M_WIDTH), cv.reshape(n, CONV_CH), p["wo_a"], p["wo_b"])

    abb, att = _group_tiles(bsz, t)
    x2, hn, te, tg, tr, cnt = _attn_router(x1.reshape(bsz, t, D_MODEL), mem_k, mem_v, cnt0, p, abb, att)
    return x2.reshape(n, D_MODEL), hn, te, tg, tr, cnt, (c1, n1, m1[:, :, 0], new_conv)


def kernel(x_prompt, x_sample, mem_prompt, state_mlstm_C, state_mlstm_n, state_mlstm_m, state_conv, cache_mem_k, cache_mem_v, norm_mix_g, w_in, b_gates, mh_norm_g, conv_w, conv_b, conv_ln_g, conv_ln_b, w_out, norm_ca_g, norm_mem_g, ca_wq, ca_wk, ca_wv, ca_wo, norm_moe_g, router_w, router_b, moe_w1, moe_b1, moe_w2, moe_b2, final_norm_g):
    assert w_in.shape[0] == 1, "single layer"
    bp, tp, _ = x_prompt.shape
    bs, ts, _ = x_sample.shape
    n_p, n_s = bp * tp, bs * ts
    n_all = n_p + n_s

    wi = w_in[0]
    o0 = 0
    parts = []
    for width in (QK_W, QK_W, M_WIDTH, M_WIDTH, M_HEADS, M_HEADS, CONV_CH, CONV_CH):
        parts.append(wi[:, o0:o0 + width])
        o0 += width
    w_q, w_k, w_v, w_o, w_gi, w_gf, w_ua, w_ug = parts
    col_tiles = []
    w_qkvo = wi[:, :2 * QK_W + 2 * M_WIDTH]
    for c in range(PROJ_STEPS):
        col_tiles += [w_qkvo[:, c * QKVO_TN:(c + 1) * QKVO_TN], w_ua[:, c * GLU_TN:(c + 1) * GLU_TN],
                      w_ug[:, c * GLU_TN:(c + 1) * GLU_TN]]
    p = {
        "g_mix": norm_mix_g[0][None, :],
        "w_main": jnp.concatenate(col_tiles, axis=1).astype(BF16),
        "wg": jnp.concatenate([w_gi, w_gf], axis=1).T.astype(BF16),
        "bg": b_gates[0][:, None],
        "mh_g": mh_norm_g[0],
        "conv_w": conv_w[0],
        "conv_b": conv_b[0][None, :],
        "ln_g": conv_ln_g[0][None, :],
        "ln_b": conv_ln_b[0][None, :],
        "wo_a": w_out[0][:M_WIDTH].astype(BF16),
        "wo_b": w_out[0][M_WIDTH:].astype(BF16),
        "g_ca": norm_ca_g[0][None, :],
        "wq": ca_wq[0].astype(BF16),
        "wo": ca_wo[0].astype(BF16),
        "g_moe": norm_moe_g[0][None, :],
        "router_w": jnp.pad(router_w[0], ((0, 0), (0, LANES - N_EXPERTS))).astype(BF16),
        "router_b": jnp.concatenate([router_b[0], jnp.full((LANES - N_EXPERTS,), NEG_BIG, F32)])[None, :],
    }

    wkv = jnp.concatenate([ca_wk[0], ca_wv[0]], axis=1).astype(BF16)
    mk, mv = _mem_kv(mem_prompt.reshape(bp * N_MEM, D_MODEL), norm_mem_g[0][None, :], wkv)
    mk = mk.reshape(bp, N_MEM, CA_W)
    mv = mv.reshape(bp, N_MEM, CA_W)
    cnt0 = jnp.zeros((1, LANES), F32)
    zc = jnp.zeros((bp, M_HEADS, M_DK, M_DV), F32)
    zn = jnp.zeros((bp, M_HEADS, M_DK), F32)
    zm = jnp.zeros((bp, M_HEADS), F32)
    zconv = jnp.zeros((bp, CONV_STATE, CONV_CH), F32)
    x2p, hn_p, te_p, tg_p, tr_p, cnt1, st_p = _layer_group(
        x_prompt, zc, zn, zm, zconv, mk.astype(BF16), mv.astype(BF16), cnt0, p)

    x2s, hn_s, te_s, tg_s, tr_s, cnt2, st_s = _layer_group(
        x_sample, state_mlstm_C[0], state_mlstm_n[0], state_mlstm_m[0], state_conv[0],
        cache_mem_k[0].reshape(bs, N_MEM, CA_W).astype(BF16), cache_mem_v[0].reshape(bs, N_MEM, CA_W).astype(BF16),
        cnt1, p)

    te = jnp.concatenate([te_p[:, :TOP_K], te_s[:, :TOP_K]], axis=0)
    tr = jnp.concatenate([tr_p[:, :TOP_K], tr_s[:, :TOP_K]], axis=0)
    counts = cnt2[0, :N_EXPERTS].astype(jnp.int32)
    tiles_per_e = (counts + MOE_TM - 1) // MOE_TM
    tile_end = jnp.cumsum(tiles_per_e)
    row_start = (tile_end - tiles_per_e) * MOE_TM
    n_tiles = -(-(n_all * TOP_K) // MOE_TM) + N_EXPERTS
    n_used = tile_end[-1:]
    tile_ids = jnp.minimum(jnp.arange(n_tiles, dtype=jnp.int32), n_used[0] - 1)
    tile_e = jnp.minimum(jnp.sum(tile_ids[:, None] >= tile_end[None, :], axis=1), N_EXPERTS - 1).astype(jnp.int32)
    tile_first = jnp.arange(n_tiles, dtype=jnp.int32) - (tile_end - tiles_per_e)[tile_e]
    tile_valid = jnp.clip(counts[tile_e] - tile_first * MOE_TM, 0, MOE_TM)
    tile_valid = jnp.where(jnp.arange(n_tiles) < n_used[0], tile_valid, 0).astype(jnp.int32)
    moe_meta = jnp.concatenate([n_used.astype(jnp.int32), tile_valid])
    dest = row_start[te] + tr
    dest3 = dest.reshape(n_all // SC_WIN, SC_WIN, TOP_K).transpose(0, 2, 1)
    x_sorted = _sc_scatter_rows(hn_p, hn_s, dest3, n_tiles * MOE_TM)

    y_sorted = _moe(tile_e, moe_meta, x_sorted, moe_w1[0], moe_b1[0][:, None, :], moe_w2[0],
                    moe_b2[0][:, None, :])

    yg = _sc_gather_rows(y_sorted, dest.reshape(n_all // SC_GWIN, SC_GWIN, TOP_K).transpose(0, 2, 1))
    fg = final_norm_g[None, :]
    y_prompt = _combine(x2p, yg, tg_p, fg, 0).reshape(bp, tp, D_MODEL)
    y_sample = _combine(x2s, yg, tg_s, fg, n_p).reshape(bs, ts, D_MODEL)

    c1, n1, m1, conv1 = st_p
    c2, n2, m2, conv2 = st_s
    mk4 = mk.reshape(1, bp, N_MEM, CA_HEADS, CA_DH)
    mv4 = mv.reshape(1, bp, N_MEM, CA_HEADS, CA_DH)
    return (y_prompt, y_sample, c1[None], n1[None], m1[None], conv1[None], mk4, mv4,
            c2[None], n2[None], m2[None], conv2[None])
```

```python
import functools

import jax
import jax.numpy as jnp
from jax import lax
from jax.experimental import pallas as pl
from jax.experimental.pallas import tpu as pltpu
from jax.experimental.pallas import tpu_sc as plsc

F32 = jnp.float32
BF16 = jnp.bfloat16

D_MODEL = 2048
HALF_D = D_MODEL // 2
M_HEADS = 4
M_DV = 256
M_DK = 128
QK_W = M_HEADS * M_DK
M_WIDTH = M_HEADS * M_DV
CONV_CH = 1024
CONV_WIDTH = 31
CONV_STATE = CONV_WIDTH - 1
CONV_PAD = 32
N_MEM = 256
CA_HEADS = 4
CA_DH = 128
CA_W = CA_HEADS * CA_DH
N_EXPERTS = 32
TOP_K = 4
D_FF = 2048
SWIGLU_ALPHA = 1.702
SWIGLU_LIMIT = 7.0
EPS = 1e-6
LANES = 128
SUBLANES = 8
NEG_BIG = -1e30

PROJ_TM = 1024
PROJ_STEPS = 4
QKVO_TN = (2 * QK_W + 2 * M_WIDTH) // PROJ_STEPS
GLU_TN = CONV_CH // PROJ_STEPS
PROJ_TN = QKVO_TN + 2 * GLU_TN
CONV_SUB = PROJ_TM // PROJ_STEPS
TOK_TILE = 512
ATTN_SPLIT = 2
MOE_TM = 512
MOE_TF = 1024
MOE_TN = 2048
MOE_VMEM_MB = 60
MLSTM_CHUNK = 256
MLSTM_BB = 2
CONV_TT = 256
CONV_RB = 32
CONV_LC = 512
SC_WIN = 32
SC_GWIN = 8


def _cparams(sem, vmem_mb):
    return pltpu.CompilerParams(dimension_semantics=sem, vmem_limit_bytes=vmem_mb << 20)


def _log_sigmoid(x):
    return jnp.minimum(x, 0.0) - jnp.log1p(jnp.exp(-jnp.abs(x)))


def _sigmoid(x):
    return 1.0 / (1.0 + jnp.exp(-x))


def _pack_pairs(lo, hi):
    return lax.bitcast_convert_type(pltpu.pack_elementwise([lo, hi], packed_dtype=BF16), jnp.uint32)


def _unpack_pairs(w):
    return tuple(pltpu.unpack_elementwise(w, index=i, packed_dtype=BF16, unpacked_dtype=F32) for i in range(2))


def _rms(x, g):
    ms = jnp.mean(x * x, axis=-1, keepdims=True)
    return x * lax.rsqrt(ms + EPS) * g


def _proj_prologue(x_ref, g_ref, wg_ref, bg_ref, h_ref, gates_ref, chunk):
    tm = x_ref.shape[0]
    h = _rms(x_ref[...], g_ref[...]).astype(BF16)
    h_ref[...] = h
    gt = lax.dot_general(wg_ref[...], h, (((1,), (1,)), ((), ())), preferred_element_type=F32)
    gt = gt + bg_ref[...]
    row = lax.broadcasted_iota(jnp.int32, gt.shape, 0)
    gt = jnp.where(row < M_HEADS, gt, _log_sigmoid(gt))
    for c in range(tm // chunk):
        gates_ref[c] = gt[:, c * chunk:(c + 1) * chunk]


def _proj_step(h_ref, w_ref):
    z = jnp.dot(h_ref[...], w_ref[...], preferred_element_type=F32)
    glu = z[:, QKVO_TN:QKVO_TN + GLU_TN] * _sigmoid(z[:, QKVO_TN + GLU_TN:])
    return z[:, :QKVO_TN].astype(BF16), glu.astype(BF16)


def _proj_in_kernel(x_ref, g_ref, w_ref, wg_ref, bg_ref, qkvo_ref, u_ref, gates_ref, h_ref, *, chunk):
    @pl.when(pl.program_id(1) == 0)
    def _():
        _proj_prologue(x_ref, g_ref, wg_ref, bg_ref, h_ref, gates_ref, chunk)

    qkvo_ref[...], u_ref[...] = _proj_step(h_ref, w_ref)


def _proj_in(x, g, w_main, wg, bg, chunk):
    n = x.shape[0]
    tm = min(PROJ_TM, n)
    return pl.pallas_call(
        functools.partial(_proj_in_kernel, chunk=chunk),
        grid=(n // tm, PROJ_STEPS),
        in_specs=[
            pl.BlockSpec((tm, D_MODEL), lambda i, j: (i, 0)),
            pl.BlockSpec((1, D_MODEL), lambda i, j: (0, 0)),
            pl.BlockSpec((D_MODEL, PROJ_TN), lambda i, j: (0, j)),
            pl.BlockSpec((2 * M_HEADS, D_MODEL), lambda i, j: (0, 0)),
            pl.BlockSpec((2 * M_HEADS, 1), lambda i, j: (0, 0)),
        ],
        out_specs=[
            pl.BlockSpec((tm, QKVO_TN), lambda i, j: (i, j)),
            pl.BlockSpec((tm, GLU_TN), lambda i, j: (i, j)),
            pl.BlockSpec((tm // chunk, 2 * M_HEADS, chunk), lambda i, j: (i, 0, 0)),
        ],
        out_shape=[
            jax.ShapeDtypeStruct((n, 2 * QK_W + 2 * M_WIDTH), BF16),
            jax.ShapeDtypeStruct((n, CONV_CH), BF16),
            jax.ShapeDtypeStruct((n // chunk, 2 * M_HEADS, chunk), F32),
        ],
        scratch_shapes=[pltpu.VMEM((tm, D_MODEL), BF16)],
        compiler_params=_cparams(("parallel", "arbitrary"), 48),
        name="proj_in",
    )(x, g, w_main, wg, bg)


def _proj_in_conv_kernel(x_ref, g_ref, w_ref, wg_ref, bg_ref, st_ref, cw_ref, cb_ref, lg_ref, lb_ref,
                         qkvo_ref, gates_ref, cv_ref, ns_ref, h_ref, u_s, hist_s, e_ref, er_ref, c_ref,
                         *, chunk, tiles_per_seq):
    i = pl.program_id(0)
    j = pl.program_id(1)
    p = i - 1
    sub = CONV_SUB
    ext = sub + CONV_PAD

    @pl.when((i == 0) & (j == 0))
    def _():
        u_s[...] = jnp.zeros(u_s.shape, BF16)
        hist_s[...] = jnp.zeros(hist_s.shape, F32)
        e_ref[:, ext:, :] = jnp.zeros((1, SUBLANES, CONV_CH), F32)

    @pl.when(j == 0)
    def _():
        _proj_prologue(x_ref, g_ref, wg_ref, bg_ref, h_ref, gates_ref, chunk)

    @pl.when((j == 0) & (p >= 0) & (p % tiles_per_seq == 0))
    def _():
        hist_s[...] = st_ref[0]

    r0 = pl.multiple_of(j * sub, sub)
    e_ref[0, 0:CONV_PAD, :] = hist_s[...]
    for c in range(PROJ_STEPS):
        e_ref[0, CONV_PAD:ext, c * GLU_TN:(c + 1) * GLU_TN] = u_s[(i + 1) % 2, c, pl.ds(r0, sub), :].astype(F32)
    prepare, row_block = _conv_stages(e_ref, er_ref, c_ref, cw_ref, cb_ref, lg_ref, lb_ref, cv_ref, 0, sub)
    prepare()
    for r0 in range(0, sub, CONV_RB):
        row_block(r0)
    hist_s[...] = e_ref[0, sub:ext, :]

    qkvo_ref[...], u_s[i % 2, j] = _proj_step(h_ref, w_ref)

    @pl.when((j == PROJ_STEPS - 1) & (p >= 0) & (p % tiles_per_seq == tiles_per_seq - 1))
    def _():
        ns_ref[0] = e_ref[0, ext - CONV_STATE:ext, :]


def _proj_in_conv(x, g, w_main, wg, bg, chunk, state_pad, conv_w, conv_b, ln_g, ln_b, seq_len):
    n = x.shape[0]
    tm = PROJ_TM
    n_tiles = n // tm
    tiles_per_seq = seq_len // tm
    bsz = n // seq_len
    assert tm == PROJ_STEPS * CONV_SUB and seq_len % tm == 0
    last = n_tiles - 1

    def tile(i, j):
        return jnp.minimum(i, last)

    def seq(i, j):
        return jnp.clip((i - 1) // tiles_per_seq, 0, bsz - 1)

    const2 = lambda i, j: (0, 0)
    return pl.pallas_call(
        functools.partial(_proj_in_conv_kernel, chunk=chunk, tiles_per_seq=tiles_per_seq),
        grid=(n_tiles + 1, PROJ_STEPS),
        in_specs=[
            pl.BlockSpec((tm, D_MODEL), lambda i, j: (tile(i, j), 0)),
            pl.BlockSpec((1, D_MODEL), const2),
            pl.BlockSpec((D_MODEL, PROJ_TN), lambda i, j: (0, j)),
            pl.BlockSpec((2 * M_HEADS, D_MODEL), const2),
            pl.BlockSpec((2 * M_HEADS, 1), const2),
            pl.BlockSpec((1, CONV_PAD, CONV_CH), lambda i, j: (seq(i, j), 0, 0)),
            pl.BlockSpec((CONV_WIDTH, CONV_CH), const2),
            pl.BlockSpec((1, CONV_CH), const2),
            pl.BlockSpec((1, CONV_CH), const2),
            pl.BlockSpec((1, CONV_CH), const2),
        ],
        out_specs=[
            pl.BlockSpec((tm, QKVO_TN), lambda i, j: (tile(i, j), jnp.where(i <= last, j, PROJ_STEPS - 1))),
            pl.BlockSpec((tm // chunk, 2 * M_HEADS, chunk), lambda i, j: (tile(i, j), 0, 0)),
            pl.BlockSpec((CONV_SUB, CONV_CH), lambda i, j: (jnp.maximum((i - 1) * PROJ_STEPS + j, 0), 0)),
            pl.BlockSpec((1, CONV_STATE, CONV_CH), lambda i, j: (seq(i, j), 0, 0)),
        ],
        out_shape=[
            jax.ShapeDtypeStruct((n, 2 * QK_W + 2 * M_WIDTH), BF16),
            jax.ShapeDtypeStruct((n // chunk, 2 * M_HEADS, chunk), F32),
            jax.ShapeDtypeStruct((n, CONV_CH), BF16),
            jax.ShapeDtypeStruct((bsz, CONV_STATE, CONV_CH), F32),
        ],
        scratch_shapes=[
            pltpu.VMEM((tm, D_MODEL), BF16),
            pltpu.VMEM((2, PROJ_STEPS, tm, GLU_TN), BF16),
            pltpu.VMEM((CONV_PAD, CONV_CH), F32),
            pltpu.VMEM((1, CONV_SUB + CONV_PAD + SUBLANES, CONV_CH), F32),
            pltpu.VMEM((SUBLANES - 1, CONV_SUB + CONV_PAD, CONV_CH), F32),
            pltpu.VMEM((CONV_SUB, CONV_CH), F32),
        ],
        compiler_params=_cparams(("arbitrary", "arbitrary"), 56),
        name="proj_in_conv",
    )(x, g, w_main, wg, bg, state_pad, conv_w, conv_b, ln_g, ln_b)


def _conv_stages(e_ref, er_ref, c_ref, w_ref, cb_ref, lg_ref, lb_ref, out_ref, b, rows):
    ext = rows + CONV_PAD
    base = CONV_PAD - CONV_STATE

    def prepare():
        for r in range(1, SUBLANES):
            er_ref[r - 1] = e_ref[b, r:r + ext, :]

    def row_block(r0):
        for lc in range(CONV_CH // CONV_LC):
            ls = slice(lc * CONV_LC, (lc + 1) * CONV_LC)
            acc = jnp.zeros((CONV_RB, CONV_LC), F32) + cb_ref[:, ls]
            for j in range(CONV_WIDTH):
                off = j + base
                a, r = (off // SUBLANES) * SUBLANES, off % SUBLANES
                if r == 0:
                    tap = e_ref[b, pl.ds(r0 + a, CONV_RB), ls]
                else:
                    tap = er_ref[r - 1, pl.ds(r0 + a, CONV_RB), ls]
                acc = acc + w_ref[j:j + 1, ls] * tap
            c_ref[pl.ds(r0, CONV_RB), ls] = acc
        c = c_ref[pl.ds(r0, CONV_RB), :]
        mu = jnp.mean(c, axis=-1, keepdims=True)
        cc = c - mu
        var = jnp.mean(cc * cc, axis=-1, keepdims=True)
        cn = cc * lax.rsqrt(var + EPS) * lg_ref[...] + lb_ref[...]
        out_ref[pl.ds(r0, CONV_RB), :] = (cn * _sigmoid(cn)).astype(BF16)

    return prepare, row_block


def _conv_rows(e_ref, er_ref, c_ref, w_ref, cb_ref, lg_ref, lb_ref, out_ref, b, rows):
    prepare, row_block = _conv_stages(e_ref, er_ref, c_ref, w_ref, cb_ref, lg_ref, lb_ref, out_ref, b, rows)
    prepare()

    def body(rb, carry):
        row_block(pl.multiple_of(rb * CONV_RB, CONV_RB))
        return carry

    lax.fori_loop(0, rows // CONV_RB, body, 0)


def _conv_kernel(u_ref, st_ref, w_ref, cb_ref, lg_ref, lb_ref, cv_ref, ns_ref, e_ref, er_ref, c_ref, *, bb, tt):
    t = pl.program_id(1)
    nt = pl.num_programs(1)
    ext = tt + CONV_PAD

    @pl.when(t == 0)
    def _():
        e_ref[:, 0:CONV_PAD, :] = st_ref[...]

    @pl.when(t > 0)
    def _():
        e_ref[:, 0:CONV_PAD, :] = e_ref[:, tt:ext, :]

    e_ref[:, CONV_PAD:ext, :] = u_ref[...].astype(F32)
    e_ref[:, ext:, :] = jnp.zeros((bb, SUBLANES, CONV_CH), F32)

    for b in range(bb):
        _conv_rows(e_ref, er_ref, c_ref, w_ref, cb_ref, lg_ref, lb_ref, cv_ref.at[b], b, tt)

    @pl.when(t == nt - 1)
    def _():
        ns_ref[...] = e_ref[:, ext - CONV_STATE:ext, :]


def _conv(u, state_pad, conv_w, conv_b, ln_g, ln_b, bb, tt):
    bsz, t, _ = u.shape
    return pl.pallas_call(
        functools.partial(_conv_kernel, bb=bb, tt=tt),
        grid=(bsz // bb, t // tt),
        in_specs=[
            pl.BlockSpec((bb, tt, CONV_CH), lambda b, s: (b, s, 0)),
            pl.BlockSpec((bb, CONV_PAD, CONV_CH), lambda b, s: (b, 0, 0)),
            pl.BlockSpec((CONV_WIDTH, CONV_CH), lambda b, s: (0, 0)),
            pl.BlockSpec((1, CONV_CH), lambda b, s: (0, 0)),
            pl.BlockSpec((1, CONV_CH), lambda b, s: (0, 0)),
            pl.BlockSpec((1, CONV_CH), lambda b, s: (0, 0)),
        ],
        out_specs=[
            pl.BlockSpec((bb, tt, CONV_CH), lambda b, s: (b, s, 0)),
            pl.BlockSpec((bb, CONV_STATE, CONV_CH), lambda b, s: (b, 0, 0)),
        ],
        out_shape=[
            jax.ShapeDtypeStruct((bsz, t, CONV_CH), BF16),
            jax.ShapeDtypeStruct((bsz, CONV_STATE, CONV_CH), F32),
        ],
        scratch_shapes=[
            pltpu.VMEM((bb, tt + CONV_PAD + SUBLANES, CONV_CH), F32),
            pltpu.VMEM((SUBLANES - 1, tt + CONV_PAD, CONV_CH), F32),
            pltpu.VMEM((tt, CONV_CH), F32),
        ],
        compiler_params=_cparams(("parallel", "arbitrary"), 40),
        name="conv",
    )(u, state_pad, conv_w, conv_b, ln_g, ln_b)


def _mlstm_kernel(q_ref, k_ref, v_ref, o_ref, gt_ref, c0_ref, n0_ref, m0_ref, ng_ref,
                  hm_ref, cn_ref, nn_ref, mn_ref, c_s, n_s, m_s, *, bb, L):
    ci = pl.program_id(1)
    nc = pl.num_programs(1)

    @pl.when(ci == 0)
    def _():
        c_s[...] = c0_ref[...]
        n_s[...] = n0_ref[...]
        m_s[...] = m0_ref[...]

    rt = lax.broadcasted_iota(jnp.int32, (L, L), 0)
    cs = lax.broadcasted_iota(jnp.int32, (L, L), 1)
    causal = cs <= rt
    lower = jnp.where(causal, 1.0, 0.0).astype(BF16)
    upper = jnp.where(rt <= cs, 1.0, 0.0).astype(BF16)
    nt_dims = (((1,), (1,)), ((), ()))
    scale = M_DK ** -0.5

    for b in range(bb):
        rows = gt_ref[b]
        cum_rows = jnp.zeros((2 * M_HEADS, L), F32)
        cum_cols = jnp.zeros((L, 2 * M_HEADS), F32)
        rest = rows
        for _ in range(3):
            piece = rest.astype(BF16)
            rest = rest - piece.astype(F32)
            cum_rows = cum_rows + jnp.dot(piece, upper, preferred_element_type=F32)
            cum_cols = cum_cols + lax.dot_general(lower, piece, nt_dims, preferred_element_type=F32)
        heads = range(M_HEADS)
        ig_row = [rows[h:h + 1, :] for h in heads]
        b_row = [cum_rows[M_HEADS + h:M_HEADS + h + 1, :] for h in heads]
        b_col = [cum_cols[:, M_HEADS + h:M_HEADS + h + 1] for h in heads]
        m0 = [m_s[b, h:h + 1, 0:1] for h in heads]
        q = [q_ref[b, :, h * M_DK:(h + 1) * M_DK] for h in heads]
        k = [k_ref[b, :, h * M_DK:(h + 1) * M_DK] for h in heads]
        v = [v_ref[b, :, h * M_DV:(h + 1) * M_DV] for h in heads]
        c0 = [c_s[b, h] for h in heads]
        n0 = [n_s[b, h:h + 1, :] for h in heads]

        g_col = [b_col[h] + m0[h] for h in heads]
        dmat = [jnp.where(causal, b_col[h] - b_row[h] + ig_row[h], -jnp.inf) for h in heads]
        m_col = [jnp.maximum(g_col[h], jnp.max(dmat[h], axis=1, keepdims=True)) for h in heads]
        w_state = [jnp.exp(g_col[h] - m_col[h]) * scale for h in heads]
        decay = [jnp.exp(dmat[h] - m_col[h]) * scale for h in heads]
        qk = [lax.dot_general(q[h], k[h], nt_dims, preferred_element_type=F32) for h in heads]
        s = [qk[h] * decay[h] for h in heads]
        qc = [jnp.dot(q[h], c0[h].astype(BF16), preferred_element_type=F32) for h in heads]
        num = [jnp.dot(s[h].astype(BF16), v[h], preferred_element_type=F32) + w_state[h] * qc[h] for h in heads]
        qn = [jnp.sum(q[h].astype(F32) * n0[h], axis=1, keepdims=True) for h in heads]
        den = [jnp.sum(s[h], axis=1, keepdims=True) + w_state[h] * qn[h] for h in heads]
        hh = [num[h] / jnp.maximum(jnp.abs(den[h]), jnp.exp(-m_col[h])) for h in heads]
        for h in heads:
            hn = hh[h] * lax.rsqrt(jnp.mean(hh[h] * hh[h], axis=-1, keepdims=True) + EPS) * ng_ref[h:h + 1, :]
            og = _sigmoid(o_ref[b, :, h * M_DV:(h + 1) * M_DV].astype(F32))
            hm_ref[b, :, h * M_DV:(h + 1) * M_DV] = (og * hn).astype(BF16)

        b_last = [b_row[h][:, L - 1:L] for h in heads]
        g_last = [b_last[h] + m0[h] for h in heads]
        wk_row = [b_last[h] - b_row[h] + ig_row[h] for h in heads]
        m_new = [jnp.maximum(g_last[h], jnp.max(wk_row[h], axis=1, keepdims=True)) for h in heads]
        a0 = [jnp.exp(g_last[h] - m_new[h]) for h in heads]
        ak_row = [jnp.exp(wk_row[h] - m_new[h]) for h in heads]
        kts = [(k[h].T.astype(F32) * ak_row[h]).astype(BF16) for h in heads]
        kv = [jnp.dot(kts[h], v[h], preferred_element_type=F32) for h in heads]
        kn = [jnp.dot(ak_row[h].astype(BF16), k[h], preferred_element_type=F32) for h in heads]
        for h in heads:
            c_s[b, h] = a0[h] * c0[h] + kv[h]
            n_s[b, h:h + 1, :] = a0[h] * n0[h] + kn[h]
            m_s[b, h:h + 1, :] = jnp.broadcast_to(m_new[h], (1, LANES))

    @pl.when(ci == nc - 1)
    def _():
        cn_ref[...] = c_s[...]
        nn_ref[...] = n_s[...]
        mn_ref[...] = m_s[...]


def _mlstm(qkvo, gates, c0, n0, m0, ng, bb, L):
    bsz, t, _ = qkvo.shape
    nc = t // L
    gates = gates.reshape(bsz, nc, 2 * M_HEADS, L)
    st_c = pl.BlockSpec((bb, M_HEADS, M_DK, M_DV), lambda b, c: (b, 0, 0, 0))
    st_n = pl.BlockSpec((bb, M_HEADS, M_DK), lambda b, c: (b, 0, 0))
    return pl.pallas_call(
        functools.partial(_mlstm_kernel, bb=bb, L=L),
        grid=(bsz // bb, nc),
        in_specs=[
            pl.BlockSpec((bb, L, QK_W), lambda b, c: (b, c, 0)),
            pl.BlockSpec((bb, L, QK_W), lambda b, c: (b, c, 1)),
            pl.BlockSpec((bb, L, M_WIDTH), lambda b, c: (b, c, 1)),
            pl.BlockSpec((bb, L, M_WIDTH), lambda b, c: (b, c, 2)),
            pl.BlockSpec((bb, None, 2 * M_HEADS, L), lambda b, c: (b, c, 0, 0)),
            st_c, st_n, st_n,
            pl.BlockSpec((M_HEADS, M_DV), lambda b, c: (0, 0)),
        ],
        out_specs=[
            pl.BlockSpec((bb, L, M_WIDTH), lambda b, c: (b, c, 0)),
            st_c, st_n, st_n,
        ],
        out_shape=[
            jax.ShapeDtypeStruct((bsz, t, M_WIDTH), BF16),
            jax.ShapeDtypeStruct((bsz, M_HEADS, M_DK, M_DV), F32),
            jax.ShapeDtypeStruct((bsz, M_HEADS, M_DK), F32),
            jax.ShapeDtypeStruct((bsz, M_HEADS, LANES), F32),
        ],
        scratch_shapes=[
            pltpu.VMEM((bb, M_HEADS, M_DK, M_DV), F32),
            pltpu.VMEM((bb, M_HEADS, M_DK), F32),
            pltpu.VMEM((bb, M_HEADS, LANES), F32),
        ],
        compiler_params=_cparams(("parallel", "arbitrary"), 40),
        name="mlstm",
    )(qkvo, qkvo, qkvo, qkvo, gates, c0, n0, m0, ng)


def _proj_out_kernel(x_ref, hm_ref, cv_ref, wa_ref, wb_ref, o_ref):
    acc = jnp.dot(hm_ref[...], wa_ref[...], preferred_element_type=F32)
    acc = acc + jnp.dot(cv_ref[...], wb_ref[...], preferred_element_type=F32)
    o_ref[...] = x_ref[...] + acc


def _proj_out(x, hm, cv, wa, wb):
    n = x.shape[0]
    tm = min(TOK_TILE, n)
    return pl.pallas_call(
        _proj_out_kernel,
        grid=(n // tm,),
        in_specs=[
            pl.BlockSpec((tm, D_MODEL), lambda i: (i, 0)),
            pl.BlockSpec((tm, M_WIDTH), lambda i: (i, 0)),
            pl.BlockSpec((tm, CONV_CH), lambda i: (i, 0)),
            pl.BlockSpec((M_WIDTH, D_MODEL), lambda i: (0, 0)),
            pl.BlockSpec((CONV_CH, D_MODEL), lambda i: (0, 0)),
        ],
        out_specs=pl.BlockSpec((tm, D_MODEL), lambda i: (i, 0)),
        out_shape=jax.ShapeDtypeStruct((n, D_MODEL), F32),
        compiler_params=_cparams(("parallel",), 48),
        name="proj_out",
    )(x, hm, cv, wa, wb)


def _mem_kv_kernel(m_ref, g_ref, w_ref, k_ref, v_ref):
    mn = _rms(m_ref[...], g_ref[...]).astype(BF16)
    z = jnp.dot(mn, w_ref[...], preferred_element_type=F32)
    k_ref[...] = z[:, :CA_W]
    v_ref[...] = z[:, CA_W:]


def _mem_kv(mem, g, wkv):
    n = mem.shape[0]
    return pl.pallas_call(
        _mem_kv_kernel,
        grid=(1,),
        in_specs=[
            pl.BlockSpec((n, D_MODEL), lambda i: (0, 0)),
            pl.BlockSpec((1, D_MODEL), lambda i: (0, 0)),
            pl.BlockSpec((D_MODEL, 2 * CA_W), lambda i: (0, 0)),
        ],
        out_specs=[pl.BlockSpec((n, CA_W), lambda i: (0, 0))] * 2,
        out_shape=[jax.ShapeDtypeStruct((n, CA_W), F32)] * 2,
        compiler_params=_cparams(("arbitrary",), 40),
        name="mem_kv",
    )(mem, g, wkv)


def _attn_router_kernel(x_ref, gca_ref, wq_ref, k_ref, v_ref, wo_ref, gmoe_ref, rw_ref, rb_ref, cnt0_ref,
                        x2_ref, hn_ref, te_ref, tg_ref, tr_ref, cnt_ref, o_s, cnt_s, *, bb, tt):
    rows = bb * tt

    @pl.when((pl.program_id(0) == 0) & (pl.program_id(1) == 0))
    def _():
        cnt_s[...] = cnt0_ref[...]

    parts = range(ATTN_SPLIT)
    pr = rows // ATTN_SPLIT
    bpp = bb // ATTN_SPLIT

    def part_ref(ref, i):
        return ref.at[0, i * pr:(i + 1) * pr, :] if bb == 1 else ref.at[i * bpp:(i + 1) * bpp]

    seqs = [[(0, pr, 0)] if bb == 1 else [(j * tt, (j + 1) * tt, i * bpp + j) for j in range(bpp)] for i in parts]

    x = [part_ref(x_ref, i)[...].reshape(pr, D_MODEL) for i in parts]
    h = [_rms(x[i], gca_ref[...]).astype(BF16) for i in parts]
    q = [(jnp.dot(h[i], wq_ref[...], preferred_element_type=F32) * (CA_DH ** -0.5)).astype(BF16) for i in parts]
    units = [(i, r0, r1, b, slice(hd * CA_DH, (hd + 1) * CA_DH))
             for i in parts for r0, r1, b in seqs[i] for hd in range(CA_HEADS)]
    nt_dims = (((1,), (1,)), ((), ()))
    s = [lax.dot_general(q[i][r0:r1, cols], k_ref[b, :, cols], nt_dims, preferred_element_type=F32)
         for i, r0, r1, b, cols in units]
    e = [jnp.exp(si - jnp.max(si, axis=-1, keepdims=True)) for si in s]
    ov = [jnp.dot(ei.astype(BF16), v_ref[b, :, cols], preferred_element_type=F32)
          for ei, (i, r0, r1, b, cols) in zip(e, units)]
    for ei, ovi, (i, r0, r1, b, cols) in zip(e, ov, units):
        o_s[i * pr + r0:i * pr + r1, cols] = (ovi / jnp.sum(ei, axis=-1, keepdims=True)).astype(BF16)
    x2 = [x[i] + jnp.dot(o_s[i * pr:(i + 1) * pr, :], wo_ref[...], preferred_element_type=F32) for i in parts]
    for i in parts:
        out = part_ref(x2_ref, i)
        out[...] = x2[i].reshape(out.shape)

    hn = [_rms(x2[i], gmoe_ref[...]) for i in parts]
    for i in parts:
        hn_ref[i * pr:(i + 1) * pr, :] = _pack_pairs(hn[i][:, :HALF_D], hn[i][:, HALF_D:])
    work = [jnp.dot(hn[i].astype(BF16), rw_ref[...], preferred_element_type=F32) + rb_ref[...] for i in parts]
    lane = lax.broadcasted_iota(jnp.int32, (pr, LANES), 1)
    lane_f = lane.astype(F32)
    sel = [[] for _ in parts]
    vals = [[] for _ in parts]
    for _ in range(TOP_K):
        mx = [jnp.max(work[i], axis=-1, keepdims=True) for i in parts]
        idx = [jnp.min(jnp.where(work[i] == mx[i], lane_f, float(LANES)), axis=-1, keepdims=True) for i in parts]
        hit = [lane_f == idx[i] for i in parts]
        work = [jnp.where(hit[i], -jnp.inf, work[i]) for i in parts]
        for i in parts:
            sel[i].append((idx[i], hit[i]))
            vals[i].append(mx[i])
    ex = [[jnp.exp(vv - vals[i][0]) for vv in vals[i]] for i in parts]
    tot = [ex[i][0] + ex[i][1] + ex[i][2] + ex[i][3] for i in parts]
    assigned = []
    for i in parts:
        a = jnp.zeros((pr, LANES), F32)
        for _, hit_k in sel[i]:
            a = a + hit_k.astype(F32)
        assigned.append(a)
    rt = lax.broadcasted_iota(jnp.int32, (pr, pr), 0)
    cs = lax.broadcasted_iota(jnp.int32, (pr, pr), 1)
    strict = (cs < rt).astype(BF16)
    inside = [jnp.dot(strict, assigned[i].astype(BF16), preferred_element_type=F32) for i in parts]
    count = cnt_s[...]
    for i in parts:
        before = inside[i] + count
        count = count + jnp.sum(assigned[i], axis=0, keepdims=True)
        te = jnp.zeros((pr, LANES), F32)
        tg = jnp.zeros((pr, LANES), F32)
        tr = jnp.zeros((pr, LANES), F32)
        for kk, (idx_k, hit_k) in enumerate(sel[i]):
            rank = jnp.sum(jnp.where(hit_k, before, 0.0), axis=-1, keepdims=True)
            te = jnp.where(lane == kk, idx_k, te)
            tg = jnp.where(lane == kk, ex[i][kk] / tot[i], tg)
            tr = jnp.where(lane == kk, rank, tr)
        te_ref[i * pr:(i + 1) * pr, :] = te.astype(jnp.int32)
        tg_ref[i * pr:(i + 1) * pr, :] = tg
        tr_ref[i * pr:(i + 1) * pr, :] = tr.astype(jnp.int32)
    cnt_s[...] = count
    cnt_ref[...] = count


def _attn_router(x1, mem_k, mem_v, cnt0, p, bb, tt):
    bsz, t, _ = x1.shape
    rows = bb * tt
    n = bsz * t
    nb, nt = bsz // bb, t // tt
    tok = lambda b, s: (b * nt + s, 0)
    const = lambda b, s: (0, 0)
    grid_spec = dict(
        grid=(nb, nt),
        in_specs=[
            pl.BlockSpec((bb, tt, D_MODEL), lambda b, s: (b, s, 0)),
            pl.BlockSpec((1, D_MODEL), const),
            pl.BlockSpec((D_MODEL, CA_W), const),
            pl.BlockSpec((bb, N_MEM, CA_W), lambda b, s: (b, 0, 0)),
            pl.BlockSpec((bb, N_MEM, CA_W), lambda b, s: (b, 0, 0)),
            pl.BlockSpec((CA_W, D_MODEL), const),
            pl.BlockSpec((1, D_MODEL), const),
            pl.BlockSpec((D_MODEL, LANES), const),
            pl.BlockSpec((1, LANES), const),
            pl.BlockSpec((1, LANES), const),
        ],
        out_specs=[
            pl.BlockSpec((bb, tt, D_MODEL), lambda b, s: (b, s, 0)),
            pl.BlockSpec((rows, HALF_D), tok),
            pl.BlockSpec((rows, LANES), tok),
            pl.BlockSpec((rows, LANES), tok),
            pl.BlockSpec((rows, LANES), tok),
            pl.BlockSpec((1, LANES), const),
        ],
    )
    return pl.pallas_call(
        functools.partial(_attn_router_kernel, bb=bb, tt=tt),
        out_shape=[
            jax.ShapeDtypeStruct((bsz, t, D_MODEL), F32),
            jax.ShapeDtypeStruct((n, HALF_D), jnp.uint32),
            jax.ShapeDtypeStruct((n, LANES), jnp.int32),
            jax.ShapeDtypeStruct((n, LANES), F32),
            jax.ShapeDtypeStruct((n, LANES), jnp.int32),
            jax.ShapeDtypeStruct((1, LANES), F32),
        ],
        scratch_shapes=[pltpu.VMEM((rows, CA_W), BF16), pltpu.VMEM((1, LANES), F32)],
        compiler_params=_cparams(("arbitrary", "arbitrary"), 48),
        name="attn_router",
        **grid_spec,
    )(x1, p["g_ca"], p["wq"], mem_k, mem_v, p["wo"], p["g_moe"], p["router_w"], p["router_b"], cnt0)


def _sc_scatter_rows(rows_a, rows_b, dest3, n_out):
    d = rows_a.shape[1]
    nwin = dest3.shape[0]
    nwin_a = rows_a.shape[0] // SC_WIN
    mesh = plsc.VectorSubcoreMesh(core_axis_name="c", subcore_axis_name="s")
    workers = mesh.num_cores * mesh.num_subcores
    assert nwin * SC_WIN == rows_a.shape[0] + rows_b.shape[0] and nwin_a * SC_WIN == rows_a.shape[0]
    assert nwin % workers == 0, (nwin, workers)
    per = nwin // workers

    @functools.partial(
        pl.kernel, out_type=jax.ShapeDtypeStruct((n_out, d), rows_a.dtype), mesh=mesh,
        scratch_types=[pltpu.VMEM((SC_WIN, d), rows_a.dtype), pltpu.VMEM((TOP_K, SC_WIN), jnp.int32),
                       pltpu.SemaphoreType.DMA])
    def scatter(xa_hbm, xb_hbm, i_hbm, o_hbm, buf, idx, sem):
        wid = lax.axis_index("s") * mesh.num_cores + lax.axis_index("c")

        @pl.loop(0, per)
        def _(j):
            w = wid * per + j

            @pl.when(w < nwin_a)
            def _():
                pltpu.sync_copy(xa_hbm.at[pl.ds(w * SC_WIN, SC_WIN)], buf)

            @pl.when(w >= nwin_a)
            def _():
                pltpu.sync_copy(xb_hbm.at[pl.ds((w - nwin_a) * SC_WIN, SC_WIN)], buf)

            pltpu.sync_copy(i_hbm.at[w], idx)
            copies = [pltpu.async_copy(buf, o_hbm.at[idx.at[kk]], sem) for kk in range(TOP_K)]
            for cp in copies:
                cp.wait()

    return scatter(rows_a, rows_b, dest3)


def _sc_gather_rows(y, dest3):
    d = y.shape[1]
    nwin = dest3.shape[0]
    n = nwin * SC_GWIN
    mesh = plsc.VectorSubcoreMesh(core_axis_name="c", subcore_axis_name="s")
    workers = mesh.num_cores * mesh.num_subcores
    assert nwin % (2 * workers) == 0, (nwin, workers)
    per = nwin // workers

    @functools.partial(
        pl.kernel, out_type=jax.ShapeDtypeStruct((TOP_K, n, d), y.dtype), mesh=mesh,
        scratch_types=[pltpu.VMEM((2 * TOP_K, SC_GWIN, d), y.dtype), pltpu.VMEM((2, TOP_K, SC_GWIN), jnp.int32),
                       pltpu.SemaphoreType.DMA, pltpu.SemaphoreType.DMA, pltpu.SemaphoreType.DMA])
    def gather(y_hbm, i_hbm, o_hbm, buf, idx, sem_gather, sem_write, sem_idx):
        base = (lax.axis_index("s") * mesh.num_cores + lax.axis_index("c")) * per

        def index_copy(w, s):
            return pltpu.make_async_copy(i_hbm.at[w], idx.at[s], sem_idx)

        def write_copy(w, s, kk):
            return pltpu.make_async_copy(buf.at[s * TOP_K + kk], o_hbm.at[kk, pl.ds(w * SC_GWIN, SC_GWIN)], sem_write)

        index_copy(base, 0).start()

        @pl.loop(0, per // 2)
        def _(g):
            for s in range(2):
                w = base + 2 * g + s
                index_copy(w, s).wait()
                if s == 0:
                    index_copy(w + 1, 1).start()
                else:
                    @pl.when(g + 1 < per // 2)
                    def _():
                        index_copy(w + 1, 0).start()

                @pl.when(g > 0)
                def _():
                    for kk in range(TOP_K):
                        write_copy(w, s, kk).wait()

                gathers = [pltpu.make_async_copy(y_hbm.at[idx.at[s, kk]], buf.at[s * TOP_K + kk], sem_gather)
                           for kk in range(TOP_K)]
                for cp in gathers:
                    cp.start()
                for kk in range(TOP_K):
                    gathers[kk].wait()
                    write_copy(w, s, kk).start()

        for s in range(2):
            for kk in range(TOP_K):
                write_copy(base, s, kk).wait()

    return gather(y, dest3)


def _expert_changed(te_ref):
    t = pl.program_id(1)
    return (t == 0) | (te_ref[t] != te_ref[jnp.maximum(t - 1, 0)])


def _moe_up_kernel(te_ref, nu_ref, x_ref, w1g_ref, w1l_ref, b1g_ref, b1l_ref, act_ref, wg_s, wl_s):
    @pl.when(_expert_changed(te_ref))
    def _():
        wg_s[...] = w1g_ref[0].astype(BF16)
        wl_s[...] = w1l_ref[0].astype(BF16)

    def up(words):
        lo, hi = _unpack_pairs(words)
        x = jnp.concatenate([lo.astype(BF16), hi.astype(BF16)], axis=1)
        glu = jnp.dot(x, wg_s[...], preferred_element_type=F32) + b1g_ref[0]
        lin = jnp.dot(x, wl_s[...], preferred_element_type=F32) + b1l_ref[0]
        glu = jnp.minimum(glu, SWIGLU_LIMIT)
        lin = jnp.clip(lin, -SWIGLU_LIMIT, SWIGLU_LIMIT)
        act_ref[...] = (glu * _sigmoid(SWIGLU_ALPHA * glu) * (lin + 1.0)).astype(BF16)

    t = pl.program_id(1)
    used = t < nu_ref[0]
    valid = nu_ref[1 + t]

    @pl.when(used & (valid >= MOE_TM))
    def _():
        up(x_ref[...])

    @pl.when(used & (valid < MOE_TM))
    def _():
        rowid = lax.broadcasted_iota(jnp.int32, (MOE_TM, HALF_D), 0)
        up(jnp.where(rowid < valid, x_ref[...], jnp.uint32(0)))


def _moe_down_kernel(te_ref, nu_ref, a_ref, w2_ref, b2_ref, y_ref, w2_s):
    @pl.when(_expert_changed(te_ref))
    def _():
        w2_s[...] = w2_ref[0].astype(BF16)

    @pl.when(pl.program_id(1) < nu_ref[0])
    def _():
        y = jnp.dot(a_ref[...], w2_s[...], preferred_element_type=F32) + b2_ref[0]
        y_ref[...] = _pack_pairs(y[:, :MOE_TN // 2], y[:, MOE_TN // 2:])


def _moe(tile_e, n_used, x_sorted, w1, b1, w2, b2):
    r = x_sorted.shape[0]
    n_tiles = r // MOE_TM
    nf = D_FF // MOE_TF
    nn = D_MODEL // MOE_TN

    def row(c, t, te, nu):
        return jnp.maximum(jnp.minimum(t, nu[0] - 1), 0)

    act = pl.pallas_call(
        _moe_up_kernel,
        grid_spec=pltpu.PrefetchScalarGridSpec(
            num_scalar_prefetch=2,
            grid=(nf, n_tiles),
            in_specs=[
                pl.BlockSpec((MOE_TM, HALF_D), lambda c, t, te, nu: (row(c, t, te, nu), 0)),
                pl.BlockSpec((1, D_MODEL, MOE_TF), lambda c, t, te, nu: (te[t], 0, c)),
                pl.BlockSpec((1, D_MODEL, MOE_TF), lambda c, t, te, nu: (te[t], 0, nf + c)),
                pl.BlockSpec((1, 1, MOE_TF), lambda c, t, te, nu: (te[t], 0, c)),
                pl.BlockSpec((1, 1, MOE_TF), lambda c, t, te, nu: (te[t], 0, nf + c)),
            ],
            out_specs=pl.BlockSpec((MOE_TM, MOE_TF), lambda c, t, te, nu: (row(c, t, te, nu), c)),
            scratch_shapes=[pltpu.VMEM((D_MODEL, MOE_TF), BF16), pltpu.VMEM((D_MODEL, MOE_TF), BF16)],
        ),
        out_shape=jax.ShapeDtypeStruct((r, D_FF), BF16),
        compiler_params=_cparams(("arbitrary", "arbitrary"), MOE_VMEM_MB),
        name="moe_up",
    )(tile_e, n_used, x_sorted, w1, w1, b1, b1)

    return pl.pallas_call(
        _moe_down_kernel,
        grid_spec=pltpu.PrefetchScalarGridSpec(
            num_scalar_prefetch=2,
            grid=(nn, n_tiles),
            in_specs=[
                pl.BlockSpec((MOE_TM, D_FF), lambda c, t, te, nu: (row(c, t, te, nu), 0)),
                pl.BlockSpec((1, D_FF, MOE_TN), lambda c, t, te, nu: (te[t], 0, c)),
                pl.BlockSpec((1, 1, MOE_TN), lambda c, t, te, nu: (te[t], 0, c)),
            ],
            out_specs=pl.BlockSpec((MOE_TM, MOE_TN // 2), lambda c, t, te, nu: (row(c, t, te, nu), c)),
            scratch_shapes=[pltpu.VMEM((D_FF, MOE_TN), BF16)],
        ),
        out_shape=jax.ShapeDtypeStruct((r, HALF_D), jnp.uint32),
        compiler_params=_cparams(("arbitrary", "arbitrary"), MOE_VMEM_MB),
        name="moe_down",
    )(tile_e, n_used, act, w2, b2)


def _combine_kernel(x_ref, y0_ref, y1_ref, y2_ref, y3_ref, tg_ref, g_ref, o_ref):
    tg = tg_ref[...]
    acc = x_ref[...]
    half = MOE_TN // 2
    for kk, y_ref in enumerate((y0_ref, y1_ref, y2_ref, y3_ref)):
        cols = []
        for c in range(D_MODEL // MOE_TN):
            cols += list(_unpack_pairs(y_ref[0, :, c * half:(c + 1) * half]))
        acc = acc + jnp.concatenate(cols, axis=1) * tg[:, kk:kk + 1]
    o_ref[...] = _rms(acc, g_ref[...])


def _combine(x2, yg, tg, g, row0):
    n = x2.shape[0]
    tm = min(TOK_TILE, n)
    blk0 = row0 // tm
    yspec = [pl.BlockSpec((1, tm, HALF_D), functools.partial(lambda i, kk: (kk, blk0 + i, 0), kk=kk))
             for kk in range(TOP_K)]
    return pl.pallas_call(
        _combine_kernel,
        grid=(n // tm,),
        in_specs=[pl.BlockSpec((tm, D_MODEL), lambda i: (i, 0))] + yspec + [
            pl.BlockSpec((tm, LANES), lambda i: (i, 0)),
            pl.BlockSpec((1, D_MODEL), lambda i: (0, 0)),
        ],
        out_specs=pl.BlockSpec((tm, D_MODEL), lambda i: (i, 0)),
        out_shape=jax.ShapeDtypeStruct((n, D_MODEL), F32),
        compiler_params=_cparams(("parallel",), 48),
        name="combine",
    )(x2, yg, yg, yg, yg, tg, g)


def _group_tiles(bsz, t):
    tt = min(t, TOK_TILE)
    bb = max(1, min(bsz, TOK_TILE // tt))
    return bb, tt


def _layer_group(x, c0, n0, m0, conv_state, mem_k, mem_v, cnt0, p):
    bsz, t, _ = x.shape
    n = bsz * t
    L = min(MLSTM_CHUNK, t)
    state_pad = jnp.pad(conv_state, ((0, 0), (CONV_PAD - CONV_STATE, 0), (0, 0)))
    conv_args = (state_pad, p["conv_w"], p["conv_b"], p["ln_g"], p["ln_b"])
    if t % PROJ_TM == 0:
        qkvo, gates, cv, new_conv = _proj_in_conv(x.reshape(n, D_MODEL), p["g_mix"], p["w_main"], p["wg"], p["bg"],
                                                  L, *conv_args, seq_len=t)
    else:
        qkvo, u, gates = _proj_in(x.reshape(n, D_MODEL), p["g_mix"], p["w_main"], p["wg"], p["bg"], L)
        ctt = min(CONV_TT, t)
        cbb = max(1, min(bsz, CONV_TT // ctt))
        cv, new_conv = _conv(u.reshape(bsz, t, CONV_CH), *conv_args, cbb, ctt)

    mbb = min(bsz, MLSTM_BB)
    m0b = jnp.broadcast_to(m0[:, :, None], (bsz, M_HEADS, LANES))
    hm, c1, n1, m1 = _mlstm(qkvo.reshape(bsz, t, -1), gates, c0, n0, m0b, p["mh_g"], mbb, L)

    x1 = _proj_out(x.reshape(n, D_MODEL), hm.reshape(n, M_WIDTH), cv.reshape(n, CONV_CH), p["wo_a"], p["wo_b"])

    abb, att = _group_tiles(bsz, t)
    x2, hn, te, tg, tr, cnt = _attn_router(x1.reshape(bsz, t, D_MODEL), mem_k, mem_v, cnt0, p, abb, att)
    return x2.reshape(n, D_MODEL), hn, te, tg, tr, cnt, (c1, n1, m1[:, :, 0], new_conv)


def kernel(x_prompt, x_sample, mem_prompt, state_mlstm_C, state_mlstm_n, state_mlstm_m, state_conv, cache_mem_k, cache_mem_v, norm_mix_g, w_in, b_gates, mh_norm_g, conv_w, conv_b, conv_ln_g, conv_ln_b, w_out, norm_ca_g, norm_mem_g, ca_wq, ca_wk, ca_wv, ca_wo, norm_moe_g, router_w, router_b, moe_w1, moe_b1, moe_w2, moe_b2, final_norm_g):
    assert w_in.shape[0] == 1, "single layer"
    bp, tp, _ = x_prompt.shape
    bs, ts, _ = x_sample.shape
    n_p, n_s = bp * tp, bs * ts
    n_all = n_p + n_s

    wi = w_in[0].astype(BF16)
    o0 = 0
    parts = []
    for width in (QK_W, QK_W, M_WIDTH, M_WIDTH, M_HEADS, M_HEADS, CONV_CH, CONV_CH):
        parts.append(wi[:, o0:o0 + width])
        o0 += width
    w_q, w_k, w_v, w_o, w_gi, w_gf, w_ua, w_ug = parts
    col_tiles = []
    w_qkvo = wi[:, :2 * QK_W + 2 * M_WIDTH]
    for c in range(PROJ_STEPS):
        col_tiles += [w_qkvo[:, c * QKVO_TN:(c + 1) * QKVO_TN], w_ua[:, c * GLU_TN:(c + 1) * GLU_TN],
                      w_ug[:, c * GLU_TN:(c + 1) * GLU_TN]]
    p = {
        "g_mix": norm_mix_g[0][None, :],
        "w_main": jnp.concatenate(col_tiles, axis=1).astype(BF16),
        "wg": jnp.concatenate([w_gi, w_gf], axis=1).T.astype(BF16),
        "bg": b_gates[0][:, None],
        "mh_g": mh_norm_g[0],
        "conv_w": conv_w[0],
        "conv_b": conv_b[0][None, :],
        "ln_g": conv_ln_g[0][None, :],
        "ln_b": conv_ln_b[0][None, :],
        "wo_a": w_out[0][:M_WIDTH].astype(BF16),
        "wo_b": w_out[0][M_WIDTH:].astype(BF16),
        "g_ca": norm_ca_g[0][None, :],
        "wq": ca_wq[0].astype(BF16),
        "wo": ca_wo[0].astype(BF16),
        "g_moe": norm_moe_g[0][None, :],
        "router_w": jnp.pad(router_w[0], ((0, 0), (0, LANES - N_EXPERTS))).astype(BF16),
        "router_b": jnp.concatenate([router_b[0], jnp.full((LANES - N_EXPERTS,), NEG_BIG, F32)])[None, :],
    }

    wkv = jnp.concatenate([ca_wk[0], ca_wv[0]], axis=1).astype(BF16)
    mk, mv = _mem_kv(mem_prompt.reshape(bp * N_MEM, D_MODEL), norm_mem_g[0][None, :], wkv)
    mk = mk.reshape(bp, N_MEM, CA_W)
    mv = mv.reshape(bp, N_MEM, CA_W)
    cnt0 = jnp.zeros((1, LANES), F32)
    zc = jnp.zeros((bp, M_HEADS, M_DK, M_DV), F32)
    zn = jnp.zeros((bp, M_HEADS, M_DK), F32)
    zm = jnp.zeros((bp, M_HEADS), F32)
    zconv = jnp.zeros((bp, CONV_STATE, CONV_CH), F32)
    x2p, hn_p, te_p, tg_p, tr_p, cnt1, st_p = _layer_group(
        x_prompt, zc, zn, zm, zconv, mk.astype(BF16), mv.astype(BF16), cnt0, p)

    x2s, hn_s, te_s, tg_s, tr_s, cnt2, st_s = _layer_group(
        x_sample, state_mlstm_C[0], state_mlstm_n[0], state_mlstm_m[0], state_conv[0],
        cache_mem_k[0].reshape(bs, N_MEM, CA_W).astype(BF16), cache_mem_v[0].reshape(bs, N_MEM, CA_W).astype(BF16),
        cnt1, p)

    te = jnp.concatenate([te_p[:, :TOP_K], te_s[:, :TOP_K]], axis=0)
    tr = jnp.concatenate([tr_p[:, :TOP_K], tr_s[:, :TOP_K]], axis=0)
    counts = cnt2[0, :N_EXPERTS].astype(jnp.int32)
    tiles_per_e = (counts + MOE_TM - 1) // MOE_TM
    tile_end = jnp.cumsum(tiles_per_e)
    row_start = (tile_end - tiles_per_e) * MOE_TM
    n_tiles = -(-(n_all * TOP_K) // MOE_TM) + N_EXPERTS
    n_used = tile_end[-1:]
    tile_ids = jnp.minimum(jnp.arange(n_tiles, dtype=jnp.int32), n_used[0] - 1)
    tile_e = jnp.minimum(jnp.sum(tile_end[:, None] <= tile_ids[None, :], axis=0), N_EXPERTS - 1).astype(jnp.int32)
    tile_first = jnp.arange(n_tiles, dtype=jnp.int32) - (tile_end - tiles_per_e)[tile_e]
    tile_valid = jnp.clip(counts[tile_e] - tile_first * MOE_TM, 0, MOE_TM)
    tile_valid = jnp.where(jnp.arange(n_tiles) < n_used[0], tile_valid, 0).astype(jnp.int32)
    moe_meta = jnp.concatenate([n_used.astype(jnp.int32), tile_valid])
    dest = row_start[te] + tr
    dest3 = dest.reshape(n_all // SC_WIN, SC_WIN, TOP_K).transpose(0, 2, 1)
    x_sorted = _sc_scatter_rows(hn_p, hn_s, dest3, n_tiles * MOE_TM)

    y_sorted = _moe(tile_e, moe_meta, x_sorted, moe_w1[0], moe_b1[0][:, None, :], moe_w2[0],
                    moe_b2[0][:, None, :])

    yg = _sc_gather_rows(y_sorted, dest.reshape(n_all // SC_GWIN, SC_GWIN, TOP_K).transpose(0, 2, 1))
    fg = final_norm_g[None, :]
    y_prompt = _combine(x2p, yg, tg_p, fg, 0).reshape(bp, tp, D_MODEL)
    y_sample = _combine(x2s, yg, tg_s, fg, n_p).reshape(bs, ts, D_MODEL)

    c1, n1, m1, conv1 = st_p
    c2, n2, m2, conv2 = st_s
    mk4 = mk.reshape(1, bp, N_MEM, CA_HEADS, CA_DH)
    mv4 = mv.reshape(1, bp, N_MEM, CA_HEADS, CA_DH)
    return (y_prompt, y_sample, c1[None], n1[None], m1[None], conv1[None], mk4, mv4,
            c2[None], n2[None], m2[None], conv2[None])
```
